```python
import math
import jax, jax.numpy as jnp
from jax import lax
import numpy as np

D_MODEL = 1024
BATCH = 2
SEQ = 8192
DEPTH = 1
DEC_BATCH = 32
DEC_SEQ = 1
PAST_LEN = 8192
PAGE_SIZE = 128

CHUNK = 128
A_GROUPS = 8
A_GROUP_DIM = 64
A_WIDTH = A_GROUPS * A_GROUP_DIM
HEAD_DIM = 64
HEADS_PER_GROUP = 4
DILATION_GROUPS = ((128, 1), (512, 4), (2048, 16))
N_DGROUPS = len(DILATION_GROUPS)
B_QK_WIDTH = N_DGROUPS * HEADS_PER_GROUP * HEAD_DIM
B_OUT_WIDTH = HEADS_PER_GROUP * HEAD_DIM
ROPE_THETA = 10000.0
IN_SPLITS = (A_WIDTH, 2 * A_WIDTH, 2 * A_WIDTH + B_QK_WIDTH, 2 * A_WIDTH + 2 * B_QK_WIDTH,
             2 * A_WIDTH + 3 * B_QK_WIDTH, 2 * A_WIDTH + 3 * B_QK_WIDTH + D_MODEL)
IN_WIDTH = 2 * A_WIDTH + 3 * B_QK_WIDTH + 2 * D_MODEL
N_EXPERT_GROUPS = 4
EXPERTS_PER_GROUP = 8
N_EXPERTS = N_EXPERT_GROUPS * EXPERTS_PER_GROUP
TOP_K = 2
D_EXPERT = 512
MOE_BLOCK = 128
EPS = 1e-6

kernel_name = 'gated_gmlp_dilated_swa_hiermoe_step'


def rms_norm(x, g):
    xf = x.astype(jnp.float32)
    y = xf * lax.rsqrt(jnp.mean(xf * xf, axis=-1, keepdims=True) + EPS)
    return (y * g.astype(jnp.float32)).astype(x.dtype)


def layer_norm(x, g, b):
    xf = x.astype(jnp.float32)
    mu = jnp.mean(xf, axis=-1, keepdims=True)
    var = jnp.mean(jnp.square(xf - mu), axis=-1, keepdims=True)
    y = (xf - mu) * lax.rsqrt(var + EPS)
    return (y * g.astype(jnp.float32) + b.astype(jnp.float32)).astype(x.dtype)


def rope(x, pos):
    half = HEAD_DIM // 2
    inv = 1.0 / (ROPE_THETA ** (jnp.arange(half, dtype=jnp.float32) * (2.0 / HEAD_DIM)))
    ang = pos.astype(jnp.float32)[:, None] * inv[None, :]
    cos = jnp.cos(ang)[None, :, None, :]
    sin = jnp.sin(ang)[None, :, None, :]
    xf = x.astype(jnp.float32)
    x1, x2 = xf[..., :half], xf[..., half:]
    return jnp.concatenate([x1 * cos - x2 * sin, x2 * cos + x1 * sin], axis=-1).astype(x.dtype)


def input_projection(h, w_in, b_in, a_ln_g, a_ln_b, pos):
    bn, L, _ = h.shape
    z = h @ w_in + b_in
    u, va, q, k, vb, gate_a, gate_b = jnp.split(z, IN_SPLITS, axis=-1)
    u = jax.nn.gelu(u)
    va = layer_norm(jax.nn.gelu(va), a_ln_g, a_ln_b)
    nh = N_DGROUPS * HEADS_PER_GROUP
    q = rope(q.reshape(bn, L, nh, HEAD_DIM), pos).reshape(bn, L, N_DGROUPS, HEADS_PER_GROUP, HEAD_DIM)
    k = rope(k.reshape(bn, L, nh, HEAD_DIM), pos).reshape(bn, L, N_DGROUPS, HEADS_PER_GROUP, HEAD_DIM)
    vb = vb.reshape(bn, L, N_DGROUPS, HEADS_PER_GROUP, HEAD_DIM)
    return u, va, q, k, vb, gate_a, gate_b


def chunk_spatial_gate(u, v, w_s, b_s):
    bn, L, _ = u.shape
    lp = -(-L // CHUNK) * CHUNK
    vp = jnp.pad(v, ((0, 0), (0, lp - L), (0, 0))).reshape(bn, lp // CHUNK, CHUNK, A_GROUPS, A_GROUP_DIM)
    causal = jnp.tril(jnp.ones((CHUNK, CHUNK), dtype=bool))
    ws = jnp.where(causal[None], w_s, 0.0).astype(v.dtype)
    mixed = jnp.einsum('gts,bnsgc->bntgc', ws, vp) + b_s.T[:, :, None]
    mixed = mixed.reshape(bn, lp, A_WIDTH)[:, :L]
    return u * mixed


def dilated_attention_prompt(q, k, v, dilation, span):
    bn, S, H, dh = q.shape
    unit = dilation * span
    sp = -(-S // unit) * unit
    m = sp // dilation
    nb = m // span

    def to_blocks(t):
        t = jnp.pad(t, ((0, 0), (0, sp - S), (0, 0), (0, 0)))
        return t.reshape(bn, m, dilation, H, dh).transpose(0, 2, 1, 3, 4).reshape(bn, dilation, nb, span, H, dh)

    def with_prev(t):
        prev = jnp.pad(t[:, :, :-1], ((0, 0), (0, 0), (1, 0), (0, 0), (0, 0), (0, 0)))
        return jnp.concatenate([prev, t], axis=3)

    qb = to_blocks(q)
    kk = with_prev(to_blocks(k))
    vv = with_prev(to_blocks(v))
    s = jnp.einsum('bdnqhe,bdnkhe->bdnhqk', qb, kk, preferred_element_type=jnp.float32) * (dh ** -0.5)
    qi = jnp.arange(span)[:, None]
    ki = jnp.arange(2 * span)[None, :]
    band = (ki >= qi) & (ki <= qi + span)
    has_prev = (jnp.arange(nb) > 0)[:, None, None] | (ki >= span)[None]
    mask = band[None] & has_prev
    s = jnp.where(mask[None, None, :, None], s, -jnp.inf)
    lse = jax.nn.logsumexp(s, axis=-1)
    p = jnp.exp(s - lse[..., None])
    o = jnp.einsum('bdnhqk,bdnkhe->bdnqhe', p.astype(v.dtype), vv)

    def from_blocks(t):
        x_dim = t.shape[-1]
        t = t.reshape(bn, dilation, m, H, x_dim).transpose(0, 2, 1, 3, 4)
        return t.reshape(bn, sp, H, x_dim)[:, :S]

    o = from_blocks(o)
    lse = from_blocks(lse.transpose(0, 1, 2, 4, 3)[..., None])[..., 0]
    return o, lse


def dilated_attention_sample(q, k_new, v_new, kv_cache, dilation, span):
    wb = kv_cache.shape[1]
    L = q.shape[1]
    keys = jnp.concatenate([kv_cache[:, :, 0], k_new.astype(kv_cache.dtype)], axis=1)
    vals = jnp.concatenate([kv_cache[:, :, 1], v_new.astype(kv_cache.dtype)], axis=1)
    qidx = wb + jnp.arange(L)
    kidx = qidx[:, None] - dilation * jnp.arange(span + 1)[None, :]
    valid = kidx >= 0
    kidx = jnp.maximum(kidx, 0)
    kg = keys[:, kidx]
    vg = vals[:, kidx]
    s = jnp.einsum('blhe,bljhe->blhj', q, kg.astype(q.dtype), preferred_element_type=jnp.float32) * (HEAD_DIM ** -0.5)
    s = jnp.where(valid[None, :, None, :], s, -jnp.inf)
    lse = jax.nn.logsumexp(s, axis=-1)
    p = jnp.exp(s - lse[..., None])
    o = jnp.einsum('blhj,bljhe->blhe', p.astype(q.dtype), vg.astype(q.dtype))
    return o, lse


def combine_by_denominator(outs, lses):
    o = jnp.stack(outs, axis=0).astype(jnp.float32)
    w = jax.nn.softmax(jnp.stack(lses, axis=0), axis=0)
    comb = jnp.sum(w[..., None] * o, axis=0).astype(outs[0].dtype)
    return comb.reshape(comb.shape[0], comb.shape[1], B_OUT_WIDTH)


def grouped_expert_ffn(hf, expert_ids, gates, w_gate, w_up, w_down):
    T, D = hf.shape
    a = T * TOP_K
    flat_e = expert_ids.reshape(a)
    flat_tok = jnp.repeat(jnp.arange(T, dtype=jnp.int32), TOP_K)
    flat_g = gates.reshape(a)
    order = jnp.argsort(flat_e)
    se, stok, sg = flat_e[order], flat_tok[order], flat_g[order]
    counts = jax.ops.segment_sum(jnp.ones((a,), jnp.int32), flat_e, num_segments=N_EXPERTS)
    starts = jnp.cumsum(counts) - counts
    padded = (counts + MOE_BLOCK - 1) // MOE_BLOCK * MOE_BLOCK
    pstarts = jnp.cumsum(padded) - padded
    dest = pstarts[se] + (jnp.arange(a, dtype=jnp.int32) - starts[se])
    n_blocks = -(-a // MOE_BLOCK) + N_EXPERTS
    rows = n_blocks * MOE_BLOCK
    xs = jnp.zeros((rows, D), hf.dtype).at[dest].set(hf[stok])
    block_e = jnp.minimum(jnp.searchsorted(pstarts + padded, jnp.arange(n_blocks) * MOE_BLOCK, side='right'),
                          N_EXPERTS - 1)

    def expert_block(args):
        xb, e = args
        hid = jax.nn.silu(xb @ w_gate[e]) * (xb @ w_up[e])
        return hid @ w_down[e]

    ys = lax.map(expert_block, (xs.reshape(n_blocks, MOE_BLOCK, D), block_e)).reshape(rows, D)
    return jnp.zeros((T, D), hf.dtype).at[stok].add(ys[dest] * sg[:, None].astype(hf.dtype))


def hierarchical_moe(h, w_group_router, b_group_router, w_expert_router, b_expert_router, w_gate, w_up, w_down):
    bn, L, D = h.shape
    hf = h.reshape(bn * L, D)
    g_logits = (hf @ w_group_router + b_group_router).astype(jnp.float32)
    g_prob = jax.nn.softmax(g_logits, axis=-1)
    grp = jnp.argmax(g_logits, axis=-1).astype(jnp.int32)
    p_grp = jnp.take_along_axis(g_prob, grp[:, None], axis=-1)[:, 0]
    e_logits = (hf @ w_expert_router + b_expert_router).astype(jnp.float32)
    e_logits = e_logits.reshape(-1, N_EXPERT_GROUPS, EXPERTS_PER_GROUP)
    sel = jnp.take_along_axis(e_logits, grp[:, None, None], axis=1)[:, 0]
    top_val, top_idx = lax.top_k(sel, TOP_K)
    gates = p_grp[:, None] * jax.nn.softmax(top_val, axis=-1)
    expert_ids = grp[:, None] * EXPERTS_PER_GROUP + top_idx.astype(jnp.int32)
    return grouped_expert_ffn(hf, expert_ids, gates, w_gate, w_up, w_down).reshape(bn, L, D)


def layer_forward(x, pos, kv_caches, norm1_g, w_in, b_in, a_ln_g, a_ln_b, w_spatial, b_spatial,
                  w_a_proj, w_b_proj, w_o, norm2_g, w_group_router, b_group_router,
                  w_expert_router, b_expert_router, w_gate, w_up, w_down):
    h = rms_norm(x, norm1_g)
    u, va, q, k, vb, gate_a, gate_b = input_projection(h, w_in, b_in, a_ln_g, a_ln_b, pos)
    s_a = chunk_spatial_gate(u, va, w_spatial, b_spatial)
    outs, lses, kv_rows = [], [], []
    for g, (window, dilation) in enumerate(DILATION_GROUPS):
        span = window // dilation
        qg, kg, vg = q[:, :, g], k[:, :, g], vb[:, :, g]
        if kv_caches is None:
            o, lse = dilated_attention_prompt(qg, kg, vg, dilation, span)
            keep = min(window, x.shape[1])
            kv_rows.append(jnp.stack([kg[:, -keep:], vg[:, -keep:]], axis=2))
        else:
            o, lse = dilated_attention_sample(qg, kg, vg, kv_caches[g], dilation, span)
            kv_rows.append(jnp.stack([kg, vg], axis=2))
        outs.append(o)
        lses.append(lse)
    o_b = combine_by_denominator(outs, lses)
    merged = jax.nn.sigmoid(gate_a) * (s_a @ w_a_proj) + jax.nn.sigmoid(gate_b) * (o_b @ w_b_proj)
    x = x + merged @ w_o
    x = x + hierarchical_moe(rms_norm(x, norm2_g), w_group_router, b_group_router,
                             w_expert_router, b_expert_router, w_gate, w_up, w_down)
    return x, kv_rows, va


def setup_inputs(seed: int = 0) -> dict:
    key = jax.random.key(seed)
    ks = jax.random.split(key, 24)

    def nrm(k, shape, scale):
        return jax.random.normal(k, shape, jnp.float32) * scale

    d = D_MODEL
    cache_shape = lambda w: (DEPTH, DEC_BATCH, min(w, PAST_LEN), 2, HEADS_PER_GROUP, HEAD_DIM)
    return {
        'x_prompt': nrm(ks[0], (BATCH, SEQ, d), 1.0),
        'x_sample': nrm(ks[1], (DEC_BATCH, DEC_SEQ, d), 1.0),
        'cache_kv_g0': nrm(ks[2], cache_shape(DILATION_GROUPS[0][0]), 1.0),
        'cache_kv_g1': nrm(ks[3], cache_shape(DILATION_GROUPS[1][0]), 1.0),
        'cache_kv_g2': nrm(ks[4], cache_shape(DILATION_GROUPS[2][0]), 1.0),
        'norm1_g': 1.0 + nrm(ks[5], (DEPTH, d), 0.02),
        'w_in': nrm(ks[6], (DEPTH, d, IN_WIDTH), d ** -0.5),
        'b_in': nrm(ks[7], (DEPTH, IN_WIDTH), 0.02),
        'a_ln_g': 1.0 + nrm(ks[8], (DEPTH, A_WIDTH), 0.02),
        'a_ln_b': nrm(ks[9], (DEPTH, A_WIDTH), 0.02),
        'w_spatial': nrm(ks[10], (DEPTH, A_GROUPS, CHUNK, CHUNK), CHUNK ** -0.5),
        'b_spatial': 1.0 + nrm(ks[11], (DEPTH, A_GROUPS, CHUNK), 0.02),
        'w_a_proj': nrm(ks[12], (DEPTH, A_WIDTH, d), A_WIDTH ** -0.5),
        'w_b_proj': nrm(ks[13], (DEPTH, B_OUT_WIDTH, d), B_OUT_WIDTH ** -0.5),
        'w_o': nrm(ks[14], (DEPTH, d, d), d ** -0.5),
        'norm2_g': 1.0 + nrm(ks[15], (DEPTH, d), 0.02),
        'w_group_router': nrm(ks[16], (DEPTH, d, N_EXPERT_GROUPS), d ** -0.5),
        'b_group_router': nrm(ks[17], (DEPTH, N_EXPERT_GROUPS), 0.01),
        'w_expert_router': nrm(ks[18], (DEPTH, d, N_EXPERTS), d ** -0.5),
        'b_expert_router': nrm(ks[19], (DEPTH, N_EXPERTS), 0.01),
        'w_gate': nrm(ks[20], (DEPTH, N_EXPERTS, d, D_EXPERT), d ** -0.5),
        'w_up': nrm(ks[21], (DEPTH, N_EXPERTS, d, D_EXPERT), d ** -0.5),
        'w_down': nrm(ks[22], (DEPTH, N_EXPERTS, D_EXPERT, d), D_EXPERT ** -0.5),
        'final_norm_g': 1.0 + nrm(ks[23], (d,), 0.02),
    }


def reference(x_prompt, x_sample, cache_kv_g0, cache_kv_g1, cache_kv_g2, norm1_g, w_in, b_in, a_ln_g, a_ln_b,
              w_spatial, b_spatial, w_a_proj, w_b_proj, w_o, norm2_g, w_group_router, b_group_router,
              w_expert_router, b_expert_router, w_gate, w_up, w_down, final_norm_g):
    pos_prompt = jnp.arange(x_prompt.shape[1], dtype=jnp.int32)
    pos_sample = PAST_LEN + jnp.arange(x_sample.shape[1], dtype=jnp.int32)
    xp, xs = x_prompt, x_sample
    kv_p = [[] for _ in range(N_DGROUPS)]
    kv_s = [[] for _ in range(N_DGROUPS)]
    v_rows = []
    for l in range(DEPTH):
        lw = (norm1_g[l], w_in[l], b_in[l], a_ln_g[l], a_ln_b[l], w_spatial[l], b_spatial[l],
              w_a_proj[l], w_b_proj[l], w_o[l], norm2_g[l], w_group_router[l], b_group_router[l],
              w_expert_router[l], b_expert_router[l], w_gate[l], w_up[l], w_down[l])
        xp, rows_p, _ = layer_forward(xp, pos_prompt, None, *lw)
        xs, rows_s, v_new = layer_forward(xs, pos_sample, (cache_kv_g0[l], cache_kv_g1[l], cache_kv_g2[l]), *lw)
        for g in range(N_DGROUPS):
            kv_p[g].append(rows_p[g])
            kv_s[g].append(rows_s[g])
        v_rows.append(v_new)
    y_prompt = rms_norm(xp, final_norm_g)
    y_sample = rms_norm(xs, final_norm_g)
    return (y_prompt, y_sample,
            jnp.stack(kv_p[0]), jnp.stack(kv_p[1]), jnp.stack(kv_p[2]),
            jnp.stack(kv_s[0]), jnp.stack(kv_s[1]), jnp.stack(kv_s[2]),
            jnp.stack(v_rows))
```

```python
import functools

import jax
import jax.numpy as jnp
from jax import lax
from jax.experimental import pallas as pl
from jax.experimental.pallas import tpu as pltpu

F32 = jnp.float32
BF16 = jnp.bfloat16
U32 = jnp.uint32
HIGHEST = lax.Precision.HIGHEST

D_MODEL = 1024
BATCH = 2
SEQ = 8192
DEC_BATCH = 32
PAST_LEN = 8192
CHUNK = 128
A_GROUPS = 8
A_GROUP_DIM = 64
A_WIDTH = 512
HEAD_DIM = 64
HEADS = 4
GROUP_W = HEADS * HEAD_DIM
DILATIONS = (1, 4, 16)
SPAN = 128
QK_W = 768
IN_WIDTH = 5376
COL_U, COL_V, COL_Q, COL_K, COL_VB, COL_GA, COL_GB = 0, 512, 1024, 1792, 2560, 3328, 4352
N_EXPERT_GROUPS = 4
EXPERTS_PER_GROUP = 8
N_EXPERTS = 32
D_EXPERT = 512
ROPE_THETA = 10000.0
EPS = 1e-6

LANES = 128
T_PROMPT = BATCH * SEQ
TM = 512
ATT_NQ = 4
MOE_TB = 256
N_SLOTS = 2 * (T_PROMPT + DEC_BATCH)
MOE_NB = -(-N_SLOTS // MOE_TB) + N_EXPERTS
MOE_ROWS = MOE_NB * MOE_TB
HALF_D = D_MODEL // 2
NEG = -1e30
VMEM_LIMIT = 56 * 1024 * 1024


def _cparams(*sem):
    return pltpu.CompilerParams(dimension_semantics=sem, vmem_limit_bytes=VMEM_LIMIT)


def _rms(x, g):
    return x * lax.rsqrt(jnp.mean(x * x, axis=-1, keepdims=True) + EPS) * g


def _layer_norm(x, g, b):
    mu = jnp.mean(x, axis=-1, keepdims=True)
    xc = x - mu
    var = jnp.mean(xc * xc, axis=-1, keepdims=True)
    return xc * lax.rsqrt(var + EPS) * g + b


def _rope_chunk(x, cos, sin_signed):
    lane = lax.broadcasted_iota(jnp.int32, x.shape, 1)
    first_half = (lane % HEAD_DIM) < (HEAD_DIM // 2)
    swapped = jnp.where(first_half, pltpu.roll(x, LANES - HEAD_DIM // 2, 1), pltpu.roll(x, HEAD_DIM // 2, 1))
    return x * cos + swapped * sin_signed


def _pack_bf16_pair(h):
    lo = lax.bitcast_convert_type(h[:, :HALF_D].astype(BF16).astype(F32), U32)
    hi = lax.bitcast_convert_type(h[:, HALF_D:].astype(BF16).astype(F32), U32)
    return (hi & jnp.uint32(0xFFFF0000)) | (lo >> 16)


def _unpack_bf16_pair(p):
    lo = lax.bitcast_convert_type(p << 16, F32).astype(BF16)
    hi = lax.bitcast_convert_type(p & jnp.uint32(0xFFFF0000), F32).astype(BF16)
    return lo, hi


def _route(logits, prior_counts, ltri, exact_rank):
    rows = logits.shape[0]
    lane = lax.broadcasted_iota(jnp.int32, (rows, LANES), 1).astype(F32)
    is_g = lane < N_EXPERT_GROUPS
    gl = jnp.where(is_g, logits, NEG)
    gmax = jnp.max(gl, axis=1, keepdims=True)
    grp = jnp.min(jnp.where(gl == gmax, lane, float(LANES)), axis=1, keepdims=True)
    p_grp = 1.0 / jnp.sum(jnp.where(is_g, jnp.exp(gl - gmax), 0.0), axis=1, keepdims=True)
    lo = N_EXPERT_GROUPS + grp * EXPERTS_PER_GROUP
    el = jnp.where((lane >= lo) & (lane < lo + EXPERTS_PER_GROUP), logits, NEG)
    v1 = jnp.max(el, axis=1, keepdims=True)
    i1 = jnp.min(jnp.where(el == v1, lane, float(LANES)), axis=1, keepdims=True)
    el2 = jnp.where(lane == i1, NEG, el)
    v2 = jnp.max(el2, axis=1, keepdims=True)
    i2 = jnp.min(jnp.where(el2 == v2, lane, float(LANES)), axis=1, keepdims=True)
    t = jnp.exp(v2 - v1)
    gate1 = p_grp / (1.0 + t)
    gate2 = p_grp * t / (1.0 + t)
    e1 = i1 - N_EXPERT_GROUPS
    e2 = i2 - N_EXPERT_GROUPS
    hit1 = lane == e1
    hit2 = lane == e2
    onehot = jnp.where(hit1 | hit2, 1.0, 0.0)
    if exact_rank:
        before = jnp.dot(ltri, onehot, preferred_element_type=F32, precision=HIGHEST)
    else:
        before = jnp.dot(ltri, onehot.astype(BF16), preferred_element_type=F32)
    rank = before + prior_counts
    pos1 = jnp.sum(jnp.where(hit1, rank, 0.0), axis=1, keepdims=True)
    pos2 = jnp.sum(jnp.where(hit2, rank, 0.0), axis=1, keepdims=True)
    slab = jnp.where(lane == 0, e1, jnp.where(lane == 1, e2, jnp.where(lane == 2, pos1, jnp.where(
        lane == 3, pos2, jnp.where(lane == 4, gate1, jnp.where(lane == 5, gate2, 0.0))))))
    return slab, prior_counts + jnp.sum(onehot, axis=0, keepdims=True)


def _inproj_kernel(x_ref, g1_ref, w_ref, b_ref, lng_ref, lnb_ref, cos_ref, sin_ref,
                   uv_ref, qkv0_ref, qkv1_ref, qkv2_ref, gates_ref, kv0_ref, kv1_ref, kv2_ref):
    hb = _rms(x_ref[...], g1_ref[...]).astype(BF16)

    def seg(lo, width):
        return jnp.dot(hb, w_ref[:, lo:lo + width], preferred_element_type=F32) + b_ref[:, lo:lo + width]

    uv_ref[:, :A_WIDTH] = jax.nn.gelu(seg(COL_U, A_WIDTH)).astype(BF16)
    uv_ref[:, A_WIDTH:] = _layer_norm(jax.nn.gelu(seg(COL_V, A_WIDTH)), lng_ref[...], lnb_ref[...]).astype(BF16)
    gates_ref[:, :D_MODEL] = jax.nn.sigmoid(seg(COL_GA, D_MODEL)).astype(BF16)
    gates_ref[:, D_MODEL:] = jax.nn.sigmoid(seg(COL_GB, D_MODEL)).astype(BF16)

    cos = cos_ref[...]
    sin = sin_ref[...]
    qkv_refs = (qkv0_ref, qkv1_ref, qkv2_ref)
    kv_refs = (kv0_ref, kv1_ref, kv2_ref)
    for g in range(3):
        q = seg(COL_Q + g * GROUP_W, GROUP_W)
        k = seg(COL_K + g * GROUP_W, GROUP_W)
        v = seg(COL_VB + g * GROUP_W, GROUP_W)
        kv_rows = kv_refs[g].shape[1]
        for c in range(GROUP_W // LANES):
            sl = slice(c * LANES, (c + 1) * LANES)
            qr = _rope_chunk(q[:, sl], cos, sin) * (HEAD_DIM ** -0.5)
            kr = _rope_chunk(k[:, sl], cos, sin)
            qkv_refs[g][:, c * LANES:(c + 1) * LANES] = qr.astype(BF16)
            qkv_refs[g][:, GROUP_W + c * LANES:GROUP_W + (c + 1) * LANES] = kr.astype(BF16)
            kv_refs[g][0, :, c * LANES:(c + 1) * LANES] = kr[TM - kv_rows:, :]
        qkv_refs[g][:, 2 * GROUP_W:] = v.astype(BF16)
        kv_refs[g][0, :, GROUP_W:] = v[TM - kv_rows:, :]


def _inproj(x2, g1, w_bf, b_in, lng, lnb, cos_t, sin_t):
    tiles_per_seq = SEQ // TM
    n_tiles = T_PROMPT // TM
    const = lambda i: (0, 0)
    windows = tuple(min(SPAN * d, SEQ) for d in DILATIONS)

    def kv_spec(w):
        rows = min(w, TM)
        first = tiles_per_seq - w // rows
        return pl.BlockSpec((1, rows, 2 * GROUP_W),
                            lambda i: (i // tiles_per_seq, jnp.maximum(i % tiles_per_seq - first, 0), 0))

    return pl.pallas_call(
        _inproj_kernel,
        grid=(n_tiles,),
        in_specs=[
            pl.BlockSpec((TM, D_MODEL), lambda i: (i, 0)),
            pl.BlockSpec((1, D_MODEL), const),
            pl.BlockSpec((D_MODEL, IN_WIDTH), const),
            pl.BlockSpec((1, IN_WIDTH), const),
            pl.BlockSpec((1, A_WIDTH), const),
            pl.BlockSpec((1, A_WIDTH), const),
            pl.BlockSpec((TM, LANES), lambda i: (i % tiles_per_seq, 0)),
            pl.BlockSpec((TM, LANES), lambda i: (i % tiles_per_seq, 0)),
        ],
        out_specs=[
            pl.BlockSpec((TM, 2 * A_WIDTH), lambda i: (i, 0)),
            pl.BlockSpec((TM, 3 * GROUP_W), lambda i: (i, 0)),
            pl.BlockSpec((TM, 3 * GROUP_W), lambda i: (i, 0)),
            pl.BlockSpec((TM, 3 * GROUP_W), lambda i: (i, 0)),
            pl.BlockSpec((TM, 2 * D_MODEL), lambda i: (i, 0)),
            kv_spec(windows[0]), kv_spec(windows[1]), kv_spec(windows[2]),
        ],
        out_shape=[
            jax.ShapeDtypeStruct((T_PROMPT, 2 * A_WIDTH), BF16),
            jax.ShapeDtypeStruct((T_PROMPT, 3 * GROUP_W), BF16),
            jax.ShapeDtypeStruct((T_PROMPT, 3 * GROUP_W), BF16),
            jax.ShapeDtypeStruct((T_PROMPT, 3 * GROUP_W), BF16),
            jax.ShapeDtypeStruct((T_PROMPT, 2 * D_MODEL), BF16),
            jax.ShapeDtypeStruct((BATCH, windows[0], 2 * GROUP_W), F32),
            jax.ShapeDtypeStruct((BATCH, windows[1], 2 * GROUP_W), F32),
            jax.ShapeDtypeStruct((BATCH, windows[2], 2 * GROUP_W), F32),
        ],
        compiler_params=_cparams("arbitrary"),
        name="inproj",
    )(x2, g1, w_bf, b_in, lng, lnb, cos_t, sin_t)


def _attn_kernel(q_ref, kc_ref, kp_ref, vc_ref, vp_ref, o_ref, lse_ref):
    n = pl.program_id(2)
    qi = lax.broadcasted_iota(jnp.int32, (SPAN, SPAN), 0)
    ki = lax.broadcasted_iota(jnp.int32, (SPAN, SPAN), 1)
    prev_band = ki >= qi
    cur_band = ki <= qi
    contract_last = (((1,), (1,)), ((), ()))
    for j in range(ATT_NQ):
        rows = slice(j * SPAN, (j + 1) * SPAN)
        q = q_ref[0, rows, :]
        kc = kc_ref[0, rows, :]
        vc = vc_ref[0, rows, :]
        if j == 0:
            kp, vp = kp_ref[0], vp_ref[0]
            prev_ok = prev_band & (n > 0)
        else:
            prev_rows = slice((j - 1) * SPAN, j * SPAN)
            kp, vp = kc_ref[0, prev_rows, :], vc_ref[0, prev_rows, :]
            prev_ok = prev_band
        outs, lses = [], []
        for h in range(HEADS):
            cols = slice(h * HEAD_DIM, (h + 1) * HEAD_DIM)
            sp = lax.dot_general(q[:, cols], kp[:, cols], contract_last, preferred_element_type=F32)
            sc = lax.dot_general(q[:, cols], kc[:, cols], contract_last, preferred_element_type=F32)
            sp = jnp.where(prev_ok, sp, NEG)
            sc = jnp.where(cur_band, sc, NEG)
            m = jnp.maximum(jnp.max(sp, axis=1, keepdims=True), jnp.max(sc, axis=1, keepdims=True))
            pp = jnp.exp(sp - m)
            pc = jnp.exp(sc - m)
            l = jnp.sum(pp, axis=1, keepdims=True) + jnp.sum(pc, axis=1, keepdims=True)
            acc = (jnp.dot(pp.astype(BF16), vp[:, cols], preferred_element_type=F32)
                   + jnp.dot(pc.astype(BF16), vc[:, cols], preferred_element_type=F32))
            outs.append(acc / l)
            lses.append(jnp.broadcast_to(m + jnp.log(l), (SPAN, HEAD_DIM)))
        o_ref[0, rows, :] = jnp.concatenate(outs, axis=1).astype(BF16)
        lse_ref[0, rows, :] = jnp.concatenate(lses, axis=1)


def _attention(qkv, d):
    m = SEQ // d
    nb = m // (SPAN * ATT_NQ)
    view = qkv.reshape(BATCH, m, d * 3 * GROUP_W)
    blk = (1, SPAN * ATT_NQ, GROUP_W)
    pblk = (1, SPAN, GROUP_W)
    prev_row = lambda n: jnp.maximum(n * ATT_NQ - 1, 0)
    o, lse = pl.pallas_call(
        _attn_kernel,
        grid=(BATCH, d, nb),
        in_specs=[
            pl.BlockSpec(blk, lambda b, r, n: (b, n, 3 * r)),
            pl.BlockSpec(blk, lambda b, r, n: (b, n, 3 * r + 1)),
            pl.BlockSpec(pblk, lambda b, r, n: (b, prev_row(n), 3 * r + 1)),
            pl.BlockSpec(blk, lambda b, r, n: (b, n, 3 * r + 2)),
            pl.BlockSpec(pblk, lambda b, r, n: (b, prev_row(n), 3 * r + 2)),
        ],
        out_specs=[
            pl.BlockSpec(blk, lambda b, r, n: (b, n, r)),
            pl.BlockSpec(blk, lambda b, r, n: (b, n, r)),
        ],
        out_shape=[
            jax.ShapeDtypeStruct((BATCH, m, d * GROUP_W), BF16),
            jax.ShapeDtypeStruct((BATCH, m, d * GROUP_W), F32),
        ],
        compiler_params=_cparams("arbitrary", "arbitrary", "arbitrary"),
        name=f"attn_d{d}",
    )(view, view, view, view, view)
    return o.reshape(T_PROMPT, GROUP_W), lse.reshape(T_PROMPT, GROUP_W)


def _merge_kernel(x_ref, uv_ref, gates_ref, o0_ref, o1_ref, o2_ref, l0_ref, l1_ref, l2_ref,
                  wsp_ref, bsp_ref, wa_ref, wb_ref, wo_ref, g2_ref, wrh_ref, wrl_ref, br_ref, ltri_ref,
                  x1_ref, h2p_ref, rt_ref, cnt_ref, run_ref):
    @pl.when(pl.program_id(0) == 0)
    def _():
        run_ref[...] = jnp.zeros_like(run_ref)

    lane = lax.broadcasted_iota(jnp.int32, (CHUNK, LANES), 1)
    left = lane < A_GROUP_DIM
    zero = jnp.zeros((CHUNK, LANES), BF16)
    sa_chunks = []
    for c in range(TM // CHUNK):
        rows = slice(c * CHUNK, (c + 1) * CHUNK)
        pairs = []
        for p in range(A_GROUPS // 2):
            vp = uv_ref[rows, A_WIDTH + p * LANES:A_WIDTH + (p + 1) * LANES]
            rhs = jnp.concatenate([jnp.where(left, vp, zero), jnp.where(left, zero, vp)], axis=0)
            pairs.append(jnp.dot(wsp_ref[p], rhs, preferred_element_type=F32))
        mixed = jnp.concatenate(pairs, axis=1) + bsp_ref[...]
        sa_chunks.append((uv_ref[rows, :A_WIDTH].astype(F32) * mixed).astype(BF16))
    s_a = jnp.concatenate(sa_chunks, axis=0)

    l0, l1, l2 = l0_ref[...], l1_ref[...], l2_ref[...]
    mx = jnp.maximum(jnp.maximum(l0, l1), l2)
    w0, w1, w2 = jnp.exp(l0 - mx), jnp.exp(l1 - mx), jnp.exp(l2 - mx)
    comb = (w0 * o0_ref[...].astype(F32) + w1 * o1_ref[...].astype(F32) + w2 * o2_ref[...].astype(F32)) / (w0 + w1 + w2)

    a = jnp.dot(s_a, wa_ref[...], preferred_element_type=F32)
    b = jnp.dot(comb.astype(BF16), wb_ref[...], preferred_element_type=F32)
    merged = gates_ref[:, :D_MODEL].astype(F32) * a + gates_ref[:, D_MODEL:].astype(F32) * b
    x1 = x_ref[...] + jnp.dot(merged.astype(BF16), wo_ref[...], preferred_element_type=F32)
    x1_ref[...] = x1

    h2 = _rms(x1, g2_ref[...])
    h2p_ref[...] = _pack_bf16_pair(h2)
    h_hi = h2.astype(BF16)
    h_lo = (h2 - h_hi.astype(F32)).astype(BF16)
    logits = (jnp.dot(h_hi, wrh_ref[...], preferred_element_type=F32)
              + jnp.dot(h_lo, wrh_ref[...], preferred_element_type=F32)
              + jnp.dot(h_hi, wrl_ref[...], preferred_element_type=F32)) + br_ref[...]
    slab, counts = _route(logits, run_ref[...], ltri_ref[...], exact_rank=False)
    rt_ref[...] = slab
    run_ref[...] = counts
    cnt_ref[...] = counts


def _merge(x2, uv, gates, o, lse, wsp, bsp, wa, wb, wo, g2, wrh, wrl, br, ltri):
    n_tiles = T_PROMPT // TM
    tile = lambda w: pl.BlockSpec((TM, w), lambda i: (i, 0))
    full = lambda a: pl.BlockSpec(a.shape, lambda i: (0,) * a.ndim)
    return pl.pallas_call(
        _merge_kernel,
        grid=(n_tiles,),
        in_specs=[tile(D_MODEL), tile(2 * A_WIDTH), tile(2 * D_MODEL),
                  tile(GROUP_W), tile(GROUP_W), tile(GROUP_W), tile(GROUP_W), tile(GROUP_W), tile(GROUP_W),
                  full(wsp), full(bsp), full(wa), full(wb), full(wo), full(g2), full(wrh), full(wrl), full(br),
                  full(ltri)],
        out_specs=[tile(D_MODEL), tile(HALF_D), tile(LANES), pl.BlockSpec((1, LANES), lambda i: (0, 0))],
        out_shape=[
            jax.ShapeDtypeStruct((T_PROMPT, D_MODEL), F32),
            jax.ShapeDtypeStruct((T_PROMPT, HALF_D), U32),
            jax.ShapeDtypeStruct((T_PROMPT, LANES), F32),
            jax.ShapeDtypeStruct((1, LANES), F32),
        ],
        scratch_shapes=[pltpu.VMEM((1, LANES), F32)],
        compiler_params=_cparams("arbitrary"),
        name="merge_route",
    )(x2, uv, gates, o[0], o[1], o[2], lse[0], lse[1], lse[2], wsp, bsp, wa, wb, wo, g2, wrh, wrl, br, ltri)


SAMPLE_COLS = 768


def _sample_inproj_kernel(x_ref, g1_ref, w_ref, b_ref, z_ref):
    h = _rms(x_ref[...], g1_ref[...])
    z_ref[...] = jnp.dot(h, w_ref[...], preferred_element_type=F32, precision=HIGHEST) + b_ref[...]


def _sample_inproj(xs, g1, w_in, b_in):
    const = lambda j: (0, 0)
    return pl.pallas_call(
        _sample_inproj_kernel,
        grid=(IN_WIDTH // SAMPLE_COLS,),
        in_specs=[pl.BlockSpec((DEC_BATCH, D_MODEL), const), pl.BlockSpec((1, D_MODEL), const),
                  pl.BlockSpec((D_MODEL, SAMPLE_COLS), lambda j: (0, j)),
                  pl.BlockSpec((1, SAMPLE_COLS), lambda j: (0, j))],
        out_specs=pl.BlockSpec((DEC_BATCH, SAMPLE_COLS), lambda j: (0, j)),
        out_shape=jax.ShapeDtypeStruct((DEC_BATCH, IN_WIDTH), F32),
        compiler_params=_cparams("arbitrary"),
        name="sample_inproj",
    )(xs, g1, w_in, b_in)


SAMPLE_BB = 8


def _sample_attn_kernel(z_ref, c0_ref, c1_ref, c2_ref, lng_ref, lnb_ref, ws0_ref, bs0_ref, cos_ref, sin_ref,
                        hsel_ref, hexp_ref, vrow_ref, kv0_ref, kv1_ref, kv2_ref, sa_ref, comb_ref):
    bb = SAMPLE_BB
    u = jax.nn.gelu(z_ref[:, COL_U:COL_U + A_WIDTH])
    va = _layer_norm(jax.nn.gelu(z_ref[:, COL_V:COL_V + A_WIDTH]), lng_ref[...], lnb_ref[...])
    vrow_ref[...] = va
    sa_ref[...] = u * (ws0_ref[...] * va + bs0_ref[...])

    cos, sin = cos_ref[...], sin_ref[...]
    hsel, hexp = hsel_ref[...], hexp_ref[...]
    seg_sum = lambda t: jnp.dot(t, hsel, preferred_element_type=F32, precision=HIGHEST)
    expand = lambda t: jnp.dot(t, hexp, preferred_element_type=F32, precision=HIGHEST)
    qkv_new = []
    for g, kv_ref in enumerate((kv0_ref, kv1_ref, kv2_ref)):
        qs, ks = [], []
        for c in range(GROUP_W // LANES):
            off = g * GROUP_W + c * LANES
            qs.append(_rope_chunk(z_ref[:, COL_Q + off:COL_Q + off + LANES], cos, sin) * (HEAD_DIM ** -0.5))
            ks.append(_rope_chunk(z_ref[:, COL_K + off:COL_K + off + LANES], cos, sin))
        q = jnp.concatenate(qs, axis=1)
        k_new = jnp.concatenate(ks, axis=1)
        v_new = z_ref[:, COL_VB + g * GROUP_W:COL_VB + (g + 1) * GROUP_W]
        kv_ref[:, :GROUP_W] = k_new
        kv_ref[:, GROUP_W:] = v_new
        qkv_new.append((q, k_new, v_new))

    for bi in range(bb):
        row = slice(bi, bi + 1)
        outs, lses = [], []
        for (q, k_new, v_new), c_ref in zip(qkv_new, (c0_ref, c1_ref, c2_ref)):
            qb, kb, vb = q[row], k_new[row], v_new[row]
            kc = c_ref[bi, :, :GROUP_W]
            vc = c_ref[bi, :, GROUP_W:]
            s_c = seg_sum(kc * qb)
            s_n = seg_sum(qb * kb)
            m = jnp.maximum(jnp.max(s_c, axis=0, keepdims=True), s_n)
            p_c = jnp.exp(s_c - m)
            p_n = jnp.exp(s_n - m)
            l = jnp.sum(p_c, axis=0, keepdims=True) + p_n
            acc = jnp.sum(expand(p_c) * vc, axis=0, keepdims=True) + expand(p_n) * vb
            outs.append(acc / expand(l))
            lses.append(m + jnp.log(l))
        mx = jnp.maximum(jnp.maximum(lses[0], lses[1]), lses[2])
        ws = [jnp.exp(t - mx) for t in lses]
        tot = ws[0] + ws[1] + ws[2]
        comb_ref[row, :] = (expand(ws[0] / tot) * outs[0] + expand(ws[1] / tot) * outs[1]
                            + expand(ws[2] / tot) * outs[2])


def _sample_attn(z, caches, lng, lnb, ws0, bs0, cos_s, sin_s, hsel, hexp):
    bb = SAMPLE_BB
    rows = lambda w: pl.BlockSpec((bb, w), lambda i: (i, 0))
    full = lambda a: pl.BlockSpec(a.shape, lambda i: (0,) * a.ndim)
    cache_spec = pl.BlockSpec((bb, SPAN, 2 * GROUP_W), lambda i: (i, 0, 0))
    return pl.pallas_call(
        _sample_attn_kernel,
        grid=(DEC_BATCH // bb,),
        in_specs=[rows(IN_WIDTH), cache_spec, cache_spec, cache_spec,
                  full(lng), full(lnb), full(ws0), full(bs0), full(cos_s), full(sin_s), full(hsel), full(hexp)],
        out_specs=[rows(A_WIDTH), rows(2 * GROUP_W), rows(2 * GROUP_W), rows(2 * GROUP_W), rows(A_WIDTH), rows(GROUP_W)],
        out_shape=[jax.ShapeDtypeStruct((DEC_BATCH, A_WIDTH), F32),
                   jax.ShapeDtypeStruct((DEC_BATCH, 2 * GROUP_W), F32),
                   jax.ShapeDtypeStruct((DEC_BATCH, 2 * GROUP_W), F32),
                   jax.ShapeDtypeStruct((DEC_BATCH, 2 * GROUP_W), F32),
                   jax.ShapeDtypeStruct((DEC_BATCH, A_WIDTH), F32),
                   jax.ShapeDtypeStruct((DEC_BATCH, GROUP_W), F32)],
        compiler_params=_cparams("arbitrary"),
        name="sample_attn",
    )(z, caches[0], caches[1], caches[2], lng, lnb, ws0, bs0, cos_s, sin_s, hsel, hexp)


def _sample_merge_kernel(x_ref, z_ref, sa_ref, comb_ref, wa_ref, wb_ref, wo_ref, g2_ref, wr_ref, br_ref,
                         ltri_ref, cnt_in_ref, x1_ref, h2p_ref, rt_ref, cnt_ref):
    dot = lambda p, q: jnp.dot(p, q, preferred_element_type=F32, precision=HIGHEST)
    a = dot(sa_ref[...], wa_ref[...])
    b = dot(comb_ref[...], wb_ref[...])
    merged = (jax.nn.sigmoid(z_ref[:, COL_GA:COL_GA + D_MODEL]) * a
              + jax.nn.sigmoid(z_ref[:, COL_GB:COL_GB + D_MODEL]) * b)
    x1 = x_ref[...] + dot(merged, wo_ref[...])
    x1_ref[...] = x1
    h2 = _rms(x1, g2_ref[...])
    h2p_ref[...] = _pack_bf16_pair(h2)
    logits = dot(h2, wr_ref[...]) + br_ref[...]
    slab, counts = _route(logits, cnt_in_ref[...], ltri_ref[...], exact_rank=True)
    rt_ref[...] = slab
    cnt_ref[...] = counts


def _sample_merge(xs, z, s_a, comb, wa, wb, wo, g2, wr, br, ltri, counts):
    return pl.pallas_call(
        _sample_merge_kernel,
        out_shape=[jax.ShapeDtypeStruct((DEC_BATCH, D_MODEL), F32),
                   jax.ShapeDtypeStruct((DEC_BATCH, HALF_D), U32),
                   jax.ShapeDtypeStruct((DEC_BATCH, LANES), F32),
                   jax.ShapeDtypeStruct((1, LANES), F32)],
        compiler_params=pltpu.CompilerParams(vmem_limit_bytes=VMEM_LIMIT),
        name="sample_merge_route",
    )(xs, z, s_a, comb, wa, wb, wo, g2, wr, br, ltri, counts)


def _dispatch_kernel(dest_ref, h_ref, xs_in_ref, xs_ref, sem):
    del xs_in_ref
    rows = h_ref.shape[0]
    base = pl.program_id(0) * (2 * rows)

    def row_copy(t, slot):
        d = dest_ref[base + 2 * t + slot]
        return pltpu.make_async_copy(h_ref.at[pl.ds(t, 1)], xs_ref.at[pl.ds(d, 1)], sem)

    def start(t, c):
        row_copy(t, 0).start()
        row_copy(t, 1).start()
        return c

    def wait(t, c):
        row_copy(t, 0).wait()
        row_copy(t, 1).wait()
        return c

    lax.fori_loop(0, rows, start, 0)
    lax.fori_loop(0, rows, wait, 0)


def _dispatch(dest, h2p, xs, rows):
    n = h2p.shape[0]
    return pl.pallas_call(
        _dispatch_kernel,
        grid_spec=pltpu.PrefetchScalarGridSpec(
            num_scalar_prefetch=1,
            grid=(n // rows,),
            in_specs=[pl.BlockSpec((rows, HALF_D), lambda i, dest: (i, 0)), pl.BlockSpec(memory_space=pl.ANY)],
            out_specs=pl.BlockSpec(memory_space=pl.ANY),
            scratch_shapes=[pltpu.SemaphoreType.DMA(())],
        ),
        out_shape=jax.ShapeDtypeStruct(xs.shape, xs.dtype),
        input_output_aliases={2: 0},
        compiler_params=_cparams("arbitrary"),
        name=f"moe_dispatch_{rows}",
    )(dest, h2p, xs)


def _ffn_kernel(be_ref, nused_ref, xs_ref, wg_ref, wu_ref, wd_ref, ys_ref, wg_b, wu_b, wd_b):
    i = pl.program_id(0)
    new_expert = jnp.logical_or(i == 0, be_ref[i] != be_ref[jnp.maximum(i - 1, 0)])

    @pl.when(new_expert)
    def _():
        wg_b[...] = wg_ref[0].astype(BF16)
        wu_b[...] = wu_ref[0].astype(BF16)
        wd_b[...] = wd_ref[0].astype(BF16)

    @pl.when(i < nused_ref[0])
    def _():
        lo, hi = _unpack_bf16_pair(xs_ref[...])
        gate = (jnp.dot(lo, wg_b[:HALF_D, :], preferred_element_type=F32)
                + jnp.dot(hi, wg_b[HALF_D:, :], preferred_element_type=F32))
        up = (jnp.dot(lo, wu_b[:HALF_D, :], preferred_element_type=F32)
              + jnp.dot(hi, wu_b[HALF_D:, :], preferred_element_type=F32))
        hid = (jax.nn.silu(gate) * up).astype(BF16)
        ys_ref[...] = jnp.dot(hid, wd_b[...], preferred_element_type=F32)

    @pl.when(i >= nused_ref[0])
    def _():
        ys_ref[...] = jnp.zeros_like(ys_ref)


def _expert_ffn(block_e, nused, xs, w_gate, w_up, w_down):
    return pl.pallas_call(
        _ffn_kernel,
        grid_spec=pltpu.PrefetchScalarGridSpec(
            num_scalar_prefetch=2,
            grid=(MOE_NB,),
            in_specs=[pl.BlockSpec((MOE_TB, HALF_D), lambda i, be, nu: (i, 0)),
                      pl.BlockSpec((1, D_MODEL, D_EXPERT), lambda i, be, nu: (be[i], 0, 0)),
                      pl.BlockSpec((1, D_MODEL, D_EXPERT), lambda i, be, nu: (be[i], 0, 0)),
                      pl.BlockSpec((1, D_EXPERT, D_MODEL), lambda i, be, nu: (be[i], 0, 0))],
            out_specs=pl.BlockSpec((MOE_TB, D_MODEL), lambda i, be, nu: (i, 0)),
            scratch_shapes=[pltpu.VMEM((D_MODEL, D_EXPERT), BF16), pltpu.VMEM((D_MODEL, D_EXPERT), BF16),
                            pltpu.VMEM((D_EXPERT, D_MODEL), BF16)],
        ),
        out_shape=jax.ShapeDtypeStruct((MOE_ROWS, D_MODEL), F32),
        compiler_params=_cparams("arbitrary"),
        name="moe_ffn",
    )(block_e, nused, xs, w_gate, w_up, w_down)


def _combine_kernel(dest_ref, x1_ref, rt_ref, gf_ref, ys_ref, y_ref, ya, yb, sem):
    rows = x1_ref.shape[0]
    base = pl.program_id(0) * (2 * rows)

    def row_copy(t, slot, buf):
        d = dest_ref[base + 2 * t + slot]
        return pltpu.make_async_copy(ys_ref.at[pl.ds(d, 1)], buf.at[pl.ds(t, 1)], sem)

    def start(t, c):
        row_copy(t, 0, ya).start()
        row_copy(t, 1, yb).start()
        return c

    def wait(t, c):
        row_copy(t, 0, ya).wait()
        row_copy(t, 1, yb).wait()
        return c

    lax.fori_loop(0, rows, start, 0)
    lax.fori_loop(0, rows, wait, 0)
    x2 = x1_ref[...] + rt_ref[:, 4:5] * ya[...] + rt_ref[:, 5:6] * yb[...]
    y_ref[...] = _rms(x2, gf_ref[...])


def _combine(dest, x1, rt, gf, ys, rows):
    n = x1.shape[0]
    return pl.pallas_call(
        _combine_kernel,
        grid_spec=pltpu.PrefetchScalarGridSpec(
            num_scalar_prefetch=1,
            grid=(n // rows,),
            in_specs=[pl.BlockSpec((rows, D_MODEL), lambda i, dest: (i, 0)),
                      pl.BlockSpec((rows, LANES), lambda i, dest: (i, 0)),
                      pl.BlockSpec((1, D_MODEL), lambda i, dest: (0, 0)),
                      pl.BlockSpec(memory_space=pl.ANY)],
            out_specs=pl.BlockSpec((rows, D_MODEL), lambda i, dest: (i, 0)),
            scratch_shapes=[pltpu.VMEM((rows, D_MODEL), F32), pltpu.VMEM((rows, D_MODEL), F32),
                            pltpu.SemaphoreType.DMA(())],
        ),
        out_shape=jax.ShapeDtypeStruct((n, D_MODEL), F32),
        compiler_params=_cparams("arbitrary"),
        name=f"moe_combine_{rows}",
    )(dest, x1, rt, gf, ys)


def _rope_tables(pos):
    half = HEAD_DIM // 2
    inv = 1.0 / (ROPE_THETA ** (jnp.arange(half, dtype=F32) * (2.0 / HEAD_DIM)))
    ang = pos.astype(F32)[:, None] * inv[None, :]
    cos, sin = jnp.cos(ang), jnp.sin(ang)
    return jnp.concatenate([cos, cos, cos, cos], axis=1), jnp.concatenate([-sin, sin, -sin, sin], axis=1)


def _slot_dest(rt, pstarts):
    e = rt[:, 0:2].astype(jnp.int32)
    pos = rt[:, 2:4].astype(jnp.int32)
    onehot = e[:, :, None] == jnp.arange(N_EXPERTS, dtype=jnp.int32)[None, None, :]
    return (jnp.sum(jnp.where(onehot, pstarts[None, None, :], 0), axis=-1) + pos).reshape(-1)


def kernel(x_prompt, x_sample, cache_kv_g0, cache_kv_g1, cache_kv_g2, norm1_g, w_in, b_in, a_ln_g, a_ln_b, w_spatial, b_spatial, w_a_proj, w_b_proj, w_o, norm2_g, w_group_router, b_group_router, w_expert_router, b_expert_router, w_gate, w_up, w_down, final_norm_g):
    x2 = x_prompt.reshape(T_PROMPT, D_MODEL)
    xs = x_sample.reshape(DEC_BATCH, D_MODEL)
    g1 = norm1_g[0][None, :]
    g2 = norm2_g[0][None, :]
    gf = final_norm_g[None, :]
    b_in2 = b_in[0][None, :]
    lng, lnb = a_ln_g[0][None, :], a_ln_b[0][None, :]

    causal = jnp.tril(jnp.ones((CHUNK, CHUNK), dtype=bool))
    ws_tril = jnp.where(causal[None], w_spatial[0], 0.0)
    wsp = jnp.concatenate([ws_tril[0::2], ws_tril[1::2]], axis=2).astype(BF16)
    bsp = jnp.repeat(b_spatial[0].T, A_GROUP_DIM, axis=1)
    ws0 = jnp.repeat(ws_tril[:, 0, 0], A_GROUP_DIM)[None, :]
    bs0 = jnp.repeat(b_spatial[0][:, 0], A_GROUP_DIM)[None, :]
    w_router = jnp.zeros((D_MODEL, LANES), F32)
    w_router = w_router.at[:, :N_EXPERT_GROUPS].set(w_group_router[0])
    w_router = w_router.at[:, N_EXPERT_GROUPS:N_EXPERT_GROUPS + N_EXPERTS].set(w_expert_router[0])
    b_router = jnp.zeros((1, LANES), F32)
    b_router = b_router.at[0, :N_EXPERT_GROUPS].set(b_group_router[0])
    b_router = b_router.at[0, N_EXPERT_GROUPS:N_EXPERT_GROUPS + N_EXPERTS].set(b_expert_router[0])
    wr_hi = w_router.astype(BF16)
    wr_lo = (w_router - wr_hi.astype(F32)).astype(BF16)
    ltri = jnp.tril(jnp.ones((TM, TM), F32), -1)
    head_of = jnp.arange(GROUP_W) // HEAD_DIM
    hsel = (head_of[:, None] == jnp.arange(LANES)[None, :]).astype(F32)
    hexp = hsel.T

    cos_p, sin_p = _rope_tables(jnp.arange(SEQ, dtype=jnp.int32))
    cos_s, sin_s = _rope_tables(jnp.full((1,), PAST_LEN, jnp.int32))

    uv, qkv0, qkv1, qkv2, gates, kvp0, kvp1, kvp2 = _inproj(
        x2, g1, w_in[0].astype(BF16), b_in2, lng, lnb, cos_p, sin_p)
    att = [_attention(qkv, d) for qkv, d in zip((qkv0, qkv1, qkv2), DILATIONS)]
    x1_p, h2p_p, rt_p, counts_p = _merge(
        x2, uv, gates, [a[0] for a in att], [a[1] for a in att], wsp, bsp,
        w_a_proj[0].astype(BF16), w_b_proj[0].astype(BF16), w_o[0].astype(BF16), g2, wr_hi, wr_lo, b_router,
        ltri.astype(BF16))

    z_s = _sample_inproj(xs, g1, w_in[0], b_in2)
    caches = [c[0].reshape(DEC_BATCH, SPAN, d * 2 * GROUP_W)
              for c, d in zip((cache_kv_g0, cache_kv_g1, cache_kv_g2), DILATIONS)]
    vrow, kvs0, kvs1, kvs2, sa_s, comb_s = _sample_attn(z_s, caches, lng, lnb, ws0, bs0, cos_s, sin_s, hsel, hexp)
    x1_s, h2p_s, rt_s, counts = _sample_merge(
        xs, z_s, sa_s, comb_s, w_a_proj[0], w_b_proj[0], w_o[0], g2, w_router, b_router,
        ltri[:DEC_BATCH, :DEC_BATCH], counts_p)

    cnt = counts[0, :N_EXPERTS].astype(jnp.int32)
    padded = (cnt + MOE_TB - 1) // MOE_TB * MOE_TB
    pends = jnp.cumsum(padded)
    pstarts = pends - padded
    block_e = jnp.minimum(jnp.searchsorted(pends, jnp.arange(MOE_NB, dtype=jnp.int32) * MOE_TB, side='right'),
                          N_EXPERTS - 1).astype(jnp.int32)
    nused = (pends[-1:] // MOE_TB).astype(jnp.int32)
    dest_p = _slot_dest(rt_p, pstarts)
    dest_s = _slot_dest(rt_s, pstarts)

    rows = jnp.zeros((MOE_ROWS, HALF_D), U32)
    rows = _dispatch(dest_p, h2p_p, rows, TM)
    rows = _dispatch(dest_s, h2p_s, rows, DEC_BATCH)
    ys = _expert_ffn(block_e, nused, rows, w_gate[0], w_up[0], w_down[0])
    y_p = _combine(dest_p, x1_p, rt_p, gf, ys, TM)
    y_s = _combine(dest_s, x1_s, rt_s, gf, ys, DEC_BATCH)

    kv_shape = lambda n, w: (1, n, w, 2, HEADS, HEAD_DIM)
    return (y_p.reshape(BATCH, SEQ, D_MODEL), y_s.reshape(DEC_BATCH, 1, D_MODEL),
            kvp0.reshape(kv_shape(BATCH, kvp0.shape[1])), kvp1.reshape(kv_shape(BATCH, kvp1.shape[1])),
            kvp2.reshape(kv_shape(BATCH, kvp2.shape[1])),
            kvs0.reshape(kv_shape(DEC_BATCH, 1)), kvs1.reshape(kv_shape(DEC_BATCH, 1)),
            kvs2.reshape(kv_shape(DEC_BATCH, 1)), vrow.reshape(1, DEC_BATCH, 1, A_WIDTH))
```

```python
import functools

import jax
import jax.numpy as jnp
from jax import lax
from jax.experimental import pallas as pl
from jax.experimental.pallas import tpu as pltpu

F32 = jnp.float32
BF16 = jnp.bfloat16
U32 = jnp.uint32
HIGHEST = lax.Precision.HIGHEST

D_MODEL = 1024
BATCH = 2
SEQ = 8192
DEC_BATCH = 32
PAST_LEN = 8192
CHUNK = 128
A_GROUPS = 8
A_GROUP_DIM = 64
A_WIDTH = 512
HEAD_DIM = 64
HEADS = 4
GROUP_W = HEADS * HEAD_DIM
DILATIONS = (1, 4, 16)
SPAN = 128
QK_W = 768
IN_WIDTH = 5376
COL_U, COL_V, COL_Q, COL_K, COL_VB, COL_GA, COL_GB = 0, 512, 1024, 1792, 2560, 3328, 4352
N_EXPERT_GROUPS = 4
EXPERTS_PER_GROUP = 8
N_EXPERTS = 32
D_EXPERT = 512
ROPE_THETA = 10000.0
EPS = 1e-6

LANES = 128
T_PROMPT = BATCH * SEQ
TM = 512
ATT_NQ = 4
MOE_TB = 256
N_SLOTS = 2 * (T_PROMPT + DEC_BATCH)
MOE_NB = -(-N_SLOTS // MOE_TB) + N_EXPERTS
MOE_ROWS = MOE_NB * MOE_TB
HALF_D = D_MODEL // 2
NEG = -1e30
VMEM_LIMIT = 56 * 1024 * 1024


def _cparams(*sem):
    return pltpu.CompilerParams(dimension_semantics=sem, vmem_limit_bytes=VMEM_LIMIT)


def _rms(x, g):
    return x * lax.rsqrt(jnp.mean(x * x, axis=-1, keepdims=True) + EPS) * g


def _layer_norm(x, g, b):
    mu = jnp.mean(x, axis=-1, keepdims=True)
    xc = x - mu
    var = jnp.mean(xc * xc, axis=-1, keepdims=True)
    return xc * lax.rsqrt(var + EPS) * g + b


def _rope_chunk(x, cos, sin_signed):
    lane = lax.broadcasted_iota(jnp.int32, x.shape, 1)
    first_half = (lane % HEAD_DIM) < (HEAD_DIM // 2)
    swapped = jnp.where(first_half, pltpu.roll(x, LANES - HEAD_DIM // 2, 1), pltpu.roll(x, HEAD_DIM // 2, 1))
    return x * cos + swapped * sin_signed


def _pack_bf16_pair(h):
    lo = lax.bitcast_convert_type(h[:, :HALF_D].astype(BF16).astype(F32), U32)
    hi = lax.bitcast_convert_type(h[:, HALF_D:].astype(BF16).astype(F32), U32)
    return (hi & jnp.uint32(0xFFFF0000)) | (lo >> 16)


def _unpack_bf16_pair(p):
    lo = lax.bitcast_convert_type(p << 16, F32).astype(BF16)
    hi = lax.bitcast_convert_type(p & jnp.uint32(0xFFFF0000), F32).astype(BF16)
    return lo, hi


def _route(logits, prior_counts, ltri, exact_rank):
    rows = logits.shape[0]
    lane = lax.broadcasted_iota(jnp.int32, (rows, LANES), 1).astype(F32)
    is_g = lane < N_EXPERT_GROUPS
    gl = jnp.where(is_g, logits, NEG)
    gmax = jnp.max(gl, axis=1, keepdims=True)
    grp = jnp.min(jnp.where(gl == gmax, lane, float(LANES)), axis=1, keepdims=True)
    p_grp = 1.0 / jnp.sum(jnp.where(is_g, jnp.exp(gl - gmax), 0.0), axis=1, keepdims=True)
    lo = N_EXPERT_GROUPS + grp * EXPERTS_PER_GROUP
    el = jnp.where((lane >= lo) & (lane < lo + EXPERTS_PER_GROUP), logits, NEG)
    v1 = jnp.max(el, axis=1, keepdims=True)
    i1 = jnp.min(jnp.where(el == v1, lane, float(LANES)), axis=1, keepdims=True)
    el2 = jnp.where(lane == i1, NEG, el)
    v2 = jnp.max(el2, axis=1, keepdims=True)
    i2 = jnp.min(jnp.where(el2 == v2, lane, float(LANES)), axis=1, keepdims=True)
    t = jnp.exp(v2 - v1)
    gate1 = p_grp / (1.0 + t)
    gate2 = p_grp * t / (1.0 + t)
    e1 = i1 - N_EXPERT_GROUPS
    e2 = i2 - N_EXPERT_GROUPS
    hit1 = lane == e1
    hit2 = lane == e2
    onehot = jnp.where(hit1 | hit2, 1.0, 0.0)
    if exact_rank:
        before = jnp.dot(ltri, onehot, preferred_element_type=F32, precision=HIGHEST)
    else:
        before = jnp.dot(ltri, onehot.astype(BF16), preferred_element_type=F32)
    rank = before + prior_counts
    pos1 = jnp.sum(jnp.where(hit1, rank, 0.0), axis=1, keepdims=True)
    pos2 = jnp.sum(jnp.where(hit2, rank, 0.0), axis=1, keepdims=True)
    slab = jnp.where(lane == 0, e1, jnp.where(lane == 1, e2, jnp.where(lane == 2, pos1, jnp.where(
        lane == 3, pos2, jnp.where(lane == 4, gate1, jnp.where(lane == 5, gate2, 0.0))))))
    return slab, prior_counts + jnp.sum(onehot, axis=0, keepdims=True)


def _inproj_kernel(x_ref, g1_ref, w_ref, b_ref, lng_ref, lnb_ref, cos_ref, sin_ref,
                   uv_ref, qkv0_ref, qkv1_ref, qkv2_ref, gates_ref, kv0_ref, kv1_ref, kv2_ref):
    hb = _rms(x_ref[...], g1_ref[...]).astype(BF16)

    def seg(lo, width):
        return jnp.dot(hb, w_ref[:, lo:lo + width], preferred_element_type=F32) + b_ref[:, lo:lo + width]

    uv_ref[:, :A_WIDTH] = jax.nn.gelu(seg(COL_U, A_WIDTH)).astype(BF16)
    uv_ref[:, A_WIDTH:] = _layer_norm(jax.nn.gelu(seg(COL_V, A_WIDTH)), lng_ref[...], lnb_ref[...]).astype(BF16)
    gates_ref[:, :D_MODEL] = jax.nn.sigmoid(seg(COL_GA, D_MODEL)).astype(BF16)
    gates_ref[:, D_MODEL:] = jax.nn.sigmoid(seg(COL_GB, D_MODEL)).astype(BF16)

    cos = cos_ref[...]
    sin = sin_ref[...]
    qkv_refs = (qkv0_ref, qkv1_ref, qkv2_ref)
    kv_refs = (kv0_ref, kv1_ref, kv2_ref)
    for g in range(3):
        q = seg(COL_Q + g * GROUP_W, GROUP_W)
        k = seg(COL_K + g * GROUP_W, GROUP_W)
        v = seg(COL_VB + g * GROUP_W, GROUP_W)
        kv_rows = kv_refs[g].shape[1]
        for c in range(GROUP_W // LANES):
            sl = slice(c * LANES, (c + 1) * LANES)
            qr = _rope_chunk(q[:, sl], cos, sin) * (HEAD_DIM ** -0.5)
            kr = _rope_chunk(k[:, sl], cos, sin)
            qkv_refs[g][:, c * LANES:(c + 1) * LANES] = qr.astype(BF16)
            qkv_refs[g][:, GROUP_W + c * LANES:GROUP_W + (c + 1) * LANES] = kr.astype(BF16)
            kv_refs[g][0, :, c * LANES:(c + 1) * LANES] = kr[TM - kv_rows:, :]
        qkv_refs[g][:, 2 * GROUP_W:] = v.astype(BF16)
        kv_refs[g][0, :, GROUP_W:] = v[TM - kv_rows:, :]


def _inproj(x2, g1, w_bf, b_in, lng, lnb, cos_t, sin_t):
    tiles_per_seq = SEQ // TM
    n_tiles = T_PROMPT // TM
    const = lambda i: (0, 0)
    windows = tuple(min(SPAN * d, SEQ) for d in DILATIONS)

    def kv_spec(w):
        rows = min(w, TM)
        first = tiles_per_seq - w // rows
        return pl.BlockSpec((1, rows, 2 * GROUP_W),
                            lambda i: (i // tiles_per_seq, jnp.maximum(i % tiles_per_seq - first, 0), 0))

    return pl.pallas_call(
        _inproj_kernel,
        grid=(n_tiles,),
        in_specs=[
            pl.BlockSpec((TM, D_MODEL), lambda i: (i, 0)),
            pl.BlockSpec((1, D_MODEL), const),
            pl.BlockSpec((D_MODEL, IN_WIDTH), const),
            pl.BlockSpec((1, IN_WIDTH), const),
            pl.BlockSpec((1, A_WIDTH), const),
            pl.BlockSpec((1, A_WIDTH), const),
            pl.BlockSpec((TM, LANES), lambda i: (i % tiles_per_seq, 0)),
            pl.BlockSpec((TM, LANES), lambda i: (i % tiles_per_seq, 0)),
        ],
        out_specs=[
            pl.BlockSpec((TM, 2 * A_WIDTH), lambda i: (i, 0)),
            pl.BlockSpec((TM, 3 * GROUP_W), lambda i: (i, 0)),
            pl.BlockSpec((TM, 3 * GROUP_W), lambda i: (i, 0)),
            pl.BlockSpec((TM, 3 * GROUP_W), lambda i: (i, 0)),
            pl.BlockSpec((TM, 2 * D_MODEL), lambda i: (i, 0)),
            kv_spec(windows[0]), kv_spec(windows[1]), kv_spec(windows[2]),
        ],
        out_shape=[
            jax.ShapeDtypeStruct((T_PROMPT, 2 * A_WIDTH), BF16),
            jax.ShapeDtypeStruct((T_PROMPT, 3 * GROUP_W), BF16),
            jax.ShapeDtypeStruct((T_PROMPT, 3 * GROUP_W), BF16),
            jax.ShapeDtypeStruct((T_PROMPT, 3 * GROUP_W), BF16),
            jax.ShapeDtypeStruct((T_PROMPT, 2 * D_MODEL), BF16),
            jax.ShapeDtypeStruct((BATCH, windows[0], 2 * GROUP_W), F32),
            jax.ShapeDtypeStruct((BATCH, windows[1], 2 * GROUP_W), F32),
            jax.ShapeDtypeStruct((BATCH, windows[2], 2 * GROUP_W), F32),
        ],
        compiler_params=_cparams("arbitrary"),
        name="inproj",
    )(x2, g1, w_bf, b_in, lng, lnb, cos_t, sin_t)


def _attn_kernel(q_ref, kc_ref, kp_ref, vc_ref, vp_ref, o_ref, lse_ref):
    n = pl.program_id(2)
    qi = lax.broadcasted_iota(jnp.int32, (SPAN, SPAN), 0)
    ki = lax.broadcasted_iota(jnp.int32, (SPAN, SPAN), 1)
    prev_band = ki >= qi
    cur_band = ki <= qi
    contract_last = (((1,), (1,)), ((), ()))
    for j in range(ATT_NQ):
        rows = slice(j * SPAN, (j + 1) * SPAN)
        q = q_ref[0, rows, :]
        kc = kc_ref[0, rows, :]
        vc = vc_ref[0, rows, :]
        if j == 0:
            kp, vp = kp_ref[0], vp_ref[0]
            prev_ok = prev_band & (n > 0)
        else:
            prev_rows = slice((j - 1) * SPAN, j * SPAN)
            kp, vp = kc_ref[0, prev_rows, :], vc_ref[0, prev_rows, :]
            prev_ok = prev_band
        outs, lses = [], []
        for h in range(HEADS):
            cols = slice(h * HEAD_DIM, (h + 1) * HEAD_DIM)
            sp = lax.dot_general(q[:, cols], kp[:, cols], contract_last, preferred_element_type=F32)
            sc = lax.dot_general(q[:, cols], kc[:, cols], contract_last, preferred_element_type=F32)
            sp = jnp.where(prev_ok, sp, NEG)
            sc = jnp.where(cur_band, sc, NEG)
            m = jnp.maximum(jnp.max(sp, axis=1, keepdims=True), jnp.max(sc, axis=1, keepdims=True))
            pp = jnp.exp(sp - m)
            pc = jnp.exp(sc - m)
            l = jnp.sum(pp, axis=1, keepdims=True) + jnp.sum(pc, axis=1, keepdims=True)
            acc = (jnp.dot(pp.astype(BF16), vp[:, cols], preferred_element_type=F32)
                   + jnp.dot(pc.astype(BF16), vc[:, cols], preferred_element_type=F32))
            outs.append(acc / l)
            lses.append(jnp.broadcast_to(m + jnp.log(l), (SPAN, HEAD_DIM)))
        o_ref[0, rows, :] = jnp.concatenate(outs, axis=1).astype(BF16)
        lse_ref[0, rows, :] = jnp.concatenate(lses, axis=1)


def _attention(qkv, d):
    m = SEQ // d
    nb = m // (SPAN * ATT_NQ)
    view = qkv.reshape(BATCH, m, d * 3 * GROUP_W)
    blk = (1, SPAN * ATT_NQ, GROUP_W)
    pblk = (1, SPAN, GROUP_W)
    prev_row = lambda n: jnp.maximum(n * ATT_NQ - 1, 0)
    o, lse = pl.pallas_call(
        _attn_kernel,
        grid=(BATCH, d, nb),
        in_specs=[
            pl.BlockSpec(blk, lambda b, r, n: (b, n, 3 * r)),
            pl.BlockSpec(blk, lambda b, r, n: (b, n, 3 * r + 1)),
            pl.BlockSpec(pblk, lambda b, r, n: (b, prev_row(n), 3 * r + 1)),
            pl.BlockSpec(blk, lambda b, r, n: (b, n, 3 * r + 2)),
            pl.BlockSpec(pblk, lambda b, r, n: (b, prev_row(n), 3 * r + 2)),
        ],
        out_specs=[
            pl.BlockSpec(blk, lambda b, r, n: (b, n, r)),
            pl.BlockSpec(blk, lambda b, r, n: (b, n, r)),
        ],
        out_shape=[
            jax.ShapeDtypeStruct((BATCH, m, d * GROUP_W), BF16),
            jax.ShapeDtypeStruct((BATCH, m, d * GROUP_W), F32),
        ],
        compiler_params=_cparams("arbitrary", "arbitrary", "arbitrary"),
        name=f"attn_d{d}",
    )(view, view, view, view, view)
    return o.reshape(T_PROMPT, GROUP_W), lse.reshape(T_PROMPT, GROUP_W)


def _merge_kernel(x_ref, uv_ref, gates_ref, o0_ref, o1_ref, o2_ref, l0_ref, l1_ref, l2_ref,
                  wsp_ref, bsp_ref, wa_ref, wb_ref, wo_ref, g2_ref, wrh_ref, wrl_ref, br_ref, ltri_ref,
                  x1_ref, h2p_ref, rt_ref, cnt_ref, run_ref):
    @pl.when(pl.program_id(0) == 0)
    def _():
        run_ref[...] = jnp.zeros_like(run_ref)

    lane = lax.broadcasted_iota(jnp.int32, (CHUNK, LANES), 1)
    left = lane < A_GROUP_DIM
    zero = jnp.zeros((CHUNK, LANES), BF16)
    sa_chunks = []
    for c in range(TM // CHUNK):
        rows = slice(c * CHUNK, (c + 1) * CHUNK)
        pairs = []
        for p in range(A_GROUPS // 2):
            vp = uv_ref[rows, A_WIDTH + p * LANES:A_WIDTH + (p + 1) * LANES]
            rhs = jnp.concatenate([jnp.where(left, vp, zero), jnp.where(left, zero, vp)], axis=0)
            pairs.append(jnp.dot(wsp_ref[p], rhs, preferred_element_type=F32))
        mixed = jnp.concatenate(pairs, axis=1) + bsp_ref[...]
        sa_chunks.append((uv_ref[rows, :A_WIDTH].astype(F32) * mixed).astype(BF16))
    s_a = jnp.concatenate(sa_chunks, axis=0)

    l0, l1, l2 = l0_ref[...], l1_ref[...], l2_ref[...]
    mx = jnp.maximum(jnp.maximum(l0, l1), l2)
    w0, w1, w2 = jnp.exp(l0 - mx), jnp.exp(l1 - mx), jnp.exp(l2 - mx)
    comb = (w0 * o0_ref[...].astype(F32) + w1 * o1_ref[...].astype(F32) + w2 * o2_ref[...].astype(F32)) / (w0 + w1 + w2)

    a = jnp.dot(s_a, wa_ref[...], preferred_element_type=F32)
    b = jnp.dot(comb.astype(BF16), wb_ref[...], preferred_element_type=F32)
    merged = gates_ref[:, :D_MODEL].astype(F32) * a + gates_ref[:, D_MODEL:].astype(F32) * b
    x1 = x_ref[...] + jnp.dot(merged.astype(BF16), wo_ref[...], preferred_element_type=F32)
    x1_ref[...] = x1

    h2 = _rms(x1, g2_ref[...])
    h2p_ref[...] = _pack_bf16_pair(h2)
    h_hi = h2.astype(BF16)
    h_lo = (h2 - h_hi.astype(F32)).astype(BF16)
    logits = (jnp.dot(h_hi, wrh_ref[...], preferred_element_type=F32)
              + jnp.dot(h_lo, wrh_ref[...], preferred_element_type=F32)
              + jnp.dot(h_hi, wrl_ref[...], preferred_element_type=F32)) + br_ref[...]
    slab, counts = _route(logits, run_ref[...], ltri_ref[...], exact_rank=False)
    rt_ref[...] = slab
    run_ref[...] = counts
    cnt_ref[...] = counts


def _merge(x2, uv, gates, o, lse, wsp, bsp, wa, wb, wo, g2, wrh, wrl, br, ltri):
    n_tiles = T_PROMPT // TM
    tile = lambda w: pl.BlockSpec((TM, w), lambda i: (i, 0))
    full = lambda a: pl.BlockSpec(a.shape, lambda i: (0,) * a.ndim)
    return pl.pallas_call(
        _merge_kernel,
        grid=(n_tiles,),
        in_specs=[tile(D_MODEL), tile(2 * A_WIDTH), tile(2 * D_MODEL),
                  tile(GROUP_W), tile(GROUP_W), tile(GROUP_W), tile(GROUP_W), tile(GROUP_W), tile(GROUP_W),
                  full(wsp), full(bsp), full(wa), full(wb), full(wo), full(g2), full(wrh), full(wrl), full(br),
                  full(ltri)],
        out_specs=[tile(D_MODEL), tile(HALF_D), tile(LANES), pl.BlockSpec((1, LANES), lambda i: (0, 0))],
        out_shape=[
            jax.ShapeDtypeStruct((T_PROMPT, D_MODEL), F32),
            jax.ShapeDtypeStruct((T_PROMPT, HALF_D), U32),
            jax.ShapeDtypeStruct((T_PROMPT, LANES), F32),
            jax.ShapeDtypeStruct((1, LANES), F32),
        ],
        scratch_shapes=[pltpu.VMEM((1, LANES), F32)],
        compiler_params=_cparams("arbitrary"),
        name="merge_route",
    )(x2, uv, gates, o[0], o[1], o[2], lse[0], lse[1], lse[2], wsp, bsp, wa, wb, wo, g2, wrh, wrl, br, ltri)


SAMPLE_COLS = 768


def _sample_inproj_kernel(x_ref, g1_ref, w_ref, b_ref, z_ref):
    h = _rms(x_ref[...], g1_ref[...])
    z_ref[...] = jnp.dot(h, w_ref[...], preferred_element_type=F32, precision=HIGHEST) + b_ref[...]


def _sample_inproj(xs, g1, w_in, b_in):
    const = lambda j: (0, 0)
    return pl.pallas_call(
        _sample_inproj_kernel,
        grid=(IN_WIDTH // SAMPLE_COLS,),
        in_specs=[pl.BlockSpec((DEC_BATCH, D_MODEL), const), pl.BlockSpec((1, D_MODEL), const),
                  pl.BlockSpec((D_MODEL, SAMPLE_COLS), lambda j: (0, j)),
                  pl.BlockSpec((1, SAMPLE_COLS), lambda j: (0, j))],
        out_specs=pl.BlockSpec((DEC_BATCH, SAMPLE_COLS), lambda j: (0, j)),
        out_shape=jax.ShapeDtypeStruct((DEC_BATCH, IN_WIDTH), F32),
        compiler_params=_cparams("arbitrary"),
        name="sample_inproj",
    )(xs, g1, w_in, b_in)


def _sample_pre_kernel(z_ref, lng_ref, lnb_ref, ws0_ref, bs0_ref, cos_ref, sin_ref,
                       vrow_ref, sa_ref, qr_ref, kr_ref, kv0_ref, kv1_ref, kv2_ref):
    u = jax.nn.gelu(z_ref[:, COL_U:COL_U + A_WIDTH])
    va = _layer_norm(jax.nn.gelu(z_ref[:, COL_V:COL_V + A_WIDTH]), lng_ref[...], lnb_ref[...])
    vrow_ref[...] = va
    sa_ref[...] = u * (ws0_ref[...] * va + bs0_ref[...])
    cos, sin = cos_ref[...], sin_ref[...]
    for g, kv_ref in enumerate((kv0_ref, kv1_ref, kv2_ref)):
        for c in range(GROUP_W // LANES):
            off = g * GROUP_W + c * LANES
            qr_ref[:, off:off + LANES] = (_rope_chunk(z_ref[:, COL_Q + off:COL_Q + off + LANES], cos, sin)
                                          * (HEAD_DIM ** -0.5))
            kr = _rope_chunk(z_ref[:, COL_K + off:COL_K + off + LANES], cos, sin)
            kr_ref[:, off:off + LANES] = kr
            kv_ref[:, c * LANES:(c + 1) * LANES] = kr
        kv_ref[:, GROUP_W:] = z_ref[:, COL_VB + g * GROUP_W:COL_VB + (g + 1) * GROUP_W]


def _sample_pre(z, lng, lnb, ws0, bs0, cos_s, sin_s):
    row = lambda w: jax.ShapeDtypeStruct((DEC_BATCH, w), F32)
    return pl.pallas_call(
        _sample_pre_kernel,
        out_shape=[row(A_WIDTH), row(A_WIDTH), row(QK_W), row(QK_W),
                   row(2 * GROUP_W), row(2 * GROUP_W), row(2 * GROUP_W)],
        compiler_params=pltpu.CompilerParams(vmem_limit_bytes=VMEM_LIMIT),
        name="sample_pre",
    )(z, lng, lnb, ws0, bs0, cos_s, sin_s)


SAMPLE_BB = 4


def _sample_attn_kernel(q_ref, k_ref, v_ref, c0_ref, c1_ref, c2_ref, comb_ref):
    for bi in range(SAMPLE_BB):
        outs, lses = [], []
        for g, c_ref in enumerate((c0_ref, c1_ref, c2_ref)):
            qb = q_ref[bi, g]
            kc = c_ref[bi, :, 0]
            vc = c_ref[bi, :, 1]
            s_c = jnp.sum(kc * qb[None], axis=-1, keepdims=True)
            s_n = jnp.sum(qb * k_ref[bi, g], axis=-1, keepdims=True)
            m = jnp.maximum(jnp.max(s_c, axis=0), s_n)
            p_c = jnp.exp(s_c - m[None])
            p_n = jnp.exp(s_n - m)
            l = jnp.sum(p_c, axis=0) + p_n
            acc = jnp.sum(p_c * vc, axis=0) + p_n * v_ref[bi, g]
            outs.append(acc / l)
            lses.append(m + jnp.log(l))
        mx = jnp.maximum(jnp.maximum(lses[0], lses[1]), lses[2])
        ws = [jnp.exp(t - mx) for t in lses]
        tot = ws[0] + ws[1] + ws[2]
        comb_ref[bi] = (ws[0] * outs[0] + ws[1] * outs[1] + ws[2] * outs[2]) / tot


def _sample_attn(q4, k4, v4, caches):
    bb = SAMPLE_BB
    new_spec = pl.BlockSpec((bb, 3, HEADS, HEAD_DIM), lambda i: (i, 0, 0, 0))
    cache_spec = pl.BlockSpec((bb, SPAN, None, 2, HEADS, HEAD_DIM), lambda i: (i, 0, 0, 0, 0, 0))
    return pl.pallas_call(
        _sample_attn_kernel,
        grid=(DEC_BATCH // bb,),
        in_specs=[new_spec, new_spec, new_spec, cache_spec, cache_spec, cache_spec],
        out_specs=pl.BlockSpec((bb, HEADS, HEAD_DIM), lambda i: (i, 0, 0)),
        out_shape=jax.ShapeDtypeStruct((DEC_BATCH, HEADS, HEAD_DIM), F32),
        compiler_params=_cparams("arbitrary"),
        name="sample_attn",
    )(q4, k4, v4, caches[0], caches[1], caches[2])


def _sample_merge_kernel(x_ref, z_ref, sa_ref, comb_ref, wa_ref, wb_ref, wo_ref, g2_ref, wr_ref, br_ref,
                         ltri_ref, cnt_in_ref, x1_ref, h2p_ref, rt_ref, cnt_ref):
    dot = lambda p, q: jnp.dot(p, q, preferred_element_type=F32, precision=HIGHEST)
    a = dot(sa_ref[...], wa_ref[...])
    b = dot(comb_ref[...], wb_ref[...])
    merged = (jax.nn.sigmoid(z_ref[:, COL_GA:COL_GA + D_MODEL]) * a
              + jax.nn.sigmoid(z_ref[:, COL_GB:COL_GB + D_MODEL]) * b)
    x1 = x_ref[...] + dot(merged, wo_ref[...])
    x1_ref[...] = x1
    h2 = _rms(x1, g2_ref[...])
    h2p_ref[...] = _pack_bf16_pair(h2)
    logits = dot(h2, wr_ref[...]) + br_ref[...]
    slab, counts = _route(logits, cnt_in_ref[...], ltri_ref[...], exact_rank=True)
    rt_ref[...] = slab
    cnt_ref[...] = counts


def _sample_merge(xs, z, s_a, comb, wa, wb, wo, g2, wr, br, ltri, counts):
    return pl.pallas_call(
        _sample_merge_kernel,
        out_shape=[jax.ShapeDtypeStruct((DEC_BATCH, D_MODEL), F32),
                   jax.ShapeDtypeStruct((DEC_BATCH, HALF_D), U32),
                   jax.ShapeDtypeStruct((DEC_BATCH, LANES), F32),
                   jax.ShapeDtypeStruct((1, LANES), F32)],
        compiler_params=pltpu.CompilerParams(vmem_limit_bytes=VMEM_LIMIT),
        name="sample_merge_route",
    )(xs, z, s_a, comb, wa, wb, wo, g2, wr, br, ltri, counts)


def _dispatch_kernel(dest_ref, h_ref, xs_in_ref, xs_ref, sem):
    del xs_in_ref
    rows = h_ref.shape[0]
    base = pl.program_id(0) * (2 * rows)

    def row_copy(t, slot):
        d = dest_ref[base + 2 * t + slot]
        return pltpu.make_async_copy(h_ref.at[pl.ds(t, 1)], xs_ref.at[pl.ds(d, 1)], sem)

    def start(t, c):
        row_copy(t, 0).start()
        row_copy(t, 1).start()
        return c

    def wait(t, c):
        row_copy(t, 0).wait()
        row_copy(t, 1).wait()
        return c

    lax.fori_loop(0, rows, start, 0)
    lax.fori_loop(0, rows, wait, 0)


def _dispatch(dest, h2p, xs, rows):
    n = h2p.shape[0]
    return pl.pallas_call(
        _dispatch_kernel,
        grid_spec=pltpu.PrefetchScalarGridSpec(
            num_scalar_prefetch=1,
            grid=(n // rows,),
            in_specs=[pl.BlockSpec((rows, HALF_D), lambda i, dest: (i, 0)), pl.BlockSpec(memory_space=pl.ANY)],
            out_specs=pl.BlockSpec(memory_space=pl.ANY),
            scratch_shapes=[pltpu.SemaphoreType.DMA(())],
        ),
        out_shape=jax.ShapeDtypeStruct(xs.shape, xs.dtype),
        input_output_aliases={2: 0},
        compiler_params=_cparams("arbitrary"),
        name=f"moe_dispatch_{rows}",
    )(dest, h2p, xs)


def _ffn_kernel(be_ref, nused_ref, xs_ref, wg_ref, wu_ref, wd_ref, ys_ref, wg_b, wu_b, wd_b):
    i = pl.program_id(0)
    new_expert = jnp.logical_or(i == 0, be_ref[i] != be_ref[jnp.maximum(i - 1, 0)])

    @pl.when(new_expert)
    def _():
        wg_b[...] = wg_ref[0].astype(BF16)
        wu_b[...] = wu_ref[0].astype(BF16)
        wd_b[...] = wd_ref[0].astype(BF16)

    @pl.when(i < nused_ref[0])
    def _():
        lo, hi = _unpack_bf16_pair(xs_ref[...])
        gate = (jnp.dot(lo, wg_b[:HALF_D, :], preferred_element_type=F32)
                + jnp.dot(hi, wg_b[HALF_D:, :], preferred_element_type=F32))
        up = (jnp.dot(lo, wu_b[:HALF_D, :], preferred_element_type=F32)
              + jnp.dot(hi, wu_b[HALF_D:, :], preferred_element_type=F32))
        hid = (jax.nn.silu(gate) * up).astype(BF16)
        ys_ref[...] = jnp.dot(hid, wd_b[...], preferred_element_type=F32)

    @pl.when(i >= nused_ref[0])
    def _():
        ys_ref[...] = jnp.zeros_like(ys_ref)


def _expert_ffn(block_e, nused, xs, w_gate, w_up, w_down):
    return pl.pallas_call(
        _ffn_kernel,
        grid_spec=pltpu.PrefetchScalarGridSpec(
            num_scalar_prefetch=2,
            grid=(MOE_NB,),
            in_specs=[pl.BlockSpec((MOE_TB, HALF_D), lambda i, be, nu: (i, 0)),
                      pl.BlockSpec((1, D_MODEL, D_EXPERT), lambda i, be, nu: (be[i], 0, 0)),
                      pl.BlockSpec((1, D_MODEL, D_EXPERT), lambda i, be, nu: (be[i], 0, 0)),
                      pl.BlockSpec((1, D_EXPERT, D_MODEL), lambda i, be, nu: (be[i], 0, 0))],
            out_specs=pl.BlockSpec((MOE_TB, D_MODEL), lambda i, be, nu: (i, 0)),
            scratch_shapes=[pltpu.VMEM((D_MODEL, D_EXPERT), BF16), pltpu.VMEM((D_MODEL, D_EXPERT), BF16),
                            pltpu.VMEM((D_EXPERT, D_MODEL), BF16)],
        ),
        out_shape=jax.ShapeDtypeStruct((MOE_ROWS, D_MODEL), F32),
        compiler_params=_cparams("arbitrary"),
        name="moe_ffn",
    )(block_e, nused, xs, w_gate, w_up, w_down)


def _combine_kernel(dest_ref, x1_ref, rt_ref, gf_ref, ys_ref, y_ref, ya, yb, sem):
    rows = x1_ref.shape[0]
    base = pl.program_id(0) * (2 * rows)

    def row_copy(t, slot, buf):
        d = dest_ref[base + 2 * t + slot]
        return pltpu.make_async_copy(ys_ref.at[pl.ds(d, 1)], buf.at[pl.ds(t, 1)], sem)

    def start(t, c):
        row_copy(t, 0, ya).start()
        row_copy(t, 1, yb).start()
        return c

    def wait(t, c):
        row_copy(t, 0, ya).wait()
        row_copy(t, 1, yb).wait()
        return c

    lax.fori_loop(0, rows, start, 0)
    lax.fori_loop(0, rows, wait, 0)
    x2 = x1_ref[...] + rt_ref[:, 4:5] * ya[...] + rt_ref[:, 5:6] * yb[...]
    y_ref[...] = _rms(x2, gf_ref[...])


def _combine(dest, x1, rt, gf, ys, rows):
    n = x1.shape[0]
    return pl.pallas_call(
        _combine_kernel,
        grid_spec=pltpu.PrefetchScalarGridSpec(
            num_scalar_prefetch=1,
            grid=(n // rows,),
            in_specs=[pl.BlockSpec((rows, D_MODEL), lambda i, dest: (i, 0)),
                      pl.BlockSpec((rows, LANES), lambda i, dest: (i, 0)),
                      pl.BlockSpec((1, D_MODEL), lambda i, dest: (0, 0)),
                      pl.BlockSpec(memory_space=pl.ANY)],
            out_specs=pl.BlockSpec((rows, D_MODEL), lambda i, dest: (i, 0)),
            scratch_shapes=[pltpu.VMEM((rows, D_MODEL), F32), pltpu.VMEM((rows, D_MODEL), F32),
                            pltpu.SemaphoreType.DMA(())],
        ),
        out_shape=jax.ShapeDtypeStruct((n, D_MODEL), F32),
        compiler_params=_cparams("arbitrary"),
        name=f"moe_combine_{rows}",
    )(dest, x1, rt, gf, ys)


def _rope_tables(pos):
    half = HEAD_DIM // 2
    inv = 1.0 / (ROPE_THETA ** (jnp.arange(half, dtype=F32) * (2.0 / HEAD_DIM)))
    ang = pos.astype(F32)[:, None] * inv[None, :]
    cos, sin = jnp.cos(ang), jnp.sin(ang)
    return jnp.concatenate([cos, cos, cos, cos], axis=1), jnp.concatenate([-sin, sin, -sin, sin], axis=1)


def _slot_dest(rt, pstarts):
    e = rt[:, 0:2].astype(jnp.int32)
    pos = rt[:, 2:4].astype(jnp.int32)
    onehot = e[:, :, None] == jnp.arange(N_EXPERTS, dtype=jnp.int32)[None, None, :]
    return (jnp.sum(jnp.where(onehot, pstarts[None, None, :], 0), axis=-1) + pos).reshape(-1)


def kernel(x_prompt, x_sample, cache_kv_g0, cache_kv_g1, cache_kv_g2, norm1_g, w_in, b_in, a_ln_g, a_ln_b, w_spatial, b_spatial, w_a_proj, w_b_proj, w_o, norm2_g, w_group_router, b_group_router, w_expert_router, b_expert_router, w_gate, w_up, w_down, final_norm_g):
    x2 = x_prompt.reshape(T_PROMPT, D_MODEL)
    xs = x_sample.reshape(DEC_BATCH, D_MODEL)
    g1 = norm1_g[0][None, :]
    g2 = norm2_g[0][None, :]
    gf = final_norm_g[None, :]
    b_in2 = b_in[0][None, :]
    lng, lnb = a_ln_g[0][None, :], a_ln_b[0][None, :]

    causal = jnp.tril(jnp.ones((CHUNK, CHUNK), dtype=bool))
    ws_tril = jnp.where(causal[None], w_spatial[0], 0.0)
    wsp = jnp.concatenate([ws_tril[0::2], ws_tril[1::2]], axis=2).astype(BF16)
    bsp = jnp.repeat(b_spatial[0].T, A_GROUP_DIM, axis=1)
    ws0 = jnp.repeat(ws_tril[:, 0, 0], A_GROUP_DIM)[None, :]
    bs0 = jnp.repeat(b_spatial[0][:, 0], A_GROUP_DIM)[None, :]
    w_router = jnp.zeros((D_MODEL, LANES), F32)
    w_router = w_router.at[:, :N_EXPERT_GROUPS].set(w_group_router[0])
    w_router = w_router.at[:, N_EXPERT_GROUPS:N_EXPERT_GROUPS + N_EXPERTS].set(w_expert_router[0])
    b_router = jnp.zeros((1, LANES), F32)
    b_router = b_router.at[0, :N_EXPERT_GROUPS].set(b_group_router[0])
    b_router = b_router.at[0, N_EXPERT_GROUPS:N_EXPERT_GROUPS + N_EXPERTS].set(b_expert_router[0])
    wr_hi = w_router.astype(BF16)
    wr_lo = (w_router - wr_hi.astype(F32)).astype(BF16)
    ltri = jnp.tril(jnp.ones((TM, TM), F32), -1)

    cos_p, sin_p = _rope_tables(jnp.arange(SEQ, dtype=jnp.int32))
    cos_s, sin_s = _rope_tables(jnp.full((1,), PAST_LEN, jnp.int32))

    uv, qkv0, qkv1, qkv2, gates, kvp0, kvp1, kvp2 = _inproj(
        x2, g1, w_in[0].astype(BF16), b_in2, lng, lnb, cos_p, sin_p)
    att = [_attention(qkv, d) for qkv, d in zip((qkv0, qkv1, qkv2), DILATIONS)]
    x1_p, h2p_p, rt_p, counts_p = _merge(
        x2, uv, gates, [a[0] for a in att], [a[1] for a in att], wsp, bsp,
        w_a_proj[0].astype(BF16), w_b_proj[0].astype(BF16), w_o[0].astype(BF16), g2, wr_hi, wr_lo, b_router,
        ltri.astype(BF16))

    z_s = _sample_inproj(xs, g1, w_in[0], b_in2)
    vrow, sa_s, qr_s, kr_s, kvs0, kvs1, kvs2 = _sample_pre(z_s, lng, lnb, ws0, bs0, cos_s, sin_s)
    per_head = lambda t: t.reshape(DEC_BATCH, 3, HEADS, HEAD_DIM)
    caches = [c.reshape(DEC_BATCH, SPAN, d, 2, HEADS, HEAD_DIM)
              for c, d in zip((cache_kv_g0, cache_kv_g1, cache_kv_g2), DILATIONS)]
    comb_s = _sample_attn(per_head(qr_s), per_head(kr_s), per_head(z_s[:, COL_VB:COL_VB + QK_W]), caches)
    comb_s = comb_s.reshape(DEC_BATCH, GROUP_W)
    x1_s, h2p_s, rt_s, counts = _sample_merge(
        xs, z_s, sa_s, comb_s, w_a_proj[0], w_b_proj[0], w_o[0], g2, w_router, b_router,
        ltri[:DEC_BATCH, :DEC_BATCH], counts_p)

    cnt = counts[0, :N_EXPERTS].astype(jnp.int32)
    padded = (cnt + MOE_TB - 1) // MOE_TB * MOE_TB
    pends = jnp.cumsum(padded)
    pstarts = pends - padded
    block_starts = jnp.arange(MOE_NB, dtype=jnp.int32) * MOE_TB
    block_e = jnp.minimum(jnp.sum((pends[None, :] <= block_starts[:, None]).astype(jnp.int32), axis=1),
                          N_EXPERTS - 1)
    nused = (pends[-1:] // MOE_TB).astype(jnp.int32)
    dest_p = _slot_dest(rt_p, pstarts)
    dest_s = _slot_dest(rt_s, pstarts)

    rows = jnp.zeros((MOE_ROWS, HALF_D), U32)
    rows = _dispatch(dest_p, h2p_p, rows, TM)
    rows = _dispatch(dest_s, h2p_s, rows, DEC_BATCH)
    ys = _expert_ffn(block_e, nused, rows, w_gate[0], w_up[0], w_down[0])
    y_p = _combine(dest_p, x1_p, rt_p, gf, ys, TM)
    y_s = _combine(dest_s, x1_s, rt_s, gf, ys, DEC_BATCH)

    kv_shape = lambda n, w: (1, n, w, 2, HEADS, HEAD_DIM)
    return (y_p.reshape(BATCH, SEQ, D_MODEL), y_s.reshape(DEC_BATCH, 1, D_MODEL),
            kvp0.reshape(kv_shape(BATCH, kvp0.shape[1])), kvp1.reshape(kv_shape(BATCH, kvp1.shape[1])),
            kvp2.reshape(kv_shape(BATCH, kvp2.shape[1])),
            kvs0.reshape(kv_shape(DEC_BATCH, 1)), kvs1.reshape(kv_shape(DEC_BATCH, 1)),
            kvs2.reshape(kv_shape(DEC_BATCH, 1)), vrow.reshape(1, DEC_BATCH, 1, A_WIDTH))
```

```python
import functools

import jax
import jax.numpy as jnp
from jax import lax
from jax.experimental import pallas as pl
from jax.experimental.pallas import tpu as pltpu

F32 = jnp.float32
BF16 = jnp.bfloat16
U32 = jnp.uint32

D_MODEL = 1024
BATCH = 2
SEQ = 8192
DEC_BATCH = 32
PAST_LEN = 8192
CHUNK = 128
A_GROUPS = 8
A_GROUP_DIM = 64
A_WIDTH = 512
HEAD_DIM = 64
HEADS = 4
GROUP_W = HEADS * HEAD_DIM
DILATIONS = (1, 4, 16)
SPAN = 128
QK_W = 768
IN_WIDTH = 5376
COL_U, COL_V, COL_Q, COL_K, COL_VB, COL_GA, COL_GB = 0, 512, 1024, 1792, 2560, 3328, 4352
N_EXPERT_GROUPS = 4
EXPERTS_PER_GROUP = 8
N_EXPERTS = 32
D_EXPERT = 512
ROPE_THETA = 10000.0
EPS = 1e-6

LANES = 128
T_PROMPT = BATCH * SEQ
TM = 512
TILES_PER_SEQ = SEQ // TM
WINDOWS = tuple(min(SPAN * d, SEQ) for d in DILATIONS)
TA = SPAN * max(DILATIONS)
MOE_TB = 256
N_SLOTS = 2 * (T_PROMPT + DEC_BATCH)
MOE_NB = -(-N_SLOTS // MOE_TB) + N_EXPERTS
MOE_ROWS = MOE_NB * MOE_TB
HALF_D = D_MODEL // 2
NEG = -1e30
VMEM_LIMIT = 56 * 1024 * 1024


def _cparams(*sem):
    return pltpu.CompilerParams(dimension_semantics=sem, vmem_limit_bytes=VMEM_LIMIT)


def _rms(x, g):
    return x * lax.rsqrt(jnp.mean(x * x, axis=-1, keepdims=True) + EPS) * g


def _layer_norm(x, g, b):
    mu = jnp.mean(x, axis=-1, keepdims=True)
    xc = x - mu
    var = jnp.mean(xc * xc, axis=-1, keepdims=True)
    return xc * lax.rsqrt(var + EPS) * g + b


def _rope_chunk(x, cos, sin_signed):
    lane = lax.broadcasted_iota(jnp.int32, x.shape, 1)
    first_half = (lane % HEAD_DIM) < (HEAD_DIM // 2)
    swapped = jnp.where(first_half, pltpu.roll(x, LANES - HEAD_DIM // 2, 1), pltpu.roll(x, HEAD_DIM // 2, 1))
    return x * cos + swapped * sin_signed


def _pack_bf16_pair(h):
    lo = lax.bitcast_convert_type(h[:, :HALF_D].astype(BF16).astype(F32), U32)
    hi = lax.bitcast_convert_type(h[:, HALF_D:].astype(BF16).astype(F32), U32)
    return (hi & jnp.uint32(0xFFFF0000)) | (lo >> 16)


def _unpack_bf16_pair(p):
    lo = lax.bitcast_convert_type(p << 16, F32).astype(BF16)
    hi = lax.bitcast_convert_type(p & jnp.uint32(0xFFFF0000), F32).astype(BF16)
    return lo, hi


def _route(logits, prior_counts, ltri):
    rows = logits.shape[0]
    lane = lax.broadcasted_iota(jnp.int32, (rows, LANES), 1).astype(F32)
    is_g = lane < N_EXPERT_GROUPS
    gl = jnp.where(is_g, logits, NEG)
    gmax = jnp.max(gl, axis=1, keepdims=True)
    grp = jnp.min(jnp.where(gl == gmax, lane, float(LANES)), axis=1, keepdims=True)
    p_grp = 1.0 / jnp.sum(jnp.where(is_g, jnp.exp(gl - gmax), 0.0), axis=1, keepdims=True)
    lo = N_EXPERT_GROUPS + grp * EXPERTS_PER_GROUP
    el = jnp.where((lane >= lo) & (lane < lo + EXPERTS_PER_GROUP), logits, NEG)
    v1 = jnp.max(el, axis=1, keepdims=True)
    i1 = jnp.min(jnp.where(el == v1, lane, float(LANES)), axis=1, keepdims=True)
    el2 = jnp.where(lane == i1, NEG, el)
    v2 = jnp.max(el2, axis=1, keepdims=True)
    i2 = jnp.min(jnp.where(el2 == v2, lane, float(LANES)), axis=1, keepdims=True)
    t = jnp.exp(v2 - v1)
    gate1 = p_grp / (1.0 + t)
    gate2 = p_grp * t / (1.0 + t)
    e1 = i1 - N_EXPERT_GROUPS
    e2 = i2 - N_EXPERT_GROUPS
    hit1 = lane == e1
    hit2 = lane == e2
    onehot = jnp.where(hit1 | hit2, 1.0, 0.0)
    rank = jnp.dot(ltri, onehot.astype(BF16), preferred_element_type=F32) + prior_counts
    pos1 = jnp.sum(jnp.where(hit1, rank, 0.0), axis=1, keepdims=True)
    pos2 = jnp.sum(jnp.where(hit2, rank, 0.0), axis=1, keepdims=True)
    slab = jnp.where(lane == 0, e1, jnp.where(lane == 1, e2, jnp.where(lane == 2, pos1, jnp.where(
        lane == 3, pos2, jnp.where(lane == 4, gate1, jnp.where(lane == 5, gate2, 0.0))))))
    return slab, prior_counts + jnp.sum(onehot, axis=0, keepdims=True)


def _inproj_kernel(x_ref, g1_ref, w_ref, b_ref, lng_ref, lnb_ref, cos_ref, sin_ref,
                   uv_ref, qkv0_ref, qkv1_ref, qkv2_ref, gates_ref, kv0_ref, kv1_ref, kv2_ref, de_ref):
    tile_in_seq = pl.program_id(0) % TILES_PER_SEQ
    hb = _rms(x_ref[...], g1_ref[...]).astype(BF16)

    def seg(lo, width):
        return jnp.dot(hb, w_ref[:, lo:lo + width], preferred_element_type=F32) + b_ref[:, lo:lo + width]

    uv_ref[:, :A_WIDTH] = jax.nn.gelu(seg(COL_U, A_WIDTH)).astype(BF16)
    uv_ref[:, A_WIDTH:] = _layer_norm(jax.nn.gelu(seg(COL_V, A_WIDTH)), lng_ref[...], lnb_ref[...]).astype(BF16)
    gates_ref[:, :D_MODEL] = jax.nn.sigmoid(seg(COL_GA, D_MODEL)).astype(BF16)
    gates_ref[:, D_MODEL:] = jax.nn.sigmoid(seg(COL_GB, D_MODEL)).astype(BF16)

    cos = cos_ref[...]
    sin = sin_ref[...]
    qkv_refs = (qkv0_ref, qkv1_ref, qkv2_ref)
    kv_refs = (kv0_ref, kv1_ref, kv2_ref)
    for g, d in enumerate(DILATIONS):
        q = seg(COL_Q + g * GROUP_W, GROUP_W)
        k = seg(COL_K + g * GROUP_W, GROUP_W)
        v = seg(COL_VB + g * GROUP_W, GROUP_W)
        chunks = GROUP_W // LANES
        for c in range(chunks):
            sl = slice(c * LANES, (c + 1) * LANES)
            de_ref[c] = _rope_chunk(q[:, sl], cos, sin) * (HEAD_DIM ** -0.5)
            de_ref[chunks + c] = _rope_chunk(k[:, sl], cos, sin)
            de_ref[2 * chunks + c] = v[:, sl]
        for c in range(3 * chunks):
            sl = slice(c * LANES, (c + 1) * LANES)
            if d == 1:
                qkv_refs[g][:, sl] = de_ref[c].astype(BF16)
            else:
                for r in range(d):
                    qkv_refs[g][0, r, :, sl] = de_ref[c, pl.ds(r, TM // d, stride=d), :].astype(BF16)
        kv_rows = kv_refs[g].shape[2]
        first = TILES_PER_SEQ - WINDOWS[g] // kv_rows

        @pl.when(tile_in_seq >= first)
        def _(kv_ref=kv_refs[g], kv_rows=kv_rows, chunks=chunks):
            for c in range(2 * chunks):
                kv_ref[0, c * LANES:(c + 1) * LANES, :] = de_ref[chunks + c, TM - kv_rows:, :].T


def _inproj(x2, g1, w_bf, b_in, lng, lnb, cos_t, sin_t):
    tiles_per_seq = TILES_PER_SEQ
    n_tiles = T_PROMPT // TM
    const = lambda i: (0, 0)

    def kv_spec(w):
        rows = min(w, TM)
        first = tiles_per_seq - w // rows
        return pl.BlockSpec((1, 2 * GROUP_W, rows),
                            lambda i: (i // tiles_per_seq, 0, jnp.maximum(i % tiles_per_seq - first, 0)))

    def regrouped_spec(d):
        return pl.BlockSpec((1, d, TM // d, 3 * GROUP_W), lambda i: (i // tiles_per_seq, 0, i % tiles_per_seq, 0))

    return pl.pallas_call(
        _inproj_kernel,
        grid=(n_tiles,),
        in_specs=[
            pl.BlockSpec((TM, D_MODEL), lambda i: (i, 0)),
            pl.BlockSpec((1, D_MODEL), const),
            pl.BlockSpec((D_MODEL, IN_WIDTH), const),
            pl.BlockSpec((1, IN_WIDTH), const),
            pl.BlockSpec((1, A_WIDTH), const),
            pl.BlockSpec((1, A_WIDTH), const),
            pl.BlockSpec((TM, LANES), lambda i: (i % tiles_per_seq, 0)),
            pl.BlockSpec((TM, LANES), lambda i: (i % tiles_per_seq, 0)),
        ],
        out_specs=[
            pl.BlockSpec((TM, 2 * A_WIDTH), lambda i: (i, 0)),
            pl.BlockSpec((TM, 3 * GROUP_W), lambda i: (i, 0)),
            regrouped_spec(DILATIONS[1]),
            regrouped_spec(DILATIONS[2]),
            pl.BlockSpec((TM, 2 * D_MODEL), lambda i: (i, 0)),
            kv_spec(WINDOWS[0]), kv_spec(WINDOWS[1]), kv_spec(WINDOWS[2]),
        ],
        out_shape=[
            jax.ShapeDtypeStruct((T_PROMPT, 2 * A_WIDTH), BF16),
            jax.ShapeDtypeStruct((T_PROMPT, 3 * GROUP_W), BF16),
            jax.ShapeDtypeStruct((BATCH, DILATIONS[1], SEQ // DILATIONS[1], 3 * GROUP_W), BF16),
            jax.ShapeDtypeStruct((BATCH, DILATIONS[2], SEQ // DILATIONS[2], 3 * GROUP_W), BF16),
            jax.ShapeDtypeStruct((T_PROMPT, 2 * D_MODEL), BF16),
            jax.ShapeDtypeStruct((BATCH, 2 * GROUP_W, WINDOWS[0]), F32),
            jax.ShapeDtypeStruct((BATCH, 2 * GROUP_W, WINDOWS[1]), F32),
            jax.ShapeDtypeStruct((BATCH, 2 * GROUP_W, WINDOWS[2]), F32),
        ],
        scratch_shapes=[pltpu.VMEM((3 * GROUP_W // LANES, TM, LANES), F32)],
        compiler_params=_cparams("arbitrary"),
        name="inproj",
    )(x2, g1, w_bf, b_in, lng, lnb, cos_t, sin_t)


def _attn_kernel(c0_ref, p0_ref, c1_ref, p1_ref, c2_ref, p2_ref, o_ref, acc_ref, m_ref, l_ref):
    n = pl.program_id(1)
    acc_ref[...] = jnp.zeros_like(acc_ref)
    l_ref[...] = jnp.zeros_like(l_ref)
    m_ref[...] = jnp.full_like(m_ref, NEG)

    qi = lax.broadcasted_iota(jnp.int32, (SPAN, 2 * SPAN), 0)
    ki = lax.broadcasted_iota(jnp.int32, (SPAN, 2 * SPAN), 1)
    band = (ki >= qi) & (ki <= qi + SPAN)
    band_first = band & ((ki >= SPAN) | (n > 0))
    q_head = lax.broadcasted_iota(jnp.int32, (SPAN, GROUP_W), 1) // HEAD_DIM
    kv_head = lax.broadcasted_iota(jnp.int32, (2 * SPAN, GROUP_W), 1) // HEAD_DIM
    contract_last = (((1,), (1,)), ((), ()))
    k_cols = slice(GROUP_W, 2 * GROUP_W)
    v_cols = slice(2 * GROUP_W, 3 * GROUP_W)

    def attend(q, keys, vals, mask, tok_rows):
        scores = []
        for h in range(HEADS):
            qh = jnp.where(q_head == h, q, jnp.zeros_like(q))
            s = lax.dot_general(qh, keys, contract_last, preferred_element_type=F32)
            scores.append(jnp.where(mask, s, NEG))
        s = jnp.concatenate(scores, axis=0)
        m = jnp.max(s, axis=1, keepdims=True)
        p = jnp.exp(s - m)
        l = jnp.sum(p, axis=1, keepdims=True)
        pb = p.astype(BF16)
        acc = jnp.zeros((SPAN, GROUP_W), F32)
        m_b = jnp.zeros((SPAN, GROUP_W), F32)
        l_b = jnp.zeros((SPAN, GROUP_W), F32)
        for h in range(HEADS):
            rows = slice(h * SPAN, (h + 1) * SPAN)
            vh = jnp.where(kv_head == h, vals, jnp.zeros_like(vals))
            acc = acc + jnp.dot(pb[rows], vh, preferred_element_type=F32)
            m_b = jnp.where(q_head == h, m[rows], m_b)
            l_b = jnp.where(q_head == h, l[rows], l_b)
        for c in range(GROUP_W // LANES):
            sl = slice(c * LANES, (c + 1) * LANES)
            m_old = m_ref[c, tok_rows, :]
            m_new = jnp.maximum(m_old, m_b[:, sl])
            a_old = jnp.exp(m_old - m_new)
            a_blk = jnp.exp(m_b[:, sl] - m_new)
            l_ref[c, tok_rows, :] = a_old * l_ref[c, tok_rows, :] + a_blk * l_b[:, sl]
            acc_ref[c, tok_rows, :] = a_old * acc_ref[c, tok_rows, :] + a_blk * acc[:, sl]
            m_ref[c, tok_rows, :] = m_new

    def stream(c_ref, p_ref, r, d):
        n_blocks = c_ref.shape[2] // SPAN

        def tok_rows(j):
            start = j * (SPAN * d) + r
            return pl.ds(start, SPAN) if d == 1 else pl.ds(start, SPAN, stride=d)

        keys = jnp.concatenate([p_ref[0, r, :, k_cols], c_ref[0, r, :SPAN, k_cols]], axis=0)
        vals = jnp.concatenate([p_ref[0, r, :, v_cols], c_ref[0, r, :SPAN, v_cols]], axis=0)
        attend(c_ref[0, r, :SPAN, :GROUP_W], keys, vals, band_first, tok_rows(0))

        def later_block(j, carry):
            q_rows = pl.ds(pl.multiple_of(j * SPAN, SPAN), SPAN)
            kv_rows = pl.ds(pl.multiple_of((j - 1) * SPAN, SPAN), 2 * SPAN)
            attend(c_ref[0, r, q_rows, :GROUP_W], c_ref[0, r, kv_rows, k_cols], c_ref[0, r, kv_rows, v_cols],
                   band, tok_rows(j))
            return carry

        if n_blocks > 1:
            lax.fori_loop(1, n_blocks, later_block, 0)

    for (c_ref, p_ref), d in zip(((c0_ref, p0_ref), (c1_ref, p1_ref), (c2_ref, p2_ref)), DILATIONS):
        if d == 1:
            stream(c_ref, p_ref, 0, d)
        else:
            lax.fori_loop(0, d, lambda r, carry, c_ref=c_ref, p_ref=p_ref, d=d: (stream(c_ref, p_ref, r, d), carry)[1], 0)

    for c in range(GROUP_W // LANES):
        o_ref[0, :, c * LANES:(c + 1) * LANES] = (acc_ref[c] / l_ref[c]).astype(BF16)


def _attention(qkv_by_group):
    in_specs, args = [], []
    for qkv, d in zip(qkv_by_group, DILATIONS):
        rows = TA // d
        blocks_per_tile = rows // SPAN
        in_specs.append(pl.BlockSpec((1, d, rows, 3 * GROUP_W), lambda b, n: (b, 0, n, 0)))
        in_specs.append(pl.BlockSpec((1, d, SPAN, 3 * GROUP_W),
                                     lambda b, n, k=blocks_per_tile: (b, 0, jnp.maximum(n * k - 1, 0), 0)))
        args += [qkv, qkv]
    return pl.pallas_call(
        _attn_kernel,
        grid=(BATCH, SEQ // TA),
        in_specs=in_specs,
        out_specs=pl.BlockSpec((1, TA, GROUP_W), lambda b, n: (b, n, 0)),
        out_shape=jax.ShapeDtypeStruct((BATCH, SEQ, GROUP_W), BF16),
        scratch_shapes=[pltpu.VMEM((GROUP_W // LANES, TA, LANES), F32)] * 3,
        compiler_params=_cparams("arbitrary", "arbitrary"),
        name="attn",
    )(*args)


def _merge_kernel(x_ref, uv_ref, gates_ref, ob_ref,
                  wsp_ref, bsp_ref, wa_ref, wb_ref, wo_ref, g2_ref, wr_ref, br_ref, ltri_ref,
                  x1_ref, h2p_ref, rt_ref, cnt_ref, run_ref):
    @pl.when(pl.program_id(0) == 0)
    def _():
        run_ref[...] = jnp.zeros_like(run_ref)

    lane = lax.broadcasted_iota(jnp.int32, (CHUNK, LANES), 1)
    left = lane < A_GROUP_DIM
    zero = jnp.zeros((CHUNK, LANES), BF16)
    sa_chunks = []
    for c in range(TM // CHUNK):
        rows = slice(c * CHUNK, (c + 1) * CHUNK)
        pairs = []
        for p in range(A_GROUPS // 2):
            vp = uv_ref[rows, A_WIDTH + p * LANES:A_WIDTH + (p + 1) * LANES]
            rhs = jnp.concatenate([jnp.where(left, vp, zero), jnp.where(left, zero, vp)], axis=0)
            pairs.append(jnp.dot(wsp_ref[p], rhs, preferred_element_type=F32))
        mixed = jnp.concatenate(pairs, axis=1) + bsp_ref[...]
        sa_chunks.append((uv_ref[rows, :A_WIDTH].astype(F32) * mixed).astype(BF16))
    s_a = jnp.concatenate(sa_chunks, axis=0)

    a = jnp.dot(s_a, wa_ref[...], preferred_element_type=F32)
    b = jnp.dot(ob_ref[...], wb_ref[...], preferred_element_type=F32)
    merged = gates_ref[:, :D_MODEL].astype(F32) * a + gates_ref[:, D_MODEL:].astype(F32) * b
    x1 = x_ref[...] + jnp.dot(merged.astype(BF16), wo_ref[...], preferred_element_type=F32)
    x1_ref[...] = x1

    h2 = _rms(x1, g2_ref[...])
    h2p_ref[...] = _pack_bf16_pair(h2)
    logits = jnp.dot(h2.astype(BF16), wr_ref[...], preferred_element_type=F32) + br_ref[...]
    slab, counts = _route(logits, run_ref[...], ltri_ref[...])
    rt_ref[...] = slab
    run_ref[...] = counts
    cnt_ref[...] = counts


def _merge(x2, uv, gates, ob, wsp, bsp, wa, wb, wo, g2, wr, br, ltri):
    n_tiles = T_PROMPT // TM
    tile = lambda w: pl.BlockSpec((TM, w), lambda i: (i, 0))
    full = lambda a: pl.BlockSpec(a.shape, lambda i: (0,) * a.ndim)
    return pl.pallas_call(
        _merge_kernel,
        grid=(n_tiles,),
        in_specs=[tile(D_MODEL), tile(2 * A_WIDTH), tile(2 * D_MODEL), tile(GROUP_W),
                  full(wsp), full(bsp), full(wa), full(wb), full(wo), full(g2), full(wr), full(br),
                  full(ltri)],
        out_specs=[tile(D_MODEL), tile(HALF_D), tile(LANES), pl.BlockSpec((1, LANES), lambda i: (0, 0))],
        out_shape=[
            jax.ShapeDtypeStruct((T_PROMPT, D_MODEL), F32),
            jax.ShapeDtypeStruct((T_PROMPT, HALF_D), U32),
            jax.ShapeDtypeStruct((T_PROMPT, LANES), F32),
            jax.ShapeDtypeStruct((1, LANES), F32),
        ],
        scratch_shapes=[pltpu.VMEM((1, LANES), F32)],
        compiler_params=_cparams("arbitrary"),
        name="merge_route",
    )(x2, uv, gates, ob, wsp, bsp, wa, wb, wo, g2, wr, br, ltri)


SAMPLE_COLS = 768


def _sample_inproj_kernel(x_ref, g1_ref, w_ref, b_ref, z_ref):
    hb = _rms(x_ref[...], g1_ref[...]).astype(BF16)
    z_ref[...] = jnp.dot(hb, w_ref[...], preferred_element_type=F32) + b_ref[...]


def _sample_inproj(xs, g1, w_in, b_in):
    const = lambda j: (0, 0)
    return pl.pallas_call(
        _sample_inproj_kernel,
        grid=(IN_WIDTH // SAMPLE_COLS,),
        in_specs=[pl.BlockSpec((DEC_BATCH, D_MODEL), const), pl.BlockSpec((1, D_MODEL), const),
                  pl.BlockSpec((D_MODEL, SAMPLE_COLS), lambda j: (0, j)),
                  pl.BlockSpec((1, SAMPLE_COLS), lambda j: (0, j))],
        out_specs=pl.BlockSpec((DEC_BATCH, SAMPLE_COLS), lambda j: (0, j)),
        out_shape=jax.ShapeDtypeStruct((DEC_BATCH, IN_WIDTH), F32),
        compiler_params=_cparams("arbitrary"),
        name="sample_inproj",
    )(xs, g1, w_in, b_in)


def _sample_pre_kernel(z_ref, lng_ref, lnb_ref, ws0_ref, bs0_ref, cos_ref, sin_ref,
                       vrow_ref, sa_ref, qr_ref, kr_ref, kv0_ref, kv1_ref, kv2_ref):
    u = jax.nn.gelu(z_ref[:, COL_U:COL_U + A_WIDTH])
    va = _layer_norm(jax.nn.gelu(z_ref[:, COL_V:COL_V + A_WIDTH]), lng_ref[...], lnb_ref[...])
    vrow_ref[...] = va
    sa_ref[...] = u * (ws0_ref[...] * va.astype(BF16).astype(F32) + bs0_ref[...])
    cos, sin = cos_ref[...], sin_ref[...]
    for g, kv_ref in enumerate((kv0_ref, kv1_ref, kv2_ref)):
        for c in range(GROUP_W // LANES):
            off = g * GROUP_W + c * LANES
            qr_ref[:, off:off + LANES] = (_rope_chunk(z_ref[:, COL_Q + off:COL_Q + off + LANES], cos, sin)
                                          * (HEAD_DIM ** -0.5))
            kr = _rope_chunk(z_ref[:, COL_K + off:COL_K + off + LANES], cos, sin)
            kr_ref[:, off:off + LANES] = kr
            kv_ref[:, c * LANES:(c + 1) * LANES] = kr
        kv_ref[:, GROUP_W:] = z_ref[:, COL_VB + g * GROUP_W:COL_VB + (g + 1) * GROUP_W]


def _sample_pre(z, lng, lnb, ws0, bs0, cos_s, sin_s):
    row = lambda w: jax.ShapeDtypeStruct((DEC_BATCH, w), F32)
    return pl.pallas_call(
        _sample_pre_kernel,
        out_shape=[row(A_WIDTH), row(A_WIDTH), row(QK_W), row(QK_W),
                   row(2 * GROUP_W), row(2 * GROUP_W), row(2 * GROUP_W)],
        compiler_params=pltpu.CompilerParams(vmem_limit_bytes=VMEM_LIMIT),
        name="sample_pre",
    )(z, lng, lnb, ws0, bs0, cos_s, sin_s)


def _sample_attn_kernel(q_ref, k_ref, v_ref, c0_ref, c1_ref, c2_ref, comb_ref):
    as_operand = lambda t: t.astype(BF16).astype(F32)
    for h in range(HEADS):
        head = slice(h, h + 1)
        outs, lses = [], []
        for g, (c_ref, d) in enumerate(zip((c0_ref, c1_ref, c2_ref), DILATIONS)):
            qc, kn, vn = (as_operand(r[0, g, :, head]) for r in (q_ref, k_ref, v_ref))
            s = jnp.sum(as_operand(c_ref[0, 0, h]) * qc, axis=0, keepdims=True)
            if d > 1:
                pos = lax.broadcasted_iota(jnp.int32, s.shape, 1)
                s = jnp.where(pos % d == 0, s, NEG)
            s_n = jnp.sum(qc * kn, axis=0, keepdims=True)
            m = jnp.maximum(jnp.max(s, axis=1, keepdims=True), s_n)
            lse = m + jnp.log(jnp.sum(jnp.exp(s - m), axis=1, keepdims=True) + jnp.exp(s_n - m))
            p = as_operand(jnp.exp(s - lse))
            p_n = as_operand(jnp.exp(s_n - lse))
            outs.append(jnp.sum(as_operand(c_ref[0, 1, h]) * p, axis=1, keepdims=True) + p_n * vn)
            lses.append(lse)
        mx = jnp.maximum(jnp.maximum(lses[0], lses[1]), lses[2])
        ws = [jnp.exp(t - mx) for t in lses]
        tot = ws[0] + ws[1] + ws[2]
        comb_ref[0, :, head] = (ws[0] * outs[0] + ws[1] * outs[1] + ws[2] * outs[2]) / tot


def _sample_attn(qt, kt, vt, caches):
    new_spec = pl.BlockSpec((1, 3, HEAD_DIM, HEADS), lambda i: (i, 0, 0, 0))
    cache_spec = lambda c: pl.BlockSpec((1,) + c.shape[1:], lambda i: (i, 0, 0, 0, 0))
    return pl.pallas_call(
        _sample_attn_kernel,
        grid=(DEC_BATCH,),
        in_specs=[new_spec, new_spec, new_spec, cache_spec(caches[0]), cache_spec(caches[1]), cache_spec(caches[2])],
        out_specs=pl.BlockSpec((1, HEAD_DIM, HEADS), lambda i: (i, 0, 0)),
        out_shape=jax.ShapeDtypeStruct((DEC_BATCH, HEAD_DIM, HEADS), F32),
        compiler_params=_cparams("arbitrary"),
        name="sample_attn",
    )(qt, kt, vt, caches[0], caches[1], caches[2])


def _sample_merge_kernel(x_ref, z_ref, sa_ref, comb_ref, wa_ref, wb_ref, wo_ref, g2_ref, wr_ref, br_ref,
                         ltri_ref, cnt_in_ref, x1_ref, h2p_ref, rt_ref, cnt_ref):
    dot = lambda p, q: jnp.dot(p.astype(BF16), q, preferred_element_type=F32)
    a = dot(sa_ref[...], wa_ref[...])
    b = dot(comb_ref[...], wb_ref[...])
    merged = (jax.nn.sigmoid(z_ref[:, COL_GA:COL_GA + D_MODEL]) * a
              + jax.nn.sigmoid(z_ref[:, COL_GB:COL_GB + D_MODEL]) * b)
    x1 = x_ref[...] + dot(merged, wo_ref[...])
    x1_ref[...] = x1
    h2 = _rms(x1, g2_ref[...])
    h2p_ref[...] = _pack_bf16_pair(h2)
    logits = dot(h2, wr_ref[...]) + br_ref[...]
    slab, counts = _route(logits, cnt_in_ref[...], ltri_ref[...])
    rt_ref[...] = slab
    cnt_ref[...] = counts


def _sample_merge(xs, z, s_a, comb, wa, wb, wo, g2, wr, br, ltri, counts):
    return pl.pallas_call(
        _sample_merge_kernel,
        out_shape=[jax.ShapeDtypeStruct((DEC_BATCH, D_MODEL), F32),
                   jax.ShapeDtypeStruct((DEC_BATCH, HALF_D), U32),
                   jax.ShapeDtypeStruct((DEC_BATCH, LANES), F32),
                   jax.ShapeDtypeStruct((1, LANES), F32)],
        compiler_params=pltpu.CompilerParams(vmem_limit_bytes=VMEM_LIMIT),
        name="sample_merge_route",
    )(xs, z, s_a, comb, wa, wb, wo, g2, wr, br, ltri, counts)


def _dispatch_kernel(dest_ref, h_ref, xs_in_ref, xs_ref, sem):
    del xs_in_ref
    rows = h_ref.shape[0]
    base = pl.program_id(0) * (2 * rows)

    def row_copy(t, slot):
        d = dest_ref[base + 2 * t + slot]
        return pltpu.make_async_copy(h_ref.at[pl.ds(t, 1)], xs_ref.at[pl.ds(d, 1)], sem)

    def start(t, c):
        row_copy(t, 0).start()
        row_copy(t, 1).start()
        return c

    def wait(t, c):
        row_copy(t, 0).wait()
        row_copy(t, 1).wait()
        return c

    lax.fori_loop(0, rows, start, 0)
    lax.fori_loop(0, rows, wait, 0)


def _dispatch(dest, h2p, xs, rows):
    n = h2p.shape[0]
    return pl.pallas_call(
        _dispatch_kernel,
        grid_spec=pltpu.PrefetchScalarGridSpec(
            num_scalar_prefetch=1,
            grid=(n // rows,),
            in_specs=[pl.BlockSpec((rows, HALF_D), lambda i, dest: (i, 0)), pl.BlockSpec(memory_space=pl.ANY)],
            out_specs=pl.BlockSpec(memory_space=pl.ANY),
            scratch_shapes=[pltpu.SemaphoreType.DMA(())],
        ),
        out_shape=jax.ShapeDtypeStruct(xs.shape, xs.dtype),
        input_output_aliases={2: 0},
        compiler_params=_cparams("arbitrary"),
        name=f"moe_dispatch_{rows}",
    )(dest, h2p, xs)


def _ffn_kernel(be_ref, nused_ref, xs_ref, wg_ref, wu_ref, wd_ref, ys_ref, wg_b, wu_b, wd_b):
    i = pl.program_id(0)
    new_expert = jnp.logical_or(i == 0, be_ref[i] != be_ref[jnp.maximum(i - 1, 0)])

    @pl.when(new_expert)
    def _():
        wg_b[...] = wg_ref[0].astype(BF16)
        wu_b[...] = wu_ref[0].astype(BF16)
        wd_b[...] = wd_ref[0].astype(BF16)

    @pl.when(i < nused_ref[0])
    def _():
        lo, hi = _unpack_bf16_pair(xs_ref[...])
        gate = (jnp.dot(lo, wg_b[:HALF_D, :], preferred_element_type=F32)
                + jnp.dot(hi, wg_b[HALF_D:, :], preferred_element_type=F32))
        up = (jnp.dot(lo, wu_b[:HALF_D, :], preferred_element_type=F32)
              + jnp.dot(hi, wu_b[HALF_D:, :], preferred_element_type=F32))
        hid = (jax.nn.silu(gate) * up).astype(BF16)
        ys_ref[...] = jnp.dot(hid, wd_b[...], preferred_element_type=F32)

    @pl.when(i >= nused_ref[0])
    def _():
        ys_ref[...] = jnp.zeros_like(ys_ref)


def _expert_ffn(block_e, nused, xs, w_gate, w_up, w_down):
    return pl.pallas_call(
        _ffn_kernel,
        grid_spec=pltpu.PrefetchScalarGridSpec(
            num_scalar_prefetch=2,
            grid=(MOE_NB,),
            in_specs=[pl.BlockSpec((MOE_TB, HALF_D), lambda i, be, nu: (i, 0)),
                      pl.BlockSpec((1, D_MODEL, D_EXPERT), lambda i, be, nu: (be[i], 0, 0)),
                      pl.BlockSpec((1, D_MODEL, D_EXPERT), lambda i, be, nu: (be[i], 0, 0)),
                      pl.BlockSpec((1, D_EXPERT, D_MODEL), lambda i, be, nu: (be[i], 0, 0))],
            out_specs=pl.BlockSpec((MOE_TB, D_MODEL), lambda i, be, nu: (i, 0)),
            scratch_shapes=[pltpu.VMEM((D_MODEL, D_EXPERT), BF16), pltpu.VMEM((D_MODEL, D_EXPERT), BF16),
                            pltpu.VMEM((D_EXPERT, D_MODEL), BF16)],
        ),
        out_shape=jax.ShapeDtypeStruct((MOE_ROWS, D_MODEL), F32),
        compiler_params=_cparams("arbitrary"),
        name="moe_ffn",
    )(block_e, nused, xs, w_gate, w_up, w_down)


def _combine_kernel(dest_ref, x1_ref, rt_ref, gf_ref, ys_ref, y_ref, ya, yb, sem):
    rows = x1_ref.shape[0]
    base = pl.program_id(0) * (2 * rows)

    def row_copy(t, slot, buf):
        d = dest_ref[base + 2 * t + slot]
        return pltpu.make_async_copy(ys_ref.at[pl.ds(d, 1)], buf.at[pl.ds(t, 1)], sem)

    def start(t, c):
        row_copy(t, 0, ya).start()
        row_copy(t, 1, yb).start()
        return c

    def wait(t, c):
        row_copy(t, 0, ya).wait()
        row_copy(t, 1, yb).wait()
        return c

    lax.fori_loop(0, rows, start, 0)
    lax.fori_loop(0, rows, wait, 0)
    x2 = x1_ref[...] + rt_ref[:, 4:5] * ya[...] + rt_ref[:, 5:6] * yb[...]
    y_ref[...] = _rms(x2, gf_ref[...])


def _combine(dest, x1, rt, gf, ys, rows):
    n = x1.shape[0]
    return pl.pallas_call(
        _combine_kernel,
        grid_spec=pltpu.PrefetchScalarGridSpec(
            num_scalar_prefetch=1,
            grid=(n // rows,),
            in_specs=[pl.BlockSpec((rows, D_MODEL), lambda i, dest: (i, 0)),
                      pl.BlockSpec((rows, LANES), lambda i, dest: (i, 0)),
                      pl.BlockSpec((1, D_MODEL), lambda i, dest: (0, 0)),
                      pl.BlockSpec(memory_space=pl.ANY)],
            out_specs=pl.BlockSpec((rows, D_MODEL), lambda i, dest: (i, 0)),
            scratch_shapes=[pltpu.VMEM((rows, D_MODEL), F32), pltpu.VMEM((rows, D_MODEL), F32),
                            pltpu.SemaphoreType.DMA(())],
        ),
        out_shape=jax.ShapeDtypeStruct((n, D_MODEL), F32),
        compiler_params=_cparams("arbitrary"),
        name=f"moe_combine_{rows}",
    )(dest, x1, rt, gf, ys)


def _rope_tables(pos):
    half = HEAD_DIM // 2
    inv = 1.0 / (ROPE_THETA ** (jnp.arange(half, dtype=F32) * (2.0 / HEAD_DIM)))
    ang = pos.astype(F32)[:, None] * inv[None, :]
    cos, sin = jnp.cos(ang), jnp.sin(ang)
    return jnp.concatenate([cos, cos, cos, cos], axis=1), jnp.concatenate([-sin, sin, -sin, sin], axis=1)


def _slot_dest(rt, pstarts):
    e = rt[:, 0:2].astype(jnp.int32)
    pos = rt[:, 2:4].astype(jnp.int32)
    onehot = e[:, :, None] == jnp.arange(N_EXPERTS, dtype=jnp.int32)[None, None, :]
    return (jnp.sum(jnp.where(onehot, pstarts[None, None, :], 0), axis=-1) + pos).reshape(-1)


def kernel(x_prompt, x_sample, cache_kv_g0, cache_kv_g1, cache_kv_g2, norm1_g, w_in, b_in, a_ln_g, a_ln_b, w_spatial, b_spatial, w_a_proj, w_b_proj, w_o, norm2_g, w_group_router, b_group_router, w_expert_router, b_expert_router, w_gate, w_up, w_down, final_norm_g):
    x2 = x_prompt.reshape(T_PROMPT, D_MODEL)
    xs = x_sample.reshape(DEC_BATCH, D_MODEL)
    g1 = norm1_g[0][None, :]
    g2 = norm2_g[0][None, :]
    gf = final_norm_g[None, :]
    b_in2 = b_in[0][None, :]
    lng, lnb = a_ln_g[0][None, :], a_ln_b[0][None, :]

    causal = jnp.tril(jnp.ones((CHUNK, CHUNK), dtype=bool))
    ws_tril = jnp.where(causal[None], w_spatial[0], 0.0)
    wsp = jnp.concatenate([ws_tril[0::2], ws_tril[1::2]], axis=2).astype(BF16)
    bsp = jnp.repeat(b_spatial[0].T, A_GROUP_DIM, axis=1)
    ws0 = jnp.repeat(ws_tril[:, 0, 0].astype(BF16).astype(F32), A_GROUP_DIM)[None, :]
    bs0 = jnp.repeat(b_spatial[0][:, 0], A_GROUP_DIM)[None, :]
    w_router = jnp.zeros((D_MODEL, LANES), F32)
    w_router = w_router.at[:, :N_EXPERT_GROUPS].set(w_group_router[0])
    w_router = w_router.at[:, N_EXPERT_GROUPS:N_EXPERT_GROUPS + N_EXPERTS].set(w_expert_router[0])
    b_router = jnp.zeros((1, LANES), F32)
    b_router = b_router.at[0, :N_EXPERT_GROUPS].set(b_group_router[0])
    b_router = b_router.at[0, N_EXPERT_GROUPS:N_EXPERT_GROUPS + N_EXPERTS].set(b_expert_router[0])
    w_router = w_router.astype(BF16)
    ltri = jnp.tril(jnp.ones((TM, TM), BF16), -1)
    w_in_b, w_a_b, w_b_b, w_o_b = (w[0].astype(BF16) for w in (w_in, w_a_proj, w_b_proj, w_o))

    cos_p, sin_p = _rope_tables(jnp.arange(SEQ, dtype=jnp.int32))
    cos_s, sin_s = _rope_tables(jnp.full((1,), PAST_LEN, jnp.int32))

    uv, qkv0, qkv1, qkv2, gates, kvp0, kvp1, kvp2 = _inproj(
        x2, g1, w_in_b, b_in2, lng, lnb, cos_p, sin_p)
    ob = _attention((qkv0.reshape(BATCH, 1, SEQ, 3 * GROUP_W), qkv1, qkv2)).reshape(T_PROMPT, GROUP_W)
    x1_p, h2p_p, rt_p, counts_p = _merge(
        x2, uv, gates, ob, wsp, bsp, w_a_b, w_b_b, w_o_b, g2, w_router, b_router, ltri)

    z_s = _sample_inproj(xs, g1, w_in_b, b_in2)
    vrow, sa_s, qr_s, kr_s, kvs0, kvs1, kvs2 = _sample_pre(z_s, lng, lnb, ws0, bs0, cos_s, sin_s)
    dim_major = lambda t: t.reshape(DEC_BATCH, 3, HEADS, HEAD_DIM).transpose(0, 1, 3, 2)
    caches = [c.transpose(0, 1, 3, 4, 5, 2).reshape(DEC_BATCH, 2, HEADS, HEAD_DIM, c.shape[2])
              for c in (cache_kv_g0, cache_kv_g1, cache_kv_g2)]
    comb_s = _sample_attn(dim_major(qr_s), dim_major(kr_s), dim_major(z_s[:, COL_VB:COL_VB + QK_W]), caches)
    comb_s = comb_s.transpose(0, 2, 1).reshape(DEC_BATCH, GROUP_W)
    x1_s, h2p_s, rt_s, counts = _sample_merge(
        xs, z_s, sa_s, comb_s, w_a_b, w_b_b, w_o_b, g2, w_router, b_router,
        ltri[:DEC_BATCH, :DEC_BATCH], counts_p)

    cnt = counts[0, :N_EXPERTS].astype(jnp.int32)
    padded = (cnt + MOE_TB - 1) // MOE_TB * MOE_TB
    pends = jnp.cumsum(padded)
    pstarts = pends - padded
    block_starts = jnp.arange(MOE_NB, dtype=jnp.int32) * MOE_TB
    block_e = jnp.minimum(jnp.sum((pends[None, :] <= block_starts[:, None]).astype(jnp.int32), axis=1),
                          N_EXPERTS - 1)
    nused = (pends[-1:] // MOE_TB).astype(jnp.int32)
    dest_p = _slot_dest(rt_p, pstarts)
    dest_s = _slot_dest(rt_s, pstarts)

    rows = jnp.zeros((MOE_ROWS, HALF_D), U32)
    rows = _dispatch(dest_p, h2p_p, rows, TM)
    rows = _dispatch(dest_s, h2p_s, rows, DEC_BATCH)
    ys = _expert_ffn(block_e, nused, rows, w_gate[0], w_up[0], w_down[0])
    y_p = _combine(dest_p, x1_p, rt_p, gf, ys, TM)
    y_s = _combine(dest_s, x1_s, rt_s, gf, ys, DEC_BATCH)

    kv_shape = lambda n, w: (1, n, w, 2, HEADS, HEAD_DIM)
    window_rows = lambda t: t.reshape(1, BATCH, 2, HEADS, HEAD_DIM, t.shape[2]).transpose(0, 1, 5, 2, 3, 4)
    return (y_p.reshape(BATCH, SEQ, D_MODEL), y_s.reshape(DEC_BATCH, 1, D_MODEL),
            window_rows(kvp0), window_rows(kvp1), window_rows(kvp2),
            kvs0.reshape(kv_shape(DEC_BATCH, 1)), kvs1.reshape(kv_shape(DEC_BATCH, 1)),
            kvs2.reshape(kv_shape(DEC_BATCH, 1)), vrow.reshape(1, DEC_BATCH, 1, A_WIDTH))
```

```python
import functools

import jax
import jax.numpy as jnp
from jax import lax
from jax.experimental import pallas as pl
from jax.experimental.pallas import tpu as pltpu

F32 = jnp.float32
BF16 = jnp.bfloat16
U32 = jnp.uint32

D_MODEL = 1024
BATCH = 2
SEQ = 8192
DEC_BATCH = 32
PAST_LEN = 8192
CHUNK = 128
A_GROUPS = 8
A_GROUP_DIM = 64
A_WIDTH = 512
HEAD_DIM = 64
HEADS = 4
GROUP_W = HEADS * HEAD_DIM
DILATIONS = (1, 4, 16)
SPAN = 128
QK_W = 768
IN_WIDTH = 5376
COL_U, COL_V, COL_Q, COL_K, COL_VB, COL_GA, COL_GB = 0, 512, 1024, 1792, 2560, 3328, 4352
N_EXPERT_GROUPS = 4
EXPERTS_PER_GROUP = 8
N_EXPERTS = 32
D_EXPERT = 512
ROPE_THETA = 10000.0
EPS = 1e-6

LANES = 128
T_PROMPT = BATCH * SEQ
TM = 512
TILES_PER_SEQ = SEQ // TM
WINDOWS = tuple(min(SPAN * d, SEQ) for d in DILATIONS)
TA = SPAN * max(DILATIONS)
MOE_TB = 256
ROW_DMA_UNROLL = 8
N_SLOTS = 2 * (T_PROMPT + DEC_BATCH)
MOE_NB = -(-N_SLOTS // MOE_TB) + N_EXPERTS
MOE_ROWS = MOE_NB * MOE_TB
HALF_D = D_MODEL // 2
NEG = -1e30
VMEM_LIMIT = 56 * 1024 * 1024


def _cparams(*sem):
    return pltpu.CompilerParams(dimension_semantics=sem, vmem_limit_bytes=VMEM_LIMIT)


def _rms(x, g):
    return x * lax.rsqrt(jnp.mean(x * x, axis=-1, keepdims=True) + EPS) * g


def _layer_norm(x, g, b):
    mu = jnp.mean(x, axis=-1, keepdims=True)
    xc = x - mu
    var = jnp.mean(xc * xc, axis=-1, keepdims=True)
    return xc * lax.rsqrt(var + EPS) * g + b


def _rope_chunk(x, cos, sin_signed):
    lane = lax.broadcasted_iota(jnp.int32, x.shape, 1)
    first_half = (lane % HEAD_DIM) < (HEAD_DIM // 2)
    swapped = jnp.where(first_half, pltpu.roll(x, LANES - HEAD_DIM // 2, 1), pltpu.roll(x, HEAD_DIM // 2, 1))
    return x * cos + swapped * sin_signed


def _pack_bf16_pair(h):
    lo = lax.bitcast_convert_type(h[:, :HALF_D].astype(BF16).astype(F32), U32)
    hi = lax.bitcast_convert_type(h[:, HALF_D:].astype(BF16).astype(F32), U32)
    return (hi & jnp.uint32(0xFFFF0000)) | (lo >> 16)


def _unpack_bf16_pair(p):
    lo = lax.bitcast_convert_type(p << 16, F32).astype(BF16)
    hi = lax.bitcast_convert_type(p & jnp.uint32(0xFFFF0000), F32).astype(BF16)
    return lo, hi


def _route(logits, prior_counts, ltri):
    rows = logits.shape[0]
    lane = lax.broadcasted_iota(jnp.int32, (rows, LANES), 1).astype(F32)
    is_g = lane < N_EXPERT_GROUPS
    gl = jnp.where(is_g, logits, NEG)
    gmax = jnp.max(gl, axis=1, keepdims=True)
    grp = jnp.min(jnp.where(gl == gmax, lane, float(LANES)), axis=1, keepdims=True)
    p_grp = 1.0 / jnp.sum(jnp.where(is_g, jnp.exp(gl - gmax), 0.0), axis=1, keepdims=True)
    lo = N_EXPERT_GROUPS + grp * EXPERTS_PER_GROUP
    el = jnp.where((lane >= lo) & (lane < lo + EXPERTS_PER_GROUP), logits, NEG)
    v1 = jnp.max(el, axis=1, keepdims=True)
    i1 = jnp.min(jnp.where(el == v1, lane, float(LANES)), axis=1, keepdims=True)
    el2 = jnp.where(lane == i1, NEG, el)
    v2 = jnp.max(el2, axis=1, keepdims=True)
    i2 = jnp.min(jnp.where(el2 == v2, lane, float(LANES)), axis=1, keepdims=True)
    t = jnp.exp(v2 - v1)
    gate1 = p_grp / (1.0 + t)
    gate2 = p_grp * t / (1.0 + t)
    e1 = i1 - N_EXPERT_GROUPS
    e2 = i2 - N_EXPERT_GROUPS
    hit1 = lane == e1
    hit2 = lane == e2
    onehot = jnp.where(hit1 | hit2, 1.0, 0.0)
    rank = jnp.dot(ltri, onehot.astype(BF16), preferred_element_type=F32) + prior_counts
    pos1 = jnp.sum(jnp.where(hit1, rank, 0.0), axis=1, keepdims=True)
    pos2 = jnp.sum(jnp.where(hit2, rank, 0.0), axis=1, keepdims=True)
    slab = jnp.where(lane == 0, e1, jnp.where(lane == 1, e2, jnp.where(lane == 2, pos1, jnp.where(
        lane == 3, pos2, jnp.where(lane == 4, gate1, jnp.where(lane == 5, gate2, 0.0))))))
    return slab, prior_counts + jnp.sum(onehot, axis=0, keepdims=True)


def _inproj_kernel(x_ref, g1_ref, w_ref, b_ref, lng_ref, lnb_ref, cos_ref, sin_ref,
                   uv_ref, qkv0_ref, qkv1_ref, qkv2_ref, gates_ref, kv0_ref, kv1_ref, kv2_ref, de_ref):
    tile_in_seq = pl.program_id(0) % TILES_PER_SEQ
    hb = _rms(x_ref[...], g1_ref[...]).astype(BF16)

    def seg(lo, width):
        return jnp.dot(hb, w_ref[:, lo:lo + width], preferred_element_type=F32) + b_ref[:, lo:lo + width]

    uv_ref[:, :A_WIDTH] = jax.nn.gelu(seg(COL_U, A_WIDTH)).astype(BF16)
    uv_ref[:, A_WIDTH:] = _layer_norm(jax.nn.gelu(seg(COL_V, A_WIDTH)), lng_ref[...], lnb_ref[...]).astype(BF16)
    gates_ref[:, :D_MODEL] = jax.nn.sigmoid(seg(COL_GA, D_MODEL)).astype(BF16)
    gates_ref[:, D_MODEL:] = jax.nn.sigmoid(seg(COL_GB, D_MODEL)).astype(BF16)

    cos = cos_ref[...]
    sin = sin_ref[...]
    qkv_refs = (qkv0_ref, qkv1_ref, qkv2_ref)
    kv_refs = (kv0_ref, kv1_ref, kv2_ref)
    for g, d in enumerate(DILATIONS):
        q = seg(COL_Q + g * GROUP_W, GROUP_W)
        k = seg(COL_K + g * GROUP_W, GROUP_W)
        v = seg(COL_VB + g * GROUP_W, GROUP_W)
        chunks = GROUP_W // LANES
        for c in range(chunks):
            sl = slice(c * LANES, (c + 1) * LANES)
            de_ref[c] = _rope_chunk(q[:, sl], cos, sin) * (HEAD_DIM ** -0.5)
            de_ref[chunks + c] = _rope_chunk(k[:, sl], cos, sin)
            de_ref[2 * chunks + c] = v[:, sl]
        for c in range(3 * chunks):
            sl = slice(c * LANES, (c + 1) * LANES)
            if d == 1:
                qkv_refs[g][:, sl] = de_ref[c].astype(BF16)
            else:
                for r in range(d):
                    qkv_refs[g][0, r, :, sl] = de_ref[c, pl.ds(r, TM // d, stride=d), :].astype(BF16)
        kv_rows = kv_refs[g].shape[2]
        first = TILES_PER_SEQ - WINDOWS[g] // kv_rows

        @pl.when(tile_in_seq >= first)
        def _(kv_ref=kv_refs[g], kv_rows=kv_rows, chunks=chunks):
            for c in range(2 * chunks):
                kv_ref[0, c * LANES:(c + 1) * LANES, :] = de_ref[chunks + c, TM - kv_rows:, :].T


def _inproj(x2, g1, w_bf, b_in, lng, lnb, cos_t, sin_t):
    tiles_per_seq = TILES_PER_SEQ
    n_tiles = T_PROMPT // TM
    const = lambda i: (0, 0)

    def kv_spec(w):
        rows = min(w, TM)
        first = tiles_per_seq - w // rows
        return pl.BlockSpec((1, 2 * GROUP_W, rows),
                            lambda i: (i // tiles_per_seq, 0, jnp.maximum(i % tiles_per_seq - first, 0)))

    def regrouped_spec(d):
        return pl.BlockSpec((1, d, TM // d, 3 * GROUP_W), lambda i: (i // tiles_per_seq, 0, i % tiles_per_seq, 0))

    return pl.pallas_call(
        _inproj_kernel,
        grid=(n_tiles,),
        in_specs=[
            pl.BlockSpec((TM, D_MODEL), lambda i: (i, 0)),
            pl.BlockSpec((1, D_MODEL), const),
            pl.BlockSpec((D_MODEL, IN_WIDTH), const),
            pl.BlockSpec((1, IN_WIDTH), const),
            pl.BlockSpec((1, A_WIDTH), const),
            pl.BlockSpec((1, A_WIDTH), const),
            pl.BlockSpec((TM, LANES), lambda i: (i % tiles_per_seq, 0)),
            pl.BlockSpec((TM, LANES), lambda i: (i % tiles_per_seq, 0)),
        ],
        out_specs=[
            pl.BlockSpec((TM, 2 * A_WIDTH), lambda i: (i, 0)),
            pl.BlockSpec((TM, 3 * GROUP_W), lambda i: (i, 0)),
            regrouped_spec(DILATIONS[1]),
            regrouped_spec(DILATIONS[2]),
            pl.BlockSpec((TM, 2 * D_MODEL), lambda i: (i, 0)),
            kv_spec(WINDOWS[0]), kv_spec(WINDOWS[1]), kv_spec(WINDOWS[2]),
        ],
        out_shape=[
            jax.ShapeDtypeStruct((T_PROMPT, 2 * A_WIDTH), BF16),
            jax.ShapeDtypeStruct((T_PROMPT, 3 * GROUP_W), BF16),
            jax.ShapeDtypeStruct((BATCH, DILATIONS[1], SEQ // DILATIONS[1], 3 * GROUP_W), BF16),
            jax.ShapeDtypeStruct((BATCH, DILATIONS[2], SEQ // DILATIONS[2], 3 * GROUP_W), BF16),
            jax.ShapeDtypeStruct((T_PROMPT, 2 * D_MODEL), BF16),
            jax.ShapeDtypeStruct((BATCH, 2 * GROUP_W, WINDOWS[0]), F32),
            jax.ShapeDtypeStruct((BATCH, 2 * GROUP_W, WINDOWS[1]), F32),
            jax.ShapeDtypeStruct((BATCH, 2 * GROUP_W, WINDOWS[2]), F32),
        ],
        scratch_shapes=[pltpu.VMEM((3 * GROUP_W // LANES, TM, LANES), F32)],
        compiler_params=_cparams("arbitrary"),
        name="inproj",
    )(x2, g1, w_bf, b_in, lng, lnb, cos_t, sin_t)


def _attn_kernel(c0_ref, p0_ref, c1_ref, p1_ref, c2_ref, p2_ref, o_ref, acc_ref, m_ref, l_ref):
    n = pl.program_id(1)
    acc_ref[...] = jnp.zeros_like(acc_ref)
    l_ref[...] = jnp.zeros_like(l_ref)
    m_ref[...] = jnp.full_like(m_ref, NEG)

    qi = lax.broadcasted_iota(jnp.int32, (SPAN, 2 * SPAN), 0)
    ki = lax.broadcasted_iota(jnp.int32, (SPAN, 2 * SPAN), 1)
    band = (ki >= qi) & (ki <= qi + SPAN)
    band_first = band & ((ki >= SPAN) | (n > 0))
    q_head = lax.broadcasted_iota(jnp.int32, (SPAN, GROUP_W), 1) // HEAD_DIM
    kv_head = lax.broadcasted_iota(jnp.int32, (2 * SPAN, GROUP_W), 1) // HEAD_DIM
    contract_last = (((1,), (1,)), ((), ()))
    k_cols = slice(GROUP_W, 2 * GROUP_W)
    v_cols = slice(2 * GROUP_W, 3 * GROUP_W)

    def attend(q, keys, vals, mask, tok_rows):
        scores = []
        for h in range(HEADS):
            qh = jnp.where(q_head == h, q, jnp.zeros_like(q))
            s = lax.dot_general(qh, keys, contract_last, preferred_element_type=F32)
            scores.append(jnp.where(mask, s, NEG))
        s = jnp.concatenate(scores, axis=0)
        m = jnp.max(s, axis=1, keepdims=True)
        p = jnp.exp(s - m)
        l = jnp.sum(p, axis=1, keepdims=True)
        pb = p.astype(BF16)
        acc = jnp.zeros((SPAN, GROUP_W), F32)
        m_b = jnp.zeros((SPAN, GROUP_W), F32)
        l_b = jnp.zeros((SPAN, GROUP_W), F32)
        for h in range(HEADS):
            rows = slice(h * SPAN, (h + 1) * SPAN)
            vh = jnp.where(kv_head == h, vals, jnp.zeros_like(vals))
            acc = acc + jnp.dot(pb[rows], vh, preferred_element_type=F32)
            m_b = jnp.where(q_head == h, m[rows], m_b)
            l_b = jnp.where(q_head == h, l[rows], l_b)
        for c in range(GROUP_W // LANES):
            sl = slice(c * LANES, (c + 1) * LANES)
            m_old = m_ref[c, tok_rows, :]
            m_new = jnp.maximum(m_old, m_b[:, sl])
            a_old = jnp.exp(m_old - m_new)
            a_blk = jnp.exp(m_b[:, sl] - m_new)
            l_ref[c, tok_rows, :] = a_old * l_ref[c, tok_rows, :] + a_blk * l_b[:, sl]
            acc_ref[c, tok_rows, :] = a_old * acc_ref[c, tok_rows, :] + a_blk * acc[:, sl]
            m_ref[c, tok_rows, :] = m_new

    def stream(c_ref, p_ref, r, d):
        n_blocks = c_ref.shape[2] // SPAN

        def tok_rows(j):
            start = j * (SPAN * d) + r
            return pl.ds(start, SPAN) if d == 1 else pl.ds(start, SPAN, stride=d)

        keys = jnp.concatenate([p_ref[0, r, :, k_cols], c_ref[0, r, :SPAN, k_cols]], axis=0)
        vals = jnp.concatenate([p_ref[0, r, :, v_cols], c_ref[0, r, :SPAN, v_cols]], axis=0)
        attend(c_ref[0, r, :SPAN, :GROUP_W], keys, vals, band_first, tok_rows(0))

        def later_block(j, carry):
            q_rows = pl.ds(pl.multiple_of(j * SPAN, SPAN), SPAN)
            kv_rows = pl.ds(pl.multiple_of((j - 1) * SPAN, SPAN), 2 * SPAN)
            attend(c_ref[0, r, q_rows, :GROUP_W], c_ref[0, r, kv_rows, k_cols], c_ref[0, r, kv_rows, v_cols],
                   band, tok_rows(j))
            return carry

        if n_blocks > 1:
            lax.fori_loop(1, n_blocks, later_block, 0)

    for (c_ref, p_ref), d in zip(((c0_ref, p0_ref), (c1_ref, p1_ref), (c2_ref, p2_ref)), DILATIONS):
        if d == 1:
            stream(c_ref, p_ref, 0, d)
        else:
            lax.fori_loop(0, d, lambda r, carry, c_ref=c_ref, p_ref=p_ref, d=d: (stream(c_ref, p_ref, r, d), carry)[1], 0)

    for c in range(GROUP_W // LANES):
        o_ref[0, :, c * LANES:(c + 1) * LANES] = (acc_ref[c] / l_ref[c]).astype(BF16)


def _attention(qkv_by_group):
    in_specs, args = [], []
    for qkv, d in zip(qkv_by_group, DILATIONS):
        rows = TA // d
        blocks_per_tile = rows // SPAN
        in_specs.append(pl.BlockSpec((1, d, rows, 3 * GROUP_W), lambda b, n: (b, 0, n, 0)))
        in_specs.append(pl.BlockSpec((1, d, SPAN, 3 * GROUP_W),
                                     lambda b, n, k=blocks_per_tile: (b, 0, jnp.maximum(n * k - 1, 0), 0)))
        args += [qkv, qkv]
    return pl.pallas_call(
        _attn_kernel,
        grid=(BATCH, SEQ // TA),
        in_specs=in_specs,
        out_specs=pl.BlockSpec((1, TA, GROUP_W), lambda b, n: (b, n, 0)),
        out_shape=jax.ShapeDtypeStruct((BATCH, SEQ, GROUP_W), BF16),
        scratch_shapes=[pltpu.VMEM((GROUP_W // LANES, TA, LANES), F32)] * 3,
        compiler_params=_cparams("arbitrary", "arbitrary"),
        name="attn",
    )(*args)


def _merge_kernel(x_ref, uv_ref, gates_ref, ob_ref,
                  wsp_ref, bsp_ref, wa_ref, wb_ref, wo_ref, g2_ref, wr_ref, br_ref, ltri_ref,
                  x1_ref, h2p_ref, rt_ref, cnt_ref, run_ref):
    @pl.when(pl.program_id(0) == 0)
    def _():
        run_ref[...] = jnp.zeros_like(run_ref)

    lane = lax.broadcasted_iota(jnp.int32, (CHUNK, LANES), 1)
    left = lane < A_GROUP_DIM
    zero = jnp.zeros((CHUNK, LANES), BF16)
    sa_chunks = []
    for c in range(TM // CHUNK):
        rows = slice(c * CHUNK, (c + 1) * CHUNK)
        pairs = []
        for p in range(A_GROUPS // 2):
            vp = uv_ref[rows, A_WIDTH + p * LANES:A_WIDTH + (p + 1) * LANES]
            rhs = jnp.concatenate([jnp.where(left, vp, zero), jnp.where(left, zero, vp)], axis=0)
            pairs.append(jnp.dot(wsp_ref[p], rhs, preferred_element_type=F32))
        mixed = jnp.concatenate(pairs, axis=1) + bsp_ref[...]
        sa_chunks.append((uv_ref[rows, :A_WIDTH].astype(F32) * mixed).astype(BF16))
    s_a = jnp.concatenate(sa_chunks, axis=0)

    a = jnp.dot(s_a, wa_ref[...], preferred_element_type=F32)
    b = jnp.dot(ob_ref[...], wb_ref[...], preferred_element_type=F32)
    merged = gates_ref[:, :D_MODEL].astype(F32) * a + gates_ref[:, D_MODEL:].astype(F32) * b
    x1 = x_ref[...] + jnp.dot(merged.astype(BF16), wo_ref[...], preferred_element_type=F32)
    x1_ref[...] = x1

    h2 = _rms(x1, g2_ref[...])
    h2p_ref[...] = _pack_bf16_pair(h2)
    logits = jnp.dot(h2.astype(BF16), wr_ref[...], preferred_element_type=F32) + br_ref[...]
    slab, counts = _route(logits, run_ref[...], ltri_ref[...])
    rt_ref[...] = slab
    run_ref[...] = counts
    cnt_ref[...] = counts


def _merge(x2, uv, gates, ob, wsp, bsp, wa, wb, wo, g2, wr, br, ltri):
    n_tiles = T_PROMPT // TM
    tile = lambda w: pl.BlockSpec((TM, w), lambda i: (i, 0))
    full = lambda a: pl.BlockSpec(a.shape, lambda i: (0,) * a.ndim)
    return pl.pallas_call(
        _merge_kernel,
        grid=(n_tiles,),
        in_specs=[tile(D_MODEL), tile(2 * A_WIDTH), tile(2 * D_MODEL), tile(GROUP_W),
                  full(wsp), full(bsp), full(wa), full(wb), full(wo), full(g2), full(wr), full(br),
                  full(ltri)],
        out_specs=[tile(D_MODEL), tile(HALF_D), tile(LANES), pl.BlockSpec((1, LANES), lambda i: (0, 0))],
        out_shape=[
            jax.ShapeDtypeStruct((T_PROMPT, D_MODEL), F32),
            jax.ShapeDtypeStruct((T_PROMPT, HALF_D), U32),
            jax.ShapeDtypeStruct((T_PROMPT, LANES), F32),
            jax.ShapeDtypeStruct((1, LANES), F32),
        ],
        scratch_shapes=[pltpu.VMEM((1, LANES), F32)],
        compiler_params=_cparams("arbitrary"),
        name="merge_route",
    )(x2, uv, gates, ob, wsp, bsp, wa, wb, wo, g2, wr, br, ltri)


SAMPLE_COLS = 768


def _sample_inproj_kernel(x_ref, g1_ref, w_ref, b_ref, z_ref):
    hb = _rms(x_ref[...], g1_ref[...]).astype(BF16)
    z_ref[...] = jnp.dot(hb, w_ref[...], preferred_element_type=F32) + b_ref[...]


def _sample_inproj(xs, g1, w_in, b_in):
    const = lambda j: (0, 0)
    return pl.pallas_call(
        _sample_inproj_kernel,
        grid=(IN_WIDTH // SAMPLE_COLS,),
        in_specs=[pl.BlockSpec((DEC_BATCH, D_MODEL), const), pl.BlockSpec((1, D_MODEL), const),
                  pl.BlockSpec((D_MODEL, SAMPLE_COLS), lambda j: (0, j)),
                  pl.BlockSpec((1, SAMPLE_COLS), lambda j: (0, j))],
        out_specs=pl.BlockSpec((DEC_BATCH, SAMPLE_COLS), lambda j: (0, j)),
        out_shape=jax.ShapeDtypeStruct((DEC_BATCH, IN_WIDTH), F32),
        compiler_params=_cparams("arbitrary"),
        name="sample_inproj",
    )(xs, g1, w_in, b_in)


def _sample_pre_kernel(z_ref, lng_ref, lnb_ref, ws0_ref, bs0_ref, cos_ref, sin_ref,
                       vrow_ref, sa_ref, qr_ref, kr_ref, kv0_ref, kv1_ref, kv2_ref):
    u = jax.nn.gelu(z_ref[:, COL_U:COL_U + A_WIDTH])
    va = _layer_norm(jax.nn.gelu(z_ref[:, COL_V:COL_V + A_WIDTH]), lng_ref[...], lnb_ref[...])
    vrow_ref[...] = va
    sa_ref[...] = u * (ws0_ref[...] * va.astype(BF16).astype(F32) + bs0_ref[...])
    cos, sin = cos_ref[...], sin_ref[...]
    for g, kv_ref in enumerate((kv0_ref, kv1_ref, kv2_ref)):
        for c in range(GROUP_W // LANES):
            off = g * GROUP_W + c * LANES
            qr_ref[:, off:off + LANES] = (_rope_chunk(z_ref[:, COL_Q + off:COL_Q + off + LANES], cos, sin)
                                          * (HEAD_DIM ** -0.5))
            kr = _rope_chunk(z_ref[:, COL_K + off:COL_K + off + LANES], cos, sin)
            kr_ref[:, off:off + LANES] = kr
            kv_ref[:, c * LANES:(c + 1) * LANES] = kr
        kv_ref[:, GROUP_W:] = z_ref[:, COL_VB + g * GROUP_W:COL_VB + (g + 1) * GROUP_W]


def _sample_pre(z, lng, lnb, ws0, bs0, cos_s, sin_s):
    row = lambda w: jax.ShapeDtypeStruct((DEC_BATCH, w), F32)
    return pl.pallas_call(
        _sample_pre_kernel,
        out_shape=[row(A_WIDTH), row(A_WIDTH), row(QK_W), row(QK_W),
                   row(2 * GROUP_W), row(2 * GROUP_W), row(2 * GROUP_W)],
        compiler_params=pltpu.CompilerParams(vmem_limit_bytes=VMEM_LIMIT),
        name="sample_pre",
    )(z, lng, lnb, ws0, bs0, cos_s, sin_s)


def _sample_attn_kernel(q_ref, k_ref, v_ref, c0_ref, c1_ref, c2_ref, comb_ref):
    as_operand = lambda t: t.astype(BF16).astype(F32)
    for h in range(HEADS):
        head = slice(h, h + 1)
        outs, lses = [], []
        for g, (c_ref, d) in enumerate(zip((c0_ref, c1_ref, c2_ref), DILATIONS)):
            qc, kn, vn = (as_operand(r[0, g, :, head]) for r in (q_ref, k_ref, v_ref))
            s = jnp.sum(as_operand(c_ref[0, 0, h]) * qc, axis=0, keepdims=True)
            if d > 1:
                pos = lax.broadcasted_iota(jnp.int32, s.shape, 1)
                s = jnp.where(pos % d == 0, s, NEG)
            s_n = jnp.sum(qc * kn, axis=0, keepdims=True)
            m = jnp.maximum(jnp.max(s, axis=1, keepdims=True), s_n)
            lse = m + jnp.log(jnp.sum(jnp.exp(s - m), axis=1, keepdims=True) + jnp.exp(s_n - m))
            p = as_operand(jnp.exp(s - lse))
            p_n = as_operand(jnp.exp(s_n - lse))
            outs.append(jnp.sum(as_operand(c_ref[0, 1, h]) * p, axis=1, keepdims=True) + p_n * vn)
            lses.append(lse)
        mx = jnp.maximum(jnp.maximum(lses[0], lses[1]), lses[2])
        ws = [jnp.exp(t - mx) for t in lses]
        tot = ws[0] + ws[1] + ws[2]
        comb_ref[0, :, head] = (ws[0] * outs[0] + ws[1] * outs[1] + ws[2] * outs[2]) / tot


def _sample_attn(qt, kt, vt, caches):
    new_spec = pl.BlockSpec((1, 3, HEAD_DIM, HEADS), lambda i: (i, 0, 0, 0))
    cache_spec = lambda c: pl.BlockSpec((1,) + c.shape[1:], lambda i: (i, 0, 0, 0, 0))
    return pl.pallas_call(
        _sample_attn_kernel,
        grid=(DEC_BATCH,),
        in_specs=[new_spec, new_spec, new_spec, cache_spec(caches[0]), cache_spec(caches[1]), cache_spec(caches[2])],
        out_specs=pl.BlockSpec((1, HEAD_DIM, HEADS), lambda i: (i, 0, 0)),
        out_shape=jax.ShapeDtypeStruct((DEC_BATCH, HEAD_DIM, HEADS), F32),
        compiler_params=_cparams("arbitrary"),
        name="sample_attn",
    )(qt, kt, vt, caches[0], caches[1], caches[2])


def _sample_merge_kernel(x_ref, z_ref, sa_ref, comb_ref, wa_ref, wb_ref, wo_ref, g2_ref, wr_ref, br_ref,
                         ltri_ref, cnt_in_ref, x1_ref, h2p_ref, rt_ref, cnt_ref):
    dot = lambda p, q: jnp.dot(p.astype(BF16), q, preferred_element_type=F32)
    a = dot(sa_ref[...], wa_ref[...])
    b = dot(comb_ref[...], wb_ref[...])
    merged = (jax.nn.sigmoid(z_ref[:, COL_GA:COL_GA + D_MODEL]) * a
              + jax.nn.sigmoid(z_ref[:, COL_GB:COL_GB + D_MODEL]) * b)
    x1 = x_ref[...] + dot(merged, wo_ref[...])
    x1_ref[...] = x1
    h2 = _rms(x1, g2_ref[...])
    h2p_ref[...] = _pack_bf16_pair(h2)
    logits = dot(h2, wr_ref[...]) + br_ref[...]
    slab, counts = _route(logits, cnt_in_ref[...], ltri_ref[...])
    rt_ref[...] = slab
    cnt_ref[...] = counts


def _sample_merge(xs, z, s_a, comb, wa, wb, wo, g2, wr, br, ltri, counts):
    return pl.pallas_call(
        _sample_merge_kernel,
        out_shape=[jax.ShapeDtypeStruct((DEC_BATCH, D_MODEL), F32),
                   jax.ShapeDtypeStruct((DEC_BATCH, HALF_D), U32),
                   jax.ShapeDtypeStruct((DEC_BATCH, LANES), F32),
                   jax.ShapeDtypeStruct((1, LANES), F32)],
        compiler_params=pltpu.CompilerParams(vmem_limit_bytes=VMEM_LIMIT),
        name="sample_merge_route",
    )(xs, z, s_a, comb, wa, wb, wo, g2, wr, br, ltri, counts)


def _dispatch_kernel(dest_ref, h_ref, xs_in_ref, xs_ref, sem):
    del xs_in_ref
    rows = h_ref.shape[0]
    base = pl.program_id(0) * (2 * rows)

    def row_copy(t, slot):
        d = dest_ref[base + 2 * t + slot]
        return pltpu.make_async_copy(h_ref.at[pl.ds(t, 1)], xs_ref.at[pl.ds(d, 1)], sem)

    def start(t, c):
        row_copy(t, 0).start()
        row_copy(t, 1).start()
        return c

    lax.fori_loop(0, rows, start, 0, unroll=ROW_DMA_UNROLL)
    for _ in range(2):
        pltpu.make_async_copy(h_ref, xs_ref.at[pl.ds(0, rows)], sem).wait()


def _dispatch(dest, h2p, xs, rows):
    n = h2p.shape[0]
    return pl.pallas_call(
        _dispatch_kernel,
        grid_spec=pltpu.PrefetchScalarGridSpec(
            num_scalar_prefetch=1,
            grid=(n // rows,),
            in_specs=[pl.BlockSpec((rows, HALF_D), lambda i, dest: (i, 0)), pl.BlockSpec(memory_space=pl.ANY)],
            out_specs=pl.BlockSpec(memory_space=pl.ANY),
            scratch_shapes=[pltpu.SemaphoreType.DMA(())],
        ),
        out_shape=jax.ShapeDtypeStruct(xs.shape, xs.dtype),
        input_output_aliases={2: 0},
        compiler_params=_cparams("arbitrary"),
        name=f"moe_dispatch_{rows}",
    )(dest, h2p, xs)


def _ffn_kernel(be_ref, nused_ref, xs_ref, wg_ref, wu_ref, wd_ref, ys_ref, wg_b, wu_b, wd_b):
    i = pl.program_id(0)
    new_expert = jnp.logical_or(i == 0, be_ref[i] != be_ref[jnp.maximum(i - 1, 0)])

    @pl.when(new_expert)
    def _():
        wg_b[...] = wg_ref[0].astype(BF16)
        wu_b[...] = wu_ref[0].astype(BF16)
        wd_b[...] = wd_ref[0].astype(BF16)

    @pl.when(i < nused_ref[0])
    def _():
        lo, hi = _unpack_bf16_pair(xs_ref[...])
        gate = (jnp.dot(lo, wg_b[:HALF_D, :], preferred_element_type=F32)
                + jnp.dot(hi, wg_b[HALF_D:, :], preferred_element_type=F32))
        up = (jnp.dot(lo, wu_b[:HALF_D, :], preferred_element_type=F32)
              + jnp.dot(hi, wu_b[HALF_D:, :], preferred_element_type=F32))
        hid = (jax.nn.silu(gate) * up).astype(BF16)
        ys_ref[...] = jnp.dot(hid, wd_b[...], preferred_element_type=F32)

    @pl.when(i >= nused_ref[0])
    def _():
        ys_ref[...] = jnp.zeros_like(ys_ref)


def _expert_ffn(block_e, nused, xs, w_gate, w_up, w_down):
    return pl.pallas_call(
        _ffn_kernel,
        grid_spec=pltpu.PrefetchScalarGridSpec(
            num_scalar_prefetch=2,
            grid=(MOE_NB,),
            in_specs=[pl.BlockSpec((MOE_TB, HALF_D), lambda i, be, nu: (i, 0)),
                      pl.BlockSpec((1, D_MODEL, D_EXPERT), lambda i, be, nu: (be[i], 0, 0)),
                      pl.BlockSpec((1, D_MODEL, D_EXPERT), lambda i, be, nu: (be[i], 0, 0)),
                      pl.BlockSpec((1, D_EXPERT, D_MODEL), lambda i, be, nu: (be[i], 0, 0))],
            out_specs=pl.BlockSpec((MOE_TB, D_MODEL), lambda i, be, nu: (i, 0)),
            scratch_shapes=[pltpu.VMEM((D_MODEL, D_EXPERT), BF16), pltpu.VMEM((D_MODEL, D_EXPERT), BF16),
                            pltpu.VMEM((D_EXPERT, D_MODEL), BF16)],
        ),
        out_shape=jax.ShapeDtypeStruct((MOE_ROWS, D_MODEL), F32),
        compiler_params=_cparams("arbitrary"),
        name="moe_ffn",
    )(block_e, nused, xs, w_gate, w_up, w_down)


def _combine_kernel(dest_ref, x1_ref, rt_ref, gf_ref, ys_ref, y_ref, ya, yb, sem):
    rows = x1_ref.shape[0]
    base = pl.program_id(0) * (2 * rows)

    def row_copy(t, slot, buf):
        d = dest_ref[base + 2 * t + slot]
        return pltpu.make_async_copy(ys_ref.at[pl.ds(d, 1)], buf.at[pl.ds(t, 1)], sem)

    def start(t, c):
        row_copy(t, 0, ya).start()
        row_copy(t, 1, yb).start()
        return c

    lax.fori_loop(0, rows, start, 0, unroll=ROW_DMA_UNROLL)
    pltpu.make_async_copy(ys_ref.at[pl.ds(0, rows)], ya, sem).wait()
    pltpu.make_async_copy(ys_ref.at[pl.ds(0, rows)], yb, sem).wait()
    x2 =x1_ref[...] + rt_ref[:, 4:5] * ya[...] + rt_ref[:, 5:6] * yb[...]
    y_ref[...] = _rms(x2, gf_ref[...])


def _combine(dest, x1, rt, gf, ys, rows):
    n = x1.shape[0]
    return pl.pallas_call(
        _combine_kernel,
        grid_spec=pltpu.PrefetchScalarGridSpec(
            num_scalar_prefetch=1,
            grid=(n // rows,),
            in_specs=[pl.BlockSpec((rows, D_MODEL), lambda i, dest: (i, 0)),
                      pl.BlockSpec((rows, LANES), lambda i, dest: (i, 0)),
                      pl.BlockSpec((1, D_MODEL), lambda i, dest: (0, 0)),
                      pl.BlockSpec(memory_space=pl.ANY)],
            out_specs=pl.BlockSpec((rows, D_MODEL), lambda i, dest: (i, 0)),
            scratch_shapes=[pltpu.VMEM((rows, D_MODEL), F32), pltpu.VMEM((rows, D_MODEL), F32),
                            pltpu.SemaphoreType.DMA(())],
        ),
        out_shape=jax.ShapeDtypeStruct((n, D_MODEL), F32),
        compiler_params=_cparams("arbitrary"),
        name=f"moe_combine_{rows}",
    )(dest, x1, rt, gf, ys)


def _rope_tables(pos):
    half = HEAD_DIM // 2
    inv = 1.0 / (ROPE_THETA ** (jnp.arange(half, dtype=F32) * (2.0 / HEAD_DIM)))
    ang = pos.astype(F32)[:, None] * inv[None, :]
    cos, sin = jnp.cos(ang), jnp.sin(ang)
    return jnp.concatenate([cos, cos, cos, cos], axis=1), jnp.concatenate([-sin, sin, -sin, sin], axis=1)


def _slot_dest(rt, pstarts):
    e = rt[:, 0:2].astype(jnp.int32)
    pos = rt[:, 2:4].astype(jnp.int32)
    onehot = e[:, :, None] == jnp.arange(N_EXPERTS, dtype=jnp.int32)[None, None, :]
    return (jnp.sum(jnp.where(onehot, pstarts[None, None, :], 0), axis=-1) + pos).reshape(-1)


def kernel(x_prompt, x_sample, cache_kv_g0, cache_kv_g1, cache_kv_g2, norm1_g, w_in, b_in, a_ln_g, a_ln_b, w_spatial, b_spatial, w_a_proj, w_b_proj, w_o, norm2_g, w_group_router, b_group_router, w_expert_router, b_expert_router, w_gate, w_up, w_down, final_norm_g):
    x2 = x_prompt.reshape(T_PROMPT, D_MODEL)
    xs = x_sample.reshape(DEC_BATCH, D_MODEL)
    g1 = norm1_g[0][None, :]
    g2 = norm2_g[0][None, :]
    gf = final_norm_g[None, :]
    b_in2 = b_in[0][None, :]
    lng, lnb = a_ln_g[0][None, :], a_ln_b[0][None, :]

    causal = jnp.tril(jnp.ones((CHUNK, CHUNK), dtype=bool))
    ws_tril = jnp.where(causal[None], w_spatial[0], 0.0)
    wsp = jnp.concatenate([ws_tril[0::2], ws_tril[1::2]], axis=2).astype(BF16)
    bsp = jnp.repeat(b_spatial[0].T, A_GROUP_DIM, axis=1)
    ws0 = jnp.repeat(ws_tril[:, 0, 0].astype(BF16).astype(F32), A_GROUP_DIM)[None, :]
    bs0 = jnp.repeat(b_spatial[0][:, 0], A_GROUP_DIM)[None, :]
    w_router = jnp.zeros((D_MODEL, LANES), F32)
    w_router = w_router.at[:, :N_EXPERT_GROUPS].set(w_group_router[0])
    w_router = w_router.at[:, N_EXPERT_GROUPS:N_EXPERT_GROUPS + N_EXPERTS].set(w_expert_router[0])
    b_router = jnp.zeros((1, LANES), F32)
    b_router = b_router.at[0, :N_EXPERT_GROUPS].set(b_group_router[0])
    b_router = b_router.at[0, N_EXPERT_GROUPS:N_EXPERT_GROUPS + N_EXPERTS].set(b_expert_router[0])
    w_router = w_router.astype(BF16)
    ltri = jnp.tril(jnp.ones((TM, TM), BF16), -1)
    w_in_b, w_a_b, w_b_b, w_o_b = (w[0].astype(BF16) for w in (w_in, w_a_proj, w_b_proj, w_o))

    cos_p, sin_p = _rope_tables(jnp.arange(SEQ, dtype=jnp.int32))
    cos_s, sin_s = _rope_tables(jnp.full((1,), PAST_LEN, jnp.int32))

    uv, qkv0, qkv1, qkv2, gates, kvp0, kvp1, kvp2 = _inproj(
        x2, g1, w_in_b, b_in2, lng, lnb, cos_p, sin_p)
    ob = _attention((qkv0.reshape(BATCH, 1, SEQ, 3 * GROUP_W), qkv1, qkv2)).reshape(T_PROMPT, GROUP_W)
    x1_p, h2p_p, rt_p, counts_p = _merge(
        x2, uv, gates, ob, wsp, bsp, w_a_b, w_b_b, w_o_b, g2, w_router, b_router, ltri)

    z_s = _sample_inproj(xs, g1, w_in_b, b_in2)
    vrow, sa_s, qr_s, kr_s, kvs0, kvs1, kvs2 = _sample_pre(z_s, lng, lnb, ws0, bs0, cos_s, sin_s)
    dim_major = lambda t: t.reshape(DEC_BATCH, 3, HEADS, HEAD_DIM).transpose(0, 1, 3, 2)
    caches = [c.transpose(0, 1, 3, 4, 5, 2).reshape(DEC_BATCH, 2, HEADS, HEAD_DIM, c.shape[2])
              for c in (cache_kv_g0, cache_kv_g1, cache_kv_g2)]
    comb_s = _sample_attn(dim_major(qr_s), dim_major(kr_s), dim_major(z_s[:, COL_VB:COL_VB + QK_W]), caches)
    comb_s = comb_s.transpose(0, 2, 1).reshape(DEC_BATCH, GROUP_W)
    x1_s, h2p_s, rt_s, counts = _sample_merge(
        xs, z_s, sa_s, comb_s, w_a_b, w_b_b, w_o_b, g2, w_router, b_router,
        ltri[:DEC_BATCH, :DEC_BATCH], counts_p)

    cnt = counts[0, :N_EXPERTS].astype(jnp.int32)
    padded = (cnt + MOE_TB - 1) // MOE_TB * MOE_TB
    pends = jnp.cumsum(padded)
    pstarts = pends - padded
    block_starts = jnp.arange(MOE_NB, dtype=jnp.int32) * MOE_TB
    block_e = jnp.minimum(jnp.sum((pends[None, :] <= block_starts[:, None]).astype(jnp.int32), axis=1),
                          N_EXPERTS - 1)
    nused = (pends[-1:] // MOE_TB).astype(jnp.int32)
    dest_p = _slot_dest(rt_p, pstarts)
    dest_s = _slot_dest(rt_s, pstarts)

    rows = jnp.zeros((MOE_ROWS, HALF_D), U32)
    rows = _dispatch(dest_p, h2p_p, rows, TM)
    rows = _dispatch(dest_s, h2p_s, rows, DEC_BATCH)
    ys = _expert_ffn(block_e, nused, rows, w_gate[0], w_up[0], w_down[0])
    y_p = _combine(dest_p, x1_p, rt_p, gf, ys, TM)
    y_s = _combine(dest_s, x1_s, rt_s, gf, ys, DEC_BATCH)

    kv_shape = lambda n, w: (1, n, w, 2, HEADS, HEAD_DIM)
    window_rows = lambda t: t.reshape(1, BATCH, 2, HEADS, HEAD_DIM, t.shape[2]).transpose(0, 1, 5, 2, 3, 4)
    return (y_p.reshape(BATCH, SEQ, D_MODEL), y_s.reshape(DEC_BATCH, 1, D_MODEL),
            window_rows(kvp0), window_rows(kvp1), window_rows(kvp2),
            kvs0.reshape(kv_shape(DEC_BATCH, 1)), kvs1.reshape(kv_shape(DEC_BATCH, 1)),
            kvs2.reshape(kv_shape(DEC_BATCH, 1)), vrow.reshape(1, DEC_BATCH, 1, A_WIDTH))
```

```python
import functools

import jax
import jax.numpy as jnp
from jax import lax
from jax.experimental import pallas as pl
from jax.experimental.pallas import tpu as pltpu

F32 = jnp.float32
BF16 = jnp.bfloat16

D_MODEL = 1024
BATCH = 2
SEQ = 8192
DEC_BATCH = 32
PAST_LEN = 8192
CHUNK = 128
A_GROUPS = 8
A_GROUP_DIM = 64
A_WIDTH = 512
HEAD_DIM = 64
HEADS = 4
GROUP_W = HEADS * HEAD_DIM
DILATIONS = (1, 4, 16)
SPAN = 128
QK_W = 768
IN_WIDTH = 5376
COL_U, COL_V, COL_Q, COL_K, COL_VB, COL_GA, COL_GB = 0, 512, 1024, 1792, 2560, 3328, 4352
N_EXPERT_GROUPS = 4
EXPERTS_PER_GROUP = 8
N_EXPERTS = 32
D_EXPERT = 512
ROPE_THETA = 10000.0
EPS = 1e-6

LANES = 128
T_PROMPT = BATCH * SEQ
TM = 512
TILES_PER_SEQ = SEQ // TM
WINDOWS = tuple(min(SPAN * d, SEQ) for d in DILATIONS)
TA = SPAN * max(DILATIONS)
MOE_TB = 256
ROW_DMA_UNROLL = 8
N_SLOTS = 2 * (T_PROMPT + DEC_BATCH)
MOE_NB = -(-N_SLOTS // MOE_TB) + N_EXPERTS
MOE_ROWS = MOE_NB * MOE_TB
ROW_TILE = D_MODEL // LANES
T_ALL = T_PROMPT + DEC_BATCH
NEG = -1e30
VMEM_LIMIT = 56 * 1024 * 1024


def _cparams(*sem):
    return pltpu.CompilerParams(dimension_semantics=sem, vmem_limit_bytes=VMEM_LIMIT)


def _rms(x, g):
    return x * lax.rsqrt(jnp.mean(x * x, axis=-1, keepdims=True) + EPS) * g


def _layer_norm(x, g, b):
    mu = jnp.mean(x, axis=-1, keepdims=True)
    xc = x - mu
    var = jnp.mean(xc * xc, axis=-1, keepdims=True)
    return xc * lax.rsqrt(var + EPS) * g + b


def _rope_chunk(x, cos, sin_signed):
    lane = lax.broadcasted_iota(jnp.int32, x.shape, 1)
    first_half = (lane % HEAD_DIM) < (HEAD_DIM // 2)
    swapped = jnp.where(first_half, pltpu.roll(x, LANES - HEAD_DIM // 2, 1), pltpu.roll(x, HEAD_DIM // 2, 1))
    return x * cos + swapped * sin_signed


def _store_token_tiles(ref, h):
    rows = h.shape[0]
    for c in range(ROW_TILE):
        ref[pl.ds(c, rows, stride=ROW_TILE), :] = h[:, c * LANES:(c + 1) * LANES]


def _load_token_tiles(ref, rows):
    return jnp.concatenate([ref[pl.ds(c, rows, stride=ROW_TILE), :] for c in range(ROW_TILE)], axis=1)


def _route(logits, prior_counts, ltri):
    rows = logits.shape[0]
    lane = lax.broadcasted_iota(jnp.int32, (rows, LANES), 1).astype(F32)
    is_g = lane < N_EXPERT_GROUPS
    gl = jnp.where(is_g, logits, NEG)
    gmax = jnp.max(gl, axis=1, keepdims=True)
    grp = jnp.min(jnp.where(gl == gmax, lane, float(LANES)), axis=1, keepdims=True)
    p_grp = 1.0 / jnp.sum(jnp.where(is_g, jnp.exp(gl - gmax), 0.0), axis=1, keepdims=True)
    lo = N_EXPERT_GROUPS + grp * EXPERTS_PER_GROUP
    el = jnp.where((lane >= lo) & (lane < lo + EXPERTS_PER_GROUP), logits, NEG)
    v1 = jnp.max(el, axis=1, keepdims=True)
    i1 = jnp.min(jnp.where(el == v1, lane, float(LANES)), axis=1, keepdims=True)
    el2 = jnp.where(lane == i1, NEG, el)
    v2 = jnp.max(el2, axis=1, keepdims=True)
    i2 = jnp.min(jnp.where(el2 == v2, lane, float(LANES)), axis=1, keepdims=True)
    t = jnp.exp(v2 - v1)
    gate1 = p_grp / (1.0 + t)
    gate2 = p_grp * t / (1.0 + t)
    e1 = i1 - N_EXPERT_GROUPS
    e2 = i2 - N_EXPERT_GROUPS
    hit1 = lane == e1
    hit2 = lane == e2
    onehot = jnp.where(hit1 | hit2, 1.0, 0.0)
    rank = jnp.dot(ltri, onehot.astype(BF16), preferred_element_type=F32) + prior_counts
    pos1 = jnp.sum(jnp.where(hit1, rank, 0.0), axis=1, keepdims=True)
    pos2 = jnp.sum(jnp.where(hit2, rank, 0.0), axis=1, keepdims=True)
    slab = jnp.where(lane == 0, e1, jnp.where(lane == 1, e2, jnp.where(lane == 2, pos1, jnp.where(
        lane == 3, pos2, jnp.where(lane == 4, gate1, jnp.where(lane == 5, gate2, 0.0))))))
    return slab, prior_counts + jnp.sum(onehot, axis=0, keepdims=True)


def _inproj_kernel(x_ref, g1_ref, w_ref, b_ref, lng_ref, lnb_ref, cos_ref, sin_ref,
                   uv_ref, qkv0_ref, qkv1_ref, qkv2_ref, gates_ref, kv0_ref, kv1_ref, kv2_ref, de_ref):
    tile_in_seq = pl.program_id(0) % TILES_PER_SEQ
    hb = _rms(x_ref[...], g1_ref[...]).astype(BF16)

    def seg(lo, width):
        return jnp.dot(hb, w_ref[:, lo:lo + width], preferred_element_type=F32) + b_ref[:, lo:lo + width]

    uv_ref[:, :A_WIDTH] = jax.nn.gelu(seg(COL_U, A_WIDTH)).astype(BF16)
    uv_ref[:, A_WIDTH:] = _layer_norm(jax.nn.gelu(seg(COL_V, A_WIDTH)), lng_ref[...], lnb_ref[...]).astype(BF16)
    gates_ref[:, :D_MODEL] = jax.nn.sigmoid(seg(COL_GA, D_MODEL)).astype(BF16)
    gates_ref[:, D_MODEL:] = jax.nn.sigmoid(seg(COL_GB, D_MODEL)).astype(BF16)

    cos = cos_ref[...]
    sin = sin_ref[...]
    qkv_refs = (qkv0_ref, qkv1_ref, qkv2_ref)
    kv_refs = (kv0_ref, kv1_ref, kv2_ref)
    for g, d in enumerate(DILATIONS):
        q = seg(COL_Q + g * GROUP_W, GROUP_W)
        k = seg(COL_K + g * GROUP_W, GROUP_W)
        v = seg(COL_VB + g * GROUP_W, GROUP_W)
        chunks = GROUP_W // LANES
        for c in range(chunks):
            sl = slice(c * LANES, (c + 1) * LANES)
            de_ref[c] = _rope_chunk(q[:, sl], cos, sin) * (HEAD_DIM ** -0.5)
            de_ref[chunks + c] = _rope_chunk(k[:, sl], cos, sin)
            de_ref[2 * chunks + c] = v[:, sl]
        for c in range(3 * chunks):
            sl = slice(c * LANES, (c + 1) * LANES)
            if d == 1:
                qkv_refs[g][:, sl] = de_ref[c].astype(BF16)
            else:
                for r in range(d):
                    qkv_refs[g][0, r, :, sl] = de_ref[c, pl.ds(r, TM // d, stride=d), :].astype(BF16)
        kv_rows = kv_refs[g].shape[2]
        first = TILES_PER_SEQ - WINDOWS[g] // kv_rows

        @pl.when(tile_in_seq >= first)
        def _(kv_ref=kv_refs[g], kv_rows=kv_rows, chunks=chunks):
            for c in range(2 * chunks):
                kv_ref[0, c * LANES:(c + 1) * LANES, :] = de_ref[chunks + c, TM - kv_rows:, :].T


def _inproj(x2, g1, w_bf, b_in, lng, lnb, cos_t, sin_t):
    tiles_per_seq = TILES_PER_SEQ
    n_tiles = T_PROMPT // TM
    const = lambda i: (0, 0)

    def kv_spec(w):
        rows = min(w, TM)
        first = tiles_per_seq - w // rows
        return pl.BlockSpec((1, 2 * GROUP_W, rows),
                            lambda i: (i // tiles_per_seq, 0, jnp.maximum(i % tiles_per_seq - first, 0)))

    def regrouped_spec(d):
        return pl.BlockSpec((1, d, TM // d, 3 * GROUP_W), lambda i: (i // tiles_per_seq, 0, i % tiles_per_seq, 0))

    return pl.pallas_call(
        _inproj_kernel,
        grid=(n_tiles,),
        in_specs=[
            pl.BlockSpec((TM, D_MODEL), lambda i: (i, 0)),
            pl.BlockSpec((1, D_MODEL), const),
            pl.BlockSpec((D_MODEL, IN_WIDTH), const),
            pl.BlockSpec((1, IN_WIDTH), const),
            pl.BlockSpec((1, A_WIDTH), const),
            pl.BlockSpec((1, A_WIDTH), const),
            pl.BlockSpec((TM, LANES), lambda i: (i % tiles_per_seq, 0)),
            pl.BlockSpec((TM, LANES), lambda i: (i % tiles_per_seq, 0)),
        ],
        out_specs=[
            pl.BlockSpec((TM, 2 * A_WIDTH), lambda i: (i, 0)),
            pl.BlockSpec((TM, 3 * GROUP_W), lambda i: (i, 0)),
            regrouped_spec(DILATIONS[1]),
            regrouped_spec(DILATIONS[2]),
            pl.BlockSpec((TM, 2 * D_MODEL), lambda i: (i, 0)),
            kv_spec(WINDOWS[0]), kv_spec(WINDOWS[1]), kv_spec(WINDOWS[2]),
        ],
        out_shape=[
            jax.ShapeDtypeStruct((T_PROMPT, 2 * A_WIDTH), BF16),
            jax.ShapeDtypeStruct((T_PROMPT, 3 * GROUP_W), BF16),
            jax.ShapeDtypeStruct((BATCH, DILATIONS[1], SEQ // DILATIONS[1], 3 * GROUP_W), BF16),
            jax.ShapeDtypeStruct((BATCH, DILATIONS[2], SEQ // DILATIONS[2], 3 * GROUP_W), BF16),
            jax.ShapeDtypeStruct((T_PROMPT, 2 * D_MODEL), BF16),
            jax.ShapeDtypeStruct((BATCH, 2 * GROUP_W, WINDOWS[0]), F32),
            jax.ShapeDtypeStruct((BATCH, 2 * GROUP_W, WINDOWS[1]), F32),
            jax.ShapeDtypeStruct((BATCH, 2 * GROUP_W, WINDOWS[2]), F32),
        ],
        scratch_shapes=[pltpu.VMEM((3 * GROUP_W // LANES, TM, LANES), F32)],
        compiler_params=_cparams("arbitrary"),
        name="inproj",
    )(x2, g1, w_bf, b_in, lng, lnb, cos_t, sin_t)


def _attn_kernel(c0_ref, p0_ref, c1_ref, p1_ref, c2_ref, p2_ref, o_ref, acc_ref, m_ref, l_ref):
    n = pl.program_id(1)
    acc_ref[...] = jnp.zeros_like(acc_ref)
    l_ref[...] = jnp.zeros_like(l_ref)
    m_ref[...] = jnp.full_like(m_ref, NEG)

    qi = lax.broadcasted_iota(jnp.int32, (SPAN, 2 * SPAN), 0)
    ki = lax.broadcasted_iota(jnp.int32, (SPAN, 2 * SPAN), 1)
    band = (ki >= qi) & (ki <= qi + SPAN)
    band_first = band & ((ki >= SPAN) | (n > 0))
    q_head = lax.broadcasted_iota(jnp.int32, (SPAN, GROUP_W), 1) // HEAD_DIM
    kv_head = lax.broadcasted_iota(jnp.int32, (2 * SPAN, GROUP_W), 1) // HEAD_DIM
    contract_last = (((1,), (1,)), ((), ()))
    k_cols = slice(GROUP_W, 2 * GROUP_W)
    v_cols = slice(2 * GROUP_W, 3 * GROUP_W)

    def attend(q, keys, vals, mask, tok_rows):
        scores = []
        for h in range(HEADS):
            qh = jnp.where(q_head == h, q, jnp.zeros_like(q))
            s = lax.dot_general(qh, keys, contract_last, preferred_element_type=F32)
            scores.append(jnp.where(mask, s, NEG))
        s = jnp.concatenate(scores, axis=0)
        m = jnp.max(s, axis=1, keepdims=True)
        p = jnp.exp(s - m)
        l = jnp.sum(p, axis=1, keepdims=True)
        pb = p.astype(BF16)
        acc = jnp.zeros((SPAN, GROUP_W), F32)
        m_b = jnp.zeros((SPAN, GROUP_W), F32)
        l_b = jnp.zeros((SPAN, GROUP_W), F32)
        for h in range(HEADS):
            rows = slice(h * SPAN, (h + 1) * SPAN)
            vh = jnp.where(kv_head == h, vals, jnp.zeros_like(vals))
            acc = acc + jnp.dot(pb[rows], vh, preferred_element_type=F32)
            m_b = jnp.where(q_head == h, m[rows], m_b)
            l_b = jnp.where(q_head == h, l[rows], l_b)
        for c in range(GROUP_W // LANES):
            sl = slice(c * LANES, (c + 1) * LANES)
            m_old = m_ref[c, tok_rows, :]
            m_new = jnp.maximum(m_old, m_b[:, sl])
            a_old = jnp.exp(m_old - m_new)
            a_blk = jnp.exp(m_b[:, sl] - m_new)
            l_ref[c, tok_rows, :] = a_old * l_ref[c, tok_rows, :] + a_blk * l_b[:, sl]
            acc_ref[c, tok_rows, :] = a_old * acc_ref[c, tok_rows, :] + a_blk * acc[:, sl]
            m_ref[c, tok_rows, :] = m_new

    def stream(c_ref, p_ref, r, d):
        n_blocks = c_ref.shape[2] // SPAN

        def tok_rows(j):
            start = j * (SPAN * d) + r
            return pl.ds(start, SPAN) if d == 1 else pl.ds(start, SPAN, stride=d)

        keys = jnp.concatenate([p_ref[0, r, :, k_cols], c_ref[0, r, :SPAN, k_cols]], axis=0)
        vals = jnp.concatenate([p_ref[0, r, :, v_cols], c_ref[0, r, :SPAN, v_cols]], axis=0)
        attend(c_ref[0, r, :SPAN, :GROUP_W], keys, vals, band_first, tok_rows(0))

        def later_block(j, carry):
            q_rows = pl.ds(pl.multiple_of(j * SPAN, SPAN), SPAN)
            kv_rows = pl.ds(pl.multiple_of((j - 1) * SPAN, SPAN), 2 * SPAN)
            attend(c_ref[0, r, q_rows, :GROUP_W], c_ref[0, r, kv_rows, k_cols], c_ref[0, r, kv_rows, v_cols],
                   band, tok_rows(j))
            return carry

        if n_blocks > 1:
            lax.fori_loop(1, n_blocks, later_block, 0)

    for (c_ref, p_ref), d in zip(((c0_ref, p0_ref), (c1_ref, p1_ref), (c2_ref, p2_ref)), DILATIONS):
        if d == 1:
            stream(c_ref, p_ref, 0, d)
        else:
            lax.fori_loop(0, d, lambda r, carry, c_ref=c_ref, p_ref=p_ref, d=d: (stream(c_ref, p_ref, r, d), carry)[1], 0)

    for c in range(GROUP_W // LANES):
        o_ref[0, :, c * LANES:(c + 1) * LANES] = (acc_ref[c] / l_ref[c]).astype(BF16)


def _attention(qkv_by_group):
    in_specs, args = [], []
    for qkv, d in zip(qkv_by_group, DILATIONS):
        rows = TA // d
        blocks_per_tile = rows // SPAN
        in_specs.append(pl.BlockSpec((1, d, rows, 3 * GROUP_W), lambda b, n: (b, 0, n, 0)))
        in_specs.append(pl.BlockSpec((1, d, SPAN, 3 * GROUP_W),
                                     lambda b, n, k=blocks_per_tile: (b, 0, jnp.maximum(n * k - 1, 0), 0)))
        args += [qkv, qkv]
    return pl.pallas_call(
        _attn_kernel,
        grid=(BATCH, SEQ // TA),
        in_specs=in_specs,
        out_specs=pl.BlockSpec((1, TA, GROUP_W), lambda b, n: (b, n, 0)),
        out_shape=jax.ShapeDtypeStruct((BATCH, SEQ, GROUP_W), BF16),
        scratch_shapes=[pltpu.VMEM((GROUP_W // LANES, TA, LANES), F32)] * 3,
        compiler_params=_cparams("arbitrary", "arbitrary"),
        name="attn",
    )(*args)


def _merge_kernel(x_ref, uv_ref, gates_ref, ob_ref,
                  wsp_ref, bsp_ref, wa_ref, wb_ref, wo_ref, g2_ref, wr_ref, br_ref, ltri_ref,
                  x1_ref, h2t_ref, rt_ref, cnt_ref, run_ref):
    @pl.when(pl.program_id(0) == 0)
    def _():
        run_ref[...] = jnp.zeros_like(run_ref)

    lane = lax.broadcasted_iota(jnp.int32, (CHUNK, LANES), 1)
    left = lane < A_GROUP_DIM
    zero = jnp.zeros((CHUNK, LANES), BF16)
    sa_chunks = []
    for c in range(TM // CHUNK):
        rows = slice(c * CHUNK, (c + 1) * CHUNK)
        pairs = []
        for p in range(A_GROUPS // 2):
            vp = uv_ref[rows, A_WIDTH + p * LANES:A_WIDTH + (p + 1) * LANES]
            rhs = jnp.concatenate([jnp.where(left, vp, zero), jnp.where(left, zero, vp)], axis=0)
            pairs.append(jnp.dot(wsp_ref[p], rhs, preferred_element_type=F32))
        mixed = jnp.concatenate(pairs, axis=1) + bsp_ref[...]
        sa_chunks.append((uv_ref[rows, :A_WIDTH].astype(F32) * mixed).astype(BF16))
    s_a = jnp.concatenate(sa_chunks, axis=0)

    a = jnp.dot(s_a, wa_ref[...], preferred_element_type=F32)
    b = jnp.dot(ob_ref[...], wb_ref[...], preferred_element_type=F32)
    merged = gates_ref[:, :D_MODEL].astype(F32) * a + gates_ref[:, D_MODEL:].astype(F32) * b
    x1 = x_ref[...] + jnp.dot(merged.astype(BF16), wo_ref[...], preferred_element_type=F32)
    x1_ref[...] = x1

    h2 = _rms(x1, g2_ref[...])
    _store_token_tiles(h2t_ref, h2)
    logits = jnp.dot(h2.astype(BF16), wr_ref[...], preferred_element_type=F32) + br_ref[...]
    slab, counts = _route(logits, run_ref[...], ltri_ref[...])
    rt_ref[...] = slab
    run_ref[...] = counts
    cnt_ref[...] = counts


def _merge(x2, uv, gates, ob, wsp, bsp, wa, wb, wo, g2, wr, br, ltri):
    n_tiles = T_PROMPT // TM
    tile = lambda w: pl.BlockSpec((TM, w), lambda i: (i, 0))
    full = lambda a: pl.BlockSpec(a.shape, lambda i: (0,) * a.ndim)
    return pl.pallas_call(
        _merge_kernel,
        grid=(n_tiles,),
        in_specs=[tile(D_MODEL), tile(2 * A_WIDTH), tile(2 * D_MODEL), tile(GROUP_W),
                  full(wsp), full(bsp), full(wa), full(wb), full(wo), full(g2), full(wr), full(br),
                  full(ltri)],
        out_specs=[tile(D_MODEL), pl.BlockSpec((TM * ROW_TILE, LANES), lambda i: (i, 0)), tile(LANES),
                   pl.BlockSpec((1, LANES), lambda i: (0, 0))],
        out_shape=[
            jax.ShapeDtypeStruct((T_PROMPT, D_MODEL), F32),
            jax.ShapeDtypeStruct((T_ALL * ROW_TILE, LANES), F32),
            jax.ShapeDtypeStruct((T_PROMPT, LANES), F32),
            jax.ShapeDtypeStruct((1, LANES), F32),
        ],
        scratch_shapes=[pltpu.VMEM((1, LANES), F32)],
        compiler_params=_cparams("arbitrary"),
        name="merge_route",
    )(x2, uv, gates, ob, wsp, bsp, wa, wb, wo, g2, wr, br, ltri)


SAMPLE_COLS = 768


def _sample_inproj_kernel(x_ref, g1_ref, w_ref, b_ref, z_ref):
    hb = _rms(x_ref[...], g1_ref[...]).astype(BF16)
    z_ref[...] = jnp.dot(hb, w_ref[...], preferred_element_type=F32) + b_ref[...]


def _sample_inproj(xs, g1, w_in, b_in):
    const = lambda j: (0, 0)
    return pl.pallas_call(
        _sample_inproj_kernel,
        grid=(IN_WIDTH // SAMPLE_COLS,),
        in_specs=[pl.BlockSpec((DEC_BATCH, D_MODEL), const), pl.BlockSpec((1, D_MODEL), const),
                  pl.BlockSpec((D_MODEL, SAMPLE_COLS), lambda j: (0, j)),
                  pl.BlockSpec((1, SAMPLE_COLS), lambda j: (0, j))],
        out_specs=pl.BlockSpec((DEC_BATCH, SAMPLE_COLS), lambda j: (0, j)),
        out_shape=jax.ShapeDtypeStruct((DEC_BATCH, IN_WIDTH), F32),
        compiler_params=_cparams("arbitrary"),
        name="sample_inproj",
    )(xs, g1, w_in, b_in)


def _sample_pre_kernel(z_ref, lng_ref, lnb_ref, ws0_ref, bs0_ref, cos_ref, sin_ref,
                       vrow_ref, sa_ref, qr_ref, kr_ref, kv0_ref, kv1_ref, kv2_ref):
    u = jax.nn.gelu(z_ref[:, COL_U:COL_U + A_WIDTH])
    va = _layer_norm(jax.nn.gelu(z_ref[:, COL_V:COL_V + A_WIDTH]), lng_ref[...], lnb_ref[...])
    vrow_ref[...] = va
    sa_ref[...] = u * (ws0_ref[...] * va.astype(BF16).astype(F32) + bs0_ref[...])
    cos, sin = cos_ref[...], sin_ref[...]
    for g, kv_ref in enumerate((kv0_ref, kv1_ref, kv2_ref)):
        for c in range(GROUP_W // LANES):
            off = g * GROUP_W + c * LANES
            qr_ref[:, off:off + LANES] = (_rope_chunk(z_ref[:, COL_Q + off:COL_Q + off + LANES], cos, sin)
                                          * (HEAD_DIM ** -0.5))
            kr = _rope_chunk(z_ref[:, COL_K + off:COL_K + off + LANES], cos, sin)
            kr_ref[:, off:off + LANES] = kr
            kv_ref[:, c * LANES:(c + 1) * LANES] = kr
        kv_ref[:, GROUP_W:] = z_ref[:, COL_VB + g * GROUP_W:COL_VB + (g + 1) * GROUP_W]


def _sample_pre(z, lng, lnb, ws0, bs0, cos_s, sin_s):
    row = lambda w: jax.ShapeDtypeStruct((DEC_BATCH, w), F32)
    return pl.pallas_call(
        _sample_pre_kernel,
        out_shape=[row(A_WIDTH), row(A_WIDTH), row(QK_W), row(QK_W),
                   row(2 * GROUP_W), row(2 * GROUP_W), row(2 * GROUP_W)],
        compiler_params=pltpu.CompilerParams(vmem_limit_bytes=VMEM_LIMIT),
        name="sample_pre",
    )(z, lng, lnb, ws0, bs0, cos_s, sin_s)


def _sample_attn_kernel(q_ref, k_ref, v_ref, c0_ref, c1_ref, c2_ref, comb_ref):
    as_operand = lambda t: t.astype(BF16).astype(F32)
    for h in range(HEADS):
        head = slice(h, h + 1)
        outs, lses = [], []
        for g, (c_ref, d) in enumerate(zip((c0_ref, c1_ref, c2_ref), DILATIONS)):
            qc, kn, vn = (as_operand(r[0, g, :, head]) for r in (q_ref, k_ref, v_ref))
            s = jnp.sum(as_operand(c_ref[0, 0, h]) * qc, axis=0, keepdims=True)
            if d > 1:
                pos = lax.broadcasted_iota(jnp.int32, s.shape, 1)
                s = jnp.where(pos % d == 0, s, NEG)
            s_n = jnp.sum(qc * kn, axis=0, keepdims=True)
            m = jnp.maximum(jnp.max(s, axis=1, keepdims=True), s_n)
            lse = m + jnp.log(jnp.sum(jnp.exp(s - m), axis=1, keepdims=True) + jnp.exp(s_n - m))
            p = as_operand(jnp.exp(s - lse))
            p_n = as_operand(jnp.exp(s_n - lse))
            outs.append(jnp.sum(as_operand(c_ref[0, 1, h]) * p, axis=1, keepdims=True) + p_n * vn)
            lses.append(lse)
        mx = jnp.maximum(jnp.maximum(lses[0], lses[1]), lses[2])
        ws = [jnp.exp(t - mx) for t in lses]
        tot = ws[0] + ws[1] + ws[2]
        comb_ref[0, :, head] = (ws[0] * outs[0] + ws[1] * outs[1] + ws[2] * outs[2]) / tot


def _sample_attn(qt, kt, vt, caches):
    new_spec = pl.BlockSpec((1, 3, HEAD_DIM, HEADS), lambda i: (i, 0, 0, 0))
    cache_spec = lambda c: pl.BlockSpec((1,) + c.shape[1:], lambda i: (i, 0, 0, 0, 0))
    return pl.pallas_call(
        _sample_attn_kernel,
        grid=(DEC_BATCH,),
        in_specs=[new_spec, new_spec, new_spec, cache_spec(caches[0]), cache_spec(caches[1]), cache_spec(caches[2])],
        out_specs=pl.BlockSpec((1, HEAD_DIM, HEADS), lambda i: (i, 0, 0)),
        out_shape=jax.ShapeDtypeStruct((DEC_BATCH, HEAD_DIM, HEADS), F32),
        compiler_params=_cparams("arbitrary"),
        name="sample_attn",
    )(qt, kt, vt, caches[0], caches[1], caches[2])


def _sample_merge_kernel(x_ref, z_ref, sa_ref, comb_ref, wa_ref, wb_ref, wo_ref, g2_ref, wr_ref, br_ref,
                         ltri_ref, cnt_in_ref, h2t_in_ref, x1_ref, h2t_ref, rt_ref, cnt_ref):
    del h2t_in_ref
    dot = lambda p, q: jnp.dot(p.astype(BF16), q, preferred_element_type=F32)
    a = dot(sa_ref[...], wa_ref[...])
    b = dot(comb_ref[...], wb_ref[...])
    merged = (jax.nn.sigmoid(z_ref[:, COL_GA:COL_GA + D_MODEL]) * a
              + jax.nn.sigmoid(z_ref[:, COL_GB:COL_GB + D_MODEL]) * b)
    x1 = x_ref[...] + dot(merged, wo_ref[...])
    x1_ref[...] = x1
    h2 = _rms(x1, g2_ref[...])
    _store_token_tiles(h2t_ref, h2)
    logits = dot(h2, wr_ref[...]) + br_ref[...]
    slab, counts = _route(logits, cnt_in_ref[...], ltri_ref[...])
    rt_ref[...] = slab
    cnt_ref[...] = counts


def _sample_merge(xs, z, s_a, comb, wa, wb, wo, g2, wr, br, ltri, counts, h2t):
    small = (xs, z, s_a, comb, wa, wb, wo, g2, wr, br, ltri, counts)
    full = lambda a: pl.BlockSpec(a.shape, lambda i: (0,) * a.ndim)
    sample_tiles = DEC_BATCH * ROW_TILE
    return pl.pallas_call(
        _sample_merge_kernel,
        grid=(1,),
        in_specs=[full(a) for a in small] + [pl.BlockSpec(memory_space=pl.ANY)],
        out_specs=[full(xs), pl.BlockSpec((sample_tiles, LANES), lambda i: (T_PROMPT * ROW_TILE // sample_tiles, 0)),
                   pl.BlockSpec((DEC_BATCH, LANES), lambda i: (0, 0)), pl.BlockSpec((1, LANES), lambda i: (0, 0))],
        out_shape=[jax.ShapeDtypeStruct((DEC_BATCH, D_MODEL), F32),
                   jax.ShapeDtypeStruct(h2t.shape, h2t.dtype),
                   jax.ShapeDtypeStruct((DEC_BATCH, LANES), F32),
                   jax.ShapeDtypeStruct((1, LANES), F32)],
        input_output_aliases={len(small): 1},
        compiler_params=_cparams("arbitrary"),
        name="sample_merge_route",
    )(*small, h2t)


def _row_tokens_kernel(dest_p_ref, dest_s_ref, tok_ref):
    def clear(r, c):
        tok_ref[r] = 0
        return c

    def put_prompt(t, c):
        tok_ref[dest_p_ref[2 * t]] = t
        tok_ref[dest_p_ref[2 * t + 1]] = t
        return c

    def put_sample(t, c):
        tok_ref[dest_s_ref[2 * t]] = T_PROMPT + t
        tok_ref[dest_s_ref[2 * t + 1]] = T_PROMPT + t
        return c

    lax.fori_loop(0, MOE_ROWS, clear, 0, unroll=16)
    lax.fori_loop(0, T_PROMPT, put_prompt, 0, unroll=8)
    lax.fori_loop(0, DEC_BATCH, put_sample, 0, unroll=8)


def _row_tokens(dest_p, dest_s):
    smem = pl.BlockSpec(memory_space=pltpu.SMEM)
    return pl.pallas_call(
        _row_tokens_kernel,
        in_specs=[smem, smem],
        out_specs=smem,
        out_shape=jax.ShapeDtypeStruct((MOE_ROWS,), jnp.int32),
        name="moe_row_tokens",
    )(dest_p, dest_s)


def _ffn_kernel(be_ref, nused_ref, tok_ref, h_ref, wg_ref, wu_ref, wd_ref, ys_ref, wg_b, wu_b, wd_b, xbuf, sems):
    i = pl.program_id(0)
    nused = nused_ref[0]
    block_tiles = MOE_TB * ROW_TILE

    def gather(block, buf):
        base = block * MOE_TB

        def start(r, c):
            src = pl.multiple_of(tok_ref[base + r] * ROW_TILE, ROW_TILE)
            dst = pl.multiple_of(r * ROW_TILE, ROW_TILE)
            pltpu.make_async_copy(h_ref.at[pl.ds(src, ROW_TILE), :], xbuf.at[buf, pl.ds(dst, ROW_TILE), :],
                                  sems.at[buf]).start()
            return c

        lax.fori_loop(0, MOE_TB, start, 0, unroll=ROW_DMA_UNROLL)

    @pl.when(i == 0)
    def _():
        gather(0, 0)

    @pl.when(i + 1 < nused)
    def _():
        gather(i + 1, (i + 1) % 2)

    new_expert = jnp.logical_or(i == 0, be_ref[i] != be_ref[jnp.maximum(i - 1, 0)])

    @pl.when(new_expert)
    def _():
        wg_b[...] = wg_ref[0].astype(BF16)
        wu_b[...] = wu_ref[0].astype(BF16)
        wd_b[...] = wd_ref[0].astype(BF16)

    @pl.when(i < nused)
    def _():
        buf = i % 2
        pltpu.make_async_copy(h_ref.at[pl.ds(0, block_tiles), :], xbuf.at[buf], sems.at[buf]).wait()
        x = _load_token_tiles(xbuf.at[buf], MOE_TB).astype(BF16)
        gate = jnp.dot(x, wg_b[...], preferred_element_type=F32)
        up = jnp.dot(x, wu_b[...], preferred_element_type=F32)
        hid = (jax.nn.silu(gate) * up).astype(BF16)
        ys_ref[...] = jnp.dot(hid, wd_b[...], preferred_element_type=F32)

    @pl.when(i >= nused)
    def _():
        ys_ref[...] = jnp.zeros_like(ys_ref)


def _expert_ffn(block_e, nused, row_tok, h2t, w_gate, w_up, w_down):
    return pl.pallas_call(
        _ffn_kernel,
        grid_spec=pltpu.PrefetchScalarGridSpec(
            num_scalar_prefetch=3,
            grid=(MOE_NB,),
            in_specs=[pl.BlockSpec(memory_space=pl.ANY),
                      pl.BlockSpec((1, D_MODEL, D_EXPERT), lambda i, be, nu, tk: (be[i], 0, 0)),
                      pl.BlockSpec((1, D_MODEL, D_EXPERT), lambda i, be, nu, tk: (be[i], 0, 0)),
                      pl.BlockSpec((1, D_EXPERT, D_MODEL), lambda i, be, nu, tk: (be[i], 0, 0))],
            out_specs=pl.BlockSpec((MOE_TB, D_MODEL), lambda i, be, nu, tk: (i, 0)),
            scratch_shapes=[pltpu.VMEM((D_MODEL, D_EXPERT), BF16), pltpu.VMEM((D_MODEL, D_EXPERT), BF16),
                            pltpu.VMEM((D_EXPERT, D_MODEL), BF16),
                            pltpu.VMEM((2, MOE_TB * ROW_TILE, LANES), F32), pltpu.SemaphoreType.DMA((2,))],
        ),
        out_shape=jax.ShapeDtypeStruct((MOE_ROWS, D_MODEL), F32),
        compiler_params=_cparams("arbitrary"),
        name="moe_ffn",
    )(block_e, nused, row_tok, h2t, w_gate, w_up, w_down)


def _combine_kernel(dest_ref, x1_ref, rt_ref, gf_ref, ys_ref, y_ref, ya, yb, sem):
    rows = x1_ref.shape[0]
    base = pl.program_id(0) * (2 * rows)

    def row_copy(t, slot, buf):
        d = dest_ref[base + 2 * t + slot]
        return pltpu.make_async_copy(ys_ref.at[pl.ds(d, 1)], buf.at[pl.ds(t, 1)], sem)

    def start(t, c):
        row_copy(t, 0, ya).start()
        row_copy(t, 1, yb).start()
        return c

    lax.fori_loop(0, rows, start, 0, unroll=ROW_DMA_UNROLL)
    pltpu.make_async_copy(ys_ref.at[pl.ds(0, rows)], ya, sem).wait()
    pltpu.make_async_copy(ys_ref.at[pl.ds(0, rows)], yb, sem).wait()
    x2 =x1_ref[...] + rt_ref[:, 4:5] * ya[...] + rt_ref[:, 5:6] * yb[...]
    y_ref[...] = _rms(x2, gf_ref[...])


def _combine(dest, x1, rt, gf, ys, rows):
    n = x1.shape[0]
    return pl.pallas_call(
        _combine_kernel,
        grid_spec=pltpu.PrefetchScalarGridSpec(
            num_scalar_prefetch=1,
            grid=(n // rows,),
            in_specs=[pl.BlockSpec((rows, D_MODEL), lambda i, dest: (i, 0)),
                      pl.BlockSpec((rows, LANES), lambda i, dest: (i, 0)),
                      pl.BlockSpec((1, D_MODEL), lambda i, dest: (0, 0)),
                      pl.BlockSpec(memory_space=pl.ANY)],
            out_specs=pl.BlockSpec((rows, D_MODEL), lambda i, dest: (i, 0)),
            scratch_shapes=[pltpu.VMEM((rows, D_MODEL), F32), pltpu.VMEM((rows, D_MODEL), F32),
                            pltpu.SemaphoreType.DMA(())],
        ),
        out_shape=jax.ShapeDtypeStruct((n, D_MODEL), F32),
        compiler_params=_cparams("arbitrary"),
        name=f"moe_combine_{rows}",
    )(dest, x1, rt, gf, ys)


def _rope_tables(pos):
    half = HEAD_DIM // 2
    inv = 1.0 / (ROPE_THETA ** (jnp.arange(half, dtype=F32) * (2.0 / HEAD_DIM)))
    ang = pos.astype(F32)[:, None] * inv[None, :]
    cos, sin = jnp.cos(ang), jnp.sin(ang)
    return jnp.concatenate([cos, cos, cos, cos], axis=1), jnp.concatenate([-sin, sin, -sin, sin], axis=1)


def _slot_dest(rt, pstarts):
    e = rt[:, 0:2].astype(jnp.int32)
    pos = rt[:, 2:4].astype(jnp.int32)
    onehot = e[:, :, None] == jnp.arange(N_EXPERTS, dtype=jnp.int32)[None, None, :]
    return (jnp.sum(jnp.where(onehot, pstarts[None, None, :], 0), axis=-1) + pos).reshape(-1)


def kernel(x_prompt, x_sample, cache_kv_g0, cache_kv_g1, cache_kv_g2, norm1_g, w_in, b_in, a_ln_g, a_ln_b, w_spatial, b_spatial, w_a_proj, w_b_proj, w_o, norm2_g, w_group_router, b_group_router, w_expert_router, b_expert_router, w_gate, w_up, w_down, final_norm_g):
    x2 = x_prompt.reshape(T_PROMPT, D_MODEL)
    xs = x_sample.reshape(DEC_BATCH, D_MODEL)
    g1 = norm1_g[0][None, :]
    g2 = norm2_g[0][None, :]
    gf = final_norm_g[None, :]
    b_in2 = b_in[0][None, :]
    lng, lnb = a_ln_g[0][None, :], a_ln_b[0][None, :]

    causal = jnp.tril(jnp.ones((CHUNK, CHUNK), dtype=bool))
    ws_tril = jnp.where(causal[None], w_spatial[0], 0.0)
    wsp = jnp.concatenate([ws_tril[0::2], ws_tril[1::2]], axis=2).astype(BF16)
    bsp = jnp.repeat(b_spatial[0].T, A_GROUP_DIM, axis=1)
    ws0 = jnp.repeat(ws_tril[:, 0, 0].astype(BF16).astype(F32), A_GROUP_DIM)[None, :]
    bs0 = jnp.repeat(b_spatial[0][:, 0], A_GROUP_DIM)[None, :]
    w_router = jnp.zeros((D_MODEL, LANES), F32)
    w_router = w_router.at[:, :N_EXPERT_GROUPS].set(w_group_router[0])
    w_router = w_router.at[:, N_EXPERT_GROUPS:N_EXPERT_GROUPS + N_EXPERTS].set(w_expert_router[0])
    b_router = jnp.zeros((1, LANES), F32)
    b_router = b_router.at[0, :N_EXPERT_GROUPS].set(b_group_router[0])
    b_router = b_router.at[0, N_EXPERT_GROUPS:N_EXPERT_GROUPS + N_EXPERTS].set(b_expert_router[0])
    w_router = w_router.astype(BF16)
    ltri = jnp.tril(jnp.ones((TM, TM), BF16), -1)
    w_in_b, w_a_b, w_b_b, w_o_b = (w[0].astype(BF16) for w in (w_in, w_a_proj, w_b_proj, w_o))

    cos_p, sin_p = _rope_tables(jnp.arange(SEQ, dtype=jnp.int32))
    cos_s, sin_s = _rope_tables(jnp.full((1,), PAST_LEN, jnp.int32))

    uv, qkv0, qkv1, qkv2, gates, kvp0, kvp1, kvp2 = _inproj(
        x2, g1, w_in_b, b_in2, lng, lnb, cos_p, sin_p)
    ob = _attention((qkv0.reshape(BATCH, 1, SEQ, 3 * GROUP_W), qkv1, qkv2)).reshape(T_PROMPT, GROUP_W)
    x1_p, h2t_p, rt_p, counts_p = _merge(
        x2, uv, gates, ob, wsp, bsp, w_a_b, w_b_b, w_o_b, g2, w_router, b_router, ltri)

    z_s = _sample_inproj(xs, g1, w_in_b, b_in2)
    vrow, sa_s, qr_s, kr_s, kvs0, kvs1, kvs2 = _sample_pre(z_s, lng, lnb, ws0, bs0, cos_s, sin_s)
    dim_major = lambda t: t.reshape(DEC_BATCH, 3, HEADS, HEAD_DIM).transpose(0, 1, 3, 2)
    caches = [c.transpose(0, 1, 3, 4, 5, 2).reshape(DEC_BATCH, 2, HEADS, HEAD_DIM, c.shape[2])
              for c in (cache_kv_g0, cache_kv_g1, cache_kv_g2)]
    comb_s = _sample_attn(dim_major(qr_s), dim_major(kr_s), dim_major(z_s[:, COL_VB:COL_VB + QK_W]), caches)
    comb_s = comb_s.transpose(0, 2, 1).reshape(DEC_BATCH, GROUP_W)
    x1_s, h2t, rt_s, counts = _sample_merge(
        xs, z_s, sa_s, comb_s, w_a_b, w_b_b, w_o_b, g2, w_router, b_router,
        ltri[:DEC_BATCH, :DEC_BATCH], counts_p, h2t_p)

    cnt = counts[0, :N_EXPERTS].astype(jnp.int32)
    padded = (cnt + MOE_TB - 1) // MOE_TB * MOE_TB
    pends = jnp.cumsum(padded)
    pstarts = pends - padded
    block_starts = jnp.arange(MOE_NB, dtype=jnp.int32) * MOE_TB
    block_e = jnp.minimum(jnp.sum((pends[None, :] <= block_starts[:, None]).astype(jnp.int32), axis=1),
                          N_EXPERTS - 1)
    nused = (pends[-1:] // MOE_TB).astype(jnp.int32)
    dest_p = _slot_dest(rt_p, pstarts)
    dest_s = _slot_dest(rt_s, pstarts)

    ys = _expert_ffn(block_e, nused, _row_tokens(dest_p, dest_s), h2t, w_gate[0], w_up[0], w_down[0])
    y_p = _combine(dest_p, x1_p, rt_p, gf, ys, TM)
    y_s = _combine(dest_s, x1_s, rt_s, gf, ys, DEC_BATCH)

    kv_shape = lambda n, w: (1, n, w, 2, HEADS, HEAD_DIM)
    window_rows = lambda t: t.reshape(1, BATCH, 2, HEADS, HEAD_DIM, t.shape[2]).transpose(0, 1, 5, 2, 3, 4)
    return (y_p.reshape(BATCH, SEQ, D_MODEL), y_s.reshape(DEC_BATCH, 1, D_MODEL),
            window_rows(kvp0), window_rows(kvp1), window_rows(kvp2),
            kvs0.reshape(kv_shape(DEC_BATCH, 1)), kvs1.reshape(kv_shape(DEC_BATCH, 1)),
            kvs2.reshape(kv_shape(DEC_BATCH, 1)), vrow.reshape(1, DEC_BATCH, 1, A_WIDTH))
```

```python
import functools

import jax
import jax.numpy as jnp
from jax import lax
from jax.experimental import pallas as pl
from jax.experimental.pallas import tpu as pltpu

F32 = jnp.float32
BF16 = jnp.bfloat16
U32 = jnp.uint32

D_MODEL = 1024
BATCH = 2
SEQ = 8192
DEC_BATCH = 32
PAST_LEN = 8192
CHUNK = 128
A_GROUPS = 8
A_GROUP_DIM = 64
A_WIDTH = 512
HEAD_DIM = 64
HEADS = 4
GROUP_W = HEADS * HEAD_DIM
DILATIONS = (1, 4, 16)
SPAN = 128
QK_W = 768
IN_WIDTH = 5376
COL_U, COL_V, COL_Q, COL_K, COL_VB, COL_GA, COL_GB = 0, 512, 1024, 1792, 2560, 3328, 4352
N_EXPERT_GROUPS = 4
EXPERTS_PER_GROUP = 8
N_EXPERTS = 32
D_EXPERT = 512
ROPE_THETA = 10000.0
EPS = 1e-6

LANES = 128
T_PROMPT = BATCH * SEQ
TM = 512
TILES_PER_SEQ = SEQ // TM
WINDOWS = tuple(min(SPAN * d, SEQ) for d in DILATIONS)
TA = SPAN * max(DILATIONS)
ATT_UNROLL = 2
MOE_TB = 256
ROW_DMA_UNROLL = 8
N_SLOTS = 2 * (T_PROMPT + DEC_BATCH)
MOE_NB = -(-N_SLOTS // MOE_TB) + N_EXPERTS
MOE_ROWS = MOE_NB * MOE_TB
HALF_D = D_MODEL // 2
NEG = -1e30
VMEM_LIMIT = 56 * 1024 * 1024


def _cparams(*sem):
    return pltpu.CompilerParams(dimension_semantics=sem, vmem_limit_bytes=VMEM_LIMIT)


def _rms(x, g):
    return x * lax.rsqrt(jnp.mean(x * x, axis=-1, keepdims=True) + EPS) * g


def _layer_norm(x, g, b):
    mu = jnp.mean(x, axis=-1, keepdims=True)
    xc = x - mu
    var = jnp.mean(xc * xc, axis=-1, keepdims=True)
    return xc * lax.rsqrt(var + EPS) * g + b


def _rope_chunk(x, cos, sin_signed):
    lane = lax.broadcasted_iota(jnp.int32, x.shape, 1)
    first_half = (lane % HEAD_DIM) < (HEAD_DIM // 2)
    swapped = jnp.where(first_half, pltpu.roll(x, LANES - HEAD_DIM // 2, 1), pltpu.roll(x, HEAD_DIM // 2, 1))
    return x * cos + swapped * sin_signed


def _pack_bf16_pair(h):
    lo = lax.bitcast_convert_type(h[:, :HALF_D].astype(BF16).astype(F32), U32)
    hi = lax.bitcast_convert_type(h[:, HALF_D:].astype(BF16).astype(F32), U32)
    return (hi & jnp.uint32(0xFFFF0000)) | (lo >> 16)


def _unpack_bf16_pair(p):
    lo = lax.bitcast_convert_type(p << 16, F32).astype(BF16)
    hi = lax.bitcast_convert_type(p & jnp.uint32(0xFFFF0000), F32).astype(BF16)
    return lo, hi


def _route(logits, prior_counts, ltri):
    rows = logits.shape[0]
    lane = lax.broadcasted_iota(jnp.int32, (rows, LANES), 1).astype(F32)
    is_g = lane < N_EXPERT_GROUPS
    gl = jnp.where(is_g, logits, NEG)
    gmax = jnp.max(gl, axis=1, keepdims=True)
    grp = jnp.min(jnp.where(gl == gmax, lane, float(LANES)), axis=1, keepdims=True)
    p_grp = 1.0 / jnp.sum(jnp.where(is_g, jnp.exp(gl - gmax), 0.0), axis=1, keepdims=True)
    lo = N_EXPERT_GROUPS + grp * EXPERTS_PER_GROUP
    el = jnp.where((lane >= lo) & (lane < lo + EXPERTS_PER_GROUP), logits, NEG)
    v1 = jnp.max(el, axis=1, keepdims=True)
    i1 = jnp.min(jnp.where(el == v1, lane, float(LANES)), axis=1, keepdims=True)
    el2 = jnp.where(lane == i1, NEG, el)
    v2 = jnp.max(el2, axis=1, keepdims=True)
    i2 = jnp.min(jnp.where(el2 == v2, lane, float(LANES)), axis=1, keepdims=True)
    t = jnp.exp(v2 - v1)
    gate1 = p_grp / (1.0 + t)
    gate2 = p_grp * t / (1.0 + t)
    e1 = i1 - N_EXPERT_GROUPS
    e2 = i2 - N_EXPERT_GROUPS
    hit1 = lane == e1
    hit2 = lane == e2
    onehot = jnp.where(hit1 | hit2, 1.0, 0.0)
    rank = jnp.dot(ltri, onehot.astype(BF16), preferred_element_type=F32) + prior_counts
    pos1 = jnp.sum(jnp.where(hit1, rank, 0.0), axis=1, keepdims=True)
    pos2 = jnp.sum(jnp.where(hit2, rank, 0.0), axis=1, keepdims=True)
    slab = jnp.where(lane == 0, e1, jnp.where(lane == 1, e2, jnp.where(lane == 2, pos1, jnp.where(
        lane == 3, pos2, jnp.where(lane == 4, gate1, jnp.where(lane == 5, gate2, 0.0))))))
    return slab, prior_counts + jnp.sum(onehot, axis=0, keepdims=True)


def _inproj_kernel(x_ref, g1_ref, w_ref, b_ref, lng_ref, lnb_ref, cos_ref, sin_ref,
                   uv_ref, qkv0_ref, qkv1_ref, qkv2_ref, gates_ref, kv0_ref, kv1_ref, kv2_ref, de_ref):
    tile_in_seq = pl.program_id(0) % TILES_PER_SEQ
    hb = _rms(x_ref[...], g1_ref[...]).astype(BF16)

    def seg(lo, width):
        return jnp.dot(hb, w_ref[:, lo:lo + width], preferred_element_type=F32) + b_ref[:, lo:lo + width]

    uv_ref[:, :A_WIDTH] = jax.nn.gelu(seg(COL_U, A_WIDTH)).astype(BF16)
    uv_ref[:, A_WIDTH:] = _layer_norm(jax.nn.gelu(seg(COL_V, A_WIDTH)), lng_ref[...], lnb_ref[...]).astype(BF16)
    gates_ref[:, :D_MODEL] = jax.nn.sigmoid(seg(COL_GA, D_MODEL)).astype(BF16)
    gates_ref[:, D_MODEL:] = jax.nn.sigmoid(seg(COL_GB, D_MODEL)).astype(BF16)

    cos = cos_ref[...]
    sin = sin_ref[...]
    qkv_refs = (qkv0_ref, qkv1_ref, qkv2_ref)
    kv_refs = (kv0_ref, kv1_ref, kv2_ref)
    for g, d in enumerate(DILATIONS):
        q = seg(COL_Q + g * GROUP_W, GROUP_W)
        k = seg(COL_K + g * GROUP_W, GROUP_W)
        v = seg(COL_VB + g * GROUP_W, GROUP_W)
        chunks = GROUP_W // LANES
        for c in range(chunks):
            sl = slice(c * LANES, (c + 1) * LANES)
            de_ref[c] = _rope_chunk(q[:, sl], cos, sin) * (HEAD_DIM ** -0.5)
            de_ref[chunks + c] = _rope_chunk(k[:, sl], cos, sin)
            de_ref[2 * chunks + c] = v[:, sl]
        for c in range(3 * chunks):
            sl = slice(c * LANES, (c + 1) * LANES)
            if d == 1:
                qkv_refs[g][:, sl] = de_ref[c].astype(BF16)
            else:
                for r in range(d):
                    qkv_refs[g][0, r, :, sl] = de_ref[c, pl.ds(r, TM // d, stride=d), :].astype(BF16)
        kv_rows = kv_refs[g].shape[2]
        first = TILES_PER_SEQ - WINDOWS[g] // kv_rows

        @pl.when(tile_in_seq >= first)
        def _(kv_ref=kv_refs[g], kv_rows=kv_rows, chunks=chunks):
            for c in range(2 * chunks):
                kv_ref[0, c * LANES:(c + 1) * LANES, :] = de_ref[chunks + c, TM - kv_rows:, :].T


def _inproj(x2, g1, w_bf, b_in, lng, lnb, cos_t, sin_t):
    tiles_per_seq = TILES_PER_SEQ
    n_tiles = T_PROMPT // TM
    const = lambda i: (0, 0)

    def kv_spec(w):
        rows = min(w, TM)
        first = tiles_per_seq - w // rows
        return pl.BlockSpec((1, 2 * GROUP_W, rows),
                            lambda i: (i // tiles_per_seq, 0, jnp.maximum(i % tiles_per_seq - first, 0)))

    def regrouped_spec(d):
        return pl.BlockSpec((1, d, TM // d, 3 * GROUP_W), lambda i: (i // tiles_per_seq, 0, i % tiles_per_seq, 0))

    return pl.pallas_call(
        _inproj_kernel,
        grid=(n_tiles,),
        in_specs=[
            pl.BlockSpec((TM, D_MODEL), lambda i: (i, 0)),
            pl.BlockSpec((1, D_MODEL), const),
            pl.BlockSpec((D_MODEL, IN_WIDTH), const),
            pl.BlockSpec((1, IN_WIDTH), const),
            pl.BlockSpec((1, A_WIDTH), const),
            pl.BlockSpec((1, A_WIDTH), const),
            pl.BlockSpec((TM, LANES), lambda i: (i % tiles_per_seq, 0)),
            pl.BlockSpec((TM, LANES), lambda i: (i % tiles_per_seq, 0)),
        ],
        out_specs=[
            pl.BlockSpec((TM, 2 * A_WIDTH), lambda i: (i, 0)),
            pl.BlockSpec((TM, 3 * GROUP_W), lambda i: (i, 0)),
            regrouped_spec(DILATIONS[1]),
            regrouped_spec(DILATIONS[2]),
            pl.BlockSpec((TM, 2 * D_MODEL), lambda i: (i, 0)),
            kv_spec(WINDOWS[0]), kv_spec(WINDOWS[1]), kv_spec(WINDOWS[2]),
        ],
        out_shape=[
            jax.ShapeDtypeStruct((T_PROMPT, 2 * A_WIDTH), BF16),
            jax.ShapeDtypeStruct((T_PROMPT, 3 * GROUP_W), BF16),
            jax.ShapeDtypeStruct((BATCH, DILATIONS[1], SEQ // DILATIONS[1], 3 * GROUP_W), BF16),
            jax.ShapeDtypeStruct((BATCH, DILATIONS[2], SEQ // DILATIONS[2], 3 * GROUP_W), BF16),
            jax.ShapeDtypeStruct((T_PROMPT, 2 * D_MODEL), BF16),
            jax.ShapeDtypeStruct((BATCH, 2 * GROUP_W, WINDOWS[0]), F32),
            jax.ShapeDtypeStruct((BATCH, 2 * GROUP_W, WINDOWS[1]), F32),
            jax.ShapeDtypeStruct((BATCH, 2 * GROUP_W, WINDOWS[2]), F32),
        ],
        scratch_shapes=[pltpu.VMEM((3 * GROUP_W // LANES, TM, LANES), F32)],
        compiler_params=_cparams("arbitrary"),
        name="inproj",
    )(x2, g1, w_bf, b_in, lng, lnb, cos_t, sin_t)


def _attn_kernel(c0_ref, p0_ref, c1_ref, p1_ref, c2_ref, p2_ref, o_ref, acc_ref, m_ref, l_ref):
    n = pl.program_id(1)
    acc_ref[...] = jnp.zeros_like(acc_ref)
    l_ref[...] = jnp.zeros_like(l_ref)
    m_ref[...] = jnp.full_like(m_ref, NEG)

    qi = lax.broadcasted_iota(jnp.int32, (SPAN, 2 * SPAN), 0)
    ki = lax.broadcasted_iota(jnp.int32, (SPAN, 2 * SPAN), 1)
    band = (ki >= qi) & (ki <= qi + SPAN)
    band_first = band & ((ki >= SPAN) | (n > 0))
    q_head = lax.broadcasted_iota(jnp.int32, (SPAN, GROUP_W), 1) // HEAD_DIM
    kv_head = lax.broadcasted_iota(jnp.int32, (2 * SPAN, GROUP_W), 1) // HEAD_DIM
    contract_last = (((1,), (1,)), ((), ()))
    k_cols = slice(GROUP_W, 2 * GROUP_W)
    v_cols = slice(2 * GROUP_W, 3 * GROUP_W)

    def attend(q, keys, vals, mask, tok_rows):
        scores = []
        for h in range(HEADS):
            qh = jnp.where(q_head == h, q, jnp.zeros_like(q))
            s = lax.dot_general(qh, keys, contract_last, preferred_element_type=F32)
            scores.append(jnp.where(mask, s, NEG))
        s = jnp.concatenate(scores, axis=0)
        m = jnp.max(s, axis=1, keepdims=True)
        p = jnp.exp(s - m)
        l = jnp.sum(p, axis=1, keepdims=True)
        pb = p.astype(BF16)
        acc = jnp.zeros((SPAN, GROUP_W), F32)
        m_b = jnp.zeros((SPAN, GROUP_W), F32)
        l_b = jnp.zeros((SPAN, GROUP_W), F32)
        for h in range(HEADS):
            rows = slice(h * SPAN, (h + 1) * SPAN)
            vh = jnp.where(kv_head == h, vals, jnp.zeros_like(vals))
            acc = acc + jnp.dot(pb[rows], vh, preferred_element_type=F32)
            m_b = jnp.where(q_head == h, m[rows], m_b)
            l_b = jnp.where(q_head == h, l[rows], l_b)
        for c in range(GROUP_W // LANES):
            sl = slice(c * LANES, (c + 1) * LANES)
            m_old = m_ref[c, tok_rows, :]
            m_new = jnp.maximum(m_old, m_b[:, sl])
            a_old = jnp.exp(m_old - m_new)
            a_blk = jnp.exp(m_b[:, sl] - m_new)
            l_ref[c, tok_rows, :] = a_old * l_ref[c, tok_rows, :] + a_blk * l_b[:, sl]
            acc_ref[c, tok_rows, :] = a_old * acc_ref[c, tok_rows, :] + a_blk * acc[:, sl]
            m_ref[c, tok_rows, :] = m_new

    def stream(c_ref, p_ref, r, d):
        n_blocks = c_ref.shape[2] // SPAN

        def tok_rows(j):
            start = j * (SPAN * d) + r
            return pl.ds(start, SPAN) if d == 1 else pl.ds(start, SPAN, stride=d)

        keys = jnp.concatenate([p_ref[0, r, :, k_cols], c_ref[0, r, :SPAN, k_cols]], axis=0)
        vals = jnp.concatenate([p_ref[0, r, :, v_cols], c_ref[0, r, :SPAN, v_cols]], axis=0)
        attend(c_ref[0, r, :SPAN, :GROUP_W], keys, vals, band_first, tok_rows(0))

        def later_block(j, carry):
            q_rows = pl.ds(pl.multiple_of(j * SPAN, SPAN), SPAN)
            kv_rows = pl.ds(pl.multiple_of((j - 1) * SPAN, SPAN), 2 * SPAN)
            attend(c_ref[0, r, q_rows, :GROUP_W], c_ref[0, r, kv_rows, k_cols], c_ref[0, r, kv_rows, v_cols],
                   band, tok_rows(j))
            return carry

        if n_blocks > 1:
            lax.fori_loop(1, n_blocks, later_block, 0, unroll=ATT_UNROLL)

    for (c_ref, p_ref), d in zip(((c0_ref, p0_ref), (c1_ref, p1_ref), (c2_ref, p2_ref)), DILATIONS):
        if d == 1:
            stream(c_ref, p_ref, 0, d)
        else:
            lax.fori_loop(0, d, lambda r, carry, c_ref=c_ref, p_ref=p_ref, d=d: (stream(c_ref, p_ref, r, d), carry)[1], 0,
                          unroll=ATT_UNROLL if c_ref.shape[2] == SPAN else 1)

    for c in range(GROUP_W // LANES):
        o_ref[0, :, c * LANES:(c + 1) * LANES] = (acc_ref[c] / l_ref[c]).astype(BF16)


def _attention(qkv_by_group):
    in_specs, args = [], []
    for qkv, d in zip(qkv_by_group, DILATIONS):
        rows = TA // d
        blocks_per_tile = rows // SPAN
        in_specs.append(pl.BlockSpec((1, d, rows, 3 * GROUP_W), lambda b, n: (b, 0, n, 0)))
        in_specs.append(pl.BlockSpec((1, d, SPAN, 3 * GROUP_W),
                                     lambda b, n, k=blocks_per_tile: (b, 0, jnp.maximum(n * k - 1, 0), 0)))
        args += [qkv, qkv]
    return pl.pallas_call(
        _attn_kernel,
        grid=(BATCH, SEQ // TA),
        in_specs=in_specs,
        out_specs=pl.BlockSpec((1, TA, GROUP_W), lambda b, n: (b, n, 0)),
        out_shape=jax.ShapeDtypeStruct((BATCH, SEQ, GROUP_W), BF16),
        scratch_shapes=[pltpu.VMEM((GROUP_W // LANES, TA, LANES), F32)] * 3,
        compiler_params=_cparams("arbitrary", "arbitrary"),
        name="attn",
    )(*args)


def _merge_kernel(x_ref, uv_ref, gates_ref, ob_ref,
                  wsp_ref, bsp_ref, wa_ref, wb_ref, wo_ref, g2_ref, wr_ref, br_ref, ltri_ref,
                  x1_ref, h2p_ref, rt_ref, cnt_ref, run_ref):
    @pl.when(pl.program_id(0) == 0)
    def _():
        run_ref[...] = jnp.zeros_like(run_ref)

    lane = lax.broadcasted_iota(jnp.int32, (CHUNK, LANES), 1)
    left = lane < A_GROUP_DIM
    zero = jnp.zeros((CHUNK, LANES), BF16)
    sa_chunks = []
    for c in range(TM // CHUNK):
        rows = slice(c * CHUNK, (c + 1) * CHUNK)
        pairs = []
        for p in range(A_GROUPS // 2):
            vp = uv_ref[rows, A_WIDTH + p * LANES:A_WIDTH + (p + 1) * LANES]
            rhs = jnp.concatenate([jnp.where(left, vp, zero), jnp.where(left, zero, vp)], axis=0)
            pairs.append(jnp.dot(wsp_ref[p], rhs, preferred_element_type=F32))
        mixed = jnp.concatenate(pairs, axis=1) + bsp_ref[...]
        sa_chunks.append((uv_ref[rows, :A_WIDTH].astype(F32) * mixed).astype(BF16))
    s_a = jnp.concatenate(sa_chunks, axis=0)

    a = jnp.dot(s_a, wa_ref[...], preferred_element_type=F32)
    b = jnp.dot(ob_ref[...], wb_ref[...], preferred_element_type=F32)
    merged = gates_ref[:, :D_MODEL].astype(F32) * a + gates_ref[:, D_MODEL:].astype(F32) * b
    x1 = x_ref[...] + jnp.dot(merged.astype(BF16), wo_ref[...], preferred_element_type=F32)
    x1_ref[...] = x1

    h2 = _rms(x1, g2_ref[...])
    h2p_ref[...] = _pack_bf16_pair(h2)
    logits = jnp.dot(h2.astype(BF16), wr_ref[...], preferred_element_type=F32) + br_ref[...]
    slab, counts = _route(logits, run_ref[...], ltri_ref[...])
    rt_ref[...] = slab
    run_ref[...] = counts
    cnt_ref[...] = counts


def _merge(x2, uv, gates, ob, wsp, bsp, wa, wb, wo, g2, wr, br, ltri):
    n_tiles = T_PROMPT // TM
    tile = lambda w: pl.BlockSpec((TM, w), lambda i: (i, 0))
    full = lambda a: pl.BlockSpec(a.shape, lambda i: (0,) * a.ndim)
    return pl.pallas_call(
        _merge_kernel,
        grid=(n_tiles,),
        in_specs=[tile(D_MODEL), tile(2 * A_WIDTH), tile(2 * D_MODEL), tile(GROUP_W),
                  full(wsp), full(bsp), full(wa), full(wb), full(wo), full(g2), full(wr), full(br),
                  full(ltri)],
        out_specs=[tile(D_MODEL), tile(HALF_D), tile(LANES), pl.BlockSpec((1, LANES), lambda i: (0, 0))],
        out_shape=[
            jax.ShapeDtypeStruct((T_PROMPT, D_MODEL), F32),
            jax.ShapeDtypeStruct((T_PROMPT, HALF_D), U32),
            jax.ShapeDtypeStruct((T_PROMPT, LANES), F32),
            jax.ShapeDtypeStruct((1, LANES), F32),
        ],
        scratch_shapes=[pltpu.VMEM((1, LANES), F32)],
        compiler_params=_cparams("arbitrary"),
        name="merge_route",
    )(x2, uv, gates, ob, wsp, bsp, wa, wb, wo, g2, wr, br, ltri)


SAMPLE_COLS = 768


def _sample_inproj_kernel(x_ref, g1_ref, w_ref, b_ref, z_ref):
    hb = _rms(x_ref[...], g1_ref[...]).astype(BF16)
    z_ref[...] = jnp.dot(hb, w_ref[...], preferred_element_type=F32) + b_ref[...]


def _sample_inproj(xs, g1, w_in, b_in):
    const = lambda j: (0, 0)
    return pl.pallas_call(
        _sample_inproj_kernel,
        grid=(IN_WIDTH // SAMPLE_COLS,),
        in_specs=[pl.BlockSpec((DEC_BATCH, D_MODEL), const), pl.BlockSpec((1, D_MODEL), const),
                  pl.BlockSpec((D_MODEL, SAMPLE_COLS), lambda j: (0, j)),
                  pl.BlockSpec((1, SAMPLE_COLS), lambda j: (0, j))],
        out_specs=pl.BlockSpec((DEC_BATCH, SAMPLE_COLS), lambda j: (0, j)),
        out_shape=jax.ShapeDtypeStruct((DEC_BATCH, IN_WIDTH), F32),
        compiler_params=_cparams("arbitrary"),
        name="sample_inproj",
    )(xs, g1, w_in, b_in)


def _sample_pre_kernel(z_ref, lng_ref, lnb_ref, ws0_ref, bs0_ref, cos_ref, sin_ref,
                       vrow_ref, sa_ref, qr_ref, kr_ref, kv0_ref, kv1_ref, kv2_ref):
    u = jax.nn.gelu(z_ref[:, COL_U:COL_U + A_WIDTH])
    va = _layer_norm(jax.nn.gelu(z_ref[:, COL_V:COL_V + A_WIDTH]), lng_ref[...], lnb_ref[...])
    vrow_ref[...] = va
    sa_ref[...] = u * (ws0_ref[...] * va.astype(BF16).astype(F32) + bs0_ref[...])
    cos, sin = cos_ref[...], sin_ref[...]
    for g, kv_ref in enumerate((kv0_ref, kv1_ref, kv2_ref)):
        for c in range(GROUP_W // LANES):
            off = g * GROUP_W + c * LANES
            qr_ref[:, off:off + LANES] = (_rope_chunk(z_ref[:, COL_Q + off:COL_Q + off + LANES], cos, sin)
                                          * (HEAD_DIM ** -0.5))
            kr = _rope_chunk(z_ref[:, COL_K + off:COL_K + off + LANES], cos, sin)
            kr_ref[:, off:off + LANES] = kr
            kv_ref[:, c * LANES:(c + 1) * LANES] = kr
        kv_ref[:, GROUP_W:] = z_ref[:, COL_VB + g * GROUP_W:COL_VB + (g + 1) * GROUP_W]


def _sample_pre(z, lng, lnb, ws0, bs0, cos_s, sin_s):
    row = lambda w: jax.ShapeDtypeStruct((DEC_BATCH, w), F32)
    return pl.pallas_call(
        _sample_pre_kernel,
        out_shape=[row(A_WIDTH), row(A_WIDTH), row(QK_W), row(QK_W),
                   row(2 * GROUP_W), row(2 * GROUP_W), row(2 * GROUP_W)],
        compiler_params=pltpu.CompilerParams(vmem_limit_bytes=VMEM_LIMIT),
        name="sample_pre",
    )(z, lng, lnb, ws0, bs0, cos_s, sin_s)


def _sample_attn_kernel(q_ref, k_ref, v_ref, c0_ref, c1_ref, c2_ref, comb_ref):
    as_operand = lambda t: t.astype(BF16).astype(F32)
    for h in range(HEADS):
        head = slice(h, h + 1)
        outs, lses = [], []
        for g, (c_ref, d) in enumerate(zip((c0_ref, c1_ref, c2_ref), DILATIONS)):
            qc, kn, vn = (as_operand(r[0, g, :, head]) for r in (q_ref, k_ref, v_ref))
            s = jnp.sum(as_operand(c_ref[0, 0, h]) * qc, axis=0, keepdims=True)
            if d > 1:
                pos = lax.broadcasted_iota(jnp.int32, s.shape, 1)
                s = jnp.where(pos % d == 0, s, NEG)
            s_n = jnp.sum(qc * kn, axis=0, keepdims=True)
            m = jnp.maximum(jnp.max(s, axis=1, keepdims=True), s_n)
            lse = m + jnp.log(jnp.sum(jnp.exp(s - m), axis=1, keepdims=True) + jnp.exp(s_n - m))
            p = as_operand(jnp.exp(s - lse))
            p_n = as_operand(jnp.exp(s_n - lse))
            outs.append(jnp.sum(as_operand(c_ref[0, 1, h]) * p, axis=1, keepdims=True) + p_n * vn)
            lses.append(lse)
        mx = jnp.maximum(jnp.maximum(lses[0], lses[1]), lses[2])
        ws = [jnp.exp(t - mx) for t in lses]
        tot = ws[0] + ws[1] + ws[2]
        comb_ref[0, :, head] = (ws[0] * outs[0] + ws[1] * outs[1] + ws[2] * outs[2]) / tot


def _sample_attn(qt, kt, vt, caches):
    new_spec = pl.BlockSpec((1, 3, HEAD_DIM, HEADS), lambda i: (i, 0, 0, 0))
    cache_spec = lambda c: pl.BlockSpec((1,) + c.shape[1:], lambda i: (i, 0, 0, 0, 0))
    return pl.pallas_call(
        _sample_attn_kernel,
        grid=(DEC_BATCH,),
        in_specs=[new_spec, new_spec, new_spec, cache_spec(caches[0]), cache_spec(caches[1]), cache_spec(caches[2])],
        out_specs=pl.BlockSpec((1, HEAD_DIM, HEADS), lambda i: (i, 0, 0)),
        out_shape=jax.ShapeDtypeStruct((DEC_BATCH, HEAD_DIM, HEADS), F32),
        compiler_params=_cparams("arbitrary"),
        name="sample_attn",
    )(qt, kt, vt, caches[0], caches[1], caches[2])


def _sample_merge_kernel(x_ref, z_ref, sa_ref, comb_ref, wa_ref, wb_ref, wo_ref, g2_ref, wr_ref, br_ref,
                         ltri_ref, cnt_in_ref, x1_ref, h2p_ref, rt_ref, cnt_ref):
    dot = lambda p, q: jnp.dot(p.astype(BF16), q, preferred_element_type=F32)
    a = dot(sa_ref[...], wa_ref[...])
    b = dot(comb_ref[...], wb_ref[...])
    merged = (jax.nn.sigmoid(z_ref[:, COL_GA:COL_GA + D_MODEL]) * a
              + jax.nn.sigmoid(z_ref[:, COL_GB:COL_GB + D_MODEL]) * b)
    x1 = x_ref[...] + dot(merged, wo_ref[...])
    x1_ref[...] = x1
    h2 = _rms(x1, g2_ref[...])
    h2p_ref[...] = _pack_bf16_pair(h2)
    logits = dot(h2, wr_ref[...]) + br_ref[...]
    slab, counts = _route(logits, cnt_in_ref[...], ltri_ref[...])
    rt_ref[...] = slab
    cnt_ref[...] = counts


def _sample_merge(xs, z, s_a, comb, wa, wb, wo, g2, wr, br, ltri, counts):
    return pl.pallas_call(
        _sample_merge_kernel,
        out_shape=[jax.ShapeDtypeStruct((DEC_BATCH, D_MODEL), F32),
                   jax.ShapeDtypeStruct((DEC_BATCH, HALF_D), U32),
                   jax.ShapeDtypeStruct((DEC_BATCH, LANES), F32),
                   jax.ShapeDtypeStruct((1, LANES), F32)],
        compiler_params=pltpu.CompilerParams(vmem_limit_bytes=VMEM_LIMIT),
        name="sample_merge_route",
    )(xs, z, s_a, comb, wa, wb, wo, g2, wr, br, ltri, counts)


def _dispatch_kernel(dest_ref, h_ref, xs_in_ref, xs_ref, sem):
    del xs_in_ref
    rows = h_ref.shape[0]
    base = pl.program_id(0) * (2 * rows)

    def row_copy(t, slot):
        d = dest_ref[base + 2 * t + slot]
        return pltpu.make_async_copy(h_ref.at[pl.ds(t, 1)], xs_ref.at[pl.ds(d, 1)], sem)

    def start(t, c):
        row_copy(t, 0).start()
        row_copy(t, 1).start()
        return c

    lax.fori_loop(0, rows, start, 0, unroll=ROW_DMA_UNROLL)
    for _ in range(2):
        pltpu.make_async_copy(h_ref, xs_ref.at[pl.ds(0, rows)], sem).wait()


def _dispatch(dest, h2p, xs, rows):
    n = h2p.shape[0]
    return pl.pallas_call(
        _dispatch_kernel,
        grid_spec=pltpu.PrefetchScalarGridSpec(
            num_scalar_prefetch=1,
            grid=(n // rows,),
            in_specs=[pl.BlockSpec((rows, HALF_D), lambda i, dest: (i, 0)), pl.BlockSpec(memory_space=pl.ANY)],
            out_specs=pl.BlockSpec(memory_space=pl.ANY),
            scratch_shapes=[pltpu.SemaphoreType.DMA(())],
        ),
        out_shape=jax.ShapeDtypeStruct(xs.shape, xs.dtype),
        input_output_aliases={2: 0},
        compiler_params=_cparams("arbitrary"),
        name=f"moe_dispatch_{rows}",
    )(dest, h2p, xs)


def _ffn_kernel(be_ref, nused_ref, nexte_ref, wslot_ref, xs_ref, wg_ref, wu_ref, wd_ref, ys_ref,
                wg_b, wu_b, wd_b, wg_f, wu_f, wd_f, wsem):
    i = pl.program_id(0)
    live = i < nused_ref[0]
    new_expert = jnp.logical_and(live, jnp.logical_or(i == 0, be_ref[i] != be_ref[jnp.maximum(i - 1, 0)]))

    def weight_copies(e, slot):
        return [pltpu.make_async_copy(src.at[e], dst.at[slot], wsem.at[slot])
                for src, dst in ((wg_ref, wg_f), (wu_ref, wu_f), (wd_ref, wd_f))]

    @pl.when(i == 0)
    def _():
        for cp in weight_copies(be_ref[0], 0):
            cp.start()

    @pl.when(new_expert)
    def _():
        slot = wslot_ref[i]
        for cp in weight_copies(be_ref[i], slot):
            cp.wait()
        wg_b[...] = wg_f[slot].astype(BF16)
        wu_b[...] = wu_f[slot].astype(BF16)
        wd_b[...] = wd_f[slot].astype(BF16)

        @pl.when(nexte_ref[i] >= 0)
        def _():
            for cp in weight_copies(nexte_ref[i], 1 - slot):
                cp.start()

    @pl.when(live)
    def _():
        lo, hi = _unpack_bf16_pair(xs_ref[...])
        gate = (jnp.dot(lo, wg_b[:HALF_D, :], preferred_element_type=F32)
                + jnp.dot(hi, wg_b[HALF_D:, :], preferred_element_type=F32))
        up = (jnp.dot(lo, wu_b[:HALF_D, :], preferred_element_type=F32)
              + jnp.dot(hi, wu_b[HALF_D:, :], preferred_element_type=F32))
        hid = (jax.nn.silu(gate) * up).astype(BF16)
        ys_ref[...] = jnp.dot(hid, wd_b[...], preferred_element_type=F32)

    @pl.when(i >= nused_ref[0])
    def _():
        ys_ref[...] = jnp.zeros_like(ys_ref)


def _expert_ffn(block_e, nused, xs, w_gate, w_up, w_down):
    idx = jnp.arange(MOE_NB, dtype=jnp.int32)
    change = ((block_e != jnp.roll(block_e, 1)) | (idx == 0)) & (idx < nused[0])
    wslot = ((jnp.cumsum(change.astype(jnp.int32)) - 1) & 1).astype(jnp.int32)
    change_at = jnp.where(change, idx, MOE_NB)
    next_change = jnp.flip(lax.cummin(jnp.flip(jnp.concatenate([change_at[1:], jnp.full((1,), MOE_NB, jnp.int32)]))))
    nexte = jnp.where(next_change < MOE_NB, block_e[jnp.minimum(next_change, MOE_NB - 1)], -1).astype(jnp.int32)
    hbm = pl.BlockSpec(memory_space=pl.ANY)
    return pl.pallas_call(
        _ffn_kernel,
        grid_spec=pltpu.PrefetchScalarGridSpec(
            num_scalar_prefetch=4,
            grid=(MOE_NB,),
            in_specs=[pl.BlockSpec((MOE_TB, HALF_D), lambda i, *_: (i, 0)), hbm, hbm, hbm],
            out_specs=pl.BlockSpec((MOE_TB, D_MODEL), lambda i, *_: (i, 0)),
            scratch_shapes=[pltpu.VMEM((D_MODEL, D_EXPERT), BF16), pltpu.VMEM((D_MODEL, D_EXPERT), BF16),
                            pltpu.VMEM((D_EXPERT, D_MODEL), BF16),
                            pltpu.VMEM((2, D_MODEL, D_EXPERT), F32), pltpu.VMEM((2, D_MODEL, D_EXPERT), F32),
                            pltpu.VMEM((2, D_EXPERT, D_MODEL), F32), pltpu.SemaphoreType.DMA((2,))],
        ),
        out_shape=jax.ShapeDtypeStruct((MOE_ROWS, D_MODEL), F32),
        compiler_params=_cparams("arbitrary"),
        name="moe_ffn",
    )(block_e, nused, nexte, wslot, xs, w_gate, w_up, w_down)


def _combine_kernel(dest_ref, x1_ref, rt_ref, gf_ref, ys_ref, y_ref, ya, yb, sem):
    rows = x1_ref.shape[0]
    base = pl.program_id(0) * (2 * rows)

    def row_copy(t, slot, buf):
        d = dest_ref[base + 2 * t + slot]
        return pltpu.make_async_copy(ys_ref.at[pl.ds(d, 1)], buf.at[pl.ds(t, 1)], sem)

    def start(t, c):
        row_copy(t, 0, ya).start()
        row_copy(t, 1, yb).start()
        return c

    lax.fori_loop(0, rows, start, 0, unroll=ROW_DMA_UNROLL)
    pltpu.make_async_copy(ys_ref.at[pl.ds(0, rows)], ya, sem).wait()
    pltpu.make_async_copy(ys_ref.at[pl.ds(0, rows)], yb, sem).wait()
    x2 = x1_ref[...] + rt_ref[:, 4:5] * ya[...] + rt_ref[:, 5:6] * yb[...]
    y_ref[...] = _rms(x2, gf_ref[...])


def _combine(dest, x1, rt, gf, ys, rows):
    n = x1.shape[0]
    return pl.pallas_call(
        _combine_kernel,
        grid_spec=pltpu.PrefetchScalarGridSpec(
            num_scalar_prefetch=1,
            grid=(n // rows,),
            in_specs=[pl.BlockSpec((rows, D_MODEL), lambda i, dest: (i, 0)),
                      pl.BlockSpec((rows, LANES), lambda i, dest: (i, 0)),
                      pl.BlockSpec((1, D_MODEL), lambda i, dest: (0, 0)),
                      pl.BlockSpec(memory_space=pl.ANY)],
            out_specs=pl.BlockSpec((rows, D_MODEL), lambda i, dest: (i, 0)),
            scratch_shapes=[pltpu.VMEM((rows, D_MODEL), F32), pltpu.VMEM((rows, D_MODEL), F32),
                            pltpu.SemaphoreType.DMA(())],
        ),
        out_shape=jax.ShapeDtypeStruct((n, D_MODEL), F32),
        compiler_params=_cparams("arbitrary"),
        name=f"moe_combine_{rows}",
    )(dest, x1, rt, gf, ys)


def _rope_tables(pos):
    half = HEAD_DIM // 2
    inv = 1.0 / (ROPE_THETA ** (jnp.arange(half, dtype=F32) * (2.0 / HEAD_DIM)))
    ang = pos.astype(F32)[:, None] * inv[None, :]
    cos, sin = jnp.cos(ang), jnp.sin(ang)
    return jnp.concatenate([cos, cos, cos, cos], axis=1), jnp.concatenate([-sin, sin, -sin, sin], axis=1)


def _slot_dest(rt, pstarts):
    e = rt[:, 0:2].astype(jnp.int32)
    pos = rt[:, 2:4].astype(jnp.int32)
    onehot = e[:, :, None] == jnp.arange(N_EXPERTS, dtype=jnp.int32)[None, None, :]
    return (jnp.sum(jnp.where(onehot, pstarts[None, None, :], 0), axis=-1) + pos).reshape(-1)


def kernel(x_prompt, x_sample, cache_kv_g0, cache_kv_g1, cache_kv_g2, norm1_g, w_in, b_in, a_ln_g, a_ln_b, w_spatial, b_spatial, w_a_proj, w_b_proj, w_o, norm2_g, w_group_router, b_group_router, w_expert_router, b_expert_router, w_gate, w_up, w_down, final_norm_g):
    x2 = x_prompt.reshape(T_PROMPT, D_MODEL)
    xs = x_sample.reshape(DEC_BATCH, D_MODEL)
    g1 = norm1_g[0][None, :]
    g2 = norm2_g[0][None, :]
    gf = final_norm_g[None, :]
    b_in2 = b_in[0][None, :]
    lng, lnb = a_ln_g[0][None, :], a_ln_b[0][None, :]

    causal = jnp.tril(jnp.ones((CHUNK, CHUNK), dtype=bool))
    ws_tril = jnp.where(causal[None], w_spatial[0], 0.0)
    wsp = jnp.concatenate([ws_tril[0::2], ws_tril[1::2]], axis=2).astype(BF16)
    bsp = jnp.repeat(b_spatial[0].T, A_GROUP_DIM, axis=1)
    ws0 = jnp.repeat(ws_tril[:, 0, 0].astype(BF16).astype(F32), A_GROUP_DIM)[None, :]
    bs0 = jnp.repeat(b_spatial[0][:, 0], A_GROUP_DIM)[None, :]
    w_router = jnp.zeros((D_MODEL, LANES), F32)
    w_router = w_router.at[:, :N_EXPERT_GROUPS].set(w_group_router[0])
    w_router = w_router.at[:, N_EXPERT_GROUPS:N_EXPERT_GROUPS + N_EXPERTS].set(w_expert_router[0])
    b_router = jnp.zeros((1, LANES), F32)
    b_router = b_router.at[0, :N_EXPERT_GROUPS].set(b_group_router[0])
    b_router = b_router.at[0, N_EXPERT_GROUPS:N_EXPERT_GROUPS + N_EXPERTS].set(b_expert_router[0])
    w_router = w_router.astype(BF16)
    ltri = jnp.tril(jnp.ones((TM, TM), BF16), -1)
    w_in_b, w_a_b, w_b_b, w_o_b = (w[0].astype(BF16) for w in (w_in, w_a_proj, w_b_proj, w_o))

    cos_p, sin_p = _rope_tables(jnp.arange(SEQ, dtype=jnp.int32))
    cos_s, sin_s = _rope_tables(jnp.full((1,), PAST_LEN, jnp.int32))

    uv, qkv0, qkv1, qkv2, gates, kvp0, kvp1, kvp2 = _inproj(
        x2, g1, w_in_b, b_in2, lng, lnb, cos_p, sin_p)
    ob = _attention((qkv0.reshape(BATCH, 1, SEQ, 3 * GROUP_W), qkv1, qkv2)).reshape(T_PROMPT, GROUP_W)
    x1_p, h2p_p, rt_p, counts_p = _merge(
        x2, uv, gates, ob, wsp, bsp, w_a_b, w_b_b, w_o_b, g2, w_router, b_router, ltri)

    z_s = _sample_inproj(xs, g1, w_in_b, b_in2)
    vrow, sa_s, qr_s, kr_s, kvs0, kvs1, kvs2 = _sample_pre(z_s, lng, lnb, ws0, bs0, cos_s, sin_s)
    dim_major = lambda t: t.reshape(DEC_BATCH, 3, HEADS, HEAD_DIM).transpose(0, 1, 3, 2)
    caches = [c.transpose(0, 1, 3, 4, 5, 2).reshape(DEC_BATCH, 2, HEADS, HEAD_DIM, c.shape[2])
              for c in (cache_kv_g0, cache_kv_g1, cache_kv_g2)]
    comb_s = _sample_attn(dim_major(qr_s), dim_major(kr_s), dim_major(z_s[:, COL_VB:COL_VB + QK_W]), caches)
    comb_s = comb_s.transpose(0, 2, 1).reshape(DEC_BATCH, GROUP_W)
    x1_s, h2p_s, rt_s, counts = _sample_merge(
        xs, z_s, sa_s, comb_s, w_a_b, w_b_b, w_o_b, g2, w_router, b_router,
        ltri[:DEC_BATCH, :DEC_BATCH], counts_p)

    cnt = counts[0, :N_EXPERTS].astype(jnp.int32)
    padded = (cnt + MOE_TB - 1) // MOE_TB * MOE_TB
    pends = jnp.cumsum(padded)
    pstarts = pends - padded
    block_starts = jnp.arange(MOE_NB, dtype=jnp.int32) * MOE_TB
    block_e = jnp.minimum(jnp.sum((pends[None, :] <= block_starts[:, None]).astype(jnp.int32), axis=1),
                          N_EXPERTS - 1)
    nused = (pends[-1:] // MOE_TB).astype(jnp.int32)
    dest_p = _slot_dest(rt_p, pstarts)
    dest_s = _slot_dest(rt_s, pstarts)

    rows = jnp.zeros((MOE_ROWS, HALF_D), U32)
    rows = _dispatch(dest_p, h2p_p, rows, TM)
    rows = _dispatch(dest_s, h2p_s, rows, DEC_BATCH)
    ys = _expert_ffn(block_e, nused, rows, w_gate[0], w_up[0], w_down[0])
    y_p = _combine(dest_p, x1_p, rt_p, gf, ys, TM)
    y_s = _combine(dest_s, x1_s, rt_s, gf, ys, DEC_BATCH)

    kv_shape = lambda n, w: (1, n, w, 2, HEADS, HEAD_DIM)
    window_rows = lambda t: t.reshape(1, BATCH, 2, HEADS, HEAD_DIM, t.shape[2]).transpose(0, 1, 5, 2, 3, 4)
    return (y_p.reshape(BATCH, SEQ, D_MODEL), y_s.reshape(DEC_BATCH, 1, D_MODEL),
            window_rows(kvp0), window_rows(kvp1), window_rows(kvp2),
            kvs0.reshape(kv_shape(DEC_BATCH, 1)), kvs1.reshape(kv_shape(DEC_BATCH, 1)),
            kvs2.reshape(kv_shape(DEC_BATCH, 1)), vrow.reshape(1, DEC_BATCH, 1, A_WIDTH))
```

```python
import functools

import jax
import jax.numpy as jnp
import numpy as np
from jax import lax
from jax.experimental import pallas as pl
from jax.experimental.pallas import tpu as pltpu

F32 = jnp.float32
BF16 = jnp.bfloat16
U32 = jnp.uint32

D_MODEL = 1024
BATCH = 2
SEQ = 8192
DEC_BATCH = 32
PAST_LEN = 8192
CHUNK = 128
A_GROUPS = 8
A_GROUP_DIM = 64
A_WIDTH = 512
HEAD_DIM = 64
HEADS = 4
GROUP_W = HEADS * HEAD_DIM
DILATIONS = (1, 4, 16)
SPAN = 128
QK_W = 768
IN_WIDTH = 5376
COL_U, COL_V, COL_Q, COL_K, COL_VB, COL_GA, COL_GB = 0, 512, 1024, 1792, 2560, 3328, 4352
N_EXPERT_GROUPS = 4
EXPERTS_PER_GROUP = 8
N_EXPERTS = 32
D_EXPERT = 512
ROPE_THETA = 10000.0
EPS = 1e-6

LANES = 128
T_PROMPT = BATCH * SEQ
TM = 512
TILES_PER_SEQ = SEQ // TM
WINDOWS = tuple(min(SPAN * d, SEQ) for d in DILATIONS)
TA = SPAN * max(DILATIONS)
ATT_UNROLL = 2
MOE_TB = 512
ROW_DMA_UNROLL = 8
ROUTE_ROWS = 8
N_SLOTS = 2 * (T_PROMPT + DEC_BATCH)
MOE_NB = -(-N_SLOTS // MOE_TB) + N_EXPERTS
MOE_ROWS = MOE_NB * MOE_TB
HALF_D = D_MODEL // 2
NEG = -1e30
VMEM_LIMIT = 56 * 1024 * 1024


def _cparams(*sem):
    return pltpu.CompilerParams(dimension_semantics=sem, vmem_limit_bytes=VMEM_LIMIT)


def _rms(x, g):
    return x * lax.rsqrt(jnp.mean(x * x, axis=-1, keepdims=True) + EPS) * g


def _layer_norm(x, g, b):
    mu = jnp.mean(x, axis=-1, keepdims=True)
    xc = x - mu
    var = jnp.mean(xc * xc, axis=-1, keepdims=True)
    return xc * lax.rsqrt(var + EPS) * g + b


def _rope_chunk(x, cos, sin_signed):
    lane = lax.broadcasted_iota(jnp.int32, x.shape, 1)
    first_half = (lane % HEAD_DIM) < (HEAD_DIM // 2)
    swapped = jnp.where(first_half, pltpu.roll(x, LANES - HEAD_DIM // 2, 1), pltpu.roll(x, HEAD_DIM // 2, 1))
    return x * cos + swapped * sin_signed


def _pack_bf16_pair(h):
    lo = lax.bitcast_convert_type(h[:, :HALF_D].astype(BF16).astype(F32), U32)
    hi = lax.bitcast_convert_type(h[:, HALF_D:].astype(BF16).astype(F32), U32)
    return (hi & jnp.uint32(0xFFFF0000)) | (lo >> 16)


def _unpack_bf16_pair(p):
    lo = lax.bitcast_convert_type(p << 16, F32).astype(BF16)
    hi = lax.bitcast_convert_type(p & jnp.uint32(0xFFFF0000), F32).astype(BF16)
    return lo, hi


def _route(logits, prior_counts, ltri):
    rows = logits.shape[0]
    lane = lax.broadcasted_iota(jnp.int32, (rows, LANES), 1).astype(F32)
    is_g = lane < N_EXPERT_GROUPS
    gl = jnp.where(is_g, logits, NEG)
    gmax = jnp.max(gl, axis=1, keepdims=True)
    grp = jnp.min(jnp.where(gl == gmax, lane, float(LANES)), axis=1, keepdims=True)
    p_grp = 1.0 / jnp.sum(jnp.where(is_g, jnp.exp(gl - gmax), 0.0), axis=1, keepdims=True)
    lo = N_EXPERT_GROUPS + grp * EXPERTS_PER_GROUP
    el = jnp.where((lane >= lo) & (lane < lo + EXPERTS_PER_GROUP), logits, NEG)
    v1 = jnp.max(el, axis=1, keepdims=True)
    i1 = jnp.min(jnp.where(el == v1, lane, float(LANES)), axis=1, keepdims=True)
    el2 = jnp.where(lane == i1, NEG, el)
    v2 = jnp.max(el2, axis=1, keepdims=True)
    i2 = jnp.min(jnp.where(el2 == v2, lane, float(LANES)), axis=1, keepdims=True)
    t = jnp.exp(v2 - v1)
    gate1 = p_grp / (1.0 + t)
    gate2 = p_grp * t / (1.0 + t)
    e1 = i1 - N_EXPERT_GROUPS
    e2 = i2 - N_EXPERT_GROUPS
    hit1 = lane == e1
    hit2 = lane == e2
    onehot = jnp.where(hit1 | hit2, 1.0, 0.0)
    rank = jnp.dot(ltri, onehot.astype(BF16), preferred_element_type=F32) + prior_counts
    pos1 = jnp.sum(jnp.where(hit1, rank, 0.0), axis=1, keepdims=True)
    pos2 = jnp.sum(jnp.where(hit2, rank, 0.0), axis=1, keepdims=True)
    slab = jnp.where(lane == 0, e1, jnp.where(lane == 1, e2, jnp.where(lane == 2, pos1, jnp.where(
        lane == 3, pos2, jnp.where(lane == 4, gate1, jnp.where(lane == 5, gate2, 0.0))))))
    return slab, prior_counts + jnp.sum(onehot, axis=0, keepdims=True)


def _inproj_kernel(x_ref, g1_ref, w_ref, b_ref, lng_ref, lnb_ref, cos_ref, sin_ref,
                   uv_ref, qkv0_ref, qkv1_ref, qkv2_ref, gates_ref, kv0_ref, kv1_ref, kv2_ref, de_ref):
    tile_in_seq = pl.program_id(0) % TILES_PER_SEQ
    hb = _rms(x_ref[...], g1_ref[...]).astype(BF16)

    def seg(lo, width):
        return jnp.dot(hb, w_ref[:, lo:lo + width], preferred_element_type=F32) + b_ref[:, lo:lo + width]

    uv_ref[:, :A_WIDTH] = jax.nn.gelu(seg(COL_U, A_WIDTH)).astype(BF16)
    uv_ref[:, A_WIDTH:] = _layer_norm(jax.nn.gelu(seg(COL_V, A_WIDTH)), lng_ref[...], lnb_ref[...]).astype(BF16)
    gates_ref[:, :D_MODEL] = jax.nn.sigmoid(seg(COL_GA, D_MODEL)).astype(BF16)
    gates_ref[:, D_MODEL:] = jax.nn.sigmoid(seg(COL_GB, D_MODEL)).astype(BF16)

    cos = cos_ref[...]
    sin = sin_ref[...]
    qkv_refs = (qkv0_ref, qkv1_ref, qkv2_ref)
    kv_refs = (kv0_ref, kv1_ref, kv2_ref)
    for g, d in enumerate(DILATIONS):
        q = seg(COL_Q + g * GROUP_W, GROUP_W)
        k = seg(COL_K + g * GROUP_W, GROUP_W)
        v = seg(COL_VB + g * GROUP_W, GROUP_W)
        chunks = GROUP_W // LANES
        for c in range(chunks):
            sl = slice(c * LANES, (c + 1) * LANES)
            de_ref[c] = _rope_chunk(q[:, sl], cos, sin) * (HEAD_DIM ** -0.5)
            de_ref[chunks + c] = _rope_chunk(k[:, sl], cos, sin)
            de_ref[2 * chunks + c] = v[:, sl]
        for c in range(3 * chunks):
            sl = slice(c * LANES, (c + 1) * LANES)
            if d == 1:
                qkv_refs[g][:, sl] = de_ref[c].astype(BF16)
            else:
                for r in range(d):
                    qkv_refs[g][0, r, :, sl] = de_ref[c, pl.ds(r, TM // d, stride=d), :].astype(BF16)
        kv_rows = kv_refs[g].shape[2]
        first = TILES_PER_SEQ - WINDOWS[g] // kv_rows

        @pl.when(tile_in_seq >= first)
        def _(kv_ref=kv_refs[g], kv_rows=kv_rows, chunks=chunks):
            for c in range(2 * chunks):
                kv_ref[0, c * LANES:(c + 1) * LANES, :] = de_ref[chunks + c, TM - kv_rows:, :].T


def _inproj(x2, g1, w_bf, b_in, lng, lnb, cos_t, sin_t):
    tiles_per_seq = TILES_PER_SEQ
    n_tiles = T_PROMPT // TM
    const = lambda i: (0, 0)

    def kv_spec(w):
        rows = min(w, TM)
        first = tiles_per_seq - w // rows
        return pl.BlockSpec((1, 2 * GROUP_W, rows),
                            lambda i: (i // tiles_per_seq, 0, jnp.maximum(i % tiles_per_seq - first, 0)))

    def regrouped_spec(d):
        return pl.BlockSpec((1, d, TM // d, 3 * GROUP_W), lambda i: (i // tiles_per_seq, 0, i % tiles_per_seq, 0))

    return pl.pallas_call(
        _inproj_kernel,
        grid=(n_tiles,),
        in_specs=[
            pl.BlockSpec((TM, D_MODEL), lambda i: (i, 0)),
            pl.BlockSpec((1, D_MODEL), const),
            pl.BlockSpec((D_MODEL, IN_WIDTH), const),
            pl.BlockSpec((1, IN_WIDTH), const),
            pl.BlockSpec((1, A_WIDTH), const),
            pl.BlockSpec((1, A_WIDTH), const),
            pl.BlockSpec((TM, LANES), lambda i: (i % tiles_per_seq, 0)),
            pl.BlockSpec((TM, LANES), lambda i: (i % tiles_per_seq, 0)),
        ],
        out_specs=[
            pl.BlockSpec((TM, 2 * A_WIDTH), lambda i: (i, 0)),
            pl.BlockSpec((TM, 3 * GROUP_W), lambda i: (i, 0)),
            regrouped_spec(DILATIONS[1]),
            regrouped_spec(DILATIONS[2]),
            pl.BlockSpec((TM, 2 * D_MODEL), lambda i: (i, 0)),
            kv_spec(WINDOWS[0]), kv_spec(WINDOWS[1]), kv_spec(WINDOWS[2]),
        ],
        out_shape=[
            jax.ShapeDtypeStruct((T_PROMPT, 2 * A_WIDTH), BF16),
            jax.ShapeDtypeStruct((T_PROMPT, 3 * GROUP_W), BF16),
            jax.ShapeDtypeStruct((BATCH, DILATIONS[1], SEQ // DILATIONS[1], 3 * GROUP_W), BF16),
            jax.ShapeDtypeStruct((BATCH, DILATIONS[2], SEQ // DILATIONS[2], 3 * GROUP_W), BF16),
            jax.ShapeDtypeStruct((T_PROMPT, 2 * D_MODEL), BF16),
            jax.ShapeDtypeStruct((BATCH, 2 * GROUP_W, WINDOWS[0]), F32),
            jax.ShapeDtypeStruct((BATCH, 2 * GROUP_W, WINDOWS[1]), F32),
            jax.ShapeDtypeStruct((BATCH, 2 * GROUP_W, WINDOWS[2]), F32),
        ],
        scratch_shapes=[pltpu.VMEM((3 * GROUP_W // LANES, TM, LANES), F32)],
        compiler_params=_cparams("arbitrary"),
        name="inproj",
    )(x2, g1, w_bf, b_in, lng, lnb, cos_t, sin_t)


def _attn_kernel(c0_ref, p0_ref, c1_ref, p1_ref, c2_ref, p2_ref, o_ref, acc_ref, m_ref, l_ref):
    n = pl.program_id(1)
    acc_ref[...] = jnp.zeros_like(acc_ref)
    l_ref[...] = jnp.zeros_like(l_ref)
    m_ref[...] = jnp.full_like(m_ref, NEG)

    qi = lax.broadcasted_iota(jnp.int32, (SPAN, 2 * SPAN), 0)
    ki = lax.broadcasted_iota(jnp.int32, (SPAN, 2 * SPAN), 1)
    band = (ki >= qi) & (ki <= qi + SPAN)
    band_first = band & ((ki >= SPAN) | (n > 0))
    q_head = lax.broadcasted_iota(jnp.int32, (SPAN, GROUP_W), 1) // HEAD_DIM
    kv_head = lax.broadcasted_iota(jnp.int32, (2 * SPAN, GROUP_W), 1) // HEAD_DIM
    contract_last = (((1,), (1,)), ((), ()))
    k_cols = slice(GROUP_W, 2 * GROUP_W)
    v_cols = slice(2 * GROUP_W, 3 * GROUP_W)

    def attend(q, keys, vals, mask, tok_rows):
        scores = []
        for h in range(HEADS):
            qh = jnp.where(q_head == h, q, jnp.zeros_like(q))
            s = lax.dot_general(qh, keys, contract_last, preferred_element_type=F32)
            scores.append(jnp.where(mask, s, NEG))
        s = jnp.concatenate(scores, axis=0)
        m = jnp.max(s, axis=1, keepdims=True)
        p = jnp.exp(s - m)
        l = jnp.sum(p, axis=1, keepdims=True)
        pb = p.astype(BF16)
        acc = jnp.zeros((SPAN, GROUP_W), F32)
        m_b = jnp.zeros((SPAN, GROUP_W), F32)
        l_b = jnp.zeros((SPAN, GROUP_W), F32)
        for h in range(HEADS):
            rows = slice(h * SPAN, (h + 1) * SPAN)
            vh = jnp.where(kv_head == h, vals, jnp.zeros_like(vals))
            acc = acc + jnp.dot(pb[rows], vh, preferred_element_type=F32)
            m_b = jnp.where(q_head == h, m[rows], m_b)
            l_b = jnp.where(q_head == h, l[rows], l_b)
        for c in range(GROUP_W // LANES):
            sl = slice(c * LANES, (c + 1) * LANES)
            m_old = m_ref[c, tok_rows, :]
            m_new = jnp.maximum(m_old, m_b[:, sl])
            a_old = jnp.exp(m_old - m_new)
            a_blk = jnp.exp(m_b[:, sl] - m_new)
            l_ref[c, tok_rows, :] = a_old * l_ref[c, tok_rows, :] + a_blk * l_b[:, sl]
            acc_ref[c, tok_rows, :] = a_old * acc_ref[c, tok_rows, :] + a_blk * acc[:, sl]
            m_ref[c, tok_rows, :] = m_new

    def stream(c_ref, p_ref, r, d):
        n_blocks = c_ref.shape[2] // SPAN

        def tok_rows(j):
            start = j * (SPAN * d) + r
            return pl.ds(start, SPAN) if d == 1 else pl.ds(start, SPAN, stride=d)

        keys = jnp.concatenate([p_ref[0, r, :, k_cols], c_ref[0, r, :SPAN, k_cols]], axis=0)
        vals = jnp.concatenate([p_ref[0, r, :, v_cols], c_ref[0, r, :SPAN, v_cols]], axis=0)
        attend(c_ref[0, r, :SPAN, :GROUP_W], keys, vals, band_first, tok_rows(0))

        def later_block(j, carry):
            q_rows = pl.ds(pl.multiple_of(j * SPAN, SPAN), SPAN)
            kv_rows = pl.ds(pl.multiple_of((j - 1) * SPAN, SPAN), 2 * SPAN)
            attend(c_ref[0, r, q_rows, :GROUP_W], c_ref[0, r, kv_rows, k_cols], c_ref[0, r, kv_rows, v_cols],
                   band, tok_rows(j))
            return carry

        if n_blocks > 1:
            lax.fori_loop(1, n_blocks, later_block, 0, unroll=3)

    for (c_ref, p_ref), d in zip(((c0_ref, p0_ref), (c1_ref, p1_ref), (c2_ref, p2_ref)), DILATIONS):
        if d == 1:
            stream(c_ref, p_ref, 0, d)
        else:
            lax.fori_loop(0, d, lambda r, carry, c_ref=c_ref, p_ref=p_ref, d=d: (stream(c_ref, p_ref, r, d), carry)[1], 0,
                          unroll=ATT_UNROLL if c_ref.shape[2] == SPAN else 1)

    for c in range(GROUP_W // LANES):
        o_ref[0, :, c * LANES:(c + 1) * LANES] = (acc_ref[c] / l_ref[c]).astype(BF16)


def _attention(qkv_by_group):
    in_specs, args = [], []
    for qkv, d in zip(qkv_by_group, DILATIONS):
        rows = TA // d
        blocks_per_tile = rows // SPAN
        in_specs.append(pl.BlockSpec((1, d, rows, 3 * GROUP_W), lambda b, n: (b, 0, n, 0)))
        in_specs.append(pl.BlockSpec((1, d, SPAN, 3 * GROUP_W),
                                     lambda b, n, k=blocks_per_tile: (b, 0, jnp.maximum(n * k - 1, 0), 0)))
        args += [qkv, qkv]
    return pl.pallas_call(
        _attn_kernel,
        grid=(BATCH, SEQ // TA),
        in_specs=in_specs,
        out_specs=pl.BlockSpec((1, TA, GROUP_W), lambda b, n: (b, n, 0)),
        out_shape=jax.ShapeDtypeStruct((BATCH, SEQ, GROUP_W), BF16),
        scratch_shapes=[pltpu.VMEM((GROUP_W // LANES, TA, LANES), F32)] * 3,
        compiler_params=_cparams("arbitrary", "arbitrary"),
        name="attn",
    )(*args)


def _merge_kernel(x_ref, uv_ref, gates_ref, ob_ref,
                  wsp_ref, bsp_ref, wa_ref, wb_ref, wo_ref, g2_ref, wr_ref, br_ref, ltri_ref,
                  x1_ref, h2p_ref, rt_ref, rtt_ref, cnt_ref, run_ref):
    @pl.when(pl.program_id(0) == 0)
    def _():
        run_ref[...] = jnp.zeros_like(run_ref)

    lane = lax.broadcasted_iota(jnp.int32, (CHUNK, LANES), 1)
    left = lane < A_GROUP_DIM
    zero = jnp.zeros((CHUNK, LANES), BF16)
    sa_chunks = []
    for c in range(TM // CHUNK):
        rows = slice(c * CHUNK, (c + 1) * CHUNK)
        pairs = []
        for p in range(A_GROUPS // 2):
            vp = uv_ref[rows, A_WIDTH + p * LANES:A_WIDTH + (p + 1) * LANES]
            rhs = jnp.concatenate([jnp.where(left, vp, zero), jnp.where(left, zero, vp)], axis=0)
            pairs.append(jnp.dot(wsp_ref[p], rhs, preferred_element_type=F32))
        mixed = jnp.concatenate(pairs, axis=1) + bsp_ref[...]
        sa_chunks.append((uv_ref[rows, :A_WIDTH].astype(F32) * mixed).astype(BF16))
    s_a = jnp.concatenate(sa_chunks, axis=0)

    a = jnp.dot(s_a, wa_ref[...], preferred_element_type=F32)
    b = jnp.dot(ob_ref[...], wb_ref[...], preferred_element_type=F32)
    merged = gates_ref[:, :D_MODEL].astype(F32) * a + gates_ref[:, D_MODEL:].astype(F32) * b
    x1 = x_ref[...] + jnp.dot(merged.astype(BF16), wo_ref[...], preferred_element_type=F32)
    x1_ref[...] = x1

    h2 = _rms(x1, g2_ref[...])
    h2p_ref[...] = _pack_bf16_pair(h2)
    logits = jnp.dot(h2.astype(BF16), wr_ref[...], preferred_element_type=F32) + br_ref[...]
    slab, counts = _route(logits, run_ref[...], ltri_ref[...])
    rt_ref[...] = slab
    rtt_ref[...] = slab.T[:ROUTE_ROWS, :]
    run_ref[...] = counts
    cnt_ref[...] = counts


def _merge(x2, uv, gates, ob, wsp, bsp, wa, wb, wo, g2, wr, br, ltri):
    n_tiles = T_PROMPT // TM
    tile = lambda w: pl.BlockSpec((TM, w), lambda i: (i, 0))
    full = lambda a: pl.BlockSpec(a.shape, lambda i: (0,) * a.ndim)
    return pl.pallas_call(
        _merge_kernel,
        grid=(n_tiles,),
        in_specs=[tile(D_MODEL), tile(2 * A_WIDTH), tile(2 * D_MODEL), tile(GROUP_W),
                  full(wsp), full(bsp), full(wa), full(wb), full(wo), full(g2), full(wr), full(br),
                  full(ltri)],
        out_specs=[tile(D_MODEL), tile(HALF_D), tile(LANES), pl.BlockSpec((ROUTE_ROWS, TM), lambda i: (i, 0)),
                   pl.BlockSpec((1, LANES), lambda i: (0, 0))],
        out_shape=[
            jax.ShapeDtypeStruct((T_PROMPT, D_MODEL), F32),
            jax.ShapeDtypeStruct((T_PROMPT, HALF_D), U32),
            jax.ShapeDtypeStruct((T_PROMPT, LANES), F32),
            jax.ShapeDtypeStruct((n_tiles * ROUTE_ROWS, TM), F32),
            jax.ShapeDtypeStruct((1, LANES), F32),
        ],
        scratch_shapes=[pltpu.VMEM((1, LANES), F32)],
        compiler_params=_cparams("arbitrary"),
        name="merge_route",
    )(x2, uv, gates, ob, wsp, bsp, wa, wb, wo, g2, wr, br, ltri)


SAMPLE_COLS = 768


def _sample_inproj_kernel(x_ref, g1_ref, w_ref, b_ref, z_ref):
    hb = _rms(x_ref[...], g1_ref[...]).astype(BF16)
    z_ref[...] = jnp.dot(hb, w_ref[...], preferred_element_type=F32) + b_ref[...]


def _sample_inproj(xs, g1, w_in, b_in):
    const = lambda j: (0, 0)
    return pl.pallas_call(
        _sample_inproj_kernel,
        grid=(IN_WIDTH // SAMPLE_COLS,),
        in_specs=[pl.BlockSpec((DEC_BATCH, D_MODEL), const), pl.BlockSpec((1, D_MODEL), const),
                  pl.BlockSpec((D_MODEL, SAMPLE_COLS), lambda j: (0, j)),
                  pl.BlockSpec((1, SAMPLE_COLS), lambda j: (0, j))],
        out_specs=pl.BlockSpec((DEC_BATCH, SAMPLE_COLS), lambda j: (0, j)),
        out_shape=jax.ShapeDtypeStruct((DEC_BATCH, IN_WIDTH), F32),
        compiler_params=_cparams("arbitrary"),
        name="sample_inproj",
    )(xs, g1, w_in, b_in)


def _sample_pre_kernel(z_ref, lng_ref, lnb_ref, ws0_ref, bs0_ref, cos_ref, sin_ref,
                       vrow_ref, sa_ref, qr_ref, kr_ref, kv0_ref, kv1_ref, kv2_ref):
    u = jax.nn.gelu(z_ref[:, COL_U:COL_U + A_WIDTH])
    va = _layer_norm(jax.nn.gelu(z_ref[:, COL_V:COL_V + A_WIDTH]), lng_ref[...], lnb_ref[...])
    vrow_ref[...] = va
    sa_ref[...] = u * (ws0_ref[...] * va.astype(BF16).astype(F32) + bs0_ref[...])
    cos, sin = cos_ref[...], sin_ref[...]
    for g, kv_ref in enumerate((kv0_ref, kv1_ref, kv2_ref)):
        for c in range(GROUP_W // LANES):
            off = g * GROUP_W + c * LANES
            qr_ref[:, off:off + LANES] = (_rope_chunk(z_ref[:, COL_Q + off:COL_Q + off + LANES], cos, sin)
                                          * (HEAD_DIM ** -0.5))
            kr = _rope_chunk(z_ref[:, COL_K + off:COL_K + off + LANES], cos, sin)
            kr_ref[:, off:off + LANES] = kr
            kv_ref[:, c * LANES:(c + 1) * LANES] = kr
        kv_ref[:, GROUP_W:] = z_ref[:, COL_VB + g * GROUP_W:COL_VB + (g + 1) * GROUP_W]


def _sample_pre(z, lng, lnb, ws0, bs0, cos_s, sin_s):
    row = lambda w: jax.ShapeDtypeStruct((DEC_BATCH, w), F32)
    return pl.pallas_call(
        _sample_pre_kernel,
        out_shape=[row(A_WIDTH), row(A_WIDTH), row(QK_W), row(QK_W),
                   row(2 * GROUP_W), row(2 * GROUP_W), row(2 * GROUP_W)],
        compiler_params=pltpu.CompilerParams(vmem_limit_bytes=VMEM_LIMIT),
        name="sample_pre",
    )(z, lng, lnb, ws0, bs0, cos_s, sin_s)


def _sample_attn_kernel(q_ref, k_ref, v_ref, c0_ref, c1_ref, c2_ref, comb_ref):
    as_operand = lambda t: t.astype(BF16).astype(F32)
    for h in range(HEADS):
        head = slice(h, h + 1)
        outs, lses = [], []
        for g, (c_ref, d) in enumerate(zip((c0_ref, c1_ref, c2_ref), DILATIONS)):
            qc, kn, vn = (as_operand(r[0, g, :, head]) for r in (q_ref, k_ref, v_ref))
            s = jnp.sum(as_operand(c_ref[0, 0, h]) * qc, axis=0, keepdims=True)
            if d > 1:
                pos = lax.broadcasted_iota(jnp.int32, s.shape, 1)
                s = jnp.where(pos % d == 0, s, NEG)
            s_n = jnp.sum(qc * kn, axis=0, keepdims=True)
            m = jnp.maximum(jnp.max(s, axis=1, keepdims=True), s_n)
            lse = m + jnp.log(jnp.sum(jnp.exp(s - m), axis=1, keepdims=True) + jnp.exp(s_n - m))
            p = as_operand(jnp.exp(s - lse))
            p_n = as_operand(jnp.exp(s_n - lse))
            outs.append(jnp.sum(as_operand(c_ref[0, 1, h]) * p, axis=1, keepdims=True) + p_n * vn)
            lses.append(lse)
        mx = jnp.maximum(jnp.maximum(lses[0], lses[1]), lses[2])
        ws = [jnp.exp(t - mx) for t in lses]
        tot = ws[0] + ws[1] + ws[2]
        comb_ref[0, :, head] = (ws[0] * outs[0] + ws[1] * outs[1] + ws[2] * outs[2]) / tot


def _sample_attn(qt, kt, vt, caches):
    new_spec = pl.BlockSpec((1, 3, HEAD_DIM, HEADS), lambda i: (i, 0, 0, 0))
    cache_spec = lambda c: pl.BlockSpec((1,) + c.shape[1:], lambda i: (i, 0, 0, 0, 0))
    return pl.pallas_call(
        _sample_attn_kernel,
        grid=(DEC_BATCH,),
        in_specs=[new_spec, new_spec, new_spec, cache_spec(caches[0]), cache_spec(caches[1]), cache_spec(caches[2])],
        out_specs=pl.BlockSpec((1, HEAD_DIM, HEADS), lambda i: (i, 0, 0)),
        out_shape=jax.ShapeDtypeStruct((DEC_BATCH, HEAD_DIM, HEADS), F32),
        compiler_params=_cparams("arbitrary"),
        name="sample_attn",
    )(qt, kt, vt, caches[0], caches[1], caches[2])


def _sample_merge_kernel(x_ref, z_ref, sa_ref, comb_ref, wa_ref, wb_ref, wo_ref, g2_ref, wr_ref, br_ref,
                         ltri_ref, cnt_in_ref, x1_ref, h2p_ref, rt_ref, cnt_ref):
    dot = lambda p, q: jnp.dot(p.astype(BF16), q, preferred_element_type=F32)
    a = dot(sa_ref[...], wa_ref[...])
    b = dot(comb_ref[...], wb_ref[...])
    merged = (jax.nn.sigmoid(z_ref[:, COL_GA:COL_GA + D_MODEL]) * a
              + jax.nn.sigmoid(z_ref[:, COL_GB:COL_GB + D_MODEL]) * b)
    x1 = x_ref[...] + dot(merged, wo_ref[...])
    x1_ref[...] = x1
    h2 = _rms(x1, g2_ref[...])
    h2p_ref[...] = _pack_bf16_pair(h2)
    logits = dot(h2, wr_ref[...]) + br_ref[...]
    slab, counts = _route(logits, cnt_in_ref[...], ltri_ref[...])
    rt_ref[...] = slab
    cnt_ref[...] = counts


def _sample_merge(xs, z, s_a, comb, wa, wb, wo, g2, wr, br, ltri, counts):
    return pl.pallas_call(
        _sample_merge_kernel,
        out_shape=[jax.ShapeDtypeStruct((DEC_BATCH, D_MODEL), F32),
                   jax.ShapeDtypeStruct((DEC_BATCH, HALF_D), U32),
                   jax.ShapeDtypeStruct((DEC_BATCH, LANES), F32),
                   jax.ShapeDtypeStruct((1, LANES), F32)],
        compiler_params=pltpu.CompilerParams(vmem_limit_bytes=VMEM_LIMIT),
        name="sample_merge_route",
    )(xs, z, s_a, comb, wa, wb, wo, g2, wr, br, ltri, counts)


def _dispatch_kernel(dest0_ref, dest1_ref, h_ref, xs_in_ref, xs_ref, sem):
    del xs_in_ref
    rows = h_ref.shape[0]
    base = pl.program_id(0) * rows

    def row_copy(t, dest_ref):
        return pltpu.make_async_copy(h_ref.at[pl.ds(t, 1)], xs_ref.at[pl.ds(dest_ref[base + t], 1)], sem)

    def start(t, c):
        row_copy(t, dest0_ref).start()
        row_copy(t, dest1_ref).start()
        return c

    lax.fori_loop(0, rows, start, 0, unroll=ROW_DMA_UNROLL)
    for _ in range(2):
        pltpu.make_async_copy(h_ref, xs_ref.at[pl.ds(0, rows)], sem).wait()


def _dispatch(dest, h2p, xs, rows):
    n = h2p.shape[0]
    return pl.pallas_call(
        _dispatch_kernel,
        grid_spec=pltpu.PrefetchScalarGridSpec(
            num_scalar_prefetch=2,
            grid=(n // rows,),
            in_specs=[pl.BlockSpec((rows, HALF_D), lambda i, *_: (i, 0)), pl.BlockSpec(memory_space=pl.ANY)],
            out_specs=pl.BlockSpec(memory_space=pl.ANY),
            scratch_shapes=[pltpu.SemaphoreType.DMA(())],
        ),
        out_shape=jax.ShapeDtypeStruct(xs.shape, xs.dtype),
        input_output_aliases={3: 0},
        compiler_params=_cparams("arbitrary"),
        name=f"moe_dispatch_{rows}",
    )(dest[0], dest[1], h2p, xs)


def _ffn_kernel(be_ref, nused_ref, nexte_ref, wslot_ref, xs_ref, wg_ref, wu_ref, wd_ref, ys_ref,
                wg_b, wu_b, wd_b, wg_f, wu_f, wd_f, wsem):
    i = pl.program_id(0)
    live = i < nused_ref[0]
    new_expert = jnp.logical_and(live, jnp.logical_or(i == 0, be_ref[i] != be_ref[jnp.maximum(i - 1, 0)]))

    def weight_copies(e, slot):
        return [pltpu.make_async_copy(src.at[e], dst.at[slot], wsem.at[slot])
                for src, dst in ((wg_ref, wg_f), (wu_ref, wu_f), (wd_ref, wd_f))]

    @pl.when(i == 0)
    def _():
        for cp in weight_copies(be_ref[0], 0):
            cp.start()

    @pl.when(new_expert)
    def _():
        slot = wslot_ref[i]
        for cp in weight_copies(be_ref[i], slot):
            cp.wait()
        wg_b[...] = wg_f[slot].astype(BF16)
        wu_b[...] = wu_f[slot].astype(BF16)
        wd_b[...] = wd_f[slot].astype(BF16)

        @pl.when(nexte_ref[i] >= 0)
        def _():
            for cp in weight_copies(nexte_ref[i], 1 - slot):
                cp.start()

    @pl.when(live)
    def _():
        lo, hi = _unpack_bf16_pair(xs_ref[...])
        gate = (jnp.dot(lo, wg_b[:HALF_D, :], preferred_element_type=F32)
                + jnp.dot(hi, wg_b[HALF_D:, :], preferred_element_type=F32))
        up = (jnp.dot(lo, wu_b[:HALF_D, :], preferred_element_type=F32)
              + jnp.dot(hi, wu_b[HALF_D:, :], preferred_element_type=F32))
        hid = (jax.nn.silu(gate) * up).astype(BF16)
        ys_ref[...] = jnp.dot(hid, wd_b[...], preferred_element_type=F32)

    @pl.when(i >= nused_ref[0])
    def _():
        ys_ref[...] = jnp.zeros_like(ys_ref)


def _expert_ffn(block_e, nused, xs, w_gate, w_up, w_down):
    idx = jnp.arange(MOE_NB, dtype=jnp.int32)
    change = ((block_e != jnp.roll(block_e, 1)) | (idx == 0)) & (idx < nused[0])
    wslot = ((jnp.cumsum(change.astype(jnp.int32)) - 1) & 1).astype(jnp.int32)
    change_at = jnp.where(change, idx, MOE_NB)
    next_change = jnp.flip(lax.cummin(jnp.flip(jnp.concatenate([change_at[1:], jnp.full((1,), MOE_NB, jnp.int32)]))))
    nexte = jnp.where(next_change < MOE_NB, block_e[jnp.minimum(next_change, MOE_NB - 1)], -1).astype(jnp.int32)
    hbm = pl.BlockSpec(memory_space=pl.ANY)
    return pl.pallas_call(
        _ffn_kernel,
        grid_spec=pltpu.PrefetchScalarGridSpec(
            num_scalar_prefetch=4,
            grid=(MOE_NB,),
            in_specs=[pl.BlockSpec((MOE_TB, HALF_D), lambda i, *_: (i, 0)), hbm, hbm, hbm],
            out_specs=pl.BlockSpec((MOE_TB, D_MODEL), lambda i, *_: (i, 0)),
            scratch_shapes=[pltpu.VMEM((D_MODEL, D_EXPERT), BF16), pltpu.VMEM((D_MODEL, D_EXPERT), BF16),
                            pltpu.VMEM((D_EXPERT, D_MODEL), BF16),
                            pltpu.VMEM((2, D_MODEL, D_EXPERT), F32), pltpu.VMEM((2, D_MODEL, D_EXPERT), F32),
                            pltpu.VMEM((2, D_EXPERT, D_MODEL), F32), pltpu.SemaphoreType.DMA((2,))],
        ),
        out_shape=jax.ShapeDtypeStruct((MOE_ROWS, D_MODEL), F32),
        compiler_params=_cparams("arbitrary"),
        name="moe_ffn",
    )(block_e, nused, nexte, wslot, xs, w_gate, w_up, w_down)


def _combine_kernel(dest0_ref, dest1_ref, x1_ref, rt_ref, gf_ref, ys_ref, y_ref, ya, yb, sem):
    rows = x1_ref.shape[0]
    base = pl.program_id(0) * rows

    def row_copy(t, dest_ref, buf):
        return pltpu.make_async_copy(ys_ref.at[pl.ds(dest_ref[base + t], 1)], buf.at[pl.ds(t, 1)], sem)

    def start(t, c):
        row_copy(t, dest0_ref, ya).start()
        row_copy(t, dest1_ref, yb).start()
        return c

    lax.fori_loop(0, rows, start, 0, unroll=ROW_DMA_UNROLL)
    pltpu.make_async_copy(ys_ref.at[pl.ds(0, rows)], ya, sem).wait()
    pltpu.make_async_copy(ys_ref.at[pl.ds(0, rows)], yb, sem).wait()
    x2 = x1_ref[...] + rt_ref[:, 4:5] * ya[...] + rt_ref[:, 5:6] * yb[...]
    y_ref[...] = _rms(x2, gf_ref[...])


def _combine(dest, x1, rt, gf, ys, rows):
    n = x1.shape[0]
    return pl.pallas_call(
        _combine_kernel,
        grid_spec=pltpu.PrefetchScalarGridSpec(
            num_scalar_prefetch=2,
            grid=(n // rows,),
            in_specs=[pl.BlockSpec((rows, D_MODEL), lambda i, *_: (i, 0)),
                      pl.BlockSpec((rows, LANES), lambda i, *_: (i, 0)),
                      pl.BlockSpec((1, D_MODEL), lambda i, *_: (0, 0)),
                      pl.BlockSpec(memory_space=pl.ANY)],
            out_specs=pl.BlockSpec((rows, D_MODEL), lambda i, *_: (i, 0)),
            scratch_shapes=[pltpu.VMEM((rows, D_MODEL), F32), pltpu.VMEM((rows, D_MODEL), F32),
                            pltpu.SemaphoreType.DMA(())],
        ),
        out_shape=jax.ShapeDtypeStruct((n, D_MODEL), F32),
        compiler_params=_cparams("arbitrary"),
        name=f"moe_combine_{rows}",
    )(dest[0], dest[1], x1, rt, gf, ys)


def _rope_tables(pos, xp):
    half = HEAD_DIM // 2
    inv = 1.0 / (xp.float32(ROPE_THETA) ** (xp.arange(half, dtype=xp.float32) * xp.float32(2.0 / HEAD_DIM)))
    ang = pos.astype(xp.float32)[:, None] * inv[None, :].astype(xp.float32)
    cos, sin = xp.cos(ang), xp.sin(ang)
    return xp.concatenate([cos, cos, cos, cos], axis=1), xp.concatenate([-sin, sin, -sin, sin], axis=1)


def _slot_dest(ids, pos, pstarts):
    ids = ids.astype(jnp.int32)
    experts = jnp.arange(N_EXPERTS, dtype=jnp.int32).reshape((N_EXPERTS,) + (1,) * ids.ndim)
    start = jnp.sum(jnp.where(ids[None] == experts, pstarts.reshape(experts.shape), 0), axis=0)
    return start + pos.astype(jnp.int32)


def kernel(x_prompt, x_sample, cache_kv_g0, cache_kv_g1, cache_kv_g2, norm1_g, w_in, b_in, a_ln_g, a_ln_b, w_spatial, b_spatial, w_a_proj, w_b_proj, w_o, norm2_g, w_group_router, b_group_router, w_expert_router, b_expert_router, w_gate, w_up, w_down, final_norm_g):
    x2 = x_prompt.reshape(T_PROMPT, D_MODEL)
    xs = x_sample.reshape(DEC_BATCH, D_MODEL)
    g1 = norm1_g[0][None, :]
    g2 = norm2_g[0][None, :]
    gf = final_norm_g[None, :]
    b_in2 = b_in[0][None, :]
    lng, lnb = a_ln_g[0][None, :], a_ln_b[0][None, :]

    causal = jnp.tril(jnp.ones((CHUNK, CHUNK), dtype=bool))
    ws_tril = jnp.where(causal[None], w_spatial[0], 0.0)
    wsp = jnp.concatenate([ws_tril[0::2], ws_tril[1::2]], axis=2).astype(BF16)
    bsp = jnp.repeat(b_spatial[0].T, A_GROUP_DIM, axis=1)
    ws0 = jnp.repeat(ws_tril[:, 0, 0].astype(BF16).astype(F32), A_GROUP_DIM)[None, :]
    bs0 = jnp.repeat(b_spatial[0][:, 0], A_GROUP_DIM)[None, :]
    w_router = jnp.zeros((D_MODEL, LANES), F32)
    w_router = w_router.at[:, :N_EXPERT_GROUPS].set(w_group_router[0])
    w_router = w_router.at[:, N_EXPERT_GROUPS:N_EXPERT_GROUPS + N_EXPERTS].set(w_expert_router[0])
    b_router = jnp.zeros((1, LANES), F32)
    b_router = b_router.at[0, :N_EXPERT_GROUPS].set(b_group_router[0])
    b_router = b_router.at[0, N_EXPERT_GROUPS:N_EXPERT_GROUPS + N_EXPERTS].set(b_expert_router[0])
    w_router = w_router.astype(BF16)
    ltri = jnp.tril(jnp.ones((TM, TM), BF16), -1)
    w_in_b, w_a_b, w_b_b, w_o_b = (w[0].astype(BF16) for w in (w_in, w_a_proj, w_b_proj, w_o))

    cos_p, sin_p = _rope_tables(np.arange(SEQ, dtype=np.int32), np)
    cos_s, sin_s = _rope_tables(jnp.full((1,), PAST_LEN, jnp.int32), jnp)

    uv, qkv0, qkv1, qkv2, gates, kvp0, kvp1, kvp2 = _inproj(
        x2, g1, w_in_b, b_in2, lng, lnb, cos_p, sin_p)
    ob = _attention((qkv0.reshape(BATCH, 1, SEQ, 3 * GROUP_W), qkv1, qkv2)).reshape(T_PROMPT, GROUP_W)
    x1_p, h2p_p, rt_p, rtt_p, counts_p = _merge(
        x2, uv, gates, ob, wsp, bsp, w_a_b, w_b_b, w_o_b, g2, w_router, b_router, ltri)

    z_s = _sample_inproj(xs, g1, w_in_b, b_in2)
    vrow, sa_s, qr_s, kr_s, kvs0, kvs1, kvs2 = _sample_pre(z_s, lng, lnb, ws0, bs0, cos_s, sin_s)
    dim_major = lambda t: t.reshape(DEC_BATCH, 3, HEADS, HEAD_DIM).transpose(0, 1, 3, 2)
    caches = [c.transpose(0, 1, 3, 4, 5, 2).reshape(DEC_BATCH, 2, HEADS, HEAD_DIM, c.shape[2])
              for c in (cache_kv_g0, cache_kv_g1, cache_kv_g2)]
    comb_s = _sample_attn(dim_major(qr_s), dim_major(kr_s), dim_major(z_s[:, COL_VB:COL_VB + QK_W]), caches)
    comb_s = comb_s.transpose(0, 2, 1).reshape(DEC_BATCH, GROUP_W)
    x1_s, h2p_s, rt_s, counts = _sample_merge(
        xs, z_s, sa_s, comb_s, w_a_b, w_b_b, w_o_b, g2, w_router, b_router,
        ltri[:DEC_BATCH, :DEC_BATCH], counts_p)

    cnt = counts[0, :N_EXPERTS].astype(jnp.int32)
    padded = (cnt + MOE_TB - 1) // MOE_TB * MOE_TB
    pends = jnp.cumsum(padded)
    pstarts = pends - padded
    block_starts = jnp.arange(MOE_NB, dtype=jnp.int32) * MOE_TB
    block_e = jnp.minimum(jnp.sum((pends[None, :] <= block_starts[:, None]).astype(jnp.int32), axis=1),
                          N_EXPERTS - 1)
    nused = (pends[-1:] // MOE_TB).astype(jnp.int32)
    rtt_p = rtt_p.reshape(T_PROMPT // TM, ROUTE_ROWS, TM)
    dest_p = [_slot_dest(rtt_p[:, k], rtt_p[:, 2 + k], pstarts).reshape(T_PROMPT) for k in range(2)]
    dest_s = [_slot_dest(rt_s[:, k], rt_s[:, 2 + k], pstarts) for k in range(2)]

    rows = jnp.zeros((MOE_ROWS, HALF_D), U32)
    rows = _dispatch(dest_p, h2p_p, rows, TM)
    rows = _dispatch(dest_s, h2p_s, rows, DEC_BATCH)
    ys = _expert_ffn(block_e, nused, rows, w_gate[0], w_up[0], w_down[0])
    y_p = _combine(dest_p, x1_p, rt_p, gf, ys, TM)
    y_s = _combine(dest_s, x1_s, rt_s, gf, ys, DEC_BATCH)

    kv_shape = lambda n, w: (1, n, w, 2, HEADS, HEAD_DIM)
    window_rows = lambda t: t.reshape(1, BATCH, 2, HEADS, HEAD_DIM, t.shape[2]).transpose(0, 1, 5, 2, 3, 4)
    return (y_p.reshape(BATCH, SEQ, D_MODEL), y_s.reshape(DEC_BATCH, 1, D_MODEL),
            window_rows(kvp0), window_rows(kvp1), window_rows(kvp2),
            kvs0.reshape(kv_shape(DEC_BATCH, 1)), kvs1.reshape(kv_shape(DEC_BATCH, 1)),
            kvs2.reshape(kv_shape(DEC_BATCH, 1)), vrow.reshape(1, DEC_BATCH, 1, A_WIDTH))
```

```python
import functools

import jax
import jax.numpy as jnp
import numpy as np
from jax import lax
from jax.experimental import pallas as pl
from jax.experimental.pallas import tpu as pltpu

F32 = jnp.float32
BF16 = jnp.bfloat16
U32 = jnp.uint32

D_MODEL = 1024
BATCH = 2
SEQ = 8192
DEC_BATCH = 32
PAST_LEN = 8192
CHUNK = 128
A_GROUPS = 8
A_GROUP_DIM = 64
A_WIDTH = 512
HEAD_DIM = 64
HEADS = 4
GROUP_W = HEADS * HEAD_DIM
DILATIONS = (1, 4, 16)
SPAN = 128
QK_W = 768
IN_WIDTH = 5376
COL_U, COL_V, COL_Q, COL_K, COL_VB, COL_GA, COL_GB = 0, 512, 1024, 1792, 2560, 3328, 4352
N_EXPERT_GROUPS = 4
EXPERTS_PER_GROUP = 8
N_EXPERTS = 32
D_EXPERT = 512
ROPE_THETA = 10000.0
EPS = 1e-6

LANES = 128
T_PROMPT = BATCH * SEQ
TM = 512
TILES_PER_SEQ = SEQ // TM
WINDOWS = tuple(min(SPAN * d, SEQ) for d in DILATIONS)
TA = SPAN * max(DILATIONS)
ATT_UNROLL = 2
MOE_TB = 512
ROW_DMA_UNROLL = 8
ROUTE_ROWS = 8
N_SLOTS = 2 * (T_PROMPT + DEC_BATCH)
MOE_NB = -(-N_SLOTS // MOE_TB) + N_EXPERTS
MOE_ROWS = MOE_NB * MOE_TB
HALF_D = D_MODEL // 2
NEG = -1e30
VMEM_LIMIT = 56 * 1024 * 1024


def _cparams(*sem):
    return pltpu.CompilerParams(dimension_semantics=sem, vmem_limit_bytes=VMEM_LIMIT)


def _rms(x, g):
    return x * lax.rsqrt(jnp.mean(x * x, axis=-1, keepdims=True) + EPS) * g


def _layer_norm(x, g, b):
    mu = jnp.mean(x, axis=-1, keepdims=True)
    xc = x - mu
    var = jnp.mean(xc * xc, axis=-1, keepdims=True)
    return xc * lax.rsqrt(var + EPS) * g + b


def _rope_chunk(x, cos, sin_signed):
    lane = lax.broadcasted_iota(jnp.int32, x.shape, 1)
    first_half = (lane % HEAD_DIM) < (HEAD_DIM // 2)
    swapped = jnp.where(first_half, pltpu.roll(x, LANES - HEAD_DIM // 2, 1), pltpu.roll(x, HEAD_DIM // 2, 1))
    return x * cos + swapped * sin_signed


def _pack_bf16_pair(h):
    lo = lax.bitcast_convert_type(h[:, :HALF_D].astype(BF16).astype(F32), U32)
    hi = lax.bitcast_convert_type(h[:, HALF_D:].astype(BF16).astype(F32), U32)
    return (hi & jnp.uint32(0xFFFF0000)) | (lo >> 16)


def _unpack_bf16_pair(p):
    lo = lax.bitcast_convert_type(p << 16, F32).astype(BF16)
    hi = lax.bitcast_convert_type(p & jnp.uint32(0xFFFF0000), F32).astype(BF16)
    return lo, hi


def _route(logits, prior_counts, ltri):
    rows = logits.shape[0]
    lane = lax.broadcasted_iota(jnp.int32, (rows, LANES), 1).astype(F32)
    is_g = lane < N_EXPERT_GROUPS
    gl = jnp.where(is_g, logits, NEG)
    gmax = jnp.max(gl, axis=1, keepdims=True)
    grp = jnp.min(jnp.where(gl == gmax, lane, float(LANES)), axis=1, keepdims=True)
    p_grp = 1.0 / jnp.sum(jnp.where(is_g, jnp.exp(gl - gmax), 0.0), axis=1, keepdims=True)
    lo = N_EXPERT_GROUPS + grp * EXPERTS_PER_GROUP
    el = jnp.where((lane >= lo) & (lane < lo + EXPERTS_PER_GROUP), logits, NEG)
    v1 = jnp.max(el, axis=1, keepdims=True)
    i1 = jnp.min(jnp.where(el == v1, lane, float(LANES)), axis=1, keepdims=True)
    el2 = jnp.where(lane == i1, NEG, el)
    v2 = jnp.max(el2, axis=1, keepdims=True)
    i2 = jnp.min(jnp.where(el2 == v2, lane, float(LANES)), axis=1, keepdims=True)
    t = jnp.exp(v2 - v1)
    gate1 = p_grp / (1.0 + t)
    gate2 = p_grp * t / (1.0 + t)
    e1 = i1 - N_EXPERT_GROUPS
    e2 = i2 - N_EXPERT_GROUPS
    hit1 = lane == e1
    hit2 = lane == e2
    onehot = jnp.where(hit1 | hit2, 1.0, 0.0)
    rank = jnp.dot(ltri, onehot.astype(BF16), preferred_element_type=F32) + prior_counts
    pos1 = jnp.sum(jnp.where(hit1, rank, 0.0), axis=1, keepdims=True)
    pos2 = jnp.sum(jnp.where(hit2, rank, 0.0), axis=1, keepdims=True)
    slab = jnp.where(lane == 0, e1, jnp.where(lane == 1, e2, jnp.where(lane == 2, pos1, jnp.where(
        lane == 3, pos2, jnp.where(lane == 4, gate1, jnp.where(lane == 5, gate2, 0.0))))))
    return slab, prior_counts + jnp.sum(onehot, axis=0, keepdims=True)


def _inproj_kernel(x_ref, g1_ref, w_ref, b_ref, lng_ref, lnb_ref, cos_ref, sin_ref,
                   uv_ref, qkv0_ref, qkv1_ref, qkv2_ref, gates_ref, kv0_ref, kv1_ref, kv2_ref, de_ref):
    tile_in_seq = pl.program_id(0) % TILES_PER_SEQ
    hb = _rms(x_ref[...], g1_ref[...]).astype(BF16)

    def seg(lo, width):
        return jnp.dot(hb, w_ref[:, lo:lo + width], preferred_element_type=F32) + b_ref[:, lo:lo + width]

    uv_ref[:, :A_WIDTH] = jax.nn.gelu(seg(COL_U, A_WIDTH)).astype(BF16)
    uv_ref[:, A_WIDTH:] = _layer_norm(jax.nn.gelu(seg(COL_V, A_WIDTH)), lng_ref[...], lnb_ref[...]).astype(BF16)
    gates_ref[:, :D_MODEL] = jax.nn.sigmoid(seg(COL_GA, D_MODEL)).astype(BF16)
    gates_ref[:, D_MODEL:] = jax.nn.sigmoid(seg(COL_GB, D_MODEL)).astype(BF16)

    cos = cos_ref[...]
    sin = sin_ref[...]
    qkv_refs = (qkv0_ref, qkv1_ref, qkv2_ref)
    kv_refs = (kv0_ref, kv1_ref, kv2_ref)
    for g, d in enumerate(DILATIONS):
        q = seg(COL_Q + g * GROUP_W, GROUP_W)
        k = seg(COL_K + g * GROUP_W, GROUP_W)
        v = seg(COL_VB + g * GROUP_W, GROUP_W)
        chunks = GROUP_W // LANES
        for c in range(chunks):
            sl = slice(c * LANES, (c + 1) * LANES)
            de_ref[c] = _rope_chunk(q[:, sl], cos, sin) * (HEAD_DIM ** -0.5)
            de_ref[chunks + c] = _rope_chunk(k[:, sl], cos, sin)
            de_ref[2 * chunks + c] = v[:, sl]
        for c in range(3 * chunks):
            sl = slice(c * LANES, (c + 1) * LANES)
            if d == 1:
                qkv_refs[g][:, sl] = de_ref[c].astype(BF16)
            else:
                for r in range(d):
                    qkv_refs[g][0, r, :, sl] = de_ref[c, pl.ds(r, TM // d, stride=d), :].astype(BF16)
        kv_rows = kv_refs[g].shape[2]
        first = TILES_PER_SEQ - WINDOWS[g] // kv_rows

        @pl.when(tile_in_seq >= first)
        def _(kv_ref=kv_refs[g], kv_rows=kv_rows, chunks=chunks):
            for c in range(2 * chunks):
                kv_ref[0, c * LANES:(c + 1) * LANES, :] = de_ref[chunks + c, TM - kv_rows:, :].T


def _inproj(x2, g1, w_bf, b_in, lng, lnb, cos_t, sin_t):
    tiles_per_seq = TILES_PER_SEQ
    n_tiles = T_PROMPT // TM
    const = lambda i: (0, 0)

    def kv_spec(w):
        rows = min(w, TM)
        first = tiles_per_seq - w // rows
        return pl.BlockSpec((1, 2 * GROUP_W, rows),
                            lambda i: (i // tiles_per_seq, 0, jnp.maximum(i % tiles_per_seq - first, 0)))

    def regrouped_spec(d):
        return pl.BlockSpec((1, d, TM // d, 3 * GROUP_W), lambda i: (i // tiles_per_seq, 0, i % tiles_per_seq, 0))

    return pl.pallas_call(
        _inproj_kernel,
        grid=(n_tiles,),
        in_specs=[
            pl.BlockSpec((TM, D_MODEL), lambda i: (i, 0)),
            pl.BlockSpec((1, D_MODEL), const),
            pl.BlockSpec((D_MODEL, IN_WIDTH), const),
            pl.BlockSpec((1, IN_WIDTH), const),
            pl.BlockSpec((1, A_WIDTH), const),
            pl.BlockSpec((1, A_WIDTH), const),
            pl.BlockSpec((TM, LANES), lambda i: (i % tiles_per_seq, 0)),
            pl.BlockSpec((TM, LANES), lambda i: (i % tiles_per_seq, 0)),
        ],
        out_specs=[
            pl.BlockSpec((TM, 2 * A_WIDTH), lambda i: (i, 0)),
            pl.BlockSpec((TM, 3 * GROUP_W), lambda i: (i, 0)),
            regrouped_spec(DILATIONS[1]),
            regrouped_spec(DILATIONS[2]),
            pl.BlockSpec((TM, 2 * D_MODEL), lambda i: (i, 0)),
            kv_spec(WINDOWS[0]), kv_spec(WINDOWS[1]), kv_spec(WINDOWS[2]),
        ],
        out_shape=[
            jax.ShapeDtypeStruct((T_PROMPT, 2 * A_WIDTH), BF16),
            jax.ShapeDtypeStruct((T_PROMPT, 3 * GROUP_W), BF16),
            jax.ShapeDtypeStruct((BATCH, DILATIONS[1], SEQ // DILATIONS[1], 3 * GROUP_W), BF16),
            jax.ShapeDtypeStruct((BATCH, DILATIONS[2], SEQ // DILATIONS[2], 3 * GROUP_W), BF16),
            jax.ShapeDtypeStruct((T_PROMPT, 2 * D_MODEL), BF16),
            jax.ShapeDtypeStruct((BATCH, 2 * GROUP_W, WINDOWS[0]), F32),
            jax.ShapeDtypeStruct((BATCH, 2 * GROUP_W, WINDOWS[1]), F32),
            jax.ShapeDtypeStruct((BATCH, 2 * GROUP_W, WINDOWS[2]), F32),
        ],
        scratch_shapes=[pltpu.VMEM((3 * GROUP_W // LANES, TM, LANES), F32)],
        compiler_params=_cparams("arbitrary"),
        name="inproj",
    )(x2, g1, w_bf, b_in, lng, lnb, cos_t, sin_t)


def _attn_kernel(c0_ref, p0_ref, c1_ref, p1_ref, c2_ref, p2_ref, o_ref, acc_ref, m_ref, l_ref):
    n = pl.program_id(1)
    qi =lax.broadcasted_iota(jnp.int32, (SPAN, 2 * SPAN), 0)
    ki = lax.broadcasted_iota(jnp.int32, (SPAN, 2 * SPAN), 1)
    band = (ki >= qi) & (ki <= qi + SPAN)
    band_first = band & ((ki >= SPAN) | (n > 0))
    q_head = lax.broadcasted_iota(jnp.int32, (SPAN, GROUP_W), 1) // HEAD_DIM
    kv_head = lax.broadcasted_iota(jnp.int32, (2 * SPAN, GROUP_W), 1) // HEAD_DIM
    contract_last = (((1,), (1,)), ((), ()))
    k_cols = slice(GROUP_W, 2 * GROUP_W)
    v_cols = slice(2 * GROUP_W, 3 * GROUP_W)

    def attend(q, keys, vals, mask, tok_rows, fresh):
        scores = []
        for h in range(HEADS):
            qh = jnp.where(q_head == h, q, jnp.zeros_like(q))
            s = lax.dot_general(qh, keys, contract_last, preferred_element_type=F32)
            scores.append(jnp.where(mask, s, NEG))
        s = jnp.concatenate(scores, axis=0)
        m = jnp.max(s, axis=1, keepdims=True)
        p = jnp.exp(s - m)
        l = jnp.sum(p, axis=1, keepdims=True)
        pb = p.astype(BF16)
        acc = jnp.zeros((SPAN, GROUP_W), F32)
        m_b = jnp.zeros((SPAN, GROUP_W), F32)
        l_b = jnp.zeros((SPAN, GROUP_W), F32)
        for h in range(HEADS):
            rows = slice(h * SPAN, (h + 1) * SPAN)
            vh = jnp.where(kv_head == h, vals, jnp.zeros_like(vals))
            acc = acc + jnp.dot(pb[rows], vh, preferred_element_type=F32)
            m_b = jnp.where(q_head == h, m[rows], m_b)
            l_b = jnp.where(q_head == h, l[rows], l_b)
        for c in range(GROUP_W // LANES):
            sl = slice(c * LANES, (c + 1) * LANES)
            if fresh:
                m_ref[c, tok_rows, :] = m_b[:, sl]
                l_ref[c, tok_rows, :] = l_b[:, sl]
                acc_ref[c, tok_rows, :] = acc[:, sl]
                continue
            m_old = m_ref[c, tok_rows, :]
            m_new = jnp.maximum(m_old, m_b[:, sl])
            a_old = jnp.exp(m_old - m_new)
            a_blk = jnp.exp(m_b[:, sl] - m_new)
            l_ref[c, tok_rows, :] = a_old * l_ref[c, tok_rows, :] + a_blk * l_b[:, sl]
            acc_ref[c, tok_rows, :] = a_old * acc_ref[c, tok_rows, :] + a_blk * acc[:, sl]
            m_ref[c, tok_rows, :] = m_new

    def stream(c_ref, p_ref, r, d, fresh):
        n_blocks = c_ref.shape[2] // SPAN

        def tok_rows(j):
            start = j * (SPAN * d) + r
            return pl.ds(start, SPAN) if d == 1 else pl.ds(start, SPAN, stride=d)

        keys = jnp.concatenate([p_ref[0, r, :, k_cols], c_ref[0, r, :SPAN, k_cols]], axis=0)
        vals = jnp.concatenate([p_ref[0, r, :, v_cols], c_ref[0, r, :SPAN, v_cols]], axis=0)
        attend(c_ref[0, r, :SPAN, :GROUP_W], keys, vals, band_first, tok_rows(0), fresh)

        def later_block(j, carry):
            q_rows = pl.ds(pl.multiple_of(j * SPAN, SPAN), SPAN)
            kv_rows = pl.ds(pl.multiple_of((j - 1) * SPAN, SPAN), 2 * SPAN)
            attend(c_ref[0, r, q_rows, :GROUP_W], c_ref[0, r, kv_rows, k_cols], c_ref[0, r, kv_rows, v_cols],
                   band, tok_rows(j), fresh)
            return carry

        if n_blocks > 1:
            lax.fori_loop(1, n_blocks, later_block, 0, unroll=3)

    groups = sorted(zip(DILATIONS, ((c0_ref, p0_ref), (c1_ref, p1_ref), (c2_ref, p2_ref))), key=lambda g: -g[0])
    for idx, (d, (c_ref, p_ref)) in enumerate(groups):
        fresh = idx == 0
        if d == 1:
            stream(c_ref, p_ref, 0, d, fresh)
        else:
            lax.fori_loop(0, d, lambda r, carry, c_ref=c_ref, p_ref=p_ref, d=d, fresh=fresh:
                          (stream(c_ref, p_ref, r, d, fresh), carry)[1], 0,
                          unroll=ATT_UNROLL if c_ref.shape[2] == SPAN else 1)

    for c in range(GROUP_W // LANES):
        o_ref[0, :, c * LANES:(c + 1) * LANES] = (acc_ref[c] / l_ref[c]).astype(BF16)


def _attention(qkv_by_group):
    in_specs, args = [], []
    for qkv, d in zip(qkv_by_group, DILATIONS):
        rows = TA // d
        blocks_per_tile = rows // SPAN
        in_specs.append(pl.BlockSpec((1, d, rows, 3 * GROUP_W), lambda b, n: (b, 0, n, 0)))
        in_specs.append(pl.BlockSpec((1, d, SPAN, 3 * GROUP_W),
                                     lambda b, n, k=blocks_per_tile: (b, 0, jnp.maximum(n * k - 1, 0), 0)))
        args += [qkv, qkv]
    return pl.pallas_call(
        _attn_kernel,
        grid=(BATCH, SEQ // TA),
        in_specs=in_specs,
        out_specs=pl.BlockSpec((1, TA, GROUP_W), lambda b, n: (b, n, 0)),
        out_shape=jax.ShapeDtypeStruct((BATCH, SEQ, GROUP_W), BF16),
        scratch_shapes=[pltpu.VMEM((GROUP_W // LANES, TA, LANES), F32)] * 3,
        compiler_params=_cparams("arbitrary", "arbitrary"),
        name="attn",
    )(*args)


def _merge_kernel(x_ref, uv_ref, gates_ref, ob_ref,
                  wsp_ref, bsp_ref, wa_ref, wb_ref, wo_ref, g2_ref, wr_ref, br_ref, ltri_ref,
                  x1_ref, h2p_ref, rt_ref, rtt_ref, cnt_ref, run_ref):
    @pl.when(pl.program_id(0) == 0)
    def _():
        run_ref[...] = jnp.zeros_like(run_ref)

    lane = lax.broadcasted_iota(jnp.int32, (CHUNK, LANES), 1)
    left = lane < A_GROUP_DIM
    zero = jnp.zeros((CHUNK, LANES), BF16)
    sa_chunks = []
    for c in range(TM // CHUNK):
        rows = slice(c * CHUNK, (c + 1) * CHUNK)
        pairs = []
        for p in range(A_GROUPS // 2):
            vp = uv_ref[rows, A_WIDTH + p * LANES:A_WIDTH + (p + 1) * LANES]
            rhs = jnp.concatenate([jnp.where(left, vp, zero), jnp.where(left, zero, vp)], axis=0)
            pairs.append(jnp.dot(wsp_ref[p], rhs, preferred_element_type=F32))
        mixed = jnp.concatenate(pairs, axis=1) + bsp_ref[...]
        sa_chunks.append((uv_ref[rows, :A_WIDTH].astype(F32) * mixed).astype(BF16))
    s_a = jnp.concatenate(sa_chunks, axis=0)

    a = jnp.dot(s_a, wa_ref[...], preferred_element_type=F32)
    b = jnp.dot(ob_ref[...], wb_ref[...], preferred_element_type=F32)
    merged = gates_ref[:, :D_MODEL].astype(F32) * a + gates_ref[:, D_MODEL:].astype(F32) * b
    x1 = x_ref[...] + jnp.dot(merged.astype(BF16), wo_ref[...], preferred_element_type=F32)
    x1_ref[...] = x1

    h2 = _rms(x1, g2_ref[...])
    h2p_ref[...] = _pack_bf16_pair(h2)
    logits = jnp.dot(h2.astype(BF16), wr_ref[...], preferred_element_type=F32) + br_ref[...]
    slab, counts = _route(logits, run_ref[...], ltri_ref[...])
    rt_ref[...] = slab
    rtt_ref[...] = slab.T[:ROUTE_ROWS, :]
    run_ref[...] = counts
    cnt_ref[...] = counts


def _merge(x2, uv, gates, ob, wsp, bsp, wa, wb, wo, g2, wr, br, ltri):
    n_tiles = T_PROMPT // TM
    tile = lambda w: pl.BlockSpec((TM, w), lambda i: (i, 0))
    full = lambda a: pl.BlockSpec(a.shape, lambda i: (0,) * a.ndim)
    return pl.pallas_call(
        _merge_kernel,
        grid=(n_tiles,),
        in_specs=[tile(D_MODEL), tile(2 * A_WIDTH), tile(2 * D_MODEL), tile(GROUP_W),
                  full(wsp), full(bsp), full(wa), full(wb), full(wo), full(g2), full(wr), full(br),
                  full(ltri)],
        out_specs=[tile(D_MODEL), tile(HALF_D), tile(LANES), pl.BlockSpec((ROUTE_ROWS, TM), lambda i: (i, 0)),
                   pl.BlockSpec((1, LANES), lambda i: (0, 0))],
        out_shape=[
            jax.ShapeDtypeStruct((T_PROMPT, D_MODEL), F32),
            jax.ShapeDtypeStruct((T_PROMPT, HALF_D), U32),
            jax.ShapeDtypeStruct((T_PROMPT, LANES), F32),
            jax.ShapeDtypeStruct((n_tiles * ROUTE_ROWS, TM), F32),
            jax.ShapeDtypeStruct((1, LANES), F32),
        ],
        scratch_shapes=[pltpu.VMEM((1, LANES), F32)],
        compiler_params=_cparams("arbitrary"),
        name="merge_route",
    )(x2, uv, gates, ob, wsp, bsp, wa, wb, wo, g2, wr, br, ltri)


SAMPLE_COLS = 768


def _sample_inproj_kernel(x_ref, g1_ref, w_ref, b_ref, z_ref):
    hb = _rms(x_ref[...], g1_ref[...]).astype(BF16)
    z_ref[...] = jnp.dot(hb, w_ref[...], preferred_element_type=F32) + b_ref[...]


def _sample_inproj(xs, g1, w_in, b_in):
    const = lambda j: (0, 0)
    return pl.pallas_call(
        _sample_inproj_kernel,
        grid=(IN_WIDTH // SAMPLE_COLS,),
        in_specs=[pl.BlockSpec((DEC_BATCH, D_MODEL), const), pl.BlockSpec((1, D_MODEL), const),
                  pl.BlockSpec((D_MODEL, SAMPLE_COLS), lambda j: (0, j)),
                  pl.BlockSpec((1, SAMPLE_COLS), lambda j: (0, j))],
        out_specs=pl.BlockSpec((DEC_BATCH, SAMPLE_COLS), lambda j: (0, j)),
        out_shape=jax.ShapeDtypeStruct((DEC_BATCH, IN_WIDTH), F32),
        compiler_params=_cparams("arbitrary"),
        name="sample_inproj",
    )(xs, g1, w_in, b_in)


def _sample_pre_kernel(z_ref, lng_ref, lnb_ref, ws0_ref, bs0_ref, cos_ref, sin_ref,
                       vrow_ref, sa_ref, qr_ref, kr_ref, kv0_ref, kv1_ref, kv2_ref):
    u = jax.nn.gelu(z_ref[:, COL_U:COL_U + A_WIDTH])
    va = _layer_norm(jax.nn.gelu(z_ref[:, COL_V:COL_V + A_WIDTH]), lng_ref[...], lnb_ref[...])
    vrow_ref[...] = va
    sa_ref[...] = u * (ws0_ref[...] * va.astype(BF16).astype(F32) + bs0_ref[...])
    cos, sin = cos_ref[...], sin_ref[...]
    for g, kv_ref in enumerate((kv0_ref, kv1_ref, kv2_ref)):
        for c in range(GROUP_W // LANES):
            off = g * GROUP_W + c * LANES
            qr_ref[:, off:off + LANES] = (_rope_chunk(z_ref[:, COL_Q + off:COL_Q + off + LANES], cos, sin)
                                          * (HEAD_DIM ** -0.5))
            kr = _rope_chunk(z_ref[:, COL_K + off:COL_K + off + LANES], cos, sin)
            kr_ref[:, off:off + LANES] = kr
            kv_ref[:, c * LANES:(c + 1) * LANES] = kr
        kv_ref[:, GROUP_W:] = z_ref[:, COL_VB + g * GROUP_W:COL_VB + (g + 1) * GROUP_W]


def _sample_pre(z, lng, lnb, ws0, bs0, cos_s, sin_s):
    row = lambda w: jax.ShapeDtypeStruct((DEC_BATCH, w), F32)
    return pl.pallas_call(
        _sample_pre_kernel,
        out_shape=[row(A_WIDTH), row(A_WIDTH), row(QK_W), row(QK_W),
                   row(2 * GROUP_W), row(2 * GROUP_W), row(2 * GROUP_W)],
        compiler_params=pltpu.CompilerParams(vmem_limit_bytes=VMEM_LIMIT),
        name="sample_pre",
    )(z, lng, lnb, ws0, bs0, cos_s, sin_s)


def _sample_attn_kernel(q_ref, k_ref, v_ref, c0_ref, c1_ref, c2_ref, comb_ref):
    as_operand = lambda t: t.astype(BF16).astype(F32)
    for h in range(HEADS):
        head = slice(h, h + 1)
        outs, lses = [], []
        for g, (c_ref, d) in enumerate(zip((c0_ref, c1_ref, c2_ref), DILATIONS)):
            qc, kn, vn = (as_operand(r[0, g, :, head]) for r in (q_ref, k_ref, v_ref))
            s = jnp.sum(as_operand(c_ref[0, 0, h]) * qc, axis=0, keepdims=True)
            if d > 1:
                pos = lax.broadcasted_iota(jnp.int32, s.shape, 1)
                s = jnp.where(pos % d == 0, s, NEG)
            s_n = jnp.sum(qc * kn, axis=0, keepdims=True)
            m = jnp.maximum(jnp.max(s, axis=1, keepdims=True), s_n)
            lse = m + jnp.log(jnp.sum(jnp.exp(s - m), axis=1, keepdims=True) + jnp.exp(s_n - m))
            p = as_operand(jnp.exp(s - lse))
            p_n = as_operand(jnp.exp(s_n - lse))
            outs.append(jnp.sum(as_operand(c_ref[0, 1, h]) * p, axis=1, keepdims=True) + p_n * vn)
            lses.append(lse)
        mx = jnp.maximum(jnp.maximum(lses[0], lses[1]), lses[2])
        ws = [jnp.exp(t - mx) for t in lses]
        tot = ws[0] + ws[1] + ws[2]
        comb_ref[0, :, head] = (ws[0] * outs[0] + ws[1] * outs[1] + ws[2] * outs[2]) / tot


def _sample_attn(qt, kt, vt, caches):
    new_spec = pl.BlockSpec((1, 3, HEAD_DIM, HEADS), lambda i: (i, 0, 0, 0))
    cache_spec = lambda c: pl.BlockSpec((1,) + c.shape[1:], lambda i: (i, 0, 0, 0, 0))
    return pl.pallas_call(
        _sample_attn_kernel,
        grid=(DEC_BATCH,),
        in_specs=[new_spec, new_spec, new_spec, cache_spec(caches[0]), cache_spec(caches[1]), cache_spec(caches[2])],
        out_specs=pl.BlockSpec((1, HEAD_DIM, HEADS), lambda i: (i, 0, 0)),
        out_shape=jax.ShapeDtypeStruct((DEC_BATCH, HEAD_DIM, HEADS), F32),
        compiler_params=_cparams("arbitrary"),
        name="sample_attn",
    )(qt, kt, vt, caches[0], caches[1], caches[2])


def _sample_merge_kernel(x_ref, z_ref, sa_ref, comb_ref, wa_ref, wb_ref, wo_ref, g2_ref, wr_ref, br_ref,
                         ltri_ref, cnt_in_ref, x1_ref, h2p_ref, rt_ref, cnt_ref):
    dot = lambda p, q: jnp.dot(p.astype(BF16), q, preferred_element_type=F32)
    a = dot(sa_ref[...], wa_ref[...])
    b = dot(comb_ref[...], wb_ref[...])
    merged = (jax.nn.sigmoid(z_ref[:, COL_GA:COL_GA + D_MODEL]) * a
              + jax.nn.sigmoid(z_ref[:, COL_GB:COL_GB + D_MODEL]) * b)
    x1 = x_ref[...] + dot(merged, wo_ref[...])
    x1_ref[...] = x1
    h2 = _rms(x1, g2_ref[...])
    h2p_ref[...] = _pack_bf16_pair(h2)
    logits = dot(h2, wr_ref[...]) + br_ref[...]
    slab, counts = _route(logits, cnt_in_ref[...], ltri_ref[...])
    rt_ref[...] = slab
    cnt_ref[...] = counts


def _sample_merge(xs, z, s_a, comb, wa, wb, wo, g2, wr, br, ltri, counts):
    return pl.pallas_call(
        _sample_merge_kernel,
        out_shape=[jax.ShapeDtypeStruct((DEC_BATCH, D_MODEL), F32),
                   jax.ShapeDtypeStruct((DEC_BATCH, HALF_D), U32),
                   jax.ShapeDtypeStruct((DEC_BATCH, LANES), F32),
                   jax.ShapeDtypeStruct((1, LANES), F32)],
        compiler_params=pltpu.CompilerParams(vmem_limit_bytes=VMEM_LIMIT),
        name="sample_merge_route",
    )(xs, z, s_a, comb, wa, wb, wo, g2, wr, br, ltri, counts)


def _dispatch_kernel(dest0_ref, dest1_ref, h_ref, xs_in_ref, xs_ref, sem):
    del xs_in_ref
    rows = h_ref.shape[0]
    base = pl.program_id(0) * rows

    def row_copy(t, dest_ref):
        return pltpu.make_async_copy(h_ref.at[pl.ds(t, 1)], xs_ref.at[pl.ds(dest_ref[base + t], 1)], sem)

    def start(t, c):
        row_copy(t, dest0_ref).start()
        row_copy(t, dest1_ref).start()
        return c

    lax.fori_loop(0, rows, start, 0, unroll=ROW_DMA_UNROLL)
    for _ in range(2):
        pltpu.make_async_copy(h_ref, xs_ref.at[pl.ds(0, rows)], sem).wait()


def _dispatch(dest, h2p, xs, rows):
    n = h2p.shape[0]
    return pl.pallas_call(
        _dispatch_kernel,
        grid_spec=pltpu.PrefetchScalarGridSpec(
            num_scalar_prefetch=2,
            grid=(n // rows,),
            in_specs=[pl.BlockSpec((rows, HALF_D), lambda i, *_: (i, 0)), pl.BlockSpec(memory_space=pl.ANY)],
            out_specs=pl.BlockSpec(memory_space=pl.ANY),
            scratch_shapes=[pltpu.SemaphoreType.DMA(())],
        ),
        out_shape=jax.ShapeDtypeStruct(xs.shape, xs.dtype),
        input_output_aliases={3: 0},
        compiler_params=_cparams("arbitrary"),
        name=f"moe_dispatch_{rows}",
    )(dest[0], dest[1], h2p, xs)


def _ffn_kernel(be_ref, nused_ref, nexte_ref, wslot_ref, xs_ref, wg_ref, wu_ref, wd_ref, ys_ref,
                wg_b, wu_b, wd_b, wg_f, wu_f, wd_f, wsem):
    i = pl.program_id(0)
    live = i < nused_ref[0]
    new_expert = jnp.logical_and(live, jnp.logical_or(i == 0, be_ref[i] != be_ref[jnp.maximum(i - 1, 0)]))

    def weight_copies(e, slot):
        return [pltpu.make_async_copy(src.at[e], dst.at[slot], wsem.at[slot])
                for src, dst in ((wg_ref, wg_f), (wu_ref, wu_f), (wd_ref, wd_f))]

    @pl.when(i == 0)
    def _():
        for cp in weight_copies(be_ref[0], 0):
            cp.start()

    @pl.when(new_expert)
    def _():
        slot = wslot_ref[i]
        for cp in weight_copies(be_ref[i], slot):
            cp.wait()
        wg_b[...] = wg_f[slot].astype(BF16)
        wu_b[...] = wu_f[slot].astype(BF16)
        wd_b[...] = wd_f[slot].astype(BF16)

        @pl.when(nexte_ref[i] >= 0)
        def _():
            for cp in weight_copies(nexte_ref[i], 1 - slot):
                cp.start()

    @pl.when(live)
    def _():
        lo, hi = _unpack_bf16_pair(xs_ref[...])
        gate = (jnp.dot(lo, wg_b[:HALF_D, :], preferred_element_type=F32)
                + jnp.dot(hi, wg_b[HALF_D:, :], preferred_element_type=F32))
        up = (jnp.dot(lo, wu_b[:HALF_D, :], preferred_element_type=F32)
              + jnp.dot(hi, wu_b[HALF_D:, :], preferred_element_type=F32))
        hid = (jax.nn.silu(gate) * up).astype(BF16)
        ys_ref[...] = jnp.dot(hid, wd_b[...], preferred_element_type=F32)

    @pl.when(i >= nused_ref[0])
    def _():
        ys_ref[...] = jnp.zeros_like(ys_ref)


def _expert_ffn(block_e, nused, xs, w_gate, w_up, w_down):
    idx = jnp.arange(MOE_NB, dtype=jnp.int32)
    change = ((block_e != jnp.roll(block_e, 1)) | (idx == 0)) & (idx < nused[0])
    wslot = ((jnp.cumsum(change.astype(jnp.int32)) - 1) & 1).astype(jnp.int32)
    change_at = jnp.where(change, idx, MOE_NB)
    next_change = jnp.flip(lax.cummin(jnp.flip(jnp.concatenate([change_at[1:], jnp.full((1,), MOE_NB, jnp.int32)]))))
    nexte = jnp.where(next_change < MOE_NB, block_e[jnp.minimum(next_change, MOE_NB - 1)], -1).astype(jnp.int32)
    hbm = pl.BlockSpec(memory_space=pl.ANY)
    return pl.pallas_call(
        _ffn_kernel,
        grid_spec=pltpu.PrefetchScalarGridSpec(
            num_scalar_prefetch=4,
            grid=(MOE_NB,),
            in_specs=[pl.BlockSpec((MOE_TB, HALF_D), lambda i, *_: (i, 0)), hbm, hbm, hbm],
            out_specs=pl.BlockSpec((MOE_TB, D_MODEL), lambda i, *_: (i, 0)),
            scratch_shapes=[pltpu.VMEM((D_MODEL, D_EXPERT), BF16), pltpu.VMEM((D_MODEL, D_EXPERT), BF16),
                            pltpu.VMEM((D_EXPERT, D_MODEL), BF16),
                            pltpu.VMEM((2, D_MODEL, D_EXPERT), F32), pltpu.VMEM((2, D_MODEL, D_EXPERT), F32),
                            pltpu.VMEM((2, D_EXPERT, D_MODEL), F32), pltpu.SemaphoreType.DMA((2,))],
        ),
        out_shape=jax.ShapeDtypeStruct((MOE_ROWS, D_MODEL), F32),
        compiler_params=_cparams("arbitrary"),
        name="moe_ffn",
    )(block_e, nused, nexte, wslot, xs, w_gate, w_up, w_down)


def _combine_kernel(dest0_ref, dest1_ref, x1_ref, rt_ref, gf_ref, ys_ref, y_ref, ya, yb, sem):
    rows = x1_ref.shape[0]
    base = pl.program_id(0) * rows

    def row_copy(t, dest_ref, buf):
        return pltpu.make_async_copy(ys_ref.at[pl.ds(dest_ref[base + t], 1)], buf.at[pl.ds(t, 1)], sem)

    def start(t, c):
        row_copy(t, dest0_ref, ya).start()
        row_copy(t, dest1_ref, yb).start()
        return c

    lax.fori_loop(0, rows, start, 0, unroll=ROW_DMA_UNROLL)
    pltpu.make_async_copy(ys_ref.at[pl.ds(0, rows)], ya, sem).wait()
    pltpu.make_async_copy(ys_ref.at[pl.ds(0, rows)], yb, sem).wait()
    x2 = x1_ref[...] + rt_ref[:, 4:5] * ya[...] + rt_ref[:, 5:6] * yb[...]
    y_ref[...] = _rms(x2, gf_ref[...])


def _combine(dest, x1, rt, gf, ys, rows):
    n = x1.shape[0]
    return pl.pallas_call(
        _combine_kernel,
        grid_spec=pltpu.PrefetchScalarGridSpec(
            num_scalar_prefetch=2,
            grid=(n // rows,),
            in_specs=[pl.BlockSpec((rows, D_MODEL), lambda i, *_: (i, 0)),
                      pl.BlockSpec((rows, LANES), lambda i, *_: (i, 0)),
                      pl.BlockSpec((1, D_MODEL), lambda i, *_: (0, 0)),
                      pl.BlockSpec(memory_space=pl.ANY)],
            out_specs=pl.BlockSpec((rows, D_MODEL), lambda i, *_: (i, 0)),
            scratch_shapes=[pltpu.VMEM((rows, D_MODEL), F32), pltpu.VMEM((rows, D_MODEL), F32),
                            pltpu.SemaphoreType.DMA(())],
        ),
        out_shape=jax.ShapeDtypeStruct((n, D_MODEL), F32),
        compiler_params=_cparams("arbitrary"),
        name=f"moe_combine_{rows}",
    )(dest[0], dest[1], x1, rt, gf, ys)


def _rope_tables(pos, xp):
    half = HEAD_DIM // 2
    inv = 1.0 / (xp.float32(ROPE_THETA) ** (xp.arange(half, dtype=xp.float32) * xp.float32(2.0 / HEAD_DIM)))
    ang = pos.astype(xp.float32)[:, None] * inv[None, :].astype(xp.float32)
    cos, sin = xp.cos(ang), xp.sin(ang)
    return xp.concatenate([cos, cos, cos, cos], axis=1), xp.concatenate([-sin, sin, -sin, sin], axis=1)


def _slot_dest(ids, pos, pstarts):
    ids = ids.astype(jnp.int32)
    experts = jnp.arange(N_EXPERTS, dtype=jnp.int32).reshape((N_EXPERTS,) + (1,) * ids.ndim)
    start = jnp.sum(jnp.where(ids[None] == experts, pstarts.reshape(experts.shape), 0), axis=0)
    return start + pos.astype(jnp.int32)


def kernel(x_prompt, x_sample, cache_kv_g0, cache_kv_g1, cache_kv_g2, norm1_g, w_in, b_in, a_ln_g, a_ln_b, w_spatial, b_spatial, w_a_proj, w_b_proj, w_o, norm2_g, w_group_router, b_group_router, w_expert_router, b_expert_router, w_gate, w_up, w_down, final_norm_g):
    x2 = x_prompt.reshape(T_PROMPT, D_MODEL)
    xs = x_sample.reshape(DEC_BATCH, D_MODEL)
    g1 = norm1_g[0][None, :]
    g2 = norm2_g[0][None, :]
    gf = final_norm_g[None, :]
    b_in2 = b_in[0][None, :]
    lng, lnb = a_ln_g[0][None, :], a_ln_b[0][None, :]

    causal = jnp.tril(jnp.ones((CHUNK, CHUNK), dtype=bool))
    ws_tril = jnp.where(causal[None], w_spatial[0], 0.0)
    wsp = jnp.concatenate([ws_tril[0::2], ws_tril[1::2]], axis=2).astype(BF16)
    bsp = jnp.repeat(b_spatial[0].T, A_GROUP_DIM, axis=1)
    ws0 = jnp.repeat(ws_tril[:, 0, 0].astype(BF16).astype(F32), A_GROUP_DIM)[None, :]
    bs0 = jnp.repeat(b_spatial[0][:, 0], A_GROUP_DIM)[None, :]
    w_router = jnp.zeros((D_MODEL, LANES), F32)
    w_router = w_router.at[:, :N_EXPERT_GROUPS].set(w_group_router[0])
    w_router = w_router.at[:, N_EXPERT_GROUPS:N_EXPERT_GROUPS + N_EXPERTS].set(w_expert_router[0])
    b_router = jnp.zeros((1, LANES), F32)
    b_router = b_router.at[0, :N_EXPERT_GROUPS].set(b_group_router[0])
    b_router = b_router.at[0, N_EXPERT_GROUPS:N_EXPERT_GROUPS + N_EXPERTS].set(b_expert_router[0])
    w_router = w_router.astype(BF16)
    ltri = jnp.tril(jnp.ones((TM, TM), BF16), -1)
    w_in_b, w_a_b, w_b_b, w_o_b = (w[0].astype(BF16) for w in (w_in, w_a_proj, w_b_proj, w_o))

    cos_p, sin_p = _rope_tables(np.arange(SEQ, dtype=np.int32), np)
    cos_s, sin_s = _rope_tables(jnp.full((1,), PAST_LEN, jnp.int32), jnp)

    uv, qkv0, qkv1, qkv2, gates, kvp0, kvp1, kvp2 = _inproj(
        x2, g1, w_in_b, b_in2, lng, lnb, cos_p, sin_p)
    ob = _attention((qkv0.reshape(BATCH, 1, SEQ, 3 * GROUP_W), qkv1, qkv2)).reshape(T_PROMPT, GROUP_W)
    x1_p, h2p_p, rt_p, rtt_p, counts_p = _merge(
        x2, uv, gates, ob, wsp, bsp, w_a_b, w_b_b, w_o_b, g2, w_router, b_router, ltri)

    z_s = _sample_inproj(xs, g1, w_in_b, b_in2)
    vrow, sa_s, qr_s, kr_s, kvs0, kvs1, kvs2 = _sample_pre(z_s, lng, lnb, ws0, bs0, cos_s, sin_s)
    dim_major = lambda t: t.reshape(DEC_BATCH, 3, HEADS, HEAD_DIM).transpose(0, 1, 3, 2)
    caches = [c.transpose(0, 1, 3, 4, 5, 2).reshape(DEC_BATCH, 2, HEADS, HEAD_DIM, c.shape[2])
              for c in (cache_kv_g0, cache_kv_g1, cache_kv_g2)]
    comb_s = _sample_attn(dim_major(qr_s), dim_major(kr_s), dim_major(z_s[:, COL_VB:COL_VB + QK_W]), caches)
    comb_s = comb_s.transpose(0, 2, 1).reshape(DEC_BATCH, GROUP_W)
    x1_s, h2p_s, rt_s, counts = _sample_merge(
        xs, z_s, sa_s, comb_s, w_a_b, w_b_b, w_o_b, g2, w_router, b_router,
        ltri[:DEC_BATCH, :DEC_BATCH], counts_p)

    cnt = counts[0, :N_EXPERTS].astype(jnp.int32)
    padded = (cnt + MOE_TB - 1) // MOE_TB * MOE_TB
    pends = jnp.cumsum(padded)
    pstarts = pends - padded
    block_starts = jnp.arange(MOE_NB, dtype=jnp.int32) * MOE_TB
    block_e = jnp.minimum(jnp.sum((pends[None, :] <= block_starts[:, None]).astype(jnp.int32), axis=1),
                          N_EXPERTS - 1)
    nused = (pends[-1:] // MOE_TB).astype(jnp.int32)
    rtt_p = rtt_p.reshape(T_PROMPT // TM, ROUTE_ROWS, TM)
    dest_p = [_slot_dest(rtt_p[:, k], rtt_p[:, 2 + k], pstarts).reshape(T_PROMPT) for k in range(2)]
    dest_s = [_slot_dest(rt_s[:, k], rt_s[:, 2 + k], pstarts) for k in range(2)]

    rows = jnp.zeros((MOE_ROWS, HALF_D), U32)
    rows = _dispatch(dest_p, h2p_p, rows, TM)
    rows = _dispatch(dest_s, h2p_s, rows, DEC_BATCH)
    ys = _expert_ffn(block_e, nused, rows, w_gate[0], w_up[0], w_down[0])
    y_p = _combine(dest_p, x1_p, rt_p, gf, ys, TM)
    y_s = _combine(dest_s, x1_s, rt_s, gf, ys, DEC_BATCH)

    kv_shape = lambda n, w: (1, n, w, 2, HEADS, HEAD_DIM)
    window_rows = lambda t: t.reshape(1, BATCH, 2, HEADS, HEAD_DIM, t.shape[2]).transpose(0, 1, 5, 2, 3, 4)
    return (y_p.reshape(BATCH, SEQ, D_MODEL), y_s.reshape(DEC_BATCH, 1, D_MODEL),
            window_rows(kvp0), window_rows(kvp1), window_rows(kvp2),
            kvs0.reshape(kv_shape(DEC_BATCH, 1)), kvs1.reshape(kv_shape(DEC_BATCH, 1)),
            kvs2.reshape(kv_shape(DEC_BATCH, 1)), vrow.reshape(1, DEC_BATCH, 1, A_WIDTH))
```

```python
import functools

import jax
import jax.numpy as jnp
import numpy as np
from jax import lax
from jax.experimental import pallas as pl
from jax.experimental.pallas import tpu as pltpu

F32 = jnp.float32
BF16 = jnp.bfloat16
U32 = jnp.uint32

D_MODEL = 1024
BATCH = 2
SEQ = 8192
DEC_BATCH = 32
PAST_LEN = 8192
CHUNK = 128
A_GROUPS = 8
A_GROUP_DIM = 64
A_WIDTH = 512
HEAD_DIM = 64
HEADS = 4
GROUP_W = HEADS * HEAD_DIM
DILATIONS = (1, 4, 16)
SPAN = 128
QK_W = 768
IN_WIDTH = 5376
COL_U, COL_V, COL_Q, COL_K, COL_VB, COL_GA, COL_GB = 0, 512, 1024, 1792, 2560, 3328, 4352
N_EXPERT_GROUPS = 4
EXPERTS_PER_GROUP = 8
N_EXPERTS = 32
D_EXPERT = 512
ROPE_THETA = 10000.0
EPS = 1e-6

LANES = 128
T_PROMPT = BATCH * SEQ
TM = 512
TILES_PER_SEQ = SEQ // TM
WINDOWS = tuple(min(SPAN * d, SEQ) for d in DILATIONS)
TA = SPAN * max(DILATIONS)
ATT_UNROLL = 2
MOE_TB = 512
ROW_DMA_UNROLL = 8
ROUTE_ROWS = 8
N_SLOTS = 2 * (T_PROMPT + DEC_BATCH)
MOE_NB = -(-N_SLOTS // MOE_TB) + N_EXPERTS
MOE_ROWS = MOE_NB * MOE_TB
HALF_D = D_MODEL // 2
NEG = -1e30
VMEM_LIMIT = 56 * 1024 * 1024


def _cparams(*sem):
    return pltpu.CompilerParams(dimension_semantics=sem, vmem_limit_bytes=VMEM_LIMIT)


def _rms(x, g):
    return x * lax.rsqrt(jnp.mean(x * x, axis=-1, keepdims=True) + EPS) * g


def _layer_norm(x, g, b):
    mu = jnp.mean(x, axis=-1, keepdims=True)
    xc = x - mu
    var = jnp.mean(xc * xc, axis=-1, keepdims=True)
    return xc * lax.rsqrt(var + EPS) * g + b


def _rope_chunk(x, cos, sin_signed):
    lane = lax.broadcasted_iota(jnp.int32, x.shape, 1)
    first_half = (lane % HEAD_DIM) < (HEAD_DIM // 2)
    swapped = jnp.where(first_half, pltpu.roll(x, LANES - HEAD_DIM // 2, 1), pltpu.roll(x, HEAD_DIM // 2, 1))
    return x * cos + swapped * sin_signed


def _pack_bf16_pair(h):
    lo = lax.bitcast_convert_type(h[:, :HALF_D].astype(BF16).astype(F32), U32)
    hi = lax.bitcast_convert_type(h[:, HALF_D:].astype(BF16).astype(F32), U32)
    return (hi & jnp.uint32(0xFFFF0000)) | (lo >> 16)


def _unpack_bf16_pair(p):
    lo = lax.bitcast_convert_type(p << 16, F32).astype(BF16)
    hi = lax.bitcast_convert_type(p & jnp.uint32(0xFFFF0000), F32).astype(BF16)
    return lo, hi


def _route(logits, prior_counts, ltri):
    rows = logits.shape[0]
    lane = lax.broadcasted_iota(jnp.int32, (rows, LANES), 1).astype(F32)
    is_g = lane < N_EXPERT_GROUPS
    gl = jnp.where(is_g, logits, NEG)
    gmax = jnp.max(gl, axis=1, keepdims=True)
    grp = jnp.min(jnp.where(gl == gmax, lane, float(LANES)), axis=1, keepdims=True)
    p_grp = 1.0 / jnp.sum(jnp.where(is_g, jnp.exp(gl - gmax), 0.0), axis=1, keepdims=True)
    lo = N_EXPERT_GROUPS + grp * EXPERTS_PER_GROUP
    el = jnp.where((lane >= lo) & (lane < lo + EXPERTS_PER_GROUP), logits, NEG)
    v1 = jnp.max(el, axis=1, keepdims=True)
    i1 = jnp.min(jnp.where(el == v1, lane, float(LANES)), axis=1, keepdims=True)
    el2 = jnp.where(lane == i1, NEG, el)
    v2 = jnp.max(el2, axis=1, keepdims=True)
    i2 = jnp.min(jnp.where(el2 == v2, lane, float(LANES)), axis=1, keepdims=True)
    t = jnp.exp(v2 - v1)
    gate1 = p_grp / (1.0 + t)
    gate2 = p_grp * t / (1.0 + t)
    e1 = i1 - N_EXPERT_GROUPS
    e2 = i2 - N_EXPERT_GROUPS
    hit1 = lane == e1
    hit2 = lane == e2
    onehot = jnp.where(hit1 | hit2, 1.0, 0.0)
    rank = jnp.dot(ltri, onehot.astype(BF16), preferred_element_type=F32) + prior_counts
    pos1 = jnp.sum(jnp.where(hit1, rank, 0.0), axis=1, keepdims=True)
    pos2 = jnp.sum(jnp.where(hit2, rank, 0.0), axis=1, keepdims=True)
    slab = jnp.where(lane == 0, e1, jnp.where(lane == 1, e2, jnp.where(lane == 2, pos1, jnp.where(
        lane == 3, pos2, jnp.where(lane == 4, gate1, jnp.where(lane == 5, gate2, 0.0))))))
    return slab, prior_counts + jnp.sum(onehot, axis=0, keepdims=True)


def _inproj_kernel(x_ref, g1_ref, w_ref, b_ref, lng_ref, lnb_ref, cos_ref, sin_ref,
                   uv_ref, qkv0_ref, qkv1_ref, qkv2_ref, gates_ref, kv0_ref, kv1_ref, kv2_ref, de_ref):
    tile_in_seq = pl.program_id(0) % TILES_PER_SEQ
    hb = _rms(x_ref[...], g1_ref[...]).astype(BF16)

    def seg(lo, width):
        return jnp.dot(hb, w_ref[:, lo:lo + width], preferred_element_type=F32) + b_ref[:, lo:lo + width]

    uv_ref[:, :A_WIDTH] = jax.nn.gelu(seg(COL_U, A_WIDTH)).astype(BF16)
    uv_ref[:, A_WIDTH:] = _layer_norm(jax.nn.gelu(seg(COL_V, A_WIDTH)), lng_ref[...], lnb_ref[...]).astype(BF16)
    gates_ref[:, :D_MODEL] = jax.nn.sigmoid(seg(COL_GA, D_MODEL)).astype(BF16)
    gates_ref[:, D_MODEL:] = jax.nn.sigmoid(seg(COL_GB, D_MODEL)).astype(BF16)

    cos = cos_ref[...]
    sin = sin_ref[...]
    qkv_refs = (qkv0_ref, qkv1_ref, qkv2_ref)
    kv_refs = (kv0_ref, kv1_ref, kv2_ref)
    for g, d in enumerate(DILATIONS):
        q = seg(COL_Q + g * GROUP_W, GROUP_W)
        k = seg(COL_K + g * GROUP_W, GROUP_W)
        v = seg(COL_VB + g * GROUP_W, GROUP_W)
        chunks = GROUP_W // LANES
        for c in range(chunks):
            sl = slice(c * LANES, (c + 1) * LANES)
            de_ref[c] = _rope_chunk(q[:, sl], cos, sin) * (HEAD_DIM ** -0.5)
            de_ref[chunks + c] = _rope_chunk(k[:, sl], cos, sin)
            de_ref[2 * chunks + c] = v[:, sl]
        for c in range(3 * chunks):
            sl = slice(c * LANES, (c + 1) * LANES)
            if d == 1:
                qkv_refs[g][:, sl] = de_ref[c].astype(BF16)
            else:
                for r in range(d):
                    qkv_refs[g][0, r, :, sl] = de_ref[c, pl.ds(r, TM // d, stride=d), :].astype(BF16)
        kv_rows = kv_refs[g].shape[2]
        first = TILES_PER_SEQ - WINDOWS[g] // kv_rows

        @pl.when(tile_in_seq >= first)
        def _(kv_ref=kv_refs[g], kv_rows=kv_rows, chunks=chunks):
            for c in range(2 * chunks):
                kv_ref[0, c * LANES:(c + 1) * LANES, :] = de_ref[chunks + c, TM - kv_rows:, :].T


def _inproj(x2, g1, w_bf, b_in, lng, lnb, cos_t, sin_t):
    tiles_per_seq = TILES_PER_SEQ
    n_tiles = T_PROMPT // TM
    const = lambda i: (0, 0)

    def kv_spec(w):
        rows = min(w, TM)
        first = tiles_per_seq - w // rows
        return pl.BlockSpec((1, 2 * GROUP_W, rows),
                            lambda i: (i // tiles_per_seq, 0, jnp.maximum(i % tiles_per_seq - first, 0)))

    def regrouped_spec(d):
        return pl.BlockSpec((1, d, TM // d, 3 * GROUP_W), lambda i: (i // tiles_per_seq, 0, i % tiles_per_seq, 0))

    return pl.pallas_call(
        _inproj_kernel,
        grid=(n_tiles,),
        in_specs=[
            pl.BlockSpec((TM, D_MODEL), lambda i: (i, 0)),
            pl.BlockSpec((1, D_MODEL), const),
            pl.BlockSpec((D_MODEL, IN_WIDTH), const),
            pl.BlockSpec((1, IN_WIDTH), const),
            pl.BlockSpec((1, A_WIDTH), const),
            pl.BlockSpec((1, A_WIDTH), const),
            pl.BlockSpec((TM, LANES), lambda i: (i % tiles_per_seq, 0)),
            pl.BlockSpec((TM, LANES), lambda i: (i % tiles_per_seq, 0)),
        ],
        out_specs=[
            pl.BlockSpec((TM, 2 * A_WIDTH), lambda i: (i, 0)),
            pl.BlockSpec((TM, 3 * GROUP_W), lambda i: (i, 0)),
            regrouped_spec(DILATIONS[1]),
            regrouped_spec(DILATIONS[2]),
            pl.BlockSpec((TM, 2 * D_MODEL), lambda i: (i, 0)),
            kv_spec(WINDOWS[0]), kv_spec(WINDOWS[1]), kv_spec(WINDOWS[2]),
        ],
        out_shape=[
            jax.ShapeDtypeStruct((T_PROMPT, 2 * A_WIDTH), BF16),
            jax.ShapeDtypeStruct((T_PROMPT, 3 * GROUP_W), BF16),
            jax.ShapeDtypeStruct((BATCH, DILATIONS[1], SEQ // DILATIONS[1], 3 * GROUP_W), BF16),
            jax.ShapeDtypeStruct((BATCH, DILATIONS[2], SEQ // DILATIONS[2], 3 * GROUP_W), BF16),
            jax.ShapeDtypeStruct((T_PROMPT, 2 * D_MODEL), BF16),
            jax.ShapeDtypeStruct((BATCH, 2 * GROUP_W, WINDOWS[0]), F32),
            jax.ShapeDtypeStruct((BATCH, 2 * GROUP_W, WINDOWS[1]), F32),
            jax.ShapeDtypeStruct((BATCH, 2 * GROUP_W, WINDOWS[2]), F32),
        ],
        scratch_shapes=[pltpu.VMEM((3 * GROUP_W // LANES, TM, LANES), F32)],
        compiler_params=_cparams("arbitrary"),
        name="inproj",
    )(x2, g1, w_bf, b_in, lng, lnb, cos_t, sin_t)


def _attn_kernel(c0_ref, p0_ref, c1_ref, p1_ref, c2_ref, p2_ref, o_ref, acc_ref, m_ref, l_ref):
    n = pl.program_id(1)
    qi = lax.broadcasted_iota(jnp.int32, (HEADS * SPAN, 2 * SPAN), 0) % SPAN
    ki = lax.broadcasted_iota(jnp.int32, (HEADS * SPAN, 2 * SPAN), 1)
    band = (ki >= qi) & (ki <= qi + SPAN)
    band_first = band & ((ki >= SPAN) | (n > 0))
    q_head = lax.broadcasted_iota(jnp.int32, (SPAN, GROUP_W), 1) // HEAD_DIM
    kv_head = lax.broadcasted_iota(jnp.int32, (2 * SPAN, GROUP_W), 1) // HEAD_DIM
    contract_last = (((1,), (1,)), ((), ()))
    k_cols = slice(GROUP_W, 2 * GROUP_W)
    v_cols = slice(2 * GROUP_W, 3 * GROUP_W)

    def attend(q, keys, vals, mask, tok_rows, fresh):
        q_heads = jnp.concatenate([jnp.where(q_head == h, q, jnp.zeros_like(q)) for h in range(HEADS)], axis=0)
        s = jnp.where(mask, lax.dot_general(q_heads, keys, contract_last, preferred_element_type=F32), NEG)
        m = jnp.max(s, axis=1, keepdims=True)
        p = jnp.exp(s - m)
        l = jnp.sum(p, axis=1, keepdims=True)
        pb = p.astype(BF16)
        p_heads = jnp.concatenate([pb[h * SPAN:(h + 1) * SPAN] for h in range(HEADS)], axis=1)
        v_heads = jnp.concatenate([jnp.where(kv_head == h, vals, jnp.zeros_like(vals)) for h in range(HEADS)], axis=0)
        acc = jnp.dot(p_heads, v_heads, preferred_element_type=F32)
        m_b = jnp.zeros((SPAN, GROUP_W), F32)
        l_b = jnp.zeros((SPAN, GROUP_W), F32)
        for h in range(HEADS):
            rows = slice(h * SPAN, (h + 1) * SPAN)
            m_b = jnp.where(q_head == h, m[rows], m_b)
            l_b = jnp.where(q_head == h, l[rows], l_b)
        for c in range(GROUP_W // LANES):
            sl = slice(c * LANES, (c + 1) * LANES)
            if fresh:
                m_ref[c, tok_rows, :] = m_b[:, sl]
                l_ref[c, tok_rows, :] = l_b[:, sl]
                acc_ref[c, tok_rows, :] = acc[:, sl]
                continue
            m_old = m_ref[c, tok_rows, :]
            m_new = jnp.maximum(m_old, m_b[:, sl])
            a_old = jnp.exp(m_old - m_new)
            a_blk = jnp.exp(m_b[:, sl] - m_new)
            l_ref[c, tok_rows, :] = a_old * l_ref[c, tok_rows, :] + a_blk * l_b[:, sl]
            acc_ref[c, tok_rows, :] = a_old * acc_ref[c, tok_rows, :] + a_blk * acc[:, sl]
            m_ref[c, tok_rows, :] = m_new

    def stream(c_ref, p_ref, r, d, fresh):
        n_blocks = c_ref.shape[2] // SPAN

        def tok_rows(j):
            start = j * (SPAN * d) + r
            return pl.ds(start, SPAN) if d == 1 else pl.ds(start, SPAN, stride=d)

        keys = jnp.concatenate([p_ref[0, r, :, k_cols], c_ref[0, r, :SPAN, k_cols]], axis=0)
        vals = jnp.concatenate([p_ref[0, r, :, v_cols], c_ref[0, r, :SPAN, v_cols]], axis=0)
        attend(c_ref[0, r, :SPAN, :GROUP_W], keys, vals, band_first, tok_rows(0), fresh)

        def later_block(j, carry):
            q_rows = pl.ds(pl.multiple_of(j * SPAN, SPAN), SPAN)
            kv_rows = pl.ds(pl.multiple_of((j - 1) * SPAN, SPAN), 2 * SPAN)
            attend(c_ref[0, r, q_rows, :GROUP_W], c_ref[0, r, kv_rows, k_cols], c_ref[0, r, kv_rows, v_cols],
                   band, tok_rows(j), fresh)
            return carry

        if n_blocks > 1:
            lax.fori_loop(1, n_blocks, later_block, 0, unroll=3)

    groups = sorted(zip(DILATIONS, ((c0_ref, p0_ref), (c1_ref, p1_ref), (c2_ref, p2_ref))), key=lambda g: -g[0])
    for idx, (d, (c_ref, p_ref)) in enumerate(groups):
        fresh = idx == 0
        if d == 1:
            stream(c_ref, p_ref, 0, d, fresh)
        else:
            lax.fori_loop(0, d, lambda r, carry, c_ref=c_ref, p_ref=p_ref, d=d, fresh=fresh:
                          (stream(c_ref, p_ref, r, d, fresh), carry)[1], 0,
                          unroll=ATT_UNROLL if c_ref.shape[2] == SPAN else 1)

    for c in range(GROUP_W // LANES):
        o_ref[0, :, c * LANES:(c + 1) * LANES] = (acc_ref[c] / l_ref[c]).astype(BF16)


def _attention(qkv_by_group):
    in_specs, args = [], []
    for qkv, d in zip(qkv_by_group, DILATIONS):
        rows = TA // d
        blocks_per_tile = rows // SPAN
        in_specs.append(pl.BlockSpec((1, d, rows, 3 * GROUP_W), lambda b, n: (b, 0, n, 0)))
        in_specs.append(pl.BlockSpec((1, d, SPAN, 3 * GROUP_W),
                                     lambda b, n, k=blocks_per_tile: (b, 0, jnp.maximum(n * k - 1, 0), 0)))
        args += [qkv, qkv]
    return pl.pallas_call(
        _attn_kernel,
        grid=(BATCH, SEQ // TA),
        in_specs=in_specs,
        out_specs=pl.BlockSpec((1, TA, GROUP_W), lambda b, n: (b, n, 0)),
        out_shape=jax.ShapeDtypeStruct((BATCH, SEQ, GROUP_W), BF16),
        scratch_shapes=[pltpu.VMEM((GROUP_W // LANES, TA, LANES), F32)] * 3,
        compiler_params=_cparams("arbitrary", "arbitrary"),
        name="attn",
    )(*args)


def _merge_kernel(x_ref, uv_ref, gates_ref, ob_ref,
                  wsp_ref, bsp_ref, wa_ref, wb_ref, wo_ref, g2_ref, wr_ref, br_ref, ltri_ref,
                  x1_ref, h2p_ref, rt_ref, rtt_ref, cnt_ref, run_ref):
    @pl.when(pl.program_id(0) == 0)
    def _():
        run_ref[...] = jnp.zeros_like(run_ref)

    lane = lax.broadcasted_iota(jnp.int32, (CHUNK, LANES), 1)
    left = lane < A_GROUP_DIM
    zero = jnp.zeros((CHUNK, LANES), BF16)
    sa_chunks = []
    for c in range(TM // CHUNK):
        rows = slice(c * CHUNK, (c + 1) * CHUNK)
        pairs = []
        for p in range(A_GROUPS // 2):
            vp = uv_ref[rows, A_WIDTH + p * LANES:A_WIDTH + (p + 1) * LANES]
            rhs = jnp.concatenate([jnp.where(left, vp, zero), jnp.where(left, zero, vp)], axis=0)
            pairs.append(jnp.dot(wsp_ref[p], rhs, preferred_element_type=F32))
        mixed = jnp.concatenate(pairs, axis=1) + bsp_ref[...]
        sa_chunks.append((uv_ref[rows, :A_WIDTH].astype(F32) * mixed).astype(BF16))
    s_a = jnp.concatenate(sa_chunks, axis=0)

    a = jnp.dot(s_a, wa_ref[...], preferred_element_type=F32)
    b = jnp.dot(ob_ref[...], wb_ref[...], preferred_element_type=F32)
    merged = gates_ref[:, :D_MODEL].astype(F32) * a + gates_ref[:, D_MODEL:].astype(F32) * b
    x1 = x_ref[...] + jnp.dot(merged.astype(BF16), wo_ref[...], preferred_element_type=F32)
    x1_ref[...] = x1

    h2 = _rms(x1, g2_ref[...])
    h2p_ref[...] = _pack_bf16_pair(h2)
    logits = jnp.dot(h2.astype(BF16), wr_ref[...], preferred_element_type=F32) + br_ref[...]
    slab, counts = _route(logits, run_ref[...], ltri_ref[...])
    rt_ref[...] = slab
    rtt_ref[...] = slab.T[:ROUTE_ROWS, :]
    run_ref[...] = counts
    cnt_ref[...] = counts


def _merge(x2, uv, gates, ob, wsp, bsp, wa, wb, wo, g2, wr, br, ltri):
    n_tiles = T_PROMPT // TM
    tile = lambda w: pl.BlockSpec((TM, w), lambda i: (i, 0))
    full = lambda a: pl.BlockSpec(a.shape, lambda i: (0,) * a.ndim)
    return pl.pallas_call(
        _merge_kernel,
        grid=(n_tiles,),
        in_specs=[tile(D_MODEL), tile(2 * A_WIDTH), tile(2 * D_MODEL), tile(GROUP_W),
                  full(wsp), full(bsp), full(wa), full(wb), full(wo), full(g2), full(wr), full(br),
                  full(ltri)],
        out_specs=[tile(D_MODEL), tile(HALF_D), tile(LANES), pl.BlockSpec((ROUTE_ROWS, TM), lambda i: (i, 0)),
                   pl.BlockSpec((1, LANES), lambda i: (0, 0))],
        out_shape=[
            jax.ShapeDtypeStruct((T_PROMPT, D_MODEL), F32),
            jax.ShapeDtypeStruct((T_PROMPT, HALF_D), U32),
            jax.ShapeDtypeStruct((T_PROMPT, LANES), F32),
            jax.ShapeDtypeStruct((n_tiles * ROUTE_ROWS, TM), F32),
            jax.ShapeDtypeStruct((1, LANES), F32),
        ],
        scratch_shapes=[pltpu.VMEM((1, LANES), F32)],
        compiler_params=_cparams("arbitrary"),
        name="merge_route",
    )(x2, uv, gates, ob, wsp, bsp, wa, wb, wo, g2, wr, br, ltri)


SAMPLE_COLS = 768


def _sample_inproj_kernel(x_ref, g1_ref, w_ref, b_ref, z_ref):
    hb = _rms(x_ref[...], g1_ref[...]).astype(BF16)
    z_ref[...] = jnp.dot(hb, w_ref[...], preferred_element_type=F32) + b_ref[...]


def _sample_inproj(xs, g1, w_in, b_in):
    const = lambda j: (0, 0)
    return pl.pallas_call(
        _sample_inproj_kernel,
        grid=(IN_WIDTH // SAMPLE_COLS,),
        in_specs=[pl.BlockSpec((DEC_BATCH, D_MODEL), const), pl.BlockSpec((1, D_MODEL), const),
                  pl.BlockSpec((D_MODEL, SAMPLE_COLS), lambda j: (0, j)),
                  pl.BlockSpec((1, SAMPLE_COLS), lambda j: (0, j))],
        out_specs=pl.BlockSpec((DEC_BATCH, SAMPLE_COLS), lambda j: (0, j)),
        out_shape=jax.ShapeDtypeStruct((DEC_BATCH, IN_WIDTH), F32),
        compiler_params=_cparams("arbitrary"),
        name="sample_inproj",
    )(xs, g1, w_in, b_in)


def _sample_pre_kernel(z_ref, lng_ref, lnb_ref, ws0_ref, bs0_ref, cos_ref, sin_ref,
                       vrow_ref, sa_ref, qr_ref, kr_ref, kv0_ref, kv1_ref, kv2_ref):
    u = jax.nn.gelu(z_ref[:, COL_U:COL_U + A_WIDTH])
    va = _layer_norm(jax.nn.gelu(z_ref[:, COL_V:COL_V + A_WIDTH]), lng_ref[...], lnb_ref[...])
    vrow_ref[...] = va
    sa_ref[...] = u * (ws0_ref[...] * va.astype(BF16).astype(F32) + bs0_ref[...])
    cos, sin = cos_ref[...], sin_ref[...]
    for g, kv_ref in enumerate((kv0_ref, kv1_ref, kv2_ref)):
        for c in range(GROUP_W // LANES):
            off = g * GROUP_W + c * LANES
            qr_ref[:, off:off + LANES] = (_rope_chunk(z_ref[:, COL_Q + off:COL_Q + off + LANES], cos, sin)
                                          * (HEAD_DIM ** -0.5))
            kr = _rope_chunk(z_ref[:, COL_K + off:COL_K + off + LANES], cos, sin)
            kr_ref[:, off:off + LANES] = kr
            kv_ref[:, c * LANES:(c + 1) * LANES] = kr
        kv_ref[:, GROUP_W:] = z_ref[:, COL_VB + g * GROUP_W:COL_VB + (g + 1) * GROUP_W]


def _sample_pre(z, lng, lnb, ws0, bs0, cos_s, sin_s):
    row = lambda w: jax.ShapeDtypeStruct((DEC_BATCH, w), F32)
    return pl.pallas_call(
        _sample_pre_kernel,
        out_shape=[row(A_WIDTH), row(A_WIDTH), row(QK_W), row(QK_W),
                   row(2 * GROUP_W), row(2 * GROUP_W), row(2 * GROUP_W)],
        compiler_params=pltpu.CompilerParams(vmem_limit_bytes=VMEM_LIMIT),
        name="sample_pre",
    )(z, lng, lnb, ws0, bs0, cos_s, sin_s)


def _sample_attn_kernel(q_ref, k_ref, v_ref, c0_ref, c1_ref, c2_ref, comb_ref):
    as_operand = lambda t: t.astype(BF16).astype(F32)
    for h in range(HEADS):
        head = slice(h, h + 1)
        outs, lses = [], []
        for g, (c_ref, d) in enumerate(zip((c0_ref, c1_ref, c2_ref), DILATIONS)):
            qc, kn, vn = (as_operand(r[0, g, :, head]) for r in (q_ref, k_ref, v_ref))
            s = jnp.sum(as_operand(c_ref[0, 0, h]) * qc, axis=0, keepdims=True)
            if d > 1:
                pos = lax.broadcasted_iota(jnp.int32, s.shape, 1)
                s = jnp.where(pos % d == 0, s, NEG)
            s_n = jnp.sum(qc * kn, axis=0, keepdims=True)
            m = jnp.maximum(jnp.max(s, axis=1, keepdims=True), s_n)
            lse = m + jnp.log(jnp.sum(jnp.exp(s - m), axis=1, keepdims=True) + jnp.exp(s_n - m))
            p = as_operand(jnp.exp(s - lse))
            p_n = as_operand(jnp.exp(s_n - lse))
            outs.append(jnp.sum(as_operand(c_ref[0, 1, h]) * p, axis=1, keepdims=True) + p_n * vn)
            lses.append(lse)
        mx = jnp.maximum(jnp.maximum(lses[0], lses[1]), lses[2])
        ws = [jnp.exp(t - mx) for t in lses]
        tot = ws[0] + ws[1] + ws[2]
        comb_ref[0, :, head] = (ws[0] * outs[0] + ws[1] * outs[1] + ws[2] * outs[2]) / tot


def _sample_attn(qt, kt, vt, caches):
    new_spec = pl.BlockSpec((1, 3, HEAD_DIM, HEADS), lambda i: (i, 0, 0, 0))
    cache_spec = lambda c: pl.BlockSpec((1,) + c.shape[1:], lambda i: (i, 0, 0, 0, 0))
    return pl.pallas_call(
        _sample_attn_kernel,
        grid=(DEC_BATCH,),
        in_specs=[new_spec, new_spec, new_spec, cache_spec(caches[0]), cache_spec(caches[1]), cache_spec(caches[2])],
        out_specs=pl.BlockSpec((1, HEAD_DIM, HEADS), lambda i: (i, 0, 0)),
        out_shape=jax.ShapeDtypeStruct((DEC_BATCH, HEAD_DIM, HEADS), F32),
        compiler_params=_cparams("arbitrary"),
        name="sample_attn",
    )(qt, kt, vt, caches[0], caches[1], caches[2])


def _sample_merge_kernel(x_ref, z_ref, sa_ref, comb_ref, wa_ref, wb_ref, wo_ref, g2_ref, wr_ref, br_ref,
                         ltri_ref, cnt_in_ref, x1_ref, h2p_ref, rt_ref, cnt_ref):
    dot = lambda p, q: jnp.dot(p.astype(BF16), q, preferred_element_type=F32)
    a = dot(sa_ref[...], wa_ref[...])
    b = dot(comb_ref[...], wb_ref[...])
    merged = (jax.nn.sigmoid(z_ref[:, COL_GA:COL_GA + D_MODEL]) * a
              + jax.nn.sigmoid(z_ref[:, COL_GB:COL_GB + D_MODEL]) * b)
    x1 = x_ref[...] + dot(merged, wo_ref[...])
    x1_ref[...] = x1
    h2 = _rms(x1, g2_ref[...])
    h2p_ref[...] = _pack_bf16_pair(h2)
    logits = dot(h2, wr_ref[...]) + br_ref[...]
    slab, counts = _route(logits, cnt_in_ref[...], ltri_ref[...])
    rt_ref[...] = slab
    cnt_ref[...] = counts


def _sample_merge(xs, z, s_a, comb, wa, wb, wo, g2, wr, br, ltri, counts):
    return pl.pallas_call(
        _sample_merge_kernel,
        out_shape=[jax.ShapeDtypeStruct((DEC_BATCH, D_MODEL), F32),
                   jax.ShapeDtypeStruct((DEC_BATCH, HALF_D), U32),
                   jax.ShapeDtypeStruct((DEC_BATCH, LANES), F32),
                   jax.ShapeDtypeStruct((1, LANES), F32)],
        compiler_params=pltpu.CompilerParams(vmem_limit_bytes=VMEM_LIMIT),
        name="sample_merge_route",
    )(xs, z, s_a, comb, wa, wb, wo, g2, wr, br, ltri, counts)


def _dispatch_kernel(dest0_ref, dest1_ref, h_ref, xs_in_ref, xs_ref, sem):
    del xs_in_ref
    rows = h_ref.shape[0]
    base = pl.program_id(0) * rows

    def row_copy(t, dest_ref):
        return pltpu.make_async_copy(h_ref.at[pl.ds(t, 1)], xs_ref.at[pl.ds(dest_ref[base + t], 1)], sem)

    def start(t, c):
        row_copy(t, dest0_ref).start()
        row_copy(t, dest1_ref).start()
        return c

    lax.fori_loop(0, rows, start, 0, unroll=ROW_DMA_UNROLL)
    for _ in range(2):
        pltpu.make_async_copy(h_ref, xs_ref.at[pl.ds(0, rows)], sem).wait()


def _dispatch(dest, h2p, xs, rows):
    n = h2p.shape[0]
    return pl.pallas_call(
        _dispatch_kernel,
        grid_spec=pltpu.PrefetchScalarGridSpec(
            num_scalar_prefetch=2,
            grid=(n // rows,),
            in_specs=[pl.BlockSpec((rows, HALF_D), lambda i, *_: (i, 0)), pl.BlockSpec(memory_space=pl.ANY)],
            out_specs=pl.BlockSpec(memory_space=pl.ANY),
            scratch_shapes=[pltpu.SemaphoreType.DMA(())],
        ),
        out_shape=jax.ShapeDtypeStruct(xs.shape, xs.dtype),
        input_output_aliases={3: 0},
        compiler_params=_cparams("arbitrary"),
        name=f"moe_dispatch_{rows}",
    )(dest[0], dest[1], h2p, xs)


def _ffn_kernel(be_ref, nused_ref, nexte_ref, wslot_ref, xs_ref, wg_ref, wu_ref, wd_ref, ys_ref,
                wg_b, wu_b, wd_b, wg_f, wu_f, wd_f, wsem):
    i = pl.program_id(0)
    live = i < nused_ref[0]
    new_expert = jnp.logical_and(live, jnp.logical_or(i == 0, be_ref[i] != be_ref[jnp.maximum(i - 1, 0)]))

    def weight_copies(e, slot):
        return [pltpu.make_async_copy(src.at[e], dst.at[slot], wsem.at[slot])
                for src, dst in ((wg_ref, wg_f), (wu_ref, wu_f), (wd_ref, wd_f))]

    @pl.when(i == 0)
    def _():
        for cp in weight_copies(be_ref[0], 0):
            cp.start()

    @pl.when(new_expert)
    def _():
        slot = wslot_ref[i]
        for cp in weight_copies(be_ref[i], slot):
            cp.wait()
        wg_b[...] = wg_f[slot].astype(BF16)
        wu_b[...] = wu_f[slot].astype(BF16)
        wd_b[...] = wd_f[slot].astype(BF16)

        @pl.when(nexte_ref[i] >= 0)
        def _():
            for cp in weight_copies(nexte_ref[i], 1 - slot):
                cp.start()

    @pl.when(live)
    def _():
        lo, hi = _unpack_bf16_pair(xs_ref[...])
        gate = (jnp.dot(lo, wg_b[:HALF_D, :], preferred_element_type=F32)
                + jnp.dot(hi, wg_b[HALF_D:, :], preferred_element_type=F32))
        up = (jnp.dot(lo, wu_b[:HALF_D, :], preferred_element_type=F32)
              + jnp.dot(hi, wu_b[HALF_D:, :], preferred_element_type=F32))
        hid = (jax.nn.silu(gate) * up).astype(BF16)
        ys_ref[...] = jnp.dot(hid, wd_b[...], preferred_element_type=F32)

    @pl.when(i >= nused_ref[0])
    def _():
        ys_ref[...] = jnp.zeros_like(ys_ref)


def _expert_ffn(block_e, nused, xs, w_gate, w_up, w_down):
    idx = jnp.arange(MOE_NB, dtype=jnp.int32)
    change = ((block_e != jnp.roll(block_e, 1)) | (idx == 0)) & (idx < nused[0])
    wslot = ((jnp.cumsum(change.astype(jnp.int32)) - 1) & 1).astype(jnp.int32)
    change_at = jnp.where(change, idx, MOE_NB)
    next_change = jnp.flip(lax.cummin(jnp.flip(jnp.concatenate([change_at[1:], jnp.full((1,), MOE_NB, jnp.int32)]))))
    nexte = jnp.where(next_change < MOE_NB, block_e[jnp.minimum(next_change, MOE_NB - 1)], -1).astype(jnp.int32)
    hbm = pl.BlockSpec(memory_space=pl.ANY)
    return pl.pallas_call(
        _ffn_kernel,
        grid_spec=pltpu.PrefetchScalarGridSpec(
            num_scalar_prefetch=4,
            grid=(MOE_NB,),
            in_specs=[pl.BlockSpec((MOE_TB, HALF_D), lambda i, *_: (i, 0)), hbm, hbm, hbm],
            out_specs=pl.BlockSpec((MOE_TB, D_MODEL), lambda i, *_: (i, 0)),
            scratch_shapes=[pltpu.VMEM((D_MODEL, D_EXPERT), BF16), pltpu.VMEM((D_MODEL, D_EXPERT), BF16),
                            pltpu.VMEM((D_EXPERT, D_MODEL), BF16),
                            pltpu.VMEM((2, D_MODEL, D_EXPERT), F32), pltpu.VMEM((2, D_MODEL, D_EXPERT), F32),
                            pltpu.VMEM((2, D_EXPERT, D_MODEL), F32), pltpu.SemaphoreType.DMA((2,))],
        ),
        out_shape=jax.ShapeDtypeStruct((MOE_ROWS, D_MODEL), F32),
        compiler_params=_cparams("arbitrary"),
        name="moe_ffn",
    )(block_e, nused, nexte, wslot, xs, w_gate, w_up, w_down)


def _combine_kernel(dest0_ref, dest1_ref, x1_ref, rt_ref, gf_ref, ys_ref, y_ref, ya, yb, sem):
    rows = x1_ref.shape[0]
    base = pl.program_id(0) * rows

    def row_copy(t, dest_ref, buf):
        return pltpu.make_async_copy(ys_ref.at[pl.ds(dest_ref[base + t], 1)], buf.at[pl.ds(t, 1)], sem)

    def start(t, c):
        row_copy(t, dest0_ref, ya).start()
        row_copy(t, dest1_ref, yb).start()
        return c

    lax.fori_loop(0, rows, start, 0, unroll=ROW_DMA_UNROLL)
    pltpu.make_async_copy(ys_ref.at[pl.ds(0, rows)], ya, sem).wait()
    pltpu.make_async_copy(ys_ref.at[pl.ds(0, rows)], yb, sem).wait()
    x2 = x1_ref[...] + rt_ref[:, 4:5] * ya[...] + rt_ref[:, 5:6] * yb[...]
    y_ref[...] = _rms(x2, gf_ref[...])


def _combine(dest, x1, rt, gf, ys, rows):
    n = x1.shape[0]
    return pl.pallas_call(
        _combine_kernel,
        grid_spec=pltpu.PrefetchScalarGridSpec(
            num_scalar_prefetch=2,
            grid=(n // rows,),
            in_specs=[pl.BlockSpec((rows, D_MODEL), lambda i, *_: (i, 0)),
                      pl.BlockSpec((rows, LANES), lambda i, *_: (i, 0)),
                      pl.BlockSpec((1, D_MODEL), lambda i, *_: (0, 0)),
                      pl.BlockSpec(memory_space=pl.ANY)],
            out_specs=pl.BlockSpec((rows, D_MODEL), lambda i, *_: (i, 0)),
            scratch_shapes=[pltpu.VMEM((rows, D_MODEL), F32), pltpu.VMEM((rows, D_MODEL), F32),
                            pltpu.SemaphoreType.DMA(())],
        ),
        out_shape=jax.ShapeDtypeStruct((n, D_MODEL), F32),
        compiler_params=_cparams("arbitrary"),
        name=f"moe_combine_{rows}",
    )(dest[0], dest[1], x1, rt, gf, ys)


def _rope_tables(pos, xp):
    half = HEAD_DIM // 2
    inv = 1.0 / (xp.float32(ROPE_THETA) ** (xp.arange(half, dtype=xp.float32) * xp.float32(2.0 / HEAD_DIM)))
    ang = pos.astype(xp.float32)[:, None] * inv[None, :].astype(xp.float32)
    cos, sin = xp.cos(ang), xp.sin(ang)
    return xp.concatenate([cos, cos, cos, cos], axis=1), xp.concatenate([-sin, sin, -sin, sin], axis=1)


def _slot_dest(ids, pos, pstarts):
    ids = ids.astype(jnp.int32)
    experts = jnp.arange(N_EXPERTS, dtype=jnp.int32).reshape((N_EXPERTS,) + (1,) * ids.ndim)
    start = jnp.sum(jnp.where(ids[None] == experts, pstarts.reshape(experts.shape), 0), axis=0)
    return start + pos.astype(jnp.int32)


def kernel(x_prompt, x_sample, cache_kv_g0, cache_kv_g1, cache_kv_g2, norm1_g, w_in, b_in, a_ln_g, a_ln_b, w_spatial, b_spatial, w_a_proj, w_b_proj, w_o, norm2_g, w_group_router, b_group_router, w_expert_router, b_expert_router, w_gate, w_up, w_down, final_norm_g):
    x2 = x_prompt.reshape(T_PROMPT, D_MODEL)
    xs = x_sample.reshape(DEC_BATCH, D_MODEL)
    g1 = norm1_g[0][None, :]
    g2 = norm2_g[0][None, :]
    gf = final_norm_g[None, :]
    b_in2 = b_in[0][None, :]
    lng, lnb = a_ln_g[0][None, :], a_ln_b[0][None, :]

    causal = jnp.tril(jnp.ones((CHUNK, CHUNK), dtype=bool))
    ws_tril = jnp.where(causal[None], w_spatial[0], 0.0)
    wsp = jnp.concatenate([ws_tril[0::2], ws_tril[1::2]], axis=2).astype(BF16)
    bsp = jnp.repeat(b_spatial[0].T, A_GROUP_DIM, axis=1)
    ws0 = jnp.repeat(ws_tril[:, 0, 0].astype(BF16).astype(F32), A_GROUP_DIM)[None, :]
    bs0 = jnp.repeat(b_spatial[0][:, 0], A_GROUP_DIM)[None, :]
    w_router = jnp.zeros((D_MODEL, LANES), F32)
    w_router = w_router.at[:, :N_EXPERT_GROUPS].set(w_group_router[0])
    w_router = w_router.at[:, N_EXPERT_GROUPS:N_EXPERT_GROUPS + N_EXPERTS].set(w_expert_router[0])
    b_router = jnp.zeros((1, LANES), F32)
    b_router = b_router.at[0, :N_EXPERT_GROUPS].set(b_group_router[0])
    b_router = b_router.at[0, N_EXPERT_GROUPS:N_EXPERT_GROUPS + N_EXPERTS].set(b_expert_router[0])
    w_router = w_router.astype(BF16)
    ltri = jnp.tril(jnp.ones((TM, TM), BF16), -1)
    w_in_b, w_a_b, w_b_b, w_o_b = (w[0].astype(BF16) for w in (w_in, w_a_proj, w_b_proj, w_o))

    cos_p, sin_p = _rope_tables(np.arange(SEQ, dtype=np.int32), np)
    cos_s, sin_s = _rope_tables(jnp.full((1,), PAST_LEN, jnp.int32), jnp)

    uv, qkv0, qkv1, qkv2, gates, kvp0, kvp1, kvp2 = _inproj(
        x2, g1, w_in_b, b_in2, lng, lnb, cos_p, sin_p)
    ob = _attention((qkv0.reshape(BATCH, 1, SEQ, 3 * GROUP_W), qkv1, qkv2)).reshape(T_PROMPT, GROUP_W)
    x1_p, h2p_p, rt_p, rtt_p, counts_p = _merge(
        x2, uv, gates, ob, wsp, bsp, w_a_b, w_b_b, w_o_b, g2, w_router, b_router, ltri)

    z_s = _sample_inproj(xs, g1, w_in_b, b_in2)
    vrow, sa_s, qr_s, kr_s, kvs0, kvs1, kvs2 = _sample_pre(z_s, lng, lnb, ws0, bs0, cos_s, sin_s)
    dim_major = lambda t: t.reshape(DEC_BATCH, 3, HEADS, HEAD_DIM).transpose(0, 1, 3, 2)
    caches = [c.transpose(0, 1, 3, 4, 5, 2).reshape(DEC_BATCH, 2, HEADS, HEAD_DIM, c.shape[2])
              for c in (cache_kv_g0, cache_kv_g1, cache_kv_g2)]
    comb_s = _sample_attn(dim_major(qr_s), dim_major(kr_s), dim_major(z_s[:, COL_VB:COL_VB + QK_W]), caches)
    comb_s = comb_s.transpose(0, 2, 1).reshape(DEC_BATCH, GROUP_W)
    x1_s, h2p_s, rt_s, counts = _sample_merge(
        xs, z_s, sa_s, comb_s, w_a_b, w_b_b, w_o_b, g2, w_router, b_router,
        ltri[:DEC_BATCH, :DEC_BATCH], counts_p)

    cnt = counts[0, :N_EXPERTS].astype(jnp.int32)
    padded = (cnt + MOE_TB - 1) // MOE_TB * MOE_TB
    pends = jnp.cumsum(padded)
    pstarts = pends - padded
    block_starts = jnp.arange(MOE_NB, dtype=jnp.int32) * MOE_TB
    block_e = jnp.minimum(jnp.sum((pends[None, :] <= block_starts[:, None]).astype(jnp.int32), axis=1),
                          N_EXPERTS - 1)
    nused = (pends[-1:] // MOE_TB).astype(jnp.int32)
    rtt_p = rtt_p.reshape(T_PROMPT // TM, ROUTE_ROWS, TM)
    dest_p = [_slot_dest(rtt_p[:, k], rtt_p[:, 2 + k], pstarts).reshape(T_PROMPT) for k in range(2)]
    dest_s = [_slot_dest(rt_s[:, k], rt_s[:, 2 + k], pstarts) for k in range(2)]

    rows = jnp.zeros((MOE_ROWS, HALF_D), U32)
    rows = _dispatch(dest_p, h2p_p, rows, TM)
    rows = _dispatch(dest_s, h2p_s, rows, DEC_BATCH)
    ys = _expert_ffn(block_e, nused, rows, w_gate[0], w_up[0], w_down[0])
    y_p = _combine(dest_p, x1_p, rt_p, gf, ys, TM)
    y_s = _combine(dest_s, x1_s, rt_s, gf, ys, DEC_BATCH)

    kv_shape = lambda n, w: (1, n, w, 2, HEADS, HEAD_DIM)
    window_rows = lambda t: t.reshape(1, BATCH, 2, HEADS, HEAD_DIM, t.shape[2]).transpose(0, 1, 5, 2, 3, 4)
    return (y_p.reshape(BATCH, SEQ, D_MODEL), y_s.reshape(DEC_BATCH, 1, D_MODEL),
            window_rows(kvp0), window_rows(kvp1), window_rows(kvp2),
            kvs0.reshape(kv_shape(DEC_BATCH, 1)), kvs1.reshape(kv_shape(DEC_BATCH, 1)),
            kvs2.reshape(kv_shape(DEC_BATCH, 1)), vrow.reshape(1, DEC_BATCH, 1, A_WIDTH))
```

```python
import functools

import jax
import jax.numpy as jnp
import numpy as np
from jax import lax
from jax.experimental import pallas as pl
from jax.experimental.pallas import tpu as pltpu

F32 = jnp.float32
BF16 = jnp.bfloat16
U32 = jnp.uint32

D_MODEL = 1024
BATCH = 2
SEQ = 8192
DEC_BATCH = 32
PAST_LEN = 8192
CHUNK = 128
A_GROUPS = 8
A_GROUP_DIM = 64
A_WIDTH = 512
HEAD_DIM = 64
HEADS = 4
GROUP_W = HEADS * HEAD_DIM
DILATIONS = (1, 4, 16)
SPAN = 128
QK_W = 768
IN_WIDTH = 5376
COL_U, COL_V, COL_Q, COL_K, COL_VB, COL_GA, COL_GB = 0, 512, 1024, 1792, 2560, 3328, 4352
N_EXPERT_GROUPS = 4
EXPERTS_PER_GROUP = 8
N_EXPERTS = 32
D_EXPERT = 512
ROPE_THETA = 10000.0
EPS = 1e-6

LANES = 128
T_PROMPT = BATCH * SEQ
TM = 512
TM_IN = 1024
WINDOWS = tuple(min(SPAN * d, SEQ) for d in DILATIONS)
TA = SPAN * max(DILATIONS)
ATT_UNROLL = 2
MOE_TB = 512
ROW_DMA_UNROLL = 8
ROUTE_ROWS = 8
N_SLOTS = 2 * (T_PROMPT + DEC_BATCH)
MOE_NB = -(-N_SLOTS // MOE_TB) + N_EXPERTS
MOE_ROWS = MOE_NB * MOE_TB
HALF_D = D_MODEL // 2
NEG = -1e30
VMEM_LIMIT = 56 * 1024 * 1024


def _cparams(*sem):
    return pltpu.CompilerParams(dimension_semantics=sem, vmem_limit_bytes=VMEM_LIMIT)


def _rms(x, g):
    return x * lax.rsqrt(jnp.mean(x * x, axis=-1, keepdims=True) + EPS) * g


def _layer_norm(x, g, b):
    mu = jnp.mean(x, axis=-1, keepdims=True)
    xc = x - mu
    var = jnp.mean(xc * xc, axis=-1, keepdims=True)
    return xc * lax.rsqrt(var + EPS) * g + b


def _rope_chunk(x, cos, sin_signed):
    lane = lax.broadcasted_iota(jnp.int32, x.shape, 1)
    first_half = (lane % HEAD_DIM) < (HEAD_DIM // 2)
    swapped = jnp.where(first_half, pltpu.roll(x, LANES - HEAD_DIM // 2, 1), pltpu.roll(x, HEAD_DIM // 2, 1))
    return x * cos + swapped * sin_signed


def _pack_bf16_pair(h):
    lo = lax.bitcast_convert_type(h[:, :HALF_D].astype(BF16).astype(F32), U32)
    hi = lax.bitcast_convert_type(h[:, HALF_D:].astype(BF16).astype(F32), U32)
    return (hi & jnp.uint32(0xFFFF0000)) | (lo >> 16)


def _unpack_bf16_pair(p):
    lo = lax.bitcast_convert_type(p << 16, F32).astype(BF16)
    hi = lax.bitcast_convert_type(p & jnp.uint32(0xFFFF0000), F32).astype(BF16)
    return lo, hi


def _route(logits, prior_counts, ltri):
    rows = logits.shape[0]
    lane = lax.broadcasted_iota(jnp.int32, (rows, LANES), 1).astype(F32)
    is_g = lane < N_EXPERT_GROUPS
    gl = jnp.where(is_g, logits, NEG)
    gmax = jnp.max(gl, axis=1, keepdims=True)
    grp = jnp.min(jnp.where(gl == gmax, lane, float(LANES)), axis=1, keepdims=True)
    p_grp = 1.0 / jnp.sum(jnp.where(is_g, jnp.exp(gl - gmax), 0.0), axis=1, keepdims=True)
    lo = N_EXPERT_GROUPS + grp * EXPERTS_PER_GROUP
    el = jnp.where((lane >= lo) & (lane < lo + EXPERTS_PER_GROUP), logits, NEG)
    v1 = jnp.max(el, axis=1, keepdims=True)
    i1 = jnp.min(jnp.where(el == v1, lane, float(LANES)), axis=1, keepdims=True)
    el2 = jnp.where(lane == i1, NEG, el)
    v2 = jnp.max(el2, axis=1, keepdims=True)
    i2 = jnp.min(jnp.where(el2 == v2, lane, float(LANES)), axis=1, keepdims=True)
    t = jnp.exp(v2 - v1)
    gate1 = p_grp / (1.0 + t)
    gate2 = p_grp * t / (1.0 + t)
    e1 = i1 - N_EXPERT_GROUPS
    e2 = i2 - N_EXPERT_GROUPS
    hit1 = lane == e1
    hit2 = lane == e2
    onehot = jnp.where(hit1 | hit2, 1.0, 0.0)
    rank = jnp.dot(ltri, onehot.astype(BF16), preferred_element_type=F32) + prior_counts
    pos1 = jnp.sum(jnp.where(hit1, rank, 0.0), axis=1, keepdims=True)
    pos2 = jnp.sum(jnp.where(hit2, rank, 0.0), axis=1, keepdims=True)
    slab = jnp.where(lane == 0, e1, jnp.where(lane == 1, e2, jnp.where(lane == 2, pos1, jnp.where(
        lane == 3, pos2, jnp.where(lane == 4, gate1, jnp.where(lane == 5, gate2, 0.0))))))
    return slab, prior_counts + jnp.sum(onehot, axis=0, keepdims=True)


def _inproj_kernel(x_ref, g1_ref, w_ref, b_ref, lng_ref, lnb_ref, cos_ref, sin_ref,
                   uv_ref, qkv0_ref, qkv1_ref, qkv2_ref, gates_ref, kv0_ref, kv1_ref, kv2_ref, de_ref):
    tm = x_ref.shape[0]
    tiles_per_seq = SEQ // tm
    tile_in_seq = pl.program_id(0) % tiles_per_seq
    hb = _rms(x_ref[...], g1_ref[...]).astype(BF16)

    def seg(lo, width):
        return jnp.dot(hb, w_ref[:, lo:lo + width], preferred_element_type=F32) + b_ref[:, lo:lo + width]

    uv_ref[:, :A_WIDTH] = jax.nn.gelu(seg(COL_U, A_WIDTH)).astype(BF16)
    uv_ref[:, A_WIDTH:] = _layer_norm(jax.nn.gelu(seg(COL_V, A_WIDTH)), lng_ref[...], lnb_ref[...]).astype(BF16)
    gates_ref[:, :D_MODEL] = jax.nn.sigmoid(seg(COL_GA, D_MODEL)).astype(BF16)
    gates_ref[:, D_MODEL:] = jax.nn.sigmoid(seg(COL_GB, D_MODEL)).astype(BF16)

    cos = cos_ref[...]
    sin = sin_ref[...]
    qkv_refs = (qkv0_ref, qkv1_ref, qkv2_ref)
    kv_refs = (kv0_ref, kv1_ref, kv2_ref)
    for g, d in enumerate(DILATIONS):
        q = seg(COL_Q + g * GROUP_W, GROUP_W)
        k = seg(COL_K + g * GROUP_W, GROUP_W)
        v = seg(COL_VB + g * GROUP_W, GROUP_W)
        chunks = GROUP_W // LANES
        for c in range(chunks):
            sl = slice(c * LANES, (c + 1) * LANES)
            de_ref[c] = _rope_chunk(q[:, sl], cos, sin) * (HEAD_DIM ** -0.5)
            de_ref[chunks + c] = _rope_chunk(k[:, sl], cos, sin)
            de_ref[2 * chunks + c] = v[:, sl]
        for c in range(3 * chunks):
            sl = slice(c * LANES, (c + 1) * LANES)
            if d == 1:
                qkv_refs[g][:, sl] = de_ref[c].astype(BF16)
            else:
                for r in range(d):
                    qkv_refs[g][0, r, :, sl] = de_ref[c, pl.ds(r, tm // d, stride=d), :].astype(BF16)
        kv_rows = kv_refs[g].shape[2]
        first = tiles_per_seq - WINDOWS[g] // kv_rows

        @pl.when(tile_in_seq >= first)
        def _(kv_ref=kv_refs[g], kv_rows=kv_rows, chunks=chunks):
            for c in range(2 * chunks):
                kv_ref[0, c * LANES:(c + 1) * LANES, :] = de_ref[chunks + c, tm - kv_rows:, :].T


def _inproj(x2, g1, w_bf, b_in, lng, lnb, cos_t, sin_t):
    tm = TM_IN
    tiles_per_seq = SEQ // tm
    n_tiles = T_PROMPT // tm
    const = lambda i: (0, 0)

    def kv_spec(w):
        rows = min(w, tm)
        first = tiles_per_seq - w // rows
        return pl.BlockSpec((1, 2 * GROUP_W, rows),
                            lambda i: (i // tiles_per_seq, 0, jnp.maximum(i % tiles_per_seq - first, 0)))

    def regrouped_spec(d):
        return pl.BlockSpec((1, d, tm // d, 3 * GROUP_W), lambda i: (i // tiles_per_seq, 0, i % tiles_per_seq, 0))

    return pl.pallas_call(
        _inproj_kernel,
        grid=(n_tiles,),
        in_specs=[
            pl.BlockSpec((tm, D_MODEL), lambda i: (i, 0)),
            pl.BlockSpec((1, D_MODEL), const),
            pl.BlockSpec((D_MODEL, IN_WIDTH), const, pipeline_mode=pl.Buffered(1)),
            pl.BlockSpec((1, IN_WIDTH), const),
            pl.BlockSpec((1, A_WIDTH), const),
            pl.BlockSpec((1, A_WIDTH), const),
            pl.BlockSpec((tm, LANES), lambda i: (i % tiles_per_seq, 0)),
            pl.BlockSpec((tm, LANES), lambda i: (i % tiles_per_seq, 0)),
        ],
        out_specs=[
            pl.BlockSpec((tm, 2 * A_WIDTH), lambda i: (i, 0)),
            pl.BlockSpec((tm, 3 * GROUP_W), lambda i: (i, 0)),
            regrouped_spec(DILATIONS[1]),
            regrouped_spec(DILATIONS[2]),
            pl.BlockSpec((tm, 2 * D_MODEL), lambda i: (i, 0)),
            kv_spec(WINDOWS[0]), kv_spec(WINDOWS[1]), kv_spec(WINDOWS[2]),
        ],
        out_shape=[
            jax.ShapeDtypeStruct((T_PROMPT, 2 * A_WIDTH), BF16),
            jax.ShapeDtypeStruct((T_PROMPT, 3 * GROUP_W), BF16),
            jax.ShapeDtypeStruct((BATCH, DILATIONS[1], SEQ // DILATIONS[1], 3 * GROUP_W), BF16),
            jax.ShapeDtypeStruct((BATCH, DILATIONS[2], SEQ // DILATIONS[2], 3 * GROUP_W), BF16),
            jax.ShapeDtypeStruct((T_PROMPT, 2 * D_MODEL), BF16),
            jax.ShapeDtypeStruct((BATCH, 2 * GROUP_W, WINDOWS[0]), F32),
            jax.ShapeDtypeStruct((BATCH, 2 * GROUP_W, WINDOWS[1]), F32),
            jax.ShapeDtypeStruct((BATCH, 2 * GROUP_W, WINDOWS[2]), F32),
        ],
        scratch_shapes=[pltpu.VMEM((3 * GROUP_W // LANES, tm, LANES), F32)],
        compiler_params=_cparams("arbitrary"),
        name="inproj",
    )(x2, g1, w_bf, b_in, lng, lnb, cos_t, sin_t)


def _attn_kernel(c0_ref, p0_ref, c1_ref, p1_ref, c2_ref, p2_ref, o_ref, acc_ref, m_ref, l_ref):
    n = pl.program_id(1)
    qi = lax.broadcasted_iota(jnp.int32, (HEADS * SPAN, 2 * SPAN), 0) % SPAN
    ki = lax.broadcasted_iota(jnp.int32, (HEADS * SPAN, 2 * SPAN), 1)
    band = (ki >= qi) & (ki <= qi + SPAN)
    band_first = band & ((ki >= SPAN) | (n > 0))
    q_head = lax.broadcasted_iota(jnp.int32, (SPAN, GROUP_W), 1) // HEAD_DIM
    kv_head = lax.broadcasted_iota(jnp.int32, (2 * SPAN, GROUP_W), 1) // HEAD_DIM
    contract_last = (((1,), (1,)), ((), ()))
    k_cols = slice(GROUP_W, 2 * GROUP_W)
    v_cols = slice(2 * GROUP_W, 3 * GROUP_W)

    def attend(q, keys, vals, mask, tok_rows, fresh):
        q_heads = jnp.concatenate([jnp.where(q_head == h, q, jnp.zeros_like(q)) for h in range(HEADS)], axis=0)
        s = jnp.where(mask, lax.dot_general(q_heads, keys, contract_last, preferred_element_type=F32), NEG)
        m = jnp.max(s, axis=1, keepdims=True)
        p = jnp.exp(s - m)
        l = jnp.sum(p, axis=1, keepdims=True)
        pb = p.astype(BF16)
        p_heads = jnp.concatenate([pb[h * SPAN:(h + 1) * SPAN] for h in range(HEADS)], axis=1)
        v_heads = jnp.concatenate([jnp.where(kv_head == h, vals, jnp.zeros_like(vals)) for h in range(HEADS)], axis=0)
        acc = jnp.dot(p_heads, v_heads, preferred_element_type=F32)
        m_b = jnp.zeros((SPAN, GROUP_W), F32)
        l_b = jnp.zeros((SPAN, GROUP_W), F32)
        for h in range(HEADS):
            rows = slice(h * SPAN, (h + 1) * SPAN)
            m_b = jnp.where(q_head == h, m[rows], m_b)
            l_b = jnp.where(q_head == h, l[rows], l_b)
        for c in range(GROUP_W // LANES):
            sl = slice(c * LANES, (c + 1) * LANES)
            if fresh:
                m_ref[c, tok_rows, :] = m_b[:, sl]
                l_ref[c, tok_rows, :] = l_b[:, sl]
                acc_ref[c, tok_rows, :] = acc[:, sl]
                continue
            m_old = m_ref[c, tok_rows, :]
            m_new = jnp.maximum(m_old, m_b[:, sl])
            a_old = jnp.exp(m_old - m_new)
            a_blk = jnp.exp(m_b[:, sl] - m_new)
            l_ref[c, tok_rows, :] = a_old * l_ref[c, tok_rows, :] + a_blk * l_b[:, sl]
            acc_ref[c, tok_rows, :] = a_old * acc_ref[c, tok_rows, :] + a_blk * acc[:, sl]
            m_ref[c, tok_rows, :] = m_new

    def stream(c_ref, p_ref, r, d, fresh):
        n_blocks = c_ref.shape[2] // SPAN

        def tok_rows(j):
            start = j * (SPAN * d) + r
            return pl.ds(start, SPAN) if d == 1 else pl.ds(start, SPAN, stride=d)

        keys = jnp.concatenate([p_ref[0, r, :, k_cols], c_ref[0, r, :SPAN, k_cols]], axis=0)
        vals = jnp.concatenate([p_ref[0, r, :, v_cols], c_ref[0, r, :SPAN, v_cols]], axis=0)
        attend(c_ref[0, r, :SPAN, :GROUP_W], keys, vals, band_first, tok_rows(0), fresh)

        def later_block(j, carry):
            q_rows = pl.ds(pl.multiple_of(j * SPAN, SPAN), SPAN)
            kv_rows = pl.ds(pl.multiple_of((j - 1) * SPAN, SPAN), 2 * SPAN)
            attend(c_ref[0, r, q_rows, :GROUP_W], c_ref[0, r, kv_rows, k_cols], c_ref[0, r, kv_rows, v_cols],
                   band, tok_rows(j), fresh)
            return carry

        if n_blocks > 1:
            lax.fori_loop(1, n_blocks, later_block, 0, unroll=3)

    groups = sorted(zip(DILATIONS, ((c0_ref, p0_ref), (c1_ref, p1_ref), (c2_ref, p2_ref))), key=lambda g: -g[0])
    for idx, (d, (c_ref, p_ref)) in enumerate(groups):
        fresh = idx == 0
        if d == 1:
            stream(c_ref, p_ref, 0, d, fresh)
        else:
            lax.fori_loop(0, d, lambda r, carry, c_ref=c_ref, p_ref=p_ref, d=d, fresh=fresh:
                          (stream(c_ref, p_ref, r, d, fresh), carry)[1], 0,
                          unroll=ATT_UNROLL if c_ref.shape[2] == SPAN else 1)

    for c in range(GROUP_W // LANES):
        o_ref[0, :, c * LANES:(c + 1) * LANES] = (acc_ref[c] / l_ref[c]).astype(BF16)


def _attention(qkv_by_group):
    in_specs, args = [], []
    for qkv, d in zip(qkv_by_group, DILATIONS):
        rows = TA // d
        blocks_per_tile = rows // SPAN
        in_specs.append(pl.BlockSpec((1, d, rows, 3 * GROUP_W), lambda b, n: (b, 0, n, 0)))
        in_specs.append(pl.BlockSpec((1, d, SPAN, 3 * GROUP_W),
                                     lambda b, n, k=blocks_per_tile: (b, 0, jnp.maximum(n * k - 1, 0), 0)))
        args += [qkv, qkv]
    return pl.pallas_call(
        _attn_kernel,
        grid=(BATCH, SEQ // TA),
        in_specs=in_specs,
        out_specs=pl.BlockSpec((1, TA, GROUP_W), lambda b, n: (b, n, 0)),
        out_shape=jax.ShapeDtypeStruct((BATCH, SEQ, GROUP_W), BF16),
        scratch_shapes=[pltpu.VMEM((GROUP_W // LANES, TA, LANES), F32)] * 3,
        compiler_params=_cparams("arbitrary", "arbitrary"),
        name="attn",
    )(*args)


def _merge_kernel(x_ref, uv_ref, gates_ref, ob_ref,
                  wsp_ref, bsp_ref, wa_ref, wb_ref, wo_ref, g2_ref, wr_ref, br_ref, ltri_ref,
                  x1_ref, h2p_ref, rt_ref, rtt_ref, cnt_ref, run_ref):
    @pl.when(pl.program_id(0) == 0)
    def _():
        run_ref[...] = jnp.zeros_like(run_ref)

    lane = lax.broadcasted_iota(jnp.int32, (CHUNK, LANES), 1)
    left = lane < A_GROUP_DIM
    zero = jnp.zeros((CHUNK, LANES), BF16)
    sa_chunks = []
    for c in range(TM // CHUNK):
        rows = slice(c * CHUNK, (c + 1) * CHUNK)
        pairs = []
        for p in range(A_GROUPS // 2):
            vp = uv_ref[rows, A_WIDTH + p * LANES:A_WIDTH + (p + 1) * LANES]
            rhs = jnp.concatenate([jnp.where(left, vp, zero), jnp.where(left, zero, vp)], axis=0)
            pairs.append(jnp.dot(wsp_ref[p], rhs, preferred_element_type=F32))
        mixed = jnp.concatenate(pairs, axis=1) + bsp_ref[...]
        sa_chunks.append((uv_ref[rows, :A_WIDTH].astype(F32) * mixed).astype(BF16))
    s_a = jnp.concatenate(sa_chunks, axis=0)

    a = jnp.dot(s_a, wa_ref[...], preferred_element_type=F32)
    b = jnp.dot(ob_ref[...], wb_ref[...], preferred_element_type=F32)
    merged = gates_ref[:, :D_MODEL].astype(F32) * a + gates_ref[:, D_MODEL:].astype(F32) * b
    x1 = x_ref[...] + jnp.dot(merged.astype(BF16), wo_ref[...], preferred_element_type=F32)
    x1_ref[...] = x1

    h2 = _rms(x1, g2_ref[...])
    h2p_ref[...] = _pack_bf16_pair(h2)
    logits = jnp.dot(h2.astype(BF16), wr_ref[...], preferred_element_type=F32) + br_ref[...]
    slab, counts = _route(logits, run_ref[...], ltri_ref[...])
    rt_ref[...] = slab
    rtt_ref[...] = slab.T[:ROUTE_ROWS, :]
    run_ref[...] = counts
    cnt_ref[...] = counts


def _merge(x2, uv, gates, ob, wsp, bsp, wa, wb, wo, g2, wr, br, ltri):
    n_tiles = T_PROMPT // TM
    tile = lambda w: pl.BlockSpec((TM, w), lambda i: (i, 0))
    full = lambda a: pl.BlockSpec(a.shape, lambda i: (0,) * a.ndim)
    return pl.pallas_call(
        _merge_kernel,
        grid=(n_tiles,),
        in_specs=[tile(D_MODEL), tile(2 * A_WIDTH), tile(2 * D_MODEL), tile(GROUP_W),
                  full(wsp), full(bsp), full(wa), full(wb), full(wo), full(g2), full(wr), full(br),
                  full(ltri)],
        out_specs=[tile(D_MODEL), tile(HALF_D), tile(LANES), pl.BlockSpec((ROUTE_ROWS, TM), lambda i: (i, 0)),
                   pl.BlockSpec((1, LANES), lambda i: (0, 0))],
        out_shape=[
            jax.ShapeDtypeStruct((T_PROMPT, D_MODEL), F32),
            jax.ShapeDtypeStruct((T_PROMPT, HALF_D), U32),
            jax.ShapeDtypeStruct((T_PROMPT, LANES), F32),
            jax.ShapeDtypeStruct((n_tiles * ROUTE_ROWS, TM), F32),
            jax.ShapeDtypeStruct((1, LANES), F32),
        ],
        scratch_shapes=[pltpu.VMEM((1, LANES), F32)],
        compiler_params=_cparams("arbitrary"),
        name="merge_route",
    )(x2, uv, gates, ob, wsp, bsp, wa, wb, wo, g2, wr, br, ltri)


SAMPLE_COLS = 768


def _sample_inproj_kernel(x_ref, g1_ref, w_ref, b_ref, z_ref):
    hb = _rms(x_ref[...], g1_ref[...]).astype(BF16)
    z_ref[...] = jnp.dot(hb, w_ref[...], preferred_element_type=F32) + b_ref[...]


def _sample_inproj(xs, g1, w_in, b_in):
    const = lambda j: (0, 0)
    return pl.pallas_call(
        _sample_inproj_kernel,
        grid=(IN_WIDTH // SAMPLE_COLS,),
        in_specs=[pl.BlockSpec((DEC_BATCH, D_MODEL), const), pl.BlockSpec((1, D_MODEL), const),
                  pl.BlockSpec((D_MODEL, SAMPLE_COLS), lambda j: (0, j)),
                  pl.BlockSpec((1, SAMPLE_COLS), lambda j: (0, j))],
        out_specs=pl.BlockSpec((DEC_BATCH, SAMPLE_COLS), lambda j: (0, j)),
        out_shape=jax.ShapeDtypeStruct((DEC_BATCH, IN_WIDTH), F32),
        compiler_params=_cparams("arbitrary"),
        name="sample_inproj",
    )(xs, g1, w_in, b_in)


def _sample_pre_kernel(z_ref, lng_ref, lnb_ref, ws0_ref, bs0_ref, cos_ref, sin_ref,
                       vrow_ref, sa_ref, qr_ref, kr_ref, kv0_ref, kv1_ref, kv2_ref):
    u = jax.nn.gelu(z_ref[:, COL_U:COL_U + A_WIDTH])
    va = _layer_norm(jax.nn.gelu(z_ref[:, COL_V:COL_V + A_WIDTH]), lng_ref[...], lnb_ref[...])
    vrow_ref[...] = va
    sa_ref[...] = u * (ws0_ref[...] * va.astype(BF16).astype(F32) + bs0_ref[...])
    cos, sin = cos_ref[...], sin_ref[...]
    for g, kv_ref in enumerate((kv0_ref, kv1_ref, kv2_ref)):
        for c in range(GROUP_W // LANES):
            off = g * GROUP_W + c * LANES
            qr_ref[:, off:off + LANES] = (_rope_chunk(z_ref[:, COL_Q + off:COL_Q + off + LANES], cos, sin)
                                          * (HEAD_DIM ** -0.5))
            kr = _rope_chunk(z_ref[:, COL_K + off:COL_K + off + LANES], cos, sin)
            kr_ref[:, off:off + LANES] = kr
            kv_ref[:, c * LANES:(c + 1) * LANES] = kr
        kv_ref[:, GROUP_W:] = z_ref[:, COL_VB + g * GROUP_W:COL_VB + (g + 1) * GROUP_W]


def _sample_pre(z, lng, lnb, ws0, bs0, cos_s, sin_s):
    row = lambda w: jax.ShapeDtypeStruct((DEC_BATCH, w), F32)
    return pl.pallas_call(
        _sample_pre_kernel,
        out_shape=[row(A_WIDTH), row(A_WIDTH), row(QK_W), row(QK_W),
                   row(2 * GROUP_W), row(2 * GROUP_W), row(2 * GROUP_W)],
        compiler_params=pltpu.CompilerParams(vmem_limit_bytes=VMEM_LIMIT),
        name="sample_pre",
    )(z, lng, lnb, ws0, bs0, cos_s, sin_s)


def _sample_attn_kernel(q_ref, k_ref, v_ref, c0_ref, c1_ref, c2_ref, comb_ref):
    as_operand = lambda t: t.astype(BF16).astype(F32)
    for h in range(HEADS):
        head = slice(h, h + 1)
        outs, lses = [], []
        for g, (c_ref, d) in enumerate(zip((c0_ref, c1_ref, c2_ref), DILATIONS)):
            qc, kn, vn = (as_operand(r[0, g, :, head]) for r in (q_ref, k_ref, v_ref))
            s = jnp.sum(as_operand(c_ref[0, 0, h]) * qc, axis=0, keepdims=True)
            if d > 1:
                pos = lax.broadcasted_iota(jnp.int32, s.shape, 1)
                s = jnp.where(pos % d == 0, s, NEG)
            s_n = jnp.sum(qc * kn, axis=0, keepdims=True)
            m = jnp.maximum(jnp.max(s, axis=1, keepdims=True), s_n)
            lse = m + jnp.log(jnp.sum(jnp.exp(s - m), axis=1, keepdims=True) + jnp.exp(s_n - m))
            p = as_operand(jnp.exp(s - lse))
            p_n = as_operand(jnp.exp(s_n - lse))
            outs.append(jnp.sum(as_operand(c_ref[0, 1, h]) * p, axis=1, keepdims=True) + p_n * vn)
            lses.append(lse)
        mx = jnp.maximum(jnp.maximum(lses[0], lses[1]), lses[2])
        ws = [jnp.exp(t - mx) for t in lses]
        tot = ws[0] + ws[1] + ws[2]
        comb_ref[0, :, head] = (ws[0] * outs[0] + ws[1] * outs[1] + ws[2] * outs[2]) / tot


def _sample_attn(qt, kt, vt, caches):
    new_spec = pl.BlockSpec((1, 3, HEAD_DIM, HEADS), lambda i: (i, 0, 0, 0))
    cache_spec = lambda c: pl.BlockSpec((1,) + c.shape[1:], lambda i: (i, 0, 0, 0, 0))
    return pl.pallas_call(
        _sample_attn_kernel,
        grid=(DEC_BATCH,),
        in_specs=[new_spec, new_spec, new_spec, cache_spec(caches[0]), cache_spec(caches[1]), cache_spec(caches[2])],
        out_specs=pl.BlockSpec((1, HEAD_DIM, HEADS), lambda i: (i, 0, 0)),
        out_shape=jax.ShapeDtypeStruct((DEC_BATCH, HEAD_DIM, HEADS), F32),
        compiler_params=_cparams("arbitrary"),
        name="sample_attn",
    )(qt, kt, vt, caches[0], caches[1], caches[2])


def _sample_merge_kernel(x_ref, z_ref, sa_ref, comb_ref, wa_ref, wb_ref, wo_ref, g2_ref, wr_ref, br_ref,
                         ltri_ref, cnt_in_ref, x1_ref, h2p_ref, rt_ref, cnt_ref):
    dot = lambda p, q: jnp.dot(p.astype(BF16), q, preferred_element_type=F32)
    a = dot(sa_ref[...], wa_ref[...])
    b = dot(comb_ref[...], wb_ref[...])
    merged = (jax.nn.sigmoid(z_ref[:, COL_GA:COL_GA + D_MODEL]) * a
              + jax.nn.sigmoid(z_ref[:, COL_GB:COL_GB + D_MODEL]) * b)
    x1 = x_ref[...] + dot(merged, wo_ref[...])
    x1_ref[...] = x1
    h2 = _rms(x1, g2_ref[...])
    h2p_ref[...] = _pack_bf16_pair(h2)
    logits = dot(h2, wr_ref[...]) + br_ref[...]
    slab, counts = _route(logits, cnt_in_ref[...], ltri_ref[...])
    rt_ref[...] = slab
    cnt_ref[...] = counts


def _sample_merge(xs, z, s_a, comb, wa, wb, wo, g2, wr, br, ltri, counts):
    return pl.pallas_call(
        _sample_merge_kernel,
        out_shape=[jax.ShapeDtypeStruct((DEC_BATCH, D_MODEL), F32),
                   jax.ShapeDtypeStruct((DEC_BATCH, HALF_D), U32),
                   jax.ShapeDtypeStruct((DEC_BATCH, LANES), F32),
                   jax.ShapeDtypeStruct((1, LANES), F32)],
        compiler_params=pltpu.CompilerParams(vmem_limit_bytes=VMEM_LIMIT),
        name="sample_merge_route",
    )(xs, z, s_a, comb, wa, wb, wo, g2, wr, br, ltri, counts)


def _dispatch_kernel(dest0_ref, dest1_ref, h_ref, xs_in_ref, xs_ref, sem):
    del xs_in_ref
    rows = h_ref.shape[0]
    base = pl.program_id(0) * rows

    def row_copy(t, dest_ref):
        return pltpu.make_async_copy(h_ref.at[pl.ds(t, 1)], xs_ref.at[pl.ds(dest_ref[base + t], 1)], sem)

    def start(t, c):
        row_copy(t, dest0_ref).start()
        row_copy(t, dest1_ref).start()
        return c

    lax.fori_loop(0, rows, start, 0, unroll=ROW_DMA_UNROLL)
    for _ in range(2):
        pltpu.make_async_copy(h_ref, xs_ref.at[pl.ds(0, rows)], sem).wait()


def _dispatch(dest, h2p, xs, rows):
    n = h2p.shape[0]
    return pl.pallas_call(
        _dispatch_kernel,
        grid_spec=pltpu.PrefetchScalarGridSpec(
            num_scalar_prefetch=2,
            grid=(n // rows,),
            in_specs=[pl.BlockSpec((rows, HALF_D), lambda i, *_: (i, 0)), pl.BlockSpec(memory_space=pl.ANY)],
            out_specs=pl.BlockSpec(memory_space=pl.ANY),
            scratch_shapes=[pltpu.SemaphoreType.DMA(())],
        ),
        out_shape=jax.ShapeDtypeStruct(xs.shape, xs.dtype),
        input_output_aliases={3: 0},
        compiler_params=_cparams("arbitrary"),
        name=f"moe_dispatch_{rows}",
    )(dest[0], dest[1], h2p, xs)


def _ffn_kernel(be_ref, nused_ref, nexte_ref, wslot_ref, xs_ref, wg_ref, wu_ref, wd_ref, ys_ref,
                wg_b, wu_b, wd_b, wg_f, wu_f, wd_f, wsem):
    i = pl.program_id(0)
    live = i < nused_ref[0]
    new_expert = jnp.logical_and(live, jnp.logical_or(i == 0, be_ref[i] != be_ref[jnp.maximum(i - 1, 0)]))

    def weight_copies(e, slot):
        return [pltpu.make_async_copy(src.at[e], dst.at[slot], wsem.at[slot])
                for src, dst in ((wg_ref, wg_f), (wu_ref, wu_f), (wd_ref, wd_f))]

    @pl.when(i == 0)
    def _():
        for cp in weight_copies(be_ref[0], 0):
            cp.start()

    @pl.when(new_expert)
    def _():
        slot = wslot_ref[i]
        for cp in weight_copies(be_ref[i], slot):
            cp.wait()
        wg_b[...] = wg_f[slot].astype(BF16)
        wu_b[...] = wu_f[slot].astype(BF16)
        wd_b[...] = wd_f[slot].astype(BF16)

        @pl.when(nexte_ref[i] >= 0)
        def _():
            for cp in weight_copies(nexte_ref[i], 1 - slot):
                cp.start()

    @pl.when(live)
    def _():
        lo, hi = _unpack_bf16_pair(xs_ref[...])
        gate = (jnp.dot(lo, wg_b[:HALF_D, :], preferred_element_type=F32)
                + jnp.dot(hi, wg_b[HALF_D:, :], preferred_element_type=F32))
        up = (jnp.dot(lo, wu_b[:HALF_D, :], preferred_element_type=F32)
              + jnp.dot(hi, wu_b[HALF_D:, :], preferred_element_type=F32))
        hid = (jax.nn.silu(gate) * up).astype(BF16)
        ys_ref[...] = jnp.dot(hid, wd_b[...], preferred_element_type=F32)

    @pl.when(i >= nused_ref[0])
    def _():
        ys_ref[...] = jnp.zeros_like(ys_ref)


def _expert_ffn(block_e, nused, xs, w_gate, w_up, w_down):
    idx = jnp.arange(MOE_NB, dtype=jnp.int32)
    change = ((block_e != jnp.roll(block_e, 1)) | (idx == 0)) & (idx < nused[0])
    wslot = ((jnp.cumsum(change.astype(jnp.int32)) - 1) & 1).astype(jnp.int32)
    change_at = jnp.where(change, idx, MOE_NB)
    next_change = jnp.flip(lax.cummin(jnp.flip(jnp.concatenate([change_at[1:], jnp.full((1,), MOE_NB, jnp.int32)]))))
    nexte = jnp.where(next_change < MOE_NB, block_e[jnp.minimum(next_change, MOE_NB - 1)], -1).astype(jnp.int32)
    hbm = pl.BlockSpec(memory_space=pl.ANY)
    return pl.pallas_call(
        _ffn_kernel,
        grid_spec=pltpu.PrefetchScalarGridSpec(
            num_scalar_prefetch=4,
            grid=(MOE_NB,),
            in_specs=[pl.BlockSpec((MOE_TB, HALF_D), lambda i, *_: (i, 0)), hbm, hbm, hbm],
            out_specs=pl.BlockSpec((MOE_TB, D_MODEL), lambda i, *_: (i, 0)),
            scratch_shapes=[pltpu.VMEM((D_MODEL, D_EXPERT), BF16), pltpu.VMEM((D_MODEL, D_EXPERT), BF16),
                            pltpu.VMEM((D_EXPERT, D_MODEL), BF16),
                            pltpu.VMEM((2, D_MODEL, D_EXPERT), F32), pltpu.VMEM((2, D_MODEL, D_EXPERT), F32),
                            pltpu.VMEM((2, D_EXPERT, D_MODEL), F32), pltpu.SemaphoreType.DMA((2,))],
        ),
        out_shape=jax.ShapeDtypeStruct((MOE_ROWS, D_MODEL), F32),
        compiler_params=_cparams("arbitrary"),
        name="moe_ffn",
    )(block_e, nused, nexte, wslot, xs, w_gate, w_up, w_down)


def _combine_kernel(dest0_ref, dest1_ref, x1_ref, rt_ref, gf_ref, ys_ref, y_ref, ya, yb, sem):
    rows = x1_ref.shape[0]
    base = pl.program_id(0) * rows

    def row_copy(t, dest_ref, buf):
        return pltpu.make_async_copy(ys_ref.at[pl.ds(dest_ref[base + t], 1)], buf.at[pl.ds(t, 1)], sem)

    def start(t, c):
        row_copy(t, dest0_ref, ya).start()
        row_copy(t, dest1_ref, yb).start()
        return c

    lax.fori_loop(0, rows, start, 0, unroll=ROW_DMA_UNROLL)
    pltpu.make_async_copy(ys_ref.at[pl.ds(0, rows)], ya, sem).wait()
    pltpu.make_async_copy(ys_ref.at[pl.ds(0, rows)], yb, sem).wait()
    x2 = x1_ref[...] + rt_ref[:, 4:5] * ya[...] + rt_ref[:, 5:6] * yb[...]
    y_ref[...] = _rms(x2, gf_ref[...])


def _combine(dest, x1, rt, gf, ys, rows):
    n = x1.shape[0]
    return pl.pallas_call(
        _combine_kernel,
        grid_spec=pltpu.PrefetchScalarGridSpec(
            num_scalar_prefetch=2,
            grid=(n // rows,),
            in_specs=[pl.BlockSpec((rows, D_MODEL), lambda i, *_: (i, 0)),
                      pl.BlockSpec((rows, LANES), lambda i, *_: (i, 0)),
                      pl.BlockSpec((1, D_MODEL), lambda i, *_: (0, 0)),
                      pl.BlockSpec(memory_space=pl.ANY)],
            out_specs=pl.BlockSpec((rows, D_MODEL), lambda i, *_: (i, 0)),
            scratch_shapes=[pltpu.VMEM((rows, D_MODEL), F32), pltpu.VMEM((rows, D_MODEL), F32),
                            pltpu.SemaphoreType.DMA(())],
        ),
        out_shape=jax.ShapeDtypeStruct((n, D_MODEL), F32),
        compiler_params=_cparams("arbitrary"),
        name=f"moe_combine_{rows}",
    )(dest[0], dest[1], x1, rt, gf, ys)


def _rope_tables(pos, xp):
    half = HEAD_DIM // 2
    inv = 1.0 / (xp.float32(ROPE_THETA) ** (xp.arange(half, dtype=xp.float32) * xp.float32(2.0 / HEAD_DIM)))
    ang = pos.astype(xp.float32)[:, None] * inv[None, :].astype(xp.float32)
    cos, sin = xp.cos(ang), xp.sin(ang)
    return xp.concatenate([cos, cos, cos, cos], axis=1), xp.concatenate([-sin, sin, -sin, sin], axis=1)


def _slot_dest(ids, pos, pstarts):
    ids = ids.astype(jnp.int32)
    experts = jnp.arange(N_EXPERTS, dtype=jnp.int32).reshape((N_EXPERTS,) + (1,) * ids.ndim)
    start = jnp.sum(jnp.where(ids[None] == experts, pstarts.reshape(experts.shape), 0), axis=0)
    return start + pos.astype(jnp.int32)


def kernel(x_prompt, x_sample, cache_kv_g0, cache_kv_g1, cache_kv_g2, norm1_g, w_in, b_in, a_ln_g, a_ln_b, w_spatial, b_spatial, w_a_proj, w_b_proj, w_o, norm2_g, w_group_router, b_group_router, w_expert_router, b_expert_router, w_gate, w_up, w_down, final_norm_g):
    x2 = x_prompt.reshape(T_PROMPT, D_MODEL)
    xs = x_sample.reshape(DEC_BATCH, D_MODEL)
    g1 = norm1_g[0][None, :]
    g2 = norm2_g[0][None, :]
    gf = final_norm_g[None, :]
    b_in2 = b_in[0][None, :]
    lng, lnb = a_ln_g[0][None, :], a_ln_b[0][None, :]

    causal = jnp.tril(jnp.ones((CHUNK, CHUNK), dtype=bool))
    ws_tril = jnp.where(causal[None], w_spatial[0], 0.0)
    wsp = jnp.concatenate([ws_tril[0::2], ws_tril[1::2]], axis=2).astype(BF16)
    bsp = jnp.repeat(b_spatial[0].T, A_GROUP_DIM, axis=1)
    ws0 = jnp.repeat(ws_tril[:, 0, 0].astype(BF16).astype(F32), A_GROUP_DIM)[None, :]
    bs0 = jnp.repeat(b_spatial[0][:, 0], A_GROUP_DIM)[None, :]
    w_router = jnp.zeros((D_MODEL, LANES), F32)
    w_router = w_router.at[:, :N_EXPERT_GROUPS].set(w_group_router[0])
    w_router = w_router.at[:, N_EXPERT_GROUPS:N_EXPERT_GROUPS + N_EXPERTS].set(w_expert_router[0])
    b_router = jnp.zeros((1, LANES), F32)
    b_router = b_router.at[0, :N_EXPERT_GROUPS].set(b_group_router[0])
    b_router = b_router.at[0, N_EXPERT_GROUPS:N_EXPERT_GROUPS + N_EXPERTS].set(b_expert_router[0])
    w_router = w_router.astype(BF16)
    ltri = jnp.tril(jnp.ones((TM, TM), BF16), -1)
    w_in_b, w_a_b, w_b_b, w_o_b = (w[0].astype(BF16) for w in (w_in, w_a_proj, w_b_proj, w_o))

    cos_p, sin_p = _rope_tables(np.arange(SEQ, dtype=np.int32), np)
    cos_s, sin_s = _rope_tables(jnp.full((1,), PAST_LEN, jnp.int32), jnp)

    uv, qkv0, qkv1, qkv2, gates, kvp0, kvp1, kvp2 = _inproj(
        x2, g1, w_in_b, b_in2, lng, lnb, cos_p, sin_p)
    ob = _attention((qkv0.reshape(BATCH, 1, SEQ, 3 * GROUP_W), qkv1, qkv2)).reshape(T_PROMPT, GROUP_W)
    x1_p, h2p_p, rt_p, rtt_p, counts_p = _merge(
        x2, uv, gates, ob, wsp, bsp, w_a_b, w_b_b, w_o_b, g2, w_router, b_router, ltri)

    z_s = _sample_inproj(xs, g1, w_in_b, b_in2)
    vrow, sa_s, qr_s, kr_s, kvs0, kvs1, kvs2 = _sample_pre(z_s, lng, lnb, ws0, bs0, cos_s, sin_s)
    dim_major = lambda t: t.reshape(DEC_BATCH, 3, HEADS, HEAD_DIM).transpose(0, 1, 3, 2)
    caches = [c.transpose(0, 1, 3, 4, 5, 2).reshape(DEC_BATCH, 2, HEADS, HEAD_DIM, c.shape[2])
              for c in (cache_kv_g0, cache_kv_g1, cache_kv_g2)]
    comb_s = _sample_attn(dim_major(qr_s), dim_major(kr_s), dim_major(z_s[:, COL_VB:COL_VB + QK_W]), caches)
    comb_s = comb_s.transpose(0, 2, 1).reshape(DEC_BATCH, GROUP_W)
    x1_s, h2p_s, rt_s, counts = _sample_merge(
        xs, z_s, sa_s, comb_s, w_a_b, w_b_b, w_o_b, g2, w_router, b_router,
        ltri[:DEC_BATCH, :DEC_BATCH], counts_p)

    cnt = counts[0, :N_EXPERTS].astype(jnp.int32)
    padded = (cnt + MOE_TB - 1) // MOE_TB * MOE_TB
    pends = jnp.cumsum(padded)
    pstarts = pends - padded
    block_starts = jnp.arange(MOE_NB, dtype=jnp.int32) * MOE_TB
    block_e = jnp.minimum(jnp.sum((pends[None, :] <= block_starts[:, None]).astype(jnp.int32), axis=1),
                          N_EXPERTS - 1)
    nused = (pends[-1:] // MOE_TB).astype(jnp.int32)
    rtt_p = rtt_p.reshape(T_PROMPT // TM, ROUTE_ROWS, TM)
    dest_p = [_slot_dest(rtt_p[:, k], rtt_p[:, 2 + k], pstarts).reshape(T_PROMPT) for k in range(2)]
    dest_s = [_slot_dest(rt_s[:, k], rt_s[:, 2 + k], pstarts) for k in range(2)]

    rows = jnp.zeros((MOE_ROWS, HALF_D), U32)
    rows = _dispatch(dest_p, h2p_p, rows, TM)
    rows = _dispatch(dest_s, h2p_s, rows, DEC_BATCH)
    ys = _expert_ffn(block_e, nused, rows, w_gate[0], w_up[0], w_down[0])
    y_p = _combine(dest_p, x1_p, rt_p, gf, ys, TM)
    y_s = _combine(dest_s, x1_s, rt_s, gf, ys, DEC_BATCH)

    kv_shape = lambda n, w: (1, n, w, 2, HEADS, HEAD_DIM)
    window_rows = lambda t: t.reshape(1, BATCH, 2, HEADS, HEAD_DIM, t.shape[2]).transpose(0, 1, 5, 2, 3, 4)
    return (y_p.reshape(BATCH, SEQ, D_MODEL), y_s.reshape(DEC_BATCH, 1, D_MODEL),
            window_rows(kvp0), window_rows(kvp1), window_rows(kvp2),
            kvs0.reshape(kv_shape(DEC_BATCH, 1)), kvs1.reshape(kv_shape(DEC_BATCH, 1)),
            kvs2.reshape(kv_shape(DEC_BATCH, 1)), vrow.reshape(1, DEC_BATCH, 1, A_WIDTH))
```

```python
import functools

import jax
import jax.numpy as jnp
import numpy as np
from jax import lax
from jax.experimental import pallas as pl
from jax.experimental.pallas import tpu as pltpu

F32 = jnp.float32
BF16 = jnp.bfloat16
U32 = jnp.uint32

D_MODEL = 1024
BATCH = 2
SEQ = 8192
DEC_BATCH = 32
PAST_LEN = 8192
CHUNK = 128
A_GROUPS = 8
A_GROUP_DIM = 64
A_WIDTH = 512
HEAD_DIM = 64
HEADS = 4
GROUP_W = HEADS * HEAD_DIM
DILATIONS = (1, 4, 16)
SPAN = 128
QK_W = 768
IN_WIDTH = 5376
COL_U, COL_V, COL_Q, COL_K, COL_VB, COL_GA, COL_GB = 0, 512, 1024, 1792, 2560, 3328, 4352
N_EXPERT_GROUPS = 4
EXPERTS_PER_GROUP = 8
N_EXPERTS = 32
D_EXPERT = 512
ROPE_THETA = 10000.0
EPS = 1e-6

LANES = 128
T_PROMPT = BATCH * SEQ
TM = 512
TM_IN = 1024
WINDOWS = tuple(min(SPAN * d, SEQ) for d in DILATIONS)
TA = SPAN * max(DILATIONS)
ATT_UNROLL = 2
MOE_TB = 512
ROW_DMA_UNROLL = 8
ROUTE_ROWS = 8
N_SLOTS = 2 * (T_PROMPT + DEC_BATCH)
MOE_NB = -(-N_SLOTS // MOE_TB) + N_EXPERTS
MOE_ROWS = MOE_NB * MOE_TB
HALF_D = D_MODEL // 2
NEG = -1e30
VMEM_LIMIT = 56 * 1024 * 1024


def _cparams(*sem):
    return pltpu.CompilerParams(dimension_semantics=sem, vmem_limit_bytes=VMEM_LIMIT)


def _rms(x, g):
    return x * lax.rsqrt(jnp.mean(x * x, axis=-1, keepdims=True) + EPS) * g


def _layer_norm(x, g, b):
    mu = jnp.mean(x, axis=-1, keepdims=True)
    xc = x - mu
    var = jnp.mean(xc * xc, axis=-1, keepdims=True)
    return xc * lax.rsqrt(var + EPS) * g + b


def _rope_chunk(x, cos, sin_signed):
    lane = lax.broadcasted_iota(jnp.int32, x.shape, 1)
    first_half = (lane % HEAD_DIM) < (HEAD_DIM // 2)
    swapped = jnp.where(first_half, pltpu.roll(x, LANES - HEAD_DIM // 2, 1), pltpu.roll(x, HEAD_DIM // 2, 1))
    return x * cos + swapped * sin_signed


def _pack_bf16_pair(h):
    lo = lax.bitcast_convert_type(h[:, :HALF_D].astype(BF16).astype(F32), U32)
    hi = lax.bitcast_convert_type(h[:, HALF_D:].astype(BF16).astype(F32), U32)
    return (hi & jnp.uint32(0xFFFF0000)) | (lo >> 16)


def _unpack_bf16_pair(p):
    lo = lax.bitcast_convert_type(p << 16, F32).astype(BF16)
    hi = lax.bitcast_convert_type(p & jnp.uint32(0xFFFF0000), F32).astype(BF16)
    return lo, hi


def _route(logits, prior_counts, ltri):
    rows = logits.shape[0]
    lane = lax.broadcasted_iota(jnp.int32, (rows, LANES), 1).astype(F32)
    is_g = lane < N_EXPERT_GROUPS
    gl = jnp.where(is_g, logits, NEG)
    gmax = jnp.max(gl, axis=1, keepdims=True)
    grp = jnp.min(jnp.where(gl == gmax, lane, float(LANES)), axis=1, keepdims=True)
    p_grp = 1.0 / jnp.sum(jnp.where(is_g, jnp.exp(gl - gmax), 0.0), axis=1, keepdims=True)
    lo = N_EXPERT_GROUPS + grp * EXPERTS_PER_GROUP
    el = jnp.where((lane >= lo) & (lane < lo + EXPERTS_PER_GROUP), logits, NEG)
    v1 = jnp.max(el, axis=1, keepdims=True)
    i1 = jnp.min(jnp.where(el == v1, lane, float(LANES)), axis=1, keepdims=True)
    el2 = jnp.where(lane == i1, NEG, el)
    v2 = jnp.max(el2, axis=1, keepdims=True)
    i2 = jnp.min(jnp.where(el2 == v2, lane, float(LANES)), axis=1, keepdims=True)
    t = jnp.exp(v2 - v1)
    gate1 = p_grp / (1.0 + t)
    gate2 = p_grp * t / (1.0 + t)
    e1 = i1 - N_EXPERT_GROUPS
    e2 = i2 - N_EXPERT_GROUPS
    hit1 = lane == e1
    hit2 = lane == e2
    onehot = jnp.where(hit1 | hit2, 1.0, 0.0)
    rank = jnp.dot(ltri, onehot.astype(BF16), preferred_element_type=F32) + prior_counts
    pos1 = jnp.sum(jnp.where(hit1, rank, 0.0), axis=1, keepdims=True)
    pos2 = jnp.sum(jnp.where(hit2, rank, 0.0), axis=1, keepdims=True)
    slab = jnp.where(lane == 0, e1, jnp.where(lane == 1, e2, jnp.where(lane == 2, pos1, jnp.where(
        lane == 3, pos2, jnp.where(lane == 4, gate1, jnp.where(lane == 5, gate2, 0.0))))))
    return slab, prior_counts + jnp.sum(onehot, axis=0, keepdims=True)


def _inproj_kernel(x_ref, g1_ref, w_ref, b_ref, lng_ref, lnb_ref, cos_ref, sin_ref,
                   uv_ref, qkv0_ref, qkv1_ref, qkv2_ref, gates_ref, kv0_ref, kv1_ref, kv2_ref, de_ref):
    tm = x_ref.shape[0]
    tiles_per_seq = SEQ // tm
    tile_in_seq = pl.program_id(0) % tiles_per_seq
    hb = _rms(x_ref[...], g1_ref[...]).astype(BF16)

    def seg(lo, width):
        return jnp.dot(hb, w_ref[:, lo:lo + width], preferred_element_type=F32) + b_ref[:, lo:lo + width]

    uv_ref[:, :A_WIDTH] = jax.nn.gelu(seg(COL_U, A_WIDTH)).astype(BF16)
    uv_ref[:, A_WIDTH:] = _layer_norm(jax.nn.gelu(seg(COL_V, A_WIDTH)), lng_ref[...], lnb_ref[...]).astype(BF16)
    gates_ref[:, :D_MODEL] = jax.nn.sigmoid(seg(COL_GA, D_MODEL)).astype(BF16)
    gates_ref[:, D_MODEL:] = jax.nn.sigmoid(seg(COL_GB, D_MODEL)).astype(BF16)

    cos = cos_ref[...]
    sin = sin_ref[...]
    qkv_refs = (qkv0_ref, qkv1_ref, qkv2_ref)
    kv_refs = (kv0_ref, kv1_ref, kv2_ref)
    for g, d in enumerate(DILATIONS):
        q = seg(COL_Q + g * GROUP_W, GROUP_W)
        k = seg(COL_K + g * GROUP_W, GROUP_W)
        v = seg(COL_VB + g * GROUP_W, GROUP_W)
        chunks = GROUP_W // LANES
        for c in range(chunks):
            sl = slice(c * LANES, (c + 1) * LANES)
            de_ref[c] = _rope_chunk(q[:, sl], cos, sin) * (HEAD_DIM ** -0.5)
            de_ref[chunks + c] = _rope_chunk(k[:, sl], cos, sin)
            de_ref[2 * chunks + c] = v[:, sl]
        for c in range(3 * chunks):
            sl = slice(c * LANES, (c + 1) * LANES)
            if d == 1:
                qkv_refs[g][:, sl] = de_ref[c].astype(BF16)
            else:
                for r in range(d):
                    qkv_refs[g][0, r, :, sl] = de_ref[c, pl.ds(r, tm // d, stride=d), :].astype(BF16)
        kv_rows = kv_refs[g].shape[2]
        first = tiles_per_seq - WINDOWS[g] // kv_rows

        @pl.when(tile_in_seq >= first)
        def _(kv_ref=kv_refs[g], kv_rows=kv_rows, chunks=chunks):
            for c in range(2 * chunks):
                kv_ref[0, c * LANES:(c + 1) * LANES, :] = de_ref[chunks + c, tm - kv_rows:, :].T


def _inproj(x2, g1, w_bf, b_in, lng, lnb, cos_t, sin_t):
    tm = TM_IN
    tiles_per_seq = SEQ // tm
    n_tiles = T_PROMPT // tm
    const = lambda i: (0, 0)

    def kv_spec(w):
        rows = min(w, tm)
        first = tiles_per_seq - w // rows
        return pl.BlockSpec((1, 2 * GROUP_W, rows),
                            lambda i: (i // tiles_per_seq, 0, jnp.maximum(i % tiles_per_seq - first, 0)))

    def regrouped_spec(d):
        return pl.BlockSpec((1, d, tm // d, 3 * GROUP_W), lambda i: (i // tiles_per_seq, 0, i % tiles_per_seq, 0))

    return pl.pallas_call(
        _inproj_kernel,
        grid=(n_tiles,),
        in_specs=[
            pl.BlockSpec((tm, D_MODEL), lambda i: (i, 0)),
            pl.BlockSpec((1, D_MODEL), const),
            pl.BlockSpec((D_MODEL, IN_WIDTH), const, pipeline_mode=pl.Buffered(1)),
            pl.BlockSpec((1, IN_WIDTH), const),
            pl.BlockSpec((1, A_WIDTH), const),
            pl.BlockSpec((1, A_WIDTH), const),
            pl.BlockSpec((tm, LANES), lambda i: (i % tiles_per_seq, 0)),
            pl.BlockSpec((tm, LANES), lambda i: (i % tiles_per_seq, 0)),
        ],
        out_specs=[
            pl.BlockSpec((tm, 2 * A_WIDTH), lambda i: (i, 0)),
            pl.BlockSpec((tm, 3 * GROUP_W), lambda i: (i, 0)),
            regrouped_spec(DILATIONS[1]),
            regrouped_spec(DILATIONS[2]),
            pl.BlockSpec((tm, 2 * D_MODEL), lambda i: (i, 0)),
            kv_spec(WINDOWS[0]), kv_spec(WINDOWS[1]), kv_spec(WINDOWS[2]),
        ],
        out_shape=[
            jax.ShapeDtypeStruct((T_PROMPT, 2 * A_WIDTH), BF16),
            jax.ShapeDtypeStruct((T_PROMPT, 3 * GROUP_W), BF16),
            jax.ShapeDtypeStruct((BATCH, DILATIONS[1], SEQ // DILATIONS[1], 3 * GROUP_W), BF16),
            jax.ShapeDtypeStruct((BATCH, DILATIONS[2], SEQ // DILATIONS[2], 3 * GROUP_W), BF16),
            jax.ShapeDtypeStruct((T_PROMPT, 2 * D_MODEL), BF16),
            jax.ShapeDtypeStruct((BATCH, 2 * GROUP_W, WINDOWS[0]), F32),
            jax.ShapeDtypeStruct((BATCH, 2 * GROUP_W, WINDOWS[1]), F32),
            jax.ShapeDtypeStruct((BATCH, 2 * GROUP_W, WINDOWS[2]), F32),
        ],
        scratch_shapes=[pltpu.VMEM((3 * GROUP_W // LANES, tm, LANES), F32)],
        compiler_params=_cparams("arbitrary"),
        name="inproj",
    )(x2, g1, w_bf, b_in, lng, lnb, cos_t, sin_t)


def _attn_kernel(c0_ref, p0_ref, c1_ref, p1_ref, c2_ref, p2_ref, o_ref, acc_ref, m_ref, l_ref):
    n = pl.program_id(1)
    qi = lax.broadcasted_iota(jnp.int32, (HEADS * SPAN, 2 * SPAN), 0) % SPAN
    ki = lax.broadcasted_iota(jnp.int32, (HEADS * SPAN, 2 * SPAN), 1)
    band = (ki >= qi) & (ki <= qi + SPAN)
    band_first = band & ((ki >= SPAN) | (n > 0))
    q_head = lax.broadcasted_iota(jnp.int32, (SPAN, GROUP_W), 1) // HEAD_DIM
    kv_head = lax.broadcasted_iota(jnp.int32, (2 * SPAN, GROUP_W), 1) // HEAD_DIM
    contract_last = (((1,), (1,)), ((), ()))
    k_cols = slice(GROUP_W, 2 * GROUP_W)
    v_cols = slice(2 * GROUP_W, 3 * GROUP_W)

    def attend(q, keys, vals, mask, tok_rows, fresh):
        q_heads = jnp.concatenate([jnp.where(q_head == h, q, jnp.zeros_like(q)) for h in range(HEADS)], axis=0)
        s = jnp.where(mask, lax.dot_general(q_heads, keys, contract_last, preferred_element_type=F32), NEG)
        m = jnp.max(s, axis=1, keepdims=True)
        p = jnp.exp(s - m)
        l = jnp.sum(p, axis=1, keepdims=True)
        pb = p.astype(BF16)
        p_heads = jnp.concatenate([pb[h * SPAN:(h + 1) * SPAN] for h in range(HEADS)], axis=1)
        v_heads = jnp.concatenate([jnp.where(kv_head == h, vals, jnp.zeros_like(vals)) for h in range(HEADS)], axis=0)
        acc = jnp.dot(p_heads, v_heads, preferred_element_type=F32)
        m_b = jnp.zeros((SPAN, GROUP_W), F32)
        l_b = jnp.zeros((SPAN, GROUP_W), F32)
        for h in range(HEADS):
            rows = slice(h * SPAN, (h + 1) * SPAN)
            m_b = jnp.where(q_head == h, m[rows], m_b)
            l_b = jnp.where(q_head == h, l[rows], l_b)
        for c in range(GROUP_W // LANES):
            sl = slice(c * LANES, (c + 1) * LANES)
            if fresh:
                m_ref[c, tok_rows, :] = m_b[:, sl]
                l_ref[c, tok_rows, :] = l_b[:, sl]
                acc_ref[c, tok_rows, :] = acc[:, sl]
                continue
            m_old = m_ref[c, tok_rows, :]
            m_new = jnp.maximum(m_old, m_b[:, sl])
            a_old = jnp.exp(m_old - m_new)
            a_blk = jnp.exp(m_b[:, sl] - m_new)
            l_ref[c, tok_rows, :] = a_old * l_ref[c, tok_rows, :] + a_blk * l_b[:, sl]
            acc_ref[c, tok_rows, :] = a_old * acc_ref[c, tok_rows, :] + a_blk * acc[:, sl]
            m_ref[c, tok_rows, :] = m_new

    def stream(c_ref, p_ref, r, d, fresh):
        n_blocks = c_ref.shape[2] // SPAN

        def tok_rows(j):
            start = j * (SPAN * d) + r
            return pl.ds(start, SPAN) if d == 1 else pl.ds(start, SPAN, stride=d)

        keys = jnp.concatenate([p_ref[0, r, :, k_cols], c_ref[0, r, :SPAN, k_cols]], axis=0)
        vals = jnp.concatenate([p_ref[0, r, :, v_cols], c_ref[0, r, :SPAN, v_cols]], axis=0)
        attend(c_ref[0, r, :SPAN, :GROUP_W], keys, vals, band_first, tok_rows(0), fresh)

        def later_block(j, carry):
            q_rows = pl.ds(pl.multiple_of(j * SPAN, SPAN), SPAN)
            kv_rows = pl.ds(pl.multiple_of((j - 1) * SPAN, SPAN), 2 * SPAN)
            attend(c_ref[0, r, q_rows, :GROUP_W], c_ref[0, r, kv_rows, k_cols], c_ref[0, r, kv_rows, v_cols],
                   band, tok_rows(j), fresh)
            return carry

        if n_blocks > 1:
            lax.fori_loop(1, n_blocks, later_block, 0, unroll=3)

    groups = sorted(zip(DILATIONS, ((c0_ref, p0_ref), (c1_ref, p1_ref), (c2_ref, p2_ref))), key=lambda g: -g[0])
    for idx, (d, (c_ref, p_ref)) in enumerate(groups):
        fresh = idx == 0
        if d == 1:
            stream(c_ref, p_ref, 0, d, fresh)
        else:
            lax.fori_loop(0, d, lambda r, carry, c_ref=c_ref, p_ref=p_ref, d=d, fresh=fresh:
                          (stream(c_ref, p_ref, r, d, fresh), carry)[1], 0,
                          unroll=ATT_UNROLL if c_ref.shape[2] == SPAN else 1)

    for c in range(GROUP_W // LANES):
        o_ref[0, :, c * LANES:(c + 1) * LANES] = (acc_ref[c] / l_ref[c]).astype(BF16)


def _attention(qkv_by_group):
    in_specs, args = [], []
    for qkv, d in zip(qkv_by_group, DILATIONS):
        rows = TA // d
        blocks_per_tile = rows // SPAN
        in_specs.append(pl.BlockSpec((1, d, rows, 3 * GROUP_W), lambda b, n: (b, 0, n, 0)))
        in_specs.append(pl.BlockSpec((1, d, SPAN, 3 * GROUP_W),
                                     lambda b, n, k=blocks_per_tile: (b, 0, jnp.maximum(n * k - 1, 0), 0)))
        args += [qkv, qkv]
    return pl.pallas_call(
        _attn_kernel,
        grid=(BATCH, SEQ // TA),
        in_specs=in_specs,
        out_specs=pl.BlockSpec((1, TA, GROUP_W), lambda b, n: (b, n, 0)),
        out_shape=jax.ShapeDtypeStruct((BATCH, SEQ, GROUP_W), BF16),
        scratch_shapes=[pltpu.VMEM((GROUP_W // LANES, TA, LANES), F32)] * 3,
        compiler_params=_cparams("arbitrary", "arbitrary"),
        name="attn",
    )(*args)


def _merge_kernel(x_ref, uv_ref, gates_ref, ob_ref,
                  wsp_ref, bsp_ref, wa_ref, wb_ref, wo_ref, g2_ref, wr_ref, br_ref, ltri_ref,
                  x1_ref, h2p_ref, rt_ref, rtt_ref, cnt_ref, run_ref):
    @pl.when(pl.program_id(0) == 0)
    def _():
        run_ref[...] = jnp.zeros_like(run_ref)

    lane = lax.broadcasted_iota(jnp.int32, (CHUNK, LANES), 1)
    left = lane < A_GROUP_DIM
    zero = jnp.zeros((CHUNK, LANES), BF16)
    sa_chunks = []
    for c in range(TM // CHUNK):
        rows = slice(c * CHUNK, (c + 1) * CHUNK)
        pairs = []
        for p in range(A_GROUPS // 2):
            vp = uv_ref[rows, A_WIDTH + p * LANES:A_WIDTH + (p + 1) * LANES]
            rhs = jnp.concatenate([jnp.where(left, vp, zero), jnp.where(left, zero, vp)], axis=0)
            pairs.append(jnp.dot(wsp_ref[p], rhs, preferred_element_type=F32))
        mixed = jnp.concatenate(pairs, axis=1) + bsp_ref[...]
        sa_chunks.append((uv_ref[rows, :A_WIDTH].astype(F32) * mixed).astype(BF16))
    s_a = jnp.concatenate(sa_chunks, axis=0)

    a = jnp.dot(s_a, wa_ref[...], preferred_element_type=F32)
    b = jnp.dot(ob_ref[...], wb_ref[...], preferred_element_type=F32)
    merged = gates_ref[:, :D_MODEL].astype(F32) * a + gates_ref[:, D_MODEL:].astype(F32) * b
    x1 = x_ref[...] + jnp.dot(merged.astype(BF16), wo_ref[...], preferred_element_type=F32)
    x1_ref[...] = x1

    h2 = _rms(x1, g2_ref[...])
    h2p_ref[...] = _pack_bf16_pair(h2)
    logits = jnp.dot(h2.astype(BF16), wr_ref[...], preferred_element_type=F32) + br_ref[...]
    slab, counts = _route(logits, run_ref[...], ltri_ref[...])
    rt_ref[...] = slab
    rtt_ref[...] = slab.T[:ROUTE_ROWS, :]
    run_ref[...] = counts
    cnt_ref[...] = counts


def _merge(x2, uv, gates, ob, wsp, bsp, wa, wb, wo, g2, wr, br, ltri):
    n_tiles = T_PROMPT // TM
    tile = lambda w: pl.BlockSpec((TM, w), lambda i: (i, 0))
    full = lambda a: pl.BlockSpec(a.shape, lambda i: (0,) * a.ndim)
    return pl.pallas_call(
        _merge_kernel,
        grid=(n_tiles,),
        in_specs=[tile(D_MODEL), tile(2 * A_WIDTH), tile(2 * D_MODEL), tile(GROUP_W),
                  full(wsp), full(bsp), full(wa), full(wb), full(wo), full(g2), full(wr), full(br),
                  full(ltri)],
        out_specs=[tile(D_MODEL), tile(HALF_D), tile(LANES), pl.BlockSpec((ROUTE_ROWS, TM), lambda i: (i, 0)),
                   pl.BlockSpec((1, LANES), lambda i: (0, 0))],
        out_shape=[
            jax.ShapeDtypeStruct((T_PROMPT, D_MODEL), F32),
            jax.ShapeDtypeStruct((T_PROMPT, HALF_D), U32),
            jax.ShapeDtypeStruct((T_PROMPT, LANES), F32),
            jax.ShapeDtypeStruct((n_tiles * ROUTE_ROWS, TM), F32),
            jax.ShapeDtypeStruct((1, LANES), F32),
        ],
        scratch_shapes=[pltpu.VMEM((1, LANES), F32)],
        compiler_params=_cparams("arbitrary"),
        name="merge_route",
    )(x2, uv, gates, ob, wsp, bsp, wa, wb, wo, g2, wr, br, ltri)


SAMPLE_COLS = 768


def _sample_inproj_kernel(x_ref, g1_ref, w_ref, b_ref, z_ref):
    hb = _rms(x_ref[...], g1_ref[...]).astype(BF16)
    z_ref[...] = jnp.dot(hb, w_ref[...], preferred_element_type=F32) + b_ref[...]


def _sample_inproj(xs, g1, w_in, b_in):
    const = lambda j: (0, 0)
    return pl.pallas_call(
        _sample_inproj_kernel,
        grid=(IN_WIDTH // SAMPLE_COLS,),
        in_specs=[pl.BlockSpec((DEC_BATCH, D_MODEL), const), pl.BlockSpec((1, D_MODEL), const),
                  pl.BlockSpec((D_MODEL, SAMPLE_COLS), lambda j: (0, j)),
                  pl.BlockSpec((1, SAMPLE_COLS), lambda j: (0, j))],
        out_specs=pl.BlockSpec((DEC_BATCH, SAMPLE_COLS), lambda j: (0, j)),
        out_shape=jax.ShapeDtypeStruct((DEC_BATCH, IN_WIDTH), F32),
        compiler_params=_cparams("arbitrary"),
        name="sample_inproj",
    )(xs, g1, w_in, b_in)


def _sample_pre_kernel(z_ref, lng_ref, lnb_ref, ws0_ref, bs0_ref, cos_ref, sin_ref,
                       vrow_ref, sa_ref, qr_ref, kr_ref, kv0_ref, kv1_ref, kv2_ref):
    u = jax.nn.gelu(z_ref[:, COL_U:COL_U + A_WIDTH])
    va = _layer_norm(jax.nn.gelu(z_ref[:, COL_V:COL_V + A_WIDTH]), lng_ref[...], lnb_ref[...])
    vrow_ref[...] = va
    sa_ref[...] = u * (ws0_ref[...] * va.astype(BF16).astype(F32) + bs0_ref[...])
    cos, sin = cos_ref[...], sin_ref[...]
    for g, kv_ref in enumerate((kv0_ref, kv1_ref, kv2_ref)):
        for c in range(GROUP_W // LANES):
            off = g * GROUP_W + c * LANES
            qr_ref[:, off:off + LANES] = (_rope_chunk(z_ref[:, COL_Q + off:COL_Q + off + LANES], cos, sin)
                                          * (HEAD_DIM ** -0.5))
            kr = _rope_chunk(z_ref[:, COL_K + off:COL_K + off + LANES], cos, sin)
            kr_ref[:, off:off + LANES] = kr
            kv_ref[:, c * LANES:(c + 1) * LANES] = kr
        kv_ref[:, GROUP_W:] = z_ref[:, COL_VB + g * GROUP_W:COL_VB + (g + 1) * GROUP_W]


def _sample_pre(z, lng, lnb, ws0, bs0, cos_s, sin_s):
    row = lambda w: jax.ShapeDtypeStruct((DEC_BATCH, w), F32)
    return pl.pallas_call(
        _sample_pre_kernel,
        out_shape=[row(A_WIDTH), row(A_WIDTH), row(QK_W), row(QK_W),
                   row(2 * GROUP_W), row(2 * GROUP_W), row(2 * GROUP_W)],
        compiler_params=pltpu.CompilerParams(vmem_limit_bytes=VMEM_LIMIT),
        name="sample_pre",
    )(z, lng, lnb, ws0, bs0, cos_s, sin_s)


def _sample_attn_kernel(q_ref, k_ref, v_ref, c0_ref, c1_ref, c2_ref, comb_ref):
    as_operand = lambda t: t.astype(BF16).astype(F32)
    contract_last = (((1,), (1,)), ((), ()))
    pad_rows = 8 // HEADS
    groups = tuple(zip((c0_ref, c1_ref, c2_ref), DILATIONS))
    pairs = [(h, g) for h in range(HEADS) for g in range(len(groups))]
    scores = {}
    for h, g in pairs:
        c_ref, d = groups[g]
        q_rows = jnp.concatenate([q_ref[0, g]] * pad_rows, axis=0).astype(BF16)
        s = jnp.dot(q_rows, c_ref[0, 0, h].astype(BF16), preferred_element_type=F32)[h:h + 1]
        if d > 1:
            pos = lax.broadcasted_iota(jnp.int32, s.shape, 1)
            s = jnp.where(pos % d == 0, s, NEG)
        scores[h, g] = s
    probs = {}
    for h, g in pairs:
        s = scores[h, g]
        qn, kn = (as_operand(r[0, g, h:h + 1, :]) for r in (q_ref, k_ref))
        s_n = jnp.sum(qn * kn, axis=1, keepdims=True)
        m = jnp.maximum(jnp.max(s, axis=1, keepdims=True), s_n)
        lse = m + jnp.log(jnp.sum(jnp.exp(s - m), axis=1, keepdims=True) + jnp.exp(s_n - m))
        probs[h, g] = (jnp.broadcast_to(jnp.exp(s - lse), (8, s.shape[1])).astype(BF16),
                       as_operand(jnp.exp(s_n - lse)), lse)
    outs = {}
    for h, g in pairs:
        p_rows, p_n, _ = probs[h, g]
        o = lax.dot_general(p_rows, groups[g][0][0, 1, h].astype(BF16), contract_last, preferred_element_type=F32)
        outs[h, g] = o[0:1] + p_n * as_operand(v_ref[0, g, h:h + 1, :])
    for h in range(HEADS):
        lses = [probs[h, g][2] for g in range(len(groups))]
        mx = jnp.maximum(jnp.maximum(lses[0], lses[1]), lses[2])
        ws = [jnp.exp(t - mx) for t in lses]
        tot = ws[0] + ws[1] + ws[2]
        comb_ref[0, h:h + 1, :] = (ws[0] * outs[h, 0] + ws[1] * outs[h, 1] + ws[2] * outs[h, 2]) / tot


def _sample_attn(qt, kt, vt, caches):
    new_spec = pl.BlockSpec((1, 3, HEADS, HEAD_DIM), lambda i: (i, 0, 0, 0))
    cache_spec = lambda c: pl.BlockSpec((1,) + c.shape[1:], lambda i: (i, 0, 0, 0, 0))
    return pl.pallas_call(
        _sample_attn_kernel,
        grid=(DEC_BATCH,),
        in_specs=[new_spec, new_spec, new_spec, cache_spec(caches[0]), cache_spec(caches[1]), cache_spec(caches[2])],
        out_specs=pl.BlockSpec((1, HEADS, HEAD_DIM), lambda i: (i, 0, 0)),
        out_shape=jax.ShapeDtypeStruct((DEC_BATCH, HEADS, HEAD_DIM), F32),
        compiler_params=_cparams("arbitrary"),
        name="sample_attn",
    )(qt, kt, vt, caches[0], caches[1], caches[2])


def _sample_merge_kernel(x_ref, z_ref, sa_ref, comb_ref, wa_ref, wb_ref, wo_ref, g2_ref, wr_ref, br_ref,
                         ltri_ref, cnt_in_ref, x1_ref, h2p_ref, rt_ref, cnt_ref):
    dot = lambda p, q: jnp.dot(p.astype(BF16), q, preferred_element_type=F32)
    a = dot(sa_ref[...], wa_ref[...])
    b = dot(comb_ref[...], wb_ref[...])
    merged = (jax.nn.sigmoid(z_ref[:, COL_GA:COL_GA + D_MODEL]) * a
              + jax.nn.sigmoid(z_ref[:, COL_GB:COL_GB + D_MODEL]) * b)
    x1 = x_ref[...] + dot(merged, wo_ref[...])
    x1_ref[...] = x1
    h2 = _rms(x1, g2_ref[...])
    h2p_ref[...] = _pack_bf16_pair(h2)
    logits = dot(h2, wr_ref[...]) + br_ref[...]
    slab, counts = _route(logits, cnt_in_ref[...], ltri_ref[...])
    rt_ref[...] = slab
    cnt_ref[...] = counts


def _sample_merge(xs, z, s_a, comb, wa, wb, wo, g2, wr, br, ltri, counts):
    return pl.pallas_call(
        _sample_merge_kernel,
        out_shape=[jax.ShapeDtypeStruct((DEC_BATCH, D_MODEL), F32),
                   jax.ShapeDtypeStruct((DEC_BATCH, HALF_D), U32),
                   jax.ShapeDtypeStruct((DEC_BATCH, LANES), F32),
                   jax.ShapeDtypeStruct((1, LANES), F32)],
        compiler_params=pltpu.CompilerParams(vmem_limit_bytes=VMEM_LIMIT),
        name="sample_merge_route",
    )(xs, z, s_a, comb, wa, wb, wo, g2, wr, br, ltri, counts)


def _dispatch_kernel(dest0_ref, dest1_ref, h_ref, xs_in_ref, xs_ref, sem):
    del xs_in_ref
    rows = h_ref.shape[0]
    base = pl.program_id(0) * rows

    def row_copy(t, dest_ref):
        return pltpu.make_async_copy(h_ref.at[pl.ds(t, 1)], xs_ref.at[pl.ds(dest_ref[base + t], 1)], sem)

    def start(t, c):
        row_copy(t, dest0_ref).start()
        row_copy(t, dest1_ref).start()
        return c

    lax.fori_loop(0, rows, start, 0, unroll=ROW_DMA_UNROLL)
    for _ in range(2):
        pltpu.make_async_copy(h_ref, xs_ref.at[pl.ds(0, rows)], sem).wait()


def _dispatch(dest, h2p, xs, rows):
    n = h2p.shape[0]
    return pl.pallas_call(
        _dispatch_kernel,
        grid_spec=pltpu.PrefetchScalarGridSpec(
            num_scalar_prefetch=2,
            grid=(n // rows,),
            in_specs=[pl.BlockSpec((rows, HALF_D), lambda i, *_: (i, 0)), pl.BlockSpec(memory_space=pl.ANY)],
            out_specs=pl.BlockSpec(memory_space=pl.ANY),
            scratch_shapes=[pltpu.SemaphoreType.DMA(())],
        ),
        out_shape=jax.ShapeDtypeStruct(xs.shape, xs.dtype),
        input_output_aliases={3: 0},
        compiler_params=_cparams("arbitrary"),
        name=f"moe_dispatch_{rows}",
    )(dest[0], dest[1], h2p, xs)


def _ffn_kernel(be_ref, nused_ref, nexte_ref, wslot_ref, xs_ref, wg_ref, wu_ref, wd_ref, ys_ref,
                wg_b, wu_b, wd_b, wg_f, wu_f, wd_f, wsem):
    i = pl.program_id(0)
    live = i < nused_ref[0]
    new_expert = jnp.logical_and(live, jnp.logical_or(i == 0, be_ref[i] != be_ref[jnp.maximum(i - 1, 0)]))

    def weight_copies(e, slot):
        return [pltpu.make_async_copy(src.at[e], dst.at[slot], wsem.at[slot])
                for src, dst in ((wg_ref, wg_f), (wu_ref, wu_f), (wd_ref, wd_f))]

    @pl.when(i == 0)
    def _():
        for cp in weight_copies(be_ref[0], 0):
            cp.start()

    @pl.when(new_expert)
    def _():
        slot = wslot_ref[i]
        for cp in weight_copies(be_ref[i], slot):
            cp.wait()
        wg_b[...] = wg_f[slot].astype(BF16)
        wu_b[...] = wu_f[slot].astype(BF16)
        wd_b[...] = wd_f[slot].astype(BF16)

        @pl.when(nexte_ref[i] >= 0)
        def _():
            for cp in weight_copies(nexte_ref[i], 1 - slot):
                cp.start()

    @pl.when(live)
    def _():
        lo, hi = _unpack_bf16_pair(xs_ref[...])
        gate = (jnp.dot(lo, wg_b[:HALF_D, :], preferred_element_type=F32)
                + jnp.dot(hi, wg_b[HALF_D:, :], preferred_element_type=F32))
        up = (jnp.dot(lo, wu_b[:HALF_D, :], preferred_element_type=F32)
              + jnp.dot(hi, wu_b[HALF_D:, :], preferred_element_type=F32))
        hid = (jax.nn.silu(gate) * up).astype(BF16)
        ys_ref[...] = jnp.dot(hid, wd_b[...], preferred_element_type=F32)

    @pl.when(i >= nused_ref[0])
    def _():
        ys_ref[...] = jnp.zeros_like(ys_ref)


def _expert_ffn(block_e, nused, xs, w_gate, w_up, w_down):
    idx = jnp.arange(MOE_NB, dtype=jnp.int32)
    change = ((block_e != jnp.roll(block_e, 1)) | (idx == 0)) & (idx < nused[0])
    wslot = ((jnp.cumsum(change.astype(jnp.int32)) - 1) & 1).astype(jnp.int32)
    change_at = jnp.where(change, idx, MOE_NB)
    next_change = jnp.flip(lax.cummin(jnp.flip(jnp.concatenate([change_at[1:], jnp.full((1,), MOE_NB, jnp.int32)]))))
    nexte = jnp.where(next_change < MOE_NB, block_e[jnp.minimum(next_change, MOE_NB - 1)], -1).astype(jnp.int32)
    hbm = pl.BlockSpec(memory_space=pl.ANY)
    return pl.pallas_call(
        _ffn_kernel,
        grid_spec=pltpu.PrefetchScalarGridSpec(
            num_scalar_prefetch=4,
            grid=(MOE_NB,),
            in_specs=[pl.BlockSpec((MOE_TB, HALF_D), lambda i, *_: (i, 0)), hbm, hbm, hbm],
            out_specs=pl.BlockSpec((MOE_TB, D_MODEL), lambda i, *_: (i, 0)),
            scratch_shapes=[pltpu.VMEM((D_MODEL, D_EXPERT), BF16), pltpu.VMEM((D_MODEL, D_EXPERT), BF16),
                            pltpu.VMEM((D_EXPERT, D_MODEL), BF16),
                            pltpu.VMEM((2, D_MODEL, D_EXPERT), F32), pltpu.VMEM((2, D_MODEL, D_EXPERT), F32),
                            pltpu.VMEM((2, D_EXPERT, D_MODEL), F32), pltpu.SemaphoreType.DMA((2,))],
        ),
        out_shape=jax.ShapeDtypeStruct((MOE_ROWS, D_MODEL), F32),
        compiler_params=_cparams("arbitrary"),
        name="moe_ffn",
    )(block_e, nused, nexte, wslot, xs, w_gate, w_up, w_down)


def _combine_kernel(dest0_ref, dest1_ref, x1_ref, rt_ref, gf_ref, ys_ref, y_ref, ya, yb, sem):
    rows = x1_ref.shape[0]
    base = pl.program_id(0) * rows

    def row_copy(t, dest_ref, buf):
        return pltpu.make_async_copy(ys_ref.at[pl.ds(dest_ref[base + t], 1)], buf.at[pl.ds(t, 1)], sem)

    def start(t, c):
        row_copy(t, dest0_ref, ya).start()
        row_copy(t, dest1_ref, yb).start()
        return c

    lax.fori_loop(0, rows, start, 0, unroll=ROW_DMA_UNROLL)
    pltpu.make_async_copy(ys_ref.at[pl.ds(0, rows)], ya, sem).wait()
    pltpu.make_async_copy(ys_ref.at[pl.ds(0, rows)], yb, sem).wait()
    x2 = x1_ref[...] + rt_ref[:, 4:5] * ya[...] + rt_ref[:, 5:6] * yb[...]
    y_ref[...] = _rms(x2, gf_ref[...])


def _combine(dest, x1, rt, gf, ys, rows):
    n = x1.shape[0]
    return pl.pallas_call(
        _combine_kernel,
        grid_spec=pltpu.PrefetchScalarGridSpec(
            num_scalar_prefetch=2,
            grid=(n // rows,),
            in_specs=[pl.BlockSpec((rows, D_MODEL), lambda i, *_: (i, 0)),
                      pl.BlockSpec((rows, LANES), lambda i, *_: (i, 0)),
                      pl.BlockSpec((1, D_MODEL), lambda i, *_: (0, 0)),
                      pl.BlockSpec(memory_space=pl.ANY)],
            out_specs=pl.BlockSpec((rows, D_MODEL), lambda i, *_: (i, 0)),
            scratch_shapes=[pltpu.VMEM((rows, D_MODEL), F32), pltpu.VMEM((rows, D_MODEL), F32),
                            pltpu.SemaphoreType.DMA(())],
        ),
        out_shape=jax.ShapeDtypeStruct((n, D_MODEL), F32),
        compiler_params=_cparams("arbitrary"),
        name=f"moe_combine_{rows}",
    )(dest[0], dest[1], x1, rt, gf, ys)


def _rope_tables(pos, xp):
    half = HEAD_DIM // 2
    inv = 1.0 / (xp.float32(ROPE_THETA) ** (xp.arange(half, dtype=xp.float32) * xp.float32(2.0 / HEAD_DIM)))
    ang = pos.astype(xp.float32)[:, None] * inv[None, :].astype(xp.float32)
    cos, sin = xp.cos(ang), xp.sin(ang)
    return xp.concatenate([cos, cos, cos, cos], axis=1), xp.concatenate([-sin, sin, -sin, sin], axis=1)


def _slot_dest(ids, pos, pstarts):
    ids = ids.astype(jnp.int32)
    experts = jnp.arange(N_EXPERTS, dtype=jnp.int32).reshape((N_EXPERTS,) + (1,) * ids.ndim)
    start = jnp.sum(jnp.where(ids[None] == experts, pstarts.reshape(experts.shape), 0), axis=0)
    return start + pos.astype(jnp.int32)


def kernel(x_prompt, x_sample, cache_kv_g0, cache_kv_g1, cache_kv_g2, norm1_g, w_in, b_in, a_ln_g, a_ln_b, w_spatial, b_spatial, w_a_proj, w_b_proj, w_o, norm2_g, w_group_router, b_group_router, w_expert_router, b_expert_router, w_gate, w_up, w_down, final_norm_g):
    x2 = x_prompt.reshape(T_PROMPT, D_MODEL)
    xs = x_sample.reshape(DEC_BATCH, D_MODEL)
    g1 = norm1_g[0][None, :]
    g2 = norm2_g[0][None, :]
    gf = final_norm_g[None, :]
    b_in2 = b_in[0][None, :]
    lng, lnb = a_ln_g[0][None, :], a_ln_b[0][None, :]

    causal = jnp.tril(jnp.ones((CHUNK, CHUNK), dtype=bool))
    ws_tril = jnp.where(causal[None], w_spatial[0], 0.0)
    wsp = jnp.concatenate([ws_tril[0::2], ws_tril[1::2]], axis=2).astype(BF16)
    bsp = jnp.repeat(b_spatial[0].T, A_GROUP_DIM, axis=1)
    ws0 = jnp.repeat(ws_tril[:, 0, 0].astype(BF16).astype(F32), A_GROUP_DIM)[None, :]
    bs0 = jnp.repeat(b_spatial[0][:, 0], A_GROUP_DIM)[None, :]
    w_router = jnp.zeros((D_MODEL, LANES), F32)
    w_router = w_router.at[:, :N_EXPERT_GROUPS].set(w_group_router[0])
    w_router = w_router.at[:, N_EXPERT_GROUPS:N_EXPERT_GROUPS + N_EXPERTS].set(w_expert_router[0])
    b_router = jnp.zeros((1, LANES), F32)
    b_router = b_router.at[0, :N_EXPERT_GROUPS].set(b_group_router[0])
    b_router = b_router.at[0, N_EXPERT_GROUPS:N_EXPERT_GROUPS + N_EXPERTS].set(b_expert_router[0])
    w_router = w_router.astype(BF16)
    ltri = jnp.tril(jnp.ones((TM, TM), BF16), -1)
    w_in_b, w_a_b, w_b_b, w_o_b = (w[0].astype(BF16) for w in (w_in, w_a_proj, w_b_proj, w_o))

    cos_p, sin_p = _rope_tables(np.arange(SEQ, dtype=np.int32), np)
    cos_s, sin_s = _rope_tables(jnp.full((1,), PAST_LEN, jnp.int32), jnp)

    uv, qkv0, qkv1, qkv2, gates, kvp0, kvp1, kvp2 = _inproj(
        x2, g1, w_in_b, b_in2, lng, lnb, cos_p, sin_p)
    ob = _attention((qkv0.reshape(BATCH, 1, SEQ, 3 * GROUP_W), qkv1, qkv2)).reshape(T_PROMPT, GROUP_W)
    x1_p, h2p_p, rt_p, rtt_p, counts_p = _merge(
        x2, uv, gates, ob, wsp, bsp, w_a_b, w_b_b, w_o_b, g2, w_router, b_router, ltri)

    z_s = _sample_inproj(xs, g1, w_in_b, b_in2)
    vrow, sa_s, qr_s, kr_s, kvs0, kvs1, kvs2 = _sample_pre(z_s, lng, lnb, ws0, bs0, cos_s, sin_s)
    per_head = lambda t: t.reshape(DEC_BATCH, 3, HEADS, HEAD_DIM)
    caches = [c.transpose(0, 1, 3, 4, 5, 2).reshape(DEC_BATCH, 2, HEADS, HEAD_DIM, c.shape[2])
              for c in (cache_kv_g0, cache_kv_g1, cache_kv_g2)]
    comb_s = _sample_attn(per_head(qr_s), per_head(kr_s), per_head(z_s[:, COL_VB:COL_VB + QK_W]), caches)
    comb_s = comb_s.reshape(DEC_BATCH, GROUP_W)
    x1_s, h2p_s, rt_s, counts = _sample_merge(
        xs, z_s, sa_s, comb_s, w_a_b, w_b_b, w_o_b, g2, w_router, b_router,
        ltri[:DEC_BATCH, :DEC_BATCH], counts_p)

    cnt = counts[0, :N_EXPERTS].astype(jnp.int32)
    padded = (cnt + MOE_TB - 1) // MOE_TB * MOE_TB
    pends = jnp.cumsum(padded)
    pstarts = pends - padded
    block_starts = jnp.arange(MOE_NB, dtype=jnp.int32) * MOE_TB
    block_e = jnp.minimum(jnp.sum((pends[None, :] <= block_starts[:, None]).astype(jnp.int32), axis=1),
                          N_EXPERTS - 1)
    nused = (pends[-1:] // MOE_TB).astype(jnp.int32)
    rtt_p = rtt_p.reshape(T_PROMPT // TM, ROUTE_ROWS, TM)
    dest_p = [_slot_dest(rtt_p[:, k], rtt_p[:, 2 + k], pstarts).reshape(T_PROMPT) for k in range(2)]
    dest_s = [_slot_dest(rt_s[:, k], rt_s[:, 2 + k], pstarts) for k in range(2)]

    rows = jnp.zeros((MOE_ROWS, HALF_D), U32)
    rows = _dispatch(dest_p, h2p_p, rows, TM)
    rows = _dispatch(dest_s, h2p_s, rows, DEC_BATCH)
    ys = _expert_ffn(block_e, nused, rows, w_gate[0], w_up[0], w_down[0])
    y_p = _combine(dest_p, x1_p, rt_p, gf, ys, TM)
    y_s = _combine(dest_s, x1_s, rt_s, gf, ys, DEC_BATCH)

    kv_shape = lambda n, w: (1, n, w, 2, HEADS, HEAD_DIM)
    window_rows = lambda t: t.reshape(1, BATCH, 2, HEADS, HEAD_DIM, t.shape[2]).transpose(0, 1, 5, 2, 3, 4)
    return (y_p.reshape(BATCH, SEQ, D_MODEL), y_s.reshape(DEC_BATCH, 1, D_MODEL),
            window_rows(kvp0), window_rows(kvp1), window_rows(kvp2),
            kvs0.reshape(kv_shape(DEC_BATCH, 1)), kvs1.reshape(kv_shape(DEC_BATCH, 1)),
            kvs2.reshape(kv_shape(DEC_BATCH, 1)), vrow.reshape(1, DEC_BATCH, 1, A_WIDTH))
```

```python
import functools

import jax
import jax.numpy as jnp
import numpy as np
from jax import lax
from jax.experimental import pallas as pl
from jax.experimental.pallas import tpu as pltpu

F32 = jnp.float32
BF16 = jnp.bfloat16
U32 = jnp.uint32

D_MODEL = 1024
BATCH = 2
SEQ = 8192
DEC_BATCH = 32
PAST_LEN = 8192
CHUNK = 128
A_GROUPS = 8
A_GROUP_DIM = 64
A_WIDTH = 512
HEAD_DIM = 64
HEADS = 4
GROUP_W = HEADS * HEAD_DIM
DILATIONS = (1, 4, 16)
SPAN = 128
QK_W = 768
IN_WIDTH = 5376
COL_U, COL_V, COL_Q, COL_K, COL_VB, COL_GA, COL_GB = 0, 512, 1024, 1792, 2560, 3328, 4352
N_EXPERT_GROUPS = 4
EXPERTS_PER_GROUP = 8
N_EXPERTS = 32
D_EXPERT = 512
ROPE_THETA = 10000.0
EPS = 1e-6

LANES = 128
T_PROMPT = BATCH * SEQ
TM = 1024
TM_IN = 1024
WINDOWS = tuple(min(SPAN * d, SEQ) for d in DILATIONS)
TA = SPAN * max(DILATIONS)
ATT_UNROLL = 2
MOE_TB = 512
ROW_DMA_UNROLL = 8
ROUTE_ROWS = 8
N_SLOTS = 2 * (T_PROMPT + DEC_BATCH)
MOE_NB = -(-N_SLOTS // MOE_TB) + N_EXPERTS
MOE_ROWS = MOE_NB * MOE_TB
HALF_D = D_MODEL // 2
NEG = -1e30
VMEM_LIMIT = 56 * 1024 * 1024


def _cparams(*sem):
    return pltpu.CompilerParams(dimension_semantics=sem, vmem_limit_bytes=VMEM_LIMIT)


def _rms(x, g):
    return x * lax.rsqrt(jnp.mean(x * x, axis=-1, keepdims=True) + EPS) * g


def _layer_norm(x, g, b):
    mu = jnp.mean(x, axis=-1, keepdims=True)
    xc = x - mu
    var = jnp.mean(xc * xc, axis=-1, keepdims=True)
    return xc * lax.rsqrt(var + EPS) * g + b


def _rope_chunk(x, cos, sin_signed):
    lane = lax.broadcasted_iota(jnp.int32, x.shape, 1)
    first_half = (lane % HEAD_DIM) < (HEAD_DIM // 2)
    swapped = jnp.where(first_half, pltpu.roll(x, LANES - HEAD_DIM // 2, 1), pltpu.roll(x, HEAD_DIM // 2, 1))
    return x * cos + swapped * sin_signed


def _pack_bf16_pair(h):
    lo = lax.bitcast_convert_type(h[:, :HALF_D].astype(BF16).astype(F32), U32)
    hi = lax.bitcast_convert_type(h[:, HALF_D:].astype(BF16).astype(F32), U32)
    return (hi & jnp.uint32(0xFFFF0000)) | (lo >> 16)


def _unpack_bf16_pair(p):
    lo = lax.bitcast_convert_type(p << 16, F32).astype(BF16)
    hi = lax.bitcast_convert_type(p & jnp.uint32(0xFFFF0000), F32).astype(BF16)
    return lo, hi


def _route(logits, prior_counts, ltri):
    rows = logits.shape[0]
    lane = lax.broadcasted_iota(jnp.int32, (rows, LANES), 1).astype(F32)
    is_g = lane < N_EXPERT_GROUPS
    gl = jnp.where(is_g, logits, NEG)
    gmax = jnp.max(gl, axis=1, keepdims=True)
    grp = jnp.min(jnp.where(gl == gmax, lane, float(LANES)), axis=1, keepdims=True)
    p_grp = 1.0 / jnp.sum(jnp.where(is_g, jnp.exp(gl - gmax), 0.0), axis=1, keepdims=True)
    lo = N_EXPERT_GROUPS + grp * EXPERTS_PER_GROUP
    el = jnp.where((lane >= lo) & (lane < lo + EXPERTS_PER_GROUP), logits, NEG)
    v1 = jnp.max(el, axis=1, keepdims=True)
    i1 = jnp.min(jnp.where(el == v1, lane, float(LANES)), axis=1, keepdims=True)
    el2 = jnp.where(lane == i1, NEG, el)
    v2 = jnp.max(el2, axis=1, keepdims=True)
    i2 = jnp.min(jnp.where(el2 == v2, lane, float(LANES)), axis=1, keepdims=True)
    t = jnp.exp(v2 - v1)
    gate1 = p_grp / (1.0 + t)
    gate2 = p_grp * t / (1.0 + t)
    e1 = i1 - N_EXPERT_GROUPS
    e2 = i2 - N_EXPERT_GROUPS
    hit1 = lane == e1
    hit2 = lane == e2
    onehot = jnp.where(hit1 | hit2, 1.0, 0.0)
    rank = jnp.dot(ltri, onehot.astype(BF16), preferred_element_type=F32) + prior_counts
    pos1 = jnp.sum(jnp.where(hit1, rank, 0.0), axis=1, keepdims=True)
    pos2 = jnp.sum(jnp.where(hit2, rank, 0.0), axis=1, keepdims=True)
    slab = jnp.where(lane == 0, e1, jnp.where(lane == 1, e2, jnp.where(lane == 2, pos1, jnp.where(
        lane == 3, pos2, jnp.where(lane == 4, gate1, jnp.where(lane == 5, gate2, 0.0))))))
    return slab, prior_counts + jnp.sum(onehot, axis=0, keepdims=True)


def _inproj_kernel(x_ref, g1_ref, w_ref, b_ref, lng_ref, lnb_ref, cos_ref, sin_ref,
                   uv_ref, qkv0_ref, qkv1_ref, qkv2_ref, gates_ref, kv0_ref, kv1_ref, kv2_ref, de_ref):
    tm = x_ref.shape[0]
    tiles_per_seq = SEQ // tm
    tile_in_seq = pl.program_id(0) % tiles_per_seq
    hb = _rms(x_ref[...], g1_ref[...]).astype(BF16)

    def seg(lo, width):
        return jnp.dot(hb, w_ref[:, lo:lo + width], preferred_element_type=F32) + b_ref[:, lo:lo + width]

    uv_ref[:, :A_WIDTH] = jax.nn.gelu(seg(COL_U, A_WIDTH)).astype(BF16)
    uv_ref[:, A_WIDTH:] = _layer_norm(jax.nn.gelu(seg(COL_V, A_WIDTH)), lng_ref[...], lnb_ref[...]).astype(BF16)
    gates_ref[:, :D_MODEL] = jax.nn.sigmoid(seg(COL_GA, D_MODEL)).astype(BF16)
    gates_ref[:, D_MODEL:] = jax.nn.sigmoid(seg(COL_GB, D_MODEL)).astype(BF16)

    cos = cos_ref[...]
    sin = sin_ref[...]
    qkv_refs = (qkv0_ref, qkv1_ref, qkv2_ref)
    kv_refs = (kv0_ref, kv1_ref, kv2_ref)
    for g, d in enumerate(DILATIONS):
        q = seg(COL_Q + g * GROUP_W, GROUP_W)
        k = seg(COL_K + g * GROUP_W, GROUP_W)
        v = seg(COL_VB + g * GROUP_W, GROUP_W)
        chunks = GROUP_W // LANES
        for c in range(chunks):
            sl = slice(c * LANES, (c + 1) * LANES)
            de_ref[c] = _rope_chunk(q[:, sl], cos, sin) * (HEAD_DIM ** -0.5)
            de_ref[chunks + c] = _rope_chunk(k[:, sl], cos, sin)
            de_ref[2 * chunks + c] = v[:, sl]
        for c in range(3 * chunks):
            sl = slice(c * LANES, (c + 1) * LANES)
            if d == 1:
                qkv_refs[g][:, sl] = de_ref[c].astype(BF16)
            else:
                for r in range(d):
                    qkv_refs[g][0, r, :, sl] = de_ref[c, pl.ds(r, tm // d, stride=d), :].astype(BF16)
        kv_rows = kv_refs[g].shape[2]
        first = tiles_per_seq - WINDOWS[g] // kv_rows

        @pl.when(tile_in_seq >= first)
        def _(kv_ref=kv_refs[g], kv_rows=kv_rows, chunks=chunks):
            for c in range(2 * chunks):
                kv_ref[0, c * LANES:(c + 1) * LANES, :] = de_ref[chunks + c, tm - kv_rows:, :].T


def _inproj(x2, g1, w_bf, b_in, lng, lnb, cos_t, sin_t):
    tm = TM_IN
    tiles_per_seq = SEQ // tm
    n_tiles = T_PROMPT // tm
    const = lambda i: (0, 0)

    def kv_spec(w):
        rows = min(w, tm)
        first = tiles_per_seq - w // rows
        return pl.BlockSpec((1, 2 * GROUP_W, rows),
                            lambda i: (i // tiles_per_seq, 0, jnp.maximum(i % tiles_per_seq - first, 0)))

    def regrouped_spec(d):
        return pl.BlockSpec((1, d, tm // d, 3 * GROUP_W), lambda i: (i // tiles_per_seq, 0, i % tiles_per_seq, 0))

    return pl.pallas_call(
        _inproj_kernel,
        grid=(n_tiles,),
        in_specs=[
            pl.BlockSpec((tm, D_MODEL), lambda i: (i, 0)),
            pl.BlockSpec((1, D_MODEL), const),
            pl.BlockSpec((D_MODEL, IN_WIDTH), const, pipeline_mode=pl.Buffered(1)),
            pl.BlockSpec((1, IN_WIDTH), const),
            pl.BlockSpec((1, A_WIDTH), const),
            pl.BlockSpec((1, A_WIDTH), const),
            pl.BlockSpec((tm, LANES), lambda i: (i % tiles_per_seq, 0)),
            pl.BlockSpec((tm, LANES), lambda i: (i % tiles_per_seq, 0)),
        ],
        out_specs=[
            pl.BlockSpec((tm, 2 * A_WIDTH), lambda i: (i, 0)),
            pl.BlockSpec((tm, 3 * GROUP_W), lambda i: (i, 0)),
            regrouped_spec(DILATIONS[1]),
            regrouped_spec(DILATIONS[2]),
            pl.BlockSpec((tm, 2 * D_MODEL), lambda i: (i, 0)),
            kv_spec(WINDOWS[0]), kv_spec(WINDOWS[1]), kv_spec(WINDOWS[2]),
        ],
        out_shape=[
            jax.ShapeDtypeStruct((T_PROMPT, 2 * A_WIDTH), BF16),
            jax.ShapeDtypeStruct((T_PROMPT, 3 * GROUP_W), BF16),
            jax.ShapeDtypeStruct((BATCH, DILATIONS[1], SEQ // DILATIONS[1], 3 * GROUP_W), BF16),
            jax.ShapeDtypeStruct((BATCH, DILATIONS[2], SEQ // DILATIONS[2], 3 * GROUP_W), BF16),
            jax.ShapeDtypeStruct((T_PROMPT, 2 * D_MODEL), BF16),
            jax.ShapeDtypeStruct((BATCH, 2 * GROUP_W, WINDOWS[0]), F32),
            jax.ShapeDtypeStruct((BATCH, 2 * GROUP_W, WINDOWS[1]), F32),
            jax.ShapeDtypeStruct((BATCH, 2 * GROUP_W, WINDOWS[2]), F32),
        ],
        scratch_shapes=[pltpu.VMEM((3 * GROUP_W // LANES, tm, LANES), F32)],
        compiler_params=_cparams("arbitrary"),
        name="inproj",
    )(x2, g1, w_bf, b_in, lng, lnb, cos_t, sin_t)


def _attn_kernel(c0_ref, p0_ref, c1_ref, p1_ref, c2_ref, p2_ref, o_ref, acc_ref, m_ref, l_ref):
    n = pl.program_id(1)
    qi = lax.broadcasted_iota(jnp.int32, (HEADS * SPAN, 2 * SPAN), 0) % SPAN
    ki = lax.broadcasted_iota(jnp.int32, (HEADS * SPAN, 2 * SPAN), 1)
    band = (ki >= qi) & (ki <= qi + SPAN)
    band_first = band & ((ki >= SPAN) | (n > 0))
    q_head = lax.broadcasted_iota(jnp.int32, (SPAN, GROUP_W), 1) // HEAD_DIM
    kv_head = lax.broadcasted_iota(jnp.int32, (2 * SPAN, GROUP_W), 1) // HEAD_DIM
    contract_last = (((1,), (1,)), ((), ()))
    k_cols = slice(GROUP_W, 2 * GROUP_W)
    v_cols = slice(2 * GROUP_W, 3 * GROUP_W)

    def attend(q, keys, vals, mask, tok_rows, fresh):
        q_heads = jnp.concatenate([jnp.where(q_head == h, q, jnp.zeros_like(q)) for h in range(HEADS)], axis=0)
        s = jnp.where(mask, lax.dot_general(q_heads, keys, contract_last, preferred_element_type=F32), NEG)
        m = jnp.max(s, axis=1, keepdims=True)
        p = jnp.exp(s - m)
        l = jnp.sum(p, axis=1, keepdims=True)
        pb = p.astype(BF16)
        p_heads = jnp.concatenate([pb[h * SPAN:(h + 1) * SPAN] for h in range(HEADS)], axis=1)
        v_heads = jnp.concatenate([jnp.where(kv_head == h, vals, jnp.zeros_like(vals)) for h in range(HEADS)], axis=0)
        acc = jnp.dot(p_heads, v_heads, preferred_element_type=F32)
        m_b = jnp.zeros((SPAN, GROUP_W), F32)
        l_b = jnp.zeros((SPAN, GROUP_W), F32)
        for h in range(HEADS):
            rows = slice(h * SPAN, (h + 1) * SPAN)
            m_b = jnp.where(q_head == h, m[rows], m_b)
            l_b = jnp.where(q_head == h, l[rows], l_b)
        for c in range(GROUP_W // LANES):
            sl = slice(c * LANES, (c + 1) * LANES)
            if fresh:
                m_ref[c, tok_rows, :] = m_b[:, sl]
                l_ref[c, tok_rows, :] = l_b[:, sl]
                acc_ref[c, tok_rows, :] = acc[:, sl]
                continue
            m_old = m_ref[c, tok_rows, :]
            m_new = jnp.maximum(m_old, m_b[:, sl])
            a_old = jnp.exp(m_old - m_new)
            a_blk = jnp.exp(m_b[:, sl] - m_new)
            l_ref[c, tok_rows, :] = a_old * l_ref[c, tok_rows, :] + a_blk * l_b[:, sl]
            acc_ref[c, tok_rows, :] = a_old * acc_ref[c, tok_rows, :] + a_blk * acc[:, sl]
            m_ref[c, tok_rows, :] = m_new

    def stream(c_ref, p_ref, r, d, fresh):
        n_blocks = c_ref.shape[2] // SPAN

        def tok_rows(j):
            start = j * (SPAN * d) + r
            return pl.ds(start, SPAN) if d == 1 else pl.ds(start, SPAN, stride=d)

        keys = jnp.concatenate([p_ref[0, r, :, k_cols], c_ref[0, r, :SPAN, k_cols]], axis=0)
        vals = jnp.concatenate([p_ref[0, r, :, v_cols], c_ref[0, r, :SPAN, v_cols]], axis=0)
        attend(c_ref[0, r, :SPAN, :GROUP_W], keys, vals, band_first, tok_rows(0), fresh)

        def later_block(j, carry):
            q_rows = pl.ds(pl.multiple_of(j * SPAN, SPAN), SPAN)
            kv_rows = pl.ds(pl.multiple_of((j - 1) * SPAN, SPAN), 2 * SPAN)
            attend(c_ref[0, r, q_rows, :GROUP_W], c_ref[0, r, kv_rows, k_cols], c_ref[0, r, kv_rows, v_cols],
                   band, tok_rows(j), fresh)
            return carry

        if n_blocks > 1:
            lax.fori_loop(1, n_blocks, later_block, 0, unroll=3)

    groups = sorted(zip(DILATIONS, ((c0_ref, p0_ref), (c1_ref, p1_ref), (c2_ref, p2_ref))), key=lambda g: -g[0])
    for idx, (d, (c_ref, p_ref)) in enumerate(groups):
        fresh = idx == 0
        if d == 1:
            stream(c_ref, p_ref, 0, d, fresh)
        else:
            lax.fori_loop(0, d, lambda r, carry, c_ref=c_ref, p_ref=p_ref, d=d, fresh=fresh:
                          (stream(c_ref, p_ref, r, d, fresh), carry)[1], 0,
                          unroll=ATT_UNROLL if c_ref.shape[2] == SPAN else 1)

    for c in range(GROUP_W // LANES):
        o_ref[0, :, c * LANES:(c + 1) * LANES] = (acc_ref[c] / l_ref[c]).astype(BF16)


def _attention(qkv_by_group):
    in_specs, args = [], []
    for qkv, d in zip(qkv_by_group, DILATIONS):
        rows = TA // d
        blocks_per_tile = rows // SPAN
        in_specs.append(pl.BlockSpec((1, d, rows, 3 * GROUP_W), lambda b, n: (b, 0, n, 0)))
        in_specs.append(pl.BlockSpec((1, d, SPAN, 3 * GROUP_W),
                                     lambda b, n, k=blocks_per_tile: (b, 0, jnp.maximum(n * k - 1, 0), 0)))
        args += [qkv, qkv]
    return pl.pallas_call(
        _attn_kernel,
        grid=(BATCH, SEQ // TA),
        in_specs=in_specs,
        out_specs=pl.BlockSpec((1, TA, GROUP_W), lambda b, n: (b, n, 0)),
        out_shape=jax.ShapeDtypeStruct((BATCH, SEQ, GROUP_W), BF16),
        scratch_shapes=[pltpu.VMEM((GROUP_W // LANES, TA, LANES), F32)] * 3,
        compiler_params=_cparams("arbitrary", "arbitrary"),
        name="attn",
    )(*args)


def _merge_kernel(x_ref, uv_ref, gates_ref, ob_ref,
                  wsp_ref, bsp_ref, wa_ref, wb_ref, wo_ref, g2_ref, wr_ref, br_ref, ltri_ref,
                  x1_ref, h2p_ref, rt_ref, rtt_ref, cnt_ref, run_ref):
    @pl.when(pl.program_id(0) == 0)
    def _():
        run_ref[...] = jnp.zeros_like(run_ref)

    lane = lax.broadcasted_iota(jnp.int32, (CHUNK, LANES), 1)
    left = lane < A_GROUP_DIM
    zero = jnp.zeros((CHUNK, LANES), BF16)
    sa_chunks = []
    for c in range(TM // CHUNK):
        rows = slice(c * CHUNK, (c + 1) * CHUNK)
        pairs = []
        for p in range(A_GROUPS // 2):
            vp = uv_ref[rows, A_WIDTH + p * LANES:A_WIDTH + (p + 1) * LANES]
            rhs = jnp.concatenate([jnp.where(left, vp, zero), jnp.where(left, zero, vp)], axis=0)
            pairs.append(jnp.dot(wsp_ref[p], rhs, preferred_element_type=F32))
        mixed = jnp.concatenate(pairs, axis=1) + bsp_ref[...]
        sa_chunks.append((uv_ref[rows, :A_WIDTH].astype(F32) * mixed).astype(BF16))
    s_a = jnp.concatenate(sa_chunks, axis=0)

    a = jnp.dot(s_a, wa_ref[...], preferred_element_type=F32)
    b = jnp.dot(ob_ref[...], wb_ref[...], preferred_element_type=F32)
    merged = gates_ref[:, :D_MODEL].astype(F32) * a + gates_ref[:, D_MODEL:].astype(F32) * b
    x1 = x_ref[...] + jnp.dot(merged.astype(BF16), wo_ref[...], preferred_element_type=F32)
    x1_ref[...] = x1

    h2 = _rms(x1, g2_ref[...])
    h2p_ref[...] = _pack_bf16_pair(h2)
    logits = jnp.dot(h2.astype(BF16), wr_ref[...], preferred_element_type=F32) + br_ref[...]
    slab, counts = _route(logits, run_ref[...], ltri_ref[...])
    rt_ref[...] = slab
    rtt_ref[...] = slab.T[:ROUTE_ROWS, :]
    run_ref[...] = counts
    cnt_ref[...] = counts


def _merge(x2, uv, gates, ob, wsp, bsp, wa, wb, wo, g2, wr, br, ltri):
    n_tiles = T_PROMPT // TM
    tile = lambda w: pl.BlockSpec((TM, w), lambda i: (i, 0))
    full = lambda a: pl.BlockSpec(a.shape, lambda i: (0,) * a.ndim)
    return pl.pallas_call(
        _merge_kernel,
        grid=(n_tiles,),
        in_specs=[tile(D_MODEL), tile(2 * A_WIDTH), tile(2 * D_MODEL), tile(GROUP_W),
                  full(wsp), full(bsp), full(wa), full(wb), full(wo), full(g2), full(wr), full(br),
                  full(ltri)],
        out_specs=[tile(D_MODEL), tile(HALF_D), tile(LANES), pl.BlockSpec((ROUTE_ROWS, TM), lambda i: (i, 0)),
                   pl.BlockSpec((1, LANES), lambda i: (0, 0))],
        out_shape=[
            jax.ShapeDtypeStruct((T_PROMPT, D_MODEL), F32),
            jax.ShapeDtypeStruct((T_PROMPT, HALF_D), U32),
            jax.ShapeDtypeStruct((T_PROMPT, LANES), F32),
            jax.ShapeDtypeStruct((n_tiles * ROUTE_ROWS, TM), F32),
            jax.ShapeDtypeStruct((1, LANES), F32),
        ],
        scratch_shapes=[pltpu.VMEM((1, LANES), F32)],
        compiler_params=_cparams("arbitrary"),
        name="merge_route",
    )(x2, uv, gates, ob, wsp, bsp, wa, wb, wo, g2, wr, br, ltri)


SAMPLE_COLS = 768


def _sample_inproj_kernel(x_ref, g1_ref, w_ref, b_ref, z_ref):
    hb = _rms(x_ref[...], g1_ref[...]).astype(BF16)
    z_ref[...] = jnp.dot(hb, w_ref[...], preferred_element_type=F32) + b_ref[...]


def _sample_inproj(xs, g1, w_in, b_in):
    const = lambda j: (0, 0)
    return pl.pallas_call(
        _sample_inproj_kernel,
        grid=(IN_WIDTH // SAMPLE_COLS,),
        in_specs=[pl.BlockSpec((DEC_BATCH, D_MODEL), const), pl.BlockSpec((1, D_MODEL), const),
                  pl.BlockSpec((D_MODEL, SAMPLE_COLS), lambda j: (0, j)),
                  pl.BlockSpec((1, SAMPLE_COLS), lambda j: (0, j))],
        out_specs=pl.BlockSpec((DEC_BATCH, SAMPLE_COLS), lambda j: (0, j)),
        out_shape=jax.ShapeDtypeStruct((DEC_BATCH, IN_WIDTH), F32),
        compiler_params=_cparams("arbitrary"),
        name="sample_inproj",
    )(xs, g1, w_in, b_in)


def _sample_pre_kernel(z_ref, lng_ref, lnb_ref, ws0_ref, bs0_ref, cos_ref, sin_ref,
                       vrow_ref, sa_ref, qr_ref, kr_ref, kv0_ref, kv1_ref, kv2_ref):
    u = jax.nn.gelu(z_ref[:, COL_U:COL_U + A_WIDTH])
    va = _layer_norm(jax.nn.gelu(z_ref[:, COL_V:COL_V + A_WIDTH]), lng_ref[...], lnb_ref[...])
    vrow_ref[...] = va
    sa_ref[...] = u * (ws0_ref[...] * va.astype(BF16).astype(F32) + bs0_ref[...])
    cos, sin = cos_ref[...], sin_ref[...]
    for g, kv_ref in enumerate((kv0_ref, kv1_ref, kv2_ref)):
        for c in range(GROUP_W // LANES):
            off = g * GROUP_W + c * LANES
            qr_ref[:, off:off + LANES] = (_rope_chunk(z_ref[:, COL_Q + off:COL_Q + off + LANES], cos, sin)
                                          * (HEAD_DIM ** -0.5))
            kr = _rope_chunk(z_ref[:, COL_K + off:COL_K + off + LANES], cos, sin)
            kr_ref[:, off:off + LANES] = kr
            kv_ref[:, c * LANES:(c + 1) * LANES] = kr
        kv_ref[:, GROUP_W:] = z_ref[:, COL_VB + g * GROUP_W:COL_VB + (g + 1) * GROUP_W]


def _sample_pre(z, lng, lnb, ws0, bs0, cos_s, sin_s):
    row = lambda w: jax.ShapeDtypeStruct((DEC_BATCH, w), F32)
    return pl.pallas_call(
        _sample_pre_kernel,
        out_shape=[row(A_WIDTH), row(A_WIDTH), row(QK_W), row(QK_W),
                   row(2 * GROUP_W), row(2 * GROUP_W), row(2 * GROUP_W)],
        compiler_params=pltpu.CompilerParams(vmem_limit_bytes=VMEM_LIMIT),
        name="sample_pre",
    )(z, lng, lnb, ws0, bs0, cos_s, sin_s)


def _sample_attn_kernel(q_ref, k_ref, v_ref, c0_ref, c1_ref, c2_ref, comb_ref):
    as_operand = lambda t: t.astype(BF16).astype(F32)
    contract_last = (((1,), (1,)), ((), ()))
    pad_rows = 8 // HEADS
    groups = tuple(zip((c0_ref, c1_ref, c2_ref), DILATIONS))
    pairs = [(h, g) for h in range(HEADS) for g in range(len(groups))]
    scores = {}
    for h, g in pairs:
        c_ref, d = groups[g]
        q_rows = jnp.concatenate([q_ref[0, g]] * pad_rows, axis=0).astype(BF16)
        s = jnp.dot(q_rows, c_ref[0, 0, h].astype(BF16), preferred_element_type=F32)[h:h + 1]
        if d > 1:
            pos = lax.broadcasted_iota(jnp.int32, s.shape, 1)
            s = jnp.where(pos % d == 0, s, NEG)
        scores[h, g] = s
    probs = {}
    for h, g in pairs:
        s = scores[h, g]
        qn, kn = (as_operand(r[0, g, h:h + 1, :]) for r in (q_ref, k_ref))
        s_n = jnp.sum(qn * kn, axis=1, keepdims=True)
        m = jnp.maximum(jnp.max(s, axis=1, keepdims=True), s_n)
        lse = m + jnp.log(jnp.sum(jnp.exp(s - m), axis=1, keepdims=True) + jnp.exp(s_n - m))
        probs[h, g] = (jnp.broadcast_to(jnp.exp(s - lse), (8, s.shape[1])).astype(BF16),
                       as_operand(jnp.exp(s_n - lse)), lse)
    outs = {}
    for h, g in pairs:
        p_rows, p_n, _ = probs[h, g]
        o = lax.dot_general(p_rows, groups[g][0][0, 1, h].astype(BF16), contract_last, preferred_element_type=F32)
        outs[h, g] = o[0:1] + p_n * as_operand(v_ref[0, g, h:h + 1, :])
    for h in range(HEADS):
        lses = [probs[h, g][2] for g in range(len(groups))]
        mx = jnp.maximum(jnp.maximum(lses[0], lses[1]), lses[2])
        ws = [jnp.exp(t - mx) for t in lses]
        tot = ws[0] + ws[1] + ws[2]
        comb_ref[0, h:h + 1, :] = (ws[0] * outs[h, 0] + ws[1] * outs[h, 1] + ws[2] * outs[h, 2]) / tot


def _sample_attn(qt, kt, vt, caches):
    new_spec = pl.BlockSpec((1, 3, HEADS, HEAD_DIM), lambda i: (i, 0, 0, 0))
    cache_spec = lambda c: pl.BlockSpec((1,) + c.shape[1:], lambda i: (i, 0, 0, 0, 0))
    return pl.pallas_call(
        _sample_attn_kernel,
        grid=(DEC_BATCH,),
        in_specs=[new_spec, new_spec, new_spec, cache_spec(caches[0]), cache_spec(caches[1]), cache_spec(caches[2])],
        out_specs=pl.BlockSpec((1, HEADS, HEAD_DIM), lambda i: (i, 0, 0)),
        out_shape=jax.ShapeDtypeStruct((DEC_BATCH, HEADS, HEAD_DIM), F32),
        compiler_params=_cparams("arbitrary"),
        name="sample_attn",
    )(qt, kt, vt, caches[0], caches[1], caches[2])


def _sample_merge_kernel(x_ref, z_ref, sa_ref, comb_ref, wa_ref, wb_ref, wo_ref, g2_ref, wr_ref, br_ref,
                         ltri_ref, cnt_in_ref, x1_ref, h2p_ref, rt_ref, cnt_ref):
    dot = lambda p, q: jnp.dot(p.astype(BF16), q, preferred_element_type=F32)
    a = dot(sa_ref[...], wa_ref[...])
    b = dot(comb_ref[...], wb_ref[...])
    merged = (jax.nn.sigmoid(z_ref[:, COL_GA:COL_GA + D_MODEL]) * a
              + jax.nn.sigmoid(z_ref[:, COL_GB:COL_GB + D_MODEL]) * b)
    x1 = x_ref[...] + dot(merged, wo_ref[...])
    x1_ref[...] = x1
    h2 = _rms(x1, g2_ref[...])
    h2p_ref[...] = _pack_bf16_pair(h2)
    logits = dot(h2, wr_ref[...]) + br_ref[...]
    slab, counts = _route(logits, cnt_in_ref[...], ltri_ref[...])
    rt_ref[...] = slab
    cnt_ref[...] = counts


def _sample_merge(xs, z, s_a, comb, wa, wb, wo, g2, wr, br, ltri, counts):
    return pl.pallas_call(
        _sample_merge_kernel,
        out_shape=[jax.ShapeDtypeStruct((DEC_BATCH, D_MODEL), F32),
                   jax.ShapeDtypeStruct((DEC_BATCH, HALF_D), U32),
                   jax.ShapeDtypeStruct((DEC_BATCH, LANES), F32),
                   jax.ShapeDtypeStruct((1, LANES), F32)],
        compiler_params=pltpu.CompilerParams(vmem_limit_bytes=VMEM_LIMIT),
        name="sample_merge_route",
    )(xs, z, s_a, comb, wa, wb, wo, g2, wr, br, ltri, counts)


def _dispatch_kernel(dest0_ref, dest1_ref, h_ref, xs_in_ref, xs_ref, sem):
    del xs_in_ref
    rows = h_ref.shape[0]
    base = pl.program_id(0) * rows

    def row_copy(t, dest_ref):
        return pltpu.make_async_copy(h_ref.at[pl.ds(t, 1)], xs_ref.at[pl.ds(dest_ref[base + t], 1)], sem)

    def start(t, c):
        row_copy(t, dest0_ref).start()
        row_copy(t, dest1_ref).start()
        return c

    lax.fori_loop(0, rows, start, 0, unroll=ROW_DMA_UNROLL)
    for _ in range(2):
        pltpu.make_async_copy(h_ref, xs_ref.at[pl.ds(0, rows)], sem).wait()


def _dispatch(dest, h2p, xs, rows):
    n = h2p.shape[0]
    return pl.pallas_call(
        _dispatch_kernel,
        grid_spec=pltpu.PrefetchScalarGridSpec(
            num_scalar_prefetch=2,
            grid=(n // rows,),
            in_specs=[pl.BlockSpec((rows, HALF_D), lambda i, *_: (i, 0)), pl.BlockSpec(memory_space=pl.ANY)],
            out_specs=pl.BlockSpec(memory_space=pl.ANY),
            scratch_shapes=[pltpu.SemaphoreType.DMA(())],
        ),
        out_shape=jax.ShapeDtypeStruct(xs.shape, xs.dtype),
        input_output_aliases={3: 0},
        compiler_params=_cparams("arbitrary"),
        name=f"moe_dispatch_{rows}",
    )(dest[0], dest[1], h2p, xs)


def _ffn_kernel(be_ref, nused_ref, nexte_ref, wslot_ref, xs_ref, wg_ref, wu_ref, wd_ref, ys_ref,
                wg_b, wu_b, wd_b, wg_f, wu_f, wd_f, wsem):
    i = pl.program_id(0)
    live = i < nused_ref[0]
    new_expert = jnp.logical_and(live, jnp.logical_or(i == 0, be_ref[i] != be_ref[jnp.maximum(i - 1, 0)]))

    def weight_copies(e, slot):
        return [pltpu.make_async_copy(src.at[e], dst.at[slot], wsem.at[slot])
                for src, dst in ((wg_ref, wg_f), (wu_ref, wu_f), (wd_ref, wd_f))]

    @pl.when(i == 0)
    def _():
        for cp in weight_copies(be_ref[0], 0):
            cp.start()

    @pl.when(new_expert)
    def _():
        slot = wslot_ref[i]
        for cp in weight_copies(be_ref[i], slot):
            cp.wait()
        wg_b[...] = wg_f[slot].astype(BF16)
        wu_b[...] = wu_f[slot].astype(BF16)
        wd_b[...] = wd_f[slot].astype(BF16)

        @pl.when(nexte_ref[i] >= 0)
        def _():
            for cp in weight_copies(nexte_ref[i], 1 - slot):
                cp.start()

    @pl.when(live)
    def _():
        lo, hi = _unpack_bf16_pair(xs_ref[...])
        gate = (jnp.dot(lo, wg_b[:HALF_D, :], preferred_element_type=F32)
                + jnp.dot(hi, wg_b[HALF_D:, :], preferred_element_type=F32))
        up = (jnp.dot(lo, wu_b[:HALF_D, :], preferred_element_type=F32)
              + jnp.dot(hi, wu_b[HALF_D:, :], preferred_element_type=F32))
        hid = (jax.nn.silu(gate) * up).astype(BF16)
        ys_ref[...] = jnp.dot(hid, wd_b[...], preferred_element_type=F32)

    @pl.when(i >= nused_ref[0])
    def _():
        ys_ref[...] = jnp.zeros_like(ys_ref)


def _expert_ffn(block_e, nused, xs, w_gate, w_up, w_down):
    idx = jnp.arange(MOE_NB, dtype=jnp.int32)
    change = ((block_e != jnp.roll(block_e, 1)) | (idx == 0)) & (idx < nused[0])
    wslot = ((jnp.cumsum(change.astype(jnp.int32)) - 1) & 1).astype(jnp.int32)
    change_at = jnp.where(change, idx, MOE_NB)
    next_change = jnp.flip(lax.cummin(jnp.flip(jnp.concatenate([change_at[1:], jnp.full((1,), MOE_NB, jnp.int32)]))))
    nexte = jnp.where(next_change < MOE_NB, block_e[jnp.minimum(next_change, MOE_NB - 1)], -1).astype(jnp.int32)
    hbm = pl.BlockSpec(memory_space=pl.ANY)
    return pl.pallas_call(
        _ffn_kernel,
        grid_spec=pltpu.PrefetchScalarGridSpec(
            num_scalar_prefetch=4,
            grid=(MOE_NB,),
            in_specs=[pl.BlockSpec((MOE_TB, HALF_D), lambda i, *_: (i, 0)), hbm, hbm, hbm],
            out_specs=pl.BlockSpec((MOE_TB, D_MODEL), lambda i, *_: (i, 0)),
            scratch_shapes=[pltpu.VMEM((D_MODEL, D_EXPERT), BF16), pltpu.VMEM((D_MODEL, D_EXPERT), BF16),
                            pltpu.VMEM((D_EXPERT, D_MODEL), BF16),
                            pltpu.VMEM((2, D_MODEL, D_EXPERT), F32), pltpu.VMEM((2, D_MODEL, D_EXPERT), F32),
                            pltpu.VMEM((2, D_EXPERT, D_MODEL), F32), pltpu.SemaphoreType.DMA((2,))],
        ),
        out_shape=jax.ShapeDtypeStruct((MOE_ROWS, D_MODEL), F32),
        compiler_params=_cparams("arbitrary"),
        name="moe_ffn",
    )(block_e, nused, nexte, wslot, xs, w_gate, w_up, w_down)


def _combine_kernel(dest0_ref, dest1_ref, x1_ref, rt_ref, gf_ref, ys_ref, y_ref, ya, yb, sem):
    rows = x1_ref.shape[0]
    base = pl.program_id(0) * rows

    def row_copy(t, dest_ref, buf):
        return pltpu.make_async_copy(ys_ref.at[pl.ds(dest_ref[base + t], 1)], buf.at[pl.ds(t, 1)], sem)

    def start(t, c):
        row_copy(t, dest0_ref, ya).start()
        row_copy(t, dest1_ref, yb).start()
        return c

    lax.fori_loop(0, rows, start, 0, unroll=ROW_DMA_UNROLL)
    pltpu.make_async_copy(ys_ref.at[pl.ds(0, rows)], ya, sem).wait()
    pltpu.make_async_copy(ys_ref.at[pl.ds(0, rows)], yb, sem).wait()
    x2 = x1_ref[...] + rt_ref[:, 4:5] * ya[...] + rt_ref[:, 5:6] * yb[...]
    y_ref[...] = _rms(x2, gf_ref[...])


def _combine(dest, x1, rt, gf, ys, rows):
    n = x1.shape[0]
    return pl.pallas_call(
        _combine_kernel,
        grid_spec=pltpu.PrefetchScalarGridSpec(
            num_scalar_prefetch=2,
            grid=(n // rows,),
            in_specs=[pl.BlockSpec((rows, D_MODEL), lambda i, *_: (i, 0)),
                      pl.BlockSpec((rows, LANES), lambda i, *_: (i, 0)),
                      pl.BlockSpec((1, D_MODEL), lambda i, *_: (0, 0)),
                      pl.BlockSpec(memory_space=pl.ANY)],
            out_specs=pl.BlockSpec((rows, D_MODEL), lambda i, *_: (i, 0)),
            scratch_shapes=[pltpu.VMEM((rows, D_MODEL), F32), pltpu.VMEM((rows, D_MODEL), F32),
                            pltpu.SemaphoreType.DMA(())],
        ),
        out_shape=jax.ShapeDtypeStruct((n, D_MODEL), F32),
        compiler_params=_cparams("arbitrary"),
        name=f"moe_combine_{rows}",
    )(dest[0], dest[1], x1, rt, gf, ys)


def _rope_tables(pos, xp):
    half = HEAD_DIM // 2
    inv = 1.0 / (xp.float32(ROPE_THETA) ** (xp.arange(half, dtype=xp.float32) * xp.float32(2.0 / HEAD_DIM)))
    ang = pos.astype(xp.float32)[:, None] * inv[None, :].astype(xp.float32)
    cos, sin = xp.cos(ang), xp.sin(ang)
    return xp.concatenate([cos, cos, cos, cos], axis=1), xp.concatenate([-sin, sin, -sin, sin], axis=1)


def _slot_dest(ids, pos, pstarts):
    ids = ids.astype(jnp.int32)
    experts = jnp.arange(N_EXPERTS, dtype=jnp.int32).reshape((N_EXPERTS,) + (1,) * ids.ndim)
    start = jnp.sum(jnp.where(ids[None] == experts, pstarts.reshape(experts.shape), 0), axis=0)
    return start + pos.astype(jnp.int32)


def kernel(x_prompt, x_sample, cache_kv_g0, cache_kv_g1, cache_kv_g2, norm1_g, w_in, b_in, a_ln_g, a_ln_b, w_spatial, b_spatial, w_a_proj, w_b_proj, w_o, norm2_g, w_group_router, b_group_router, w_expert_router, b_expert_router, w_gate, w_up, w_down, final_norm_g):
    x2 = x_prompt.reshape(T_PROMPT, D_MODEL)
    xs = x_sample.reshape(DEC_BATCH, D_MODEL)
    g1 = norm1_g[0][None, :]
    g2 = norm2_g[0][None, :]
    gf = final_norm_g[None, :]
    b_in2 = b_in[0][None, :]
    lng, lnb = a_ln_g[0][None, :], a_ln_b[0][None, :]

    causal = jnp.tril(jnp.ones((CHUNK, CHUNK), dtype=bool))
    ws_tril = jnp.where(causal[None], w_spatial[0], 0.0)
    wsp = jnp.concatenate([ws_tril[0::2], ws_tril[1::2]], axis=2).astype(BF16)
    bsp = jnp.repeat(b_spatial[0].T, A_GROUP_DIM, axis=1)
    ws0 = jnp.repeat(ws_tril[:, 0, 0].astype(BF16).astype(F32), A_GROUP_DIM)[None, :]
    bs0 = jnp.repeat(b_spatial[0][:, 0], A_GROUP_DIM)[None, :]
    w_router = jnp.zeros((D_MODEL, LANES), F32)
    w_router = w_router.at[:, :N_EXPERT_GROUPS].set(w_group_router[0])
    w_router = w_router.at[:, N_EXPERT_GROUPS:N_EXPERT_GROUPS + N_EXPERTS].set(w_expert_router[0])
    b_router = jnp.zeros((1, LANES), F32)
    b_router = b_router.at[0, :N_EXPERT_GROUPS].set(b_group_router[0])
    b_router = b_router.at[0, N_EXPERT_GROUPS:N_EXPERT_GROUPS + N_EXPERTS].set(b_expert_router[0])
    w_router = w_router.astype(BF16)
    ltri = jnp.tril(jnp.ones((TM, TM), BF16), -1)
    w_in_b, w_a_b, w_b_b, w_o_b = (w[0].astype(BF16) for w in (w_in, w_a_proj, w_b_proj, w_o))

    cos_p, sin_p = _rope_tables(np.arange(SEQ, dtype=np.int32), np)
    cos_s, sin_s = _rope_tables(jnp.full((1,), PAST_LEN, jnp.int32), jnp)

    uv, qkv0, qkv1, qkv2, gates, kvp0, kvp1, kvp2 = _inproj(
        x2, g1, w_in_b, b_in2, lng, lnb, cos_p, sin_p)
    ob = _attention((qkv0.reshape(BATCH, 1, SEQ, 3 * GROUP_W), qkv1, qkv2)).reshape(T_PROMPT, GROUP_W)
    x1_p, h2p_p, rt_p, rtt_p, counts_p = _merge(
        x2, uv, gates, ob, wsp, bsp, w_a_b, w_b_b, w_o_b, g2, w_router, b_router, ltri)

    z_s = _sample_inproj(xs, g1, w_in_b, b_in2)
    vrow, sa_s, qr_s, kr_s, kvs0, kvs1, kvs2 = _sample_pre(z_s, lng, lnb, ws0, bs0, cos_s, sin_s)
    per_head = lambda t: t.reshape(DEC_BATCH, 3, HEADS, HEAD_DIM)
    caches = [c.transpose(0, 1, 3, 4, 5, 2).reshape(DEC_BATCH, 2, HEADS, HEAD_DIM, c.shape[2])
              for c in (cache_kv_g0, cache_kv_g1, cache_kv_g2)]
    comb_s = _sample_attn(per_head(qr_s), per_head(kr_s), per_head(z_s[:, COL_VB:COL_VB + QK_W]), caches)
    comb_s = comb_s.reshape(DEC_BATCH, GROUP_W)
    x1_s, h2p_s, rt_s, counts = _sample_merge(
        xs, z_s, sa_s, comb_s, w_a_b, w_b_b, w_o_b, g2, w_router, b_router,
        ltri[:DEC_BATCH, :DEC_BATCH], counts_p)

    cnt = counts[0, :N_EXPERTS].astype(jnp.int32)
    padded = (cnt + MOE_TB - 1) // MOE_TB * MOE_TB
    pends = jnp.cumsum(padded)
    pstarts = pends - padded
    block_starts = jnp.arange(MOE_NB, dtype=jnp.int32) * MOE_TB
    block_e = jnp.minimum(jnp.sum((pends[None, :] <= block_starts[:, None]).astype(jnp.int32), axis=1),
                          N_EXPERTS - 1)
    nused = (pends[-1:] // MOE_TB).astype(jnp.int32)
    rtt_p = rtt_p.reshape(T_PROMPT // TM, ROUTE_ROWS, TM)
    dest_p = [_slot_dest(rtt_p[:, k], rtt_p[:, 2 + k], pstarts).reshape(T_PROMPT) for k in range(2)]
    dest_s = [_slot_dest(rt_s[:, k], rt_s[:, 2 + k], pstarts) for k in range(2)]

    rows = jnp.zeros((MOE_ROWS, HALF_D), U32)
    rows = _dispatch(dest_p, h2p_p, rows, TM)
    rows = _dispatch(dest_s, h2p_s, rows, DEC_BATCH)
    ys = _expert_ffn(block_e, nused, rows, w_gate[0], w_up[0], w_down[0])
    y_p = _combine(dest_p, x1_p, rt_p, gf, ys, TM)
    y_s = _combine(dest_s, x1_s, rt_s, gf, ys, DEC_BATCH)

    kv_shape = lambda n, w: (1, n, w, 2, HEADS, HEAD_DIM)
    window_rows = lambda t: t.reshape(1, BATCH, 2, HEADS, HEAD_DIM, t.shape[2]).transpose(0, 1, 5, 2, 3, 4)
    return (y_p.reshape(BATCH, SEQ, D_MODEL), y_s.reshape(DEC_BATCH, 1, D_MODEL),
            window_rows(kvp0), window_rows(kvp1), window_rows(kvp2),
            kvs0.reshape(kv_shape(DEC_BATCH, 1)), kvs1.reshape(kv_shape(DEC_BATCH, 1)),
            kvs2.reshape(kv_shape(DEC_BATCH, 1)), vrow.reshape(1, DEC_BATCH, 1, A_WIDTH))
```

```python
import functools

import jax
import jax.numpy as jnp
import numpy as np
from jax import lax
from jax.experimental import pallas as pl
from jax.experimental.pallas import tpu as pltpu

F32 = jnp.float32
BF16 = jnp.bfloat16
U32 = jnp.uint32

D_MODEL = 1024
BATCH = 2
SEQ = 8192
DEC_BATCH = 32
PAST_LEN = 8192
CHUNK = 128
A_GROUPS = 8
A_GROUP_DIM = 64
A_WIDTH = 512
HEAD_DIM = 64
HEADS = 4
GROUP_W = HEADS * HEAD_DIM
DILATIONS = (1, 4, 16)
SPAN = 128
QK_W = 768
IN_WIDTH = 5376
COL_U, COL_V, COL_Q, COL_K, COL_VB, COL_GA, COL_GB = 0, 512, 1024, 1792, 2560, 3328, 4352
N_EXPERT_GROUPS = 4
EXPERTS_PER_GROUP = 8
N_EXPERTS = 32
D_EXPERT = 512
ROPE_THETA = 10000.0
EPS = 1e-6

LANES = 128
T_PROMPT = BATCH * SEQ
TM = 1024
TM_IN = 1024
WINDOWS = tuple(min(SPAN * d, SEQ) for d in DILATIONS)
TA = SPAN * max(DILATIONS)
ATT_UNROLL = 2
ATT_GROUP = 3
MOE_TB = 512
ROW_DMA_UNROLL = 8
ROUTE_ROWS = 8
N_SLOTS = 2 * (T_PROMPT + DEC_BATCH)
MOE_NB = -(-N_SLOTS // MOE_TB) + N_EXPERTS
MOE_ROWS = MOE_NB * MOE_TB
HALF_D = D_MODEL // 2
NEG = -1e30
VMEM_LIMIT = 56 * 1024 * 1024


def _cparams(*sem):
    return pltpu.CompilerParams(dimension_semantics=sem, vmem_limit_bytes=VMEM_LIMIT)


def _rms(x, g):
    return x * lax.rsqrt(jnp.mean(x * x, axis=-1, keepdims=True) + EPS) * g


def _layer_norm(x, g, b):
    mu = jnp.mean(x, axis=-1, keepdims=True)
    xc = x - mu
    var = jnp.mean(xc * xc, axis=-1, keepdims=True)
    return xc * lax.rsqrt(var + EPS) * g + b


def _rope_chunk(x, cos, sin_signed):
    lane = lax.broadcasted_iota(jnp.int32, x.shape, 1)
    first_half = (lane % HEAD_DIM) < (HEAD_DIM // 2)
    swapped = jnp.where(first_half, pltpu.roll(x, LANES - HEAD_DIM // 2, 1), pltpu.roll(x, HEAD_DIM // 2, 1))
    return x * cos + swapped * sin_signed


def _pack_bf16_pair(h):
    lo = lax.bitcast_convert_type(h[:, :HALF_D].astype(BF16).astype(F32), U32)
    hi = lax.bitcast_convert_type(h[:, HALF_D:].astype(BF16).astype(F32), U32)
    return (hi & jnp.uint32(0xFFFF0000)) | (lo >> 16)


def _unpack_bf16_pair(p):
    lo = lax.bitcast_convert_type(p << 16, F32).astype(BF16)
    hi = lax.bitcast_convert_type(p & jnp.uint32(0xFFFF0000), F32).astype(BF16)
    return lo, hi


def _route(logits, prior_counts, ltri):
    rows = logits.shape[0]
    lane = lax.broadcasted_iota(jnp.int32, (rows, LANES), 1).astype(F32)
    is_g = lane < N_EXPERT_GROUPS
    gl = jnp.where(is_g, logits, NEG)
    gmax = jnp.max(gl, axis=1, keepdims=True)
    grp = jnp.min(jnp.where(gl == gmax, lane, float(LANES)), axis=1, keepdims=True)
    p_grp = 1.0 / jnp.sum(jnp.where(is_g, jnp.exp(gl - gmax), 0.0), axis=1, keepdims=True)
    lo = N_EXPERT_GROUPS + grp * EXPERTS_PER_GROUP
    el = jnp.where((lane >= lo) & (lane < lo + EXPERTS_PER_GROUP), logits, NEG)
    v1 = jnp.max(el, axis=1, keepdims=True)
    i1 = jnp.min(jnp.where(el == v1, lane, float(LANES)), axis=1, keepdims=True)
    el2 = jnp.where(lane == i1, NEG, el)
    v2 = jnp.max(el2, axis=1, keepdims=True)
    i2 = jnp.min(jnp.where(el2 == v2, lane, float(LANES)), axis=1, keepdims=True)
    t = jnp.exp(v2 - v1)
    gate1 = p_grp / (1.0 + t)
    gate2 = p_grp * t / (1.0 + t)
    e1 = i1 - N_EXPERT_GROUPS
    e2 = i2 - N_EXPERT_GROUPS
    hit1 = lane == e1
    hit2 = lane == e2
    onehot = jnp.where(hit1 | hit2, 1.0, 0.0)
    rank = jnp.dot(ltri, onehot.astype(BF16), preferred_element_type=F32) + prior_counts
    pos1 = jnp.sum(jnp.where(hit1, rank, 0.0), axis=1, keepdims=True)
    pos2 = jnp.sum(jnp.where(hit2, rank, 0.0), axis=1, keepdims=True)
    slab = jnp.where(lane == 0, e1, jnp.where(lane == 1, e2, jnp.where(lane == 2, pos1, jnp.where(
        lane == 3, pos2, jnp.where(lane == 4, gate1, jnp.where(lane == 5, gate2, 0.0))))))
    return slab, prior_counts + jnp.sum(onehot, axis=0, keepdims=True)


def _inproj_kernel(x_ref, g1_ref, w_ref, b_ref, lng_ref, lnb_ref, cos_ref, sin_ref,
                   uv_ref, qkv0_ref, qkv1_ref, qkv2_ref, gates_ref, kv0_ref, kv1_ref, kv2_ref, de_ref):
    tm = x_ref.shape[0]
    tiles_per_seq = SEQ // tm
    tile_in_seq = pl.program_id(0) % tiles_per_seq
    hb = _rms(x_ref[...], g1_ref[...]).astype(BF16)

    def seg(lo, width):
        return jnp.dot(hb, w_ref[:, lo:lo + width], preferred_element_type=F32) + b_ref[:, lo:lo + width]

    uv_ref[:, :A_WIDTH] = jax.nn.gelu(seg(COL_U, A_WIDTH)).astype(BF16)
    uv_ref[:, A_WIDTH:] = _layer_norm(jax.nn.gelu(seg(COL_V, A_WIDTH)), lng_ref[...], lnb_ref[...]).astype(BF16)
    gates_ref[:, :D_MODEL] = jax.nn.sigmoid(seg(COL_GA, D_MODEL)).astype(BF16)
    gates_ref[:, D_MODEL:] = jax.nn.sigmoid(seg(COL_GB, D_MODEL)).astype(BF16)

    cos = cos_ref[...]
    sin = sin_ref[...]
    qkv_refs = (qkv0_ref, qkv1_ref, qkv2_ref)
    kv_refs = (kv0_ref, kv1_ref, kv2_ref)
    for g, d in enumerate(DILATIONS):
        q = seg(COL_Q + g * GROUP_W, GROUP_W)
        k = seg(COL_K + g * GROUP_W, GROUP_W)
        v = seg(COL_VB + g * GROUP_W, GROUP_W)
        chunks = GROUP_W // LANES
        for c in range(chunks):
            sl = slice(c * LANES, (c + 1) * LANES)
            de_ref[c] = _rope_chunk(q[:, sl], cos, sin) * (HEAD_DIM ** -0.5)
            de_ref[chunks + c] = _rope_chunk(k[:, sl], cos, sin)
            de_ref[2 * chunks + c] = v[:, sl]
        for c in range(3 * chunks):
            sl = slice(c * LANES, (c + 1) * LANES)
            if d == 1:
                qkv_refs[g][:, sl] = de_ref[c].astype(BF16)
            else:
                for r in range(d):
                    qkv_refs[g][0, r, :, sl] = de_ref[c, pl.ds(r, tm // d, stride=d), :].astype(BF16)
        kv_rows = kv_refs[g].shape[2]
        first = tiles_per_seq - WINDOWS[g] // kv_rows

        @pl.when(tile_in_seq >= first)
        def _(kv_ref=kv_refs[g], kv_rows=kv_rows, chunks=chunks):
            for c in range(2 * chunks):
                kv_ref[0, c * LANES:(c + 1) * LANES, :] = de_ref[chunks + c, tm - kv_rows:, :].T


def _inproj(x2, g1, w_bf, b_in, lng, lnb, cos_t, sin_t):
    tm = TM_IN
    tiles_per_seq = SEQ // tm
    n_tiles = T_PROMPT // tm
    const = lambda i: (0, 0)

    def kv_spec(w):
        rows = min(w, tm)
        first = tiles_per_seq - w // rows
        return pl.BlockSpec((1, 2 * GROUP_W, rows),
                            lambda i: (i // tiles_per_seq, 0, jnp.maximum(i % tiles_per_seq - first, 0)))

    def regrouped_spec(d):
        return pl.BlockSpec((1, d, tm // d, 3 * GROUP_W), lambda i: (i // tiles_per_seq, 0, i % tiles_per_seq, 0))

    return pl.pallas_call(
        _inproj_kernel,
        grid=(n_tiles,),
        in_specs=[
            pl.BlockSpec((tm, D_MODEL), lambda i: (i, 0)),
            pl.BlockSpec((1, D_MODEL), const),
            pl.BlockSpec((D_MODEL, IN_WIDTH), const, pipeline_mode=pl.Buffered(1)),
            pl.BlockSpec((1, IN_WIDTH), const),
            pl.BlockSpec((1, A_WIDTH), const),
            pl.BlockSpec((1, A_WIDTH), const),
            pl.BlockSpec((tm, LANES), lambda i: (i % tiles_per_seq, 0)),
            pl.BlockSpec((tm, LANES), lambda i: (i % tiles_per_seq, 0)),
        ],
        out_specs=[
            pl.BlockSpec((tm, 2 * A_WIDTH), lambda i: (i, 0)),
            pl.BlockSpec((tm, 3 * GROUP_W), lambda i: (i, 0)),
            regrouped_spec(DILATIONS[1]),
            regrouped_spec(DILATIONS[2]),
            pl.BlockSpec((tm, 2 * D_MODEL), lambda i: (i, 0)),
            kv_spec(WINDOWS[0]), kv_spec(WINDOWS[1]), kv_spec(WINDOWS[2]),
        ],
        out_shape=[
            jax.ShapeDtypeStruct((T_PROMPT, 2 * A_WIDTH), BF16),
            jax.ShapeDtypeStruct((T_PROMPT, 3 * GROUP_W), BF16),
            jax.ShapeDtypeStruct((BATCH, DILATIONS[1], SEQ // DILATIONS[1], 3 * GROUP_W), BF16),
            jax.ShapeDtypeStruct((BATCH, DILATIONS[2], SEQ // DILATIONS[2], 3 * GROUP_W), BF16),
            jax.ShapeDtypeStruct((T_PROMPT, 2 * D_MODEL), BF16),
            jax.ShapeDtypeStruct((BATCH, 2 * GROUP_W, WINDOWS[0]), F32),
            jax.ShapeDtypeStruct((BATCH, 2 * GROUP_W, WINDOWS[1]), F32),
            jax.ShapeDtypeStruct((BATCH, 2 * GROUP_W, WINDOWS[2]), F32),
        ],
        scratch_shapes=[pltpu.VMEM((3 * GROUP_W // LANES, tm, LANES), F32)],
        compiler_params=_cparams("arbitrary"),
        name="inproj",
    )(x2, g1, w_bf, b_in, lng, lnb, cos_t, sin_t)


def _attn_kernel(c0_ref, p0_ref, c1_ref, p1_ref, c2_ref, p2_ref, o_ref, acc_ref, m_ref, l_ref):
    n = pl.program_id(1)
    qi = lax.broadcasted_iota(jnp.int32, (HEADS * SPAN, 2 * SPAN), 0) % SPAN
    ki = lax.broadcasted_iota(jnp.int32, (HEADS * SPAN, 2 * SPAN), 1)
    band = (ki >= qi) & (ki <= qi + SPAN)
    band_first = band & ((ki >= SPAN) | (n > 0))
    q_head = lax.broadcasted_iota(jnp.int32, (SPAN, GROUP_W), 1) // HEAD_DIM
    kv_head = lax.broadcasted_iota(jnp.int32, (2 * SPAN, GROUP_W), 1) // HEAD_DIM
    first_head_lanes = lax.broadcasted_iota(jnp.int32, (SPAN, LANES), 1) < HEAD_DIM
    contract_last = (((1,), (1,)), ((), ()))
    k_cols = slice(GROUP_W, 2 * GROUP_W)
    v_cols = slice(2 * GROUP_W, 3 * GROUP_W)

    def attend(blocks, fresh):
        scores = []
        for q, keys, _, mask, _ in blocks:
            q_heads = jnp.concatenate([jnp.where(q_head == h, q, jnp.zeros_like(q)) for h in range(HEADS)], axis=0)
            scores.append(jnp.where(mask, lax.dot_general(q_heads, keys, contract_last, preferred_element_type=F32), NEG))
        soft = []
        for s in scores:
            m = jnp.max(s, axis=1, keepdims=True)
            p = jnp.exp(s - m)
            soft.append((m, jnp.sum(p, axis=1, keepdims=True), p.astype(BF16)))
        accs = []
        for (_, _, pb), (_, _, vals, _, _) in zip(soft, blocks):
            p_heads = jnp.concatenate([pb[h * SPAN:(h + 1) * SPAN] for h in range(HEADS)], axis=1)
            v_heads = jnp.concatenate([jnp.where(kv_head == h, vals, jnp.zeros_like(vals)) for h in range(HEADS)],
                                      axis=0)
            accs.append(jnp.dot(p_heads, v_heads, preferred_element_type=F32))
        for (m, l, _), acc, (_, _, _, _, tok_rows) in zip(soft, accs, blocks):
            merge(m, l, acc, tok_rows, fresh)

    def merge(m, l, acc, tok_rows, fresh):
        heads_per_chunk = LANES // HEAD_DIM
        for c in range(GROUP_W // LANES):
            sl = slice(c * LANES, (c + 1) * LANES)
            lo = slice(c * heads_per_chunk * SPAN, (c * heads_per_chunk + 1) * SPAN)
            hi = slice((c * heads_per_chunk + 1) * SPAN, (c * heads_per_chunk + 2) * SPAN)
            m_b = jnp.where(first_head_lanes, m[lo], m[hi])
            l_b = jnp.where(first_head_lanes, l[lo], l[hi])
            if fresh:
                m_ref[c, tok_rows, :] = m_b
                l_ref[c, tok_rows, :] = l_b
                acc_ref[c, tok_rows, :] = acc[:, sl]
                continue
            m_old = m_ref[c, tok_rows, :]
            m_new = jnp.maximum(m_old, m_b)
            a_old = jnp.exp(m_old - m_new)
            a_blk = jnp.exp(m_b - m_new)
            l_ref[c, tok_rows, :] = a_old * l_ref[c, tok_rows, :] + a_blk * l_b
            acc_ref[c, tok_rows, :] = a_old * acc_ref[c, tok_rows, :] + a_blk * acc[:, sl]
            m_ref[c, tok_rows, :] = m_new

    def tok_rows(r, j, d):
        start = j * (SPAN * d) + r
        return pl.ds(start, SPAN) if d == 1 else pl.ds(start, SPAN, stride=d)

    def first_block(c_ref, p_ref, r, d):
        keys = jnp.concatenate([p_ref[0, r, :, k_cols], c_ref[0, r, :SPAN, k_cols]], axis=0)
        vals = jnp.concatenate([p_ref[0, r, :, v_cols], c_ref[0, r, :SPAN, v_cols]], axis=0)
        return c_ref[0, r, :SPAN, :GROUP_W], keys, vals, band_first, tok_rows(r, 0, d)

    def later_block(c_ref, r, j, d):
        q_rows = pl.ds(pl.multiple_of(j * SPAN, SPAN), SPAN)
        kv_rows = pl.ds(pl.multiple_of((j - 1) * SPAN, SPAN), 2 * SPAN)
        return (c_ref[0, r, q_rows, :GROUP_W], c_ref[0, r, kv_rows, k_cols], c_ref[0, r, kv_rows, v_cols],
                band, tok_rows(r, j, d))

    def stream(c_ref, p_ref, r, d, fresh):
        later = c_ref.shape[2] // SPAN - 1
        assert later % ATT_GROUP == 0
        attend([first_block(c_ref, p_ref, r, d)], fresh)

        def body(i, carry):
            j0 = 1 + i * ATT_GROUP
            attend([later_block(c_ref, r, j0 + k, d) for k in range(ATT_GROUP)], fresh)
            return carry

        lax.fori_loop(0, later // ATT_GROUP, body, 0)

    groups = sorted(zip(DILATIONS, ((c0_ref, p0_ref), (c1_ref, p1_ref), (c2_ref, p2_ref))), key=lambda g: -g[0])
    for idx, (d, (c_ref, p_ref)) in enumerate(groups):
        fresh = idx == 0
        if d == 1:
            stream(c_ref, p_ref, 0, d, fresh)
        elif c_ref.shape[2] == SPAN:
            def pair(i, carry, c_ref=c_ref, p_ref=p_ref, d=d, fresh=fresh):
                attend([first_block(c_ref, p_ref, i * ATT_UNROLL + k, d) for k in range(ATT_UNROLL)], fresh)
                return carry
            lax.fori_loop(0, d // ATT_UNROLL, pair, 0)
        else:
            lax.fori_loop(0, d, lambda r, carry, c_ref=c_ref, p_ref=p_ref, d=d, fresh=fresh:
                          (stream(c_ref, p_ref, r, d, fresh), carry)[1], 0)

    for c in range(GROUP_W // LANES):
        o_ref[0, :, c * LANES:(c + 1) * LANES] = (acc_ref[c] / l_ref[c]).astype(BF16)


def _attention(qkv_by_group):
    in_specs, args = [], []
    for qkv, d in zip(qkv_by_group, DILATIONS):
        rows = TA // d
        blocks_per_tile = rows // SPAN
        in_specs.append(pl.BlockSpec((1, d, rows, 3 * GROUP_W), lambda b, n: (b, 0, n, 0)))
        in_specs.append(pl.BlockSpec((1, d, SPAN, 3 * GROUP_W),
                                     lambda b, n, k=blocks_per_tile: (b, 0, jnp.maximum(n * k - 1, 0), 0)))
        args += [qkv, qkv]
    return pl.pallas_call(
        _attn_kernel,
        grid=(BATCH, SEQ // TA),
        in_specs=in_specs,
        out_specs=pl.BlockSpec((1, TA, GROUP_W), lambda b, n: (b, n, 0)),
        out_shape=jax.ShapeDtypeStruct((BATCH, SEQ, GROUP_W), BF16),
        scratch_shapes=[pltpu.VMEM((GROUP_W // LANES, TA, LANES), F32)] * 3,
        compiler_params=_cparams("arbitrary", "arbitrary"),
        name="attn",
    )(*args)


def _merge_kernel(x_ref, uv_ref, gates_ref, ob_ref,
                  wsp_ref, bsp_ref, wa_ref, wb_ref, wo_ref, g2_ref, wr_ref, br_ref, ltri_ref,
                  x1_ref, h2p_ref, rt_ref, rtt_ref, cnt_ref, run_ref):
    @pl.when(pl.program_id(0) == 0)
    def _():
        run_ref[...] = jnp.zeros_like(run_ref)

    lane = lax.broadcasted_iota(jnp.int32, (CHUNK, LANES), 1)
    left = lane < A_GROUP_DIM
    zero = jnp.zeros((CHUNK, LANES), BF16)
    n_chunks, n_pairs = TM // CHUNK, A_GROUPS // 2
    chunk_rows = lambda c: slice(c * CHUNK, (c + 1) * CHUNK)
    mixes = []
    for c in range(n_chunks):
        for p in range(n_pairs):
            vp = uv_ref[chunk_rows(c), A_WIDTH + p * LANES:A_WIDTH + (p + 1) * LANES]
            rhs = jnp.concatenate([jnp.where(left, vp, zero), jnp.where(left, zero, vp)], axis=0)
            mixes.append(jnp.dot(wsp_ref[p], rhs, preferred_element_type=F32))
    sa_chunks = []
    for c in range(n_chunks):
        mixed = jnp.concatenate(mixes[c * n_pairs:(c + 1) * n_pairs], axis=1) + bsp_ref[...]
        sa_chunks.append((uv_ref[chunk_rows(c), :A_WIDTH].astype(F32) * mixed).astype(BF16))
    s_a = jnp.concatenate(sa_chunks, axis=0)

    a = jnp.dot(s_a, wa_ref[...], preferred_element_type=F32)
    b = jnp.dot(ob_ref[...], wb_ref[...], preferred_element_type=F32)
    merged = gates_ref[:, :D_MODEL].astype(F32) * a + gates_ref[:, D_MODEL:].astype(F32) * b
    x1 = x_ref[...] + jnp.dot(merged.astype(BF16), wo_ref[...], preferred_element_type=F32)
    x1_ref[...] = x1

    h2 = _rms(x1, g2_ref[...])
    h2p_ref[...] = _pack_bf16_pair(h2)
    logits = jnp.dot(h2.astype(BF16), wr_ref[...], preferred_element_type=F32) + br_ref[...]
    slab, counts = _route(logits, run_ref[...], ltri_ref[...])
    rt_ref[...] = slab
    rtt_ref[...] = slab.T[:ROUTE_ROWS, :]
    run_ref[...] = counts
    cnt_ref[...] = counts


def _merge(x2, uv, gates, ob, wsp, bsp, wa, wb, wo, g2, wr, br, ltri):
    n_tiles = T_PROMPT // TM
    tile = lambda w: pl.BlockSpec((TM, w), lambda i: (i, 0))
    full = lambda a: pl.BlockSpec(a.shape, lambda i: (0,) * a.ndim)
    return pl.pallas_call(
        _merge_kernel,
        grid=(n_tiles,),
        in_specs=[tile(D_MODEL), tile(2 * A_WIDTH), tile(2 * D_MODEL), tile(GROUP_W),
                  full(wsp), full(bsp), full(wa), full(wb), full(wo), full(g2), full(wr), full(br),
                  full(ltri)],
        out_specs=[tile(D_MODEL), tile(HALF_D), tile(LANES), pl.BlockSpec((ROUTE_ROWS, TM), lambda i: (i, 0)),
                   pl.BlockSpec((1, LANES), lambda i: (0, 0))],
        out_shape=[
            jax.ShapeDtypeStruct((T_PROMPT, D_MODEL), F32),
            jax.ShapeDtypeStruct((T_PROMPT, HALF_D), U32),
            jax.ShapeDtypeStruct((T_PROMPT, LANES), F32),
            jax.ShapeDtypeStruct((n_tiles * ROUTE_ROWS, TM), F32),
            jax.ShapeDtypeStruct((1, LANES), F32),
        ],
        scratch_shapes=[pltpu.VMEM((1, LANES), F32)],
        compiler_params=_cparams("arbitrary"),
        name="merge_route",
    )(x2, uv, gates, ob, wsp, bsp, wa, wb, wo, g2, wr, br, ltri)


SAMPLE_COLS = 768


def _sample_inproj_kernel(x_ref, g1_ref, w_ref, b_ref, z_ref):
    hb = _rms(x_ref[...], g1_ref[...]).astype(BF16)
    z_ref[...] = jnp.dot(hb, w_ref[...], preferred_element_type=F32) + b_ref[...]


def _sample_inproj(xs, g1, w_in, b_in):
    const = lambda j: (0, 0)
    return pl.pallas_call(
        _sample_inproj_kernel,
        grid=(IN_WIDTH // SAMPLE_COLS,),
        in_specs=[pl.BlockSpec((DEC_BATCH, D_MODEL), const), pl.BlockSpec((1, D_MODEL), const),
                  pl.BlockSpec((D_MODEL, SAMPLE_COLS), lambda j: (0, j)),
                  pl.BlockSpec((1, SAMPLE_COLS), lambda j: (0, j))],
        out_specs=pl.BlockSpec((DEC_BATCH, SAMPLE_COLS), lambda j: (0, j)),
        out_shape=jax.ShapeDtypeStruct((DEC_BATCH, IN_WIDTH), F32),
        compiler_params=_cparams("arbitrary"),
        name="sample_inproj",
    )(xs, g1, w_in, b_in)


def _sample_pre_kernel(z_ref, lng_ref, lnb_ref, ws0_ref, bs0_ref, cos_ref, sin_ref,
                       vrow_ref, sa_ref, qr_ref, kr_ref, kv0_ref, kv1_ref, kv2_ref):
    u = jax.nn.gelu(z_ref[:, COL_U:COL_U + A_WIDTH])
    va = _layer_norm(jax.nn.gelu(z_ref[:, COL_V:COL_V + A_WIDTH]), lng_ref[...], lnb_ref[...])
    vrow_ref[...] = va
    sa_ref[...] = u * (ws0_ref[...] * va.astype(BF16).astype(F32) + bs0_ref[...])
    cos, sin = cos_ref[...], sin_ref[...]
    for g, kv_ref in enumerate((kv0_ref, kv1_ref, kv2_ref)):
        for c in range(GROUP_W // LANES):
            off = g * GROUP_W + c * LANES
            qr_ref[:, off:off + LANES] = (_rope_chunk(z_ref[:, COL_Q + off:COL_Q + off + LANES], cos, sin)
                                          * (HEAD_DIM ** -0.5))
            kr = _rope_chunk(z_ref[:, COL_K + off:COL_K + off + LANES], cos, sin)
            kr_ref[:, off:off + LANES] = kr
            kv_ref[:, c * LANES:(c + 1) * LANES] = kr
        kv_ref[:, GROUP_W:] = z_ref[:, COL_VB + g * GROUP_W:COL_VB + (g + 1) * GROUP_W]


def _sample_pre(z, lng, lnb, ws0, bs0, cos_s, sin_s):
    row = lambda w: jax.ShapeDtypeStruct((DEC_BATCH, w), F32)
    return pl.pallas_call(
        _sample_pre_kernel,
        out_shape=[row(A_WIDTH), row(A_WIDTH), row(QK_W), row(QK_W),
                   row(2 * GROUP_W), row(2 * GROUP_W), row(2 * GROUP_W)],
        compiler_params=pltpu.CompilerParams(vmem_limit_bytes=VMEM_LIMIT),
        name="sample_pre",
    )(z, lng, lnb, ws0, bs0, cos_s, sin_s)


def _sample_attn_kernel(q_ref, k_ref, v_ref, c0_ref, c1_ref, c2_ref, comb_ref):
    as_operand = lambda t: t.astype(BF16).astype(F32)
    contract_last = (((1,), (1,)), ((), ()))
    pad_rows = 8 // HEADS
    groups = tuple(zip((c0_ref, c1_ref, c2_ref), DILATIONS))
    pairs = [(h, g) for h in range(HEADS) for g in range(len(groups))]
    scores = {}
    for h, g in pairs:
        c_ref, d = groups[g]
        q_rows = jnp.concatenate([q_ref[0, g]] * pad_rows, axis=0).astype(BF16)
        s = jnp.dot(q_rows, c_ref[0, 0, h].astype(BF16), preferred_element_type=F32)[h:h + 1]
        if d > 1:
            pos = lax.broadcasted_iota(jnp.int32, s.shape, 1)
            s = jnp.where(pos % d == 0, s, NEG)
        scores[h, g] = s
    probs = {}
    for h, g in pairs:
        s = scores[h, g]
        qn, kn = (as_operand(r[0, g, h:h + 1, :]) for r in (q_ref, k_ref))
        s_n = jnp.sum(qn * kn, axis=1, keepdims=True)
        m = jnp.maximum(jnp.max(s, axis=1, keepdims=True), s_n)
        lse = m + jnp.log(jnp.sum(jnp.exp(s - m), axis=1, keepdims=True) + jnp.exp(s_n - m))
        probs[h, g] = (jnp.broadcast_to(jnp.exp(s - lse), (8, s.shape[1])).astype(BF16),
                       as_operand(jnp.exp(s_n - lse)), lse)
    outs = {}
    for h, g in pairs:
        p_rows, p_n, _ = probs[h, g]
        o = lax.dot_general(p_rows, groups[g][0][0, 1, h].astype(BF16), contract_last, preferred_element_type=F32)
        outs[h, g] = o[0:1] + p_n * as_operand(v_ref[0, g, h:h + 1, :])
    for h in range(HEADS):
        lses = [probs[h, g][2] for g in range(len(groups))]
        mx = jnp.maximum(jnp.maximum(lses[0], lses[1]), lses[2])
        ws = [jnp.exp(t - mx) for t in lses]
        tot = ws[0] + ws[1] + ws[2]
        comb_ref[0, h:h + 1, :] = (ws[0] * outs[h, 0] + ws[1] * outs[h, 1] + ws[2] * outs[h, 2]) / tot


def _sample_attn(qt, kt, vt, caches):
    new_spec = pl.BlockSpec((1, 3, HEADS, HEAD_DIM), lambda i: (i, 0, 0, 0))
    cache_spec = lambda c: pl.BlockSpec((1,) + c.shape[1:], lambda i: (i, 0, 0, 0, 0))
    return pl.pallas_call(
        _sample_attn_kernel,
        grid=(DEC_BATCH,),
        in_specs=[new_spec, new_spec, new_spec, cache_spec(caches[0]), cache_spec(caches[1]), cache_spec(caches[2])],
        out_specs=pl.BlockSpec((1, HEADS, HEAD_DIM), lambda i: (i, 0, 0)),
        out_shape=jax.ShapeDtypeStruct((DEC_BATCH, HEADS, HEAD_DIM), F32),
        compiler_params=_cparams("arbitrary"),
        name="sample_attn",
    )(qt, kt, vt, caches[0], caches[1], caches[2])


def _sample_merge_kernel(x_ref, z_ref, sa_ref, comb_ref, wa_ref, wb_ref, wo_ref, g2_ref, wr_ref, br_ref,
                         ltri_ref, cnt_in_ref, x1_ref, h2p_ref, rt_ref, cnt_ref):
    dot = lambda p, q: jnp.dot(p.astype(BF16), q, preferred_element_type=F32)
    a = dot(sa_ref[...], wa_ref[...])
    b = dot(comb_ref[...], wb_ref[...])
    merged = (jax.nn.sigmoid(z_ref[:, COL_GA:COL_GA + D_MODEL]) * a
              + jax.nn.sigmoid(z_ref[:, COL_GB:COL_GB + D_MODEL]) * b)
    x1 = x_ref[...] + dot(merged, wo_ref[...])
    x1_ref[...] = x1
    h2 = _rms(x1, g2_ref[...])
    h2p_ref[...] = _pack_bf16_pair(h2)
    logits = dot(h2, wr_ref[...]) + br_ref[...]
    slab, counts = _route(logits, cnt_in_ref[...], ltri_ref[...])
    rt_ref[...] = slab
    cnt_ref[...] = counts


def _sample_merge(xs, z, s_a, comb, wa, wb, wo, g2, wr, br, ltri, counts):
    return pl.pallas_call(
        _sample_merge_kernel,
        out_shape=[jax.ShapeDtypeStruct((DEC_BATCH, D_MODEL), F32),
                   jax.ShapeDtypeStruct((DEC_BATCH, HALF_D), U32),
                   jax.ShapeDtypeStruct((DEC_BATCH, LANES), F32),
                   jax.ShapeDtypeStruct((1, LANES), F32)],
        compiler_params=pltpu.CompilerParams(vmem_limit_bytes=VMEM_LIMIT),
        name="sample_merge_route",
    )(xs, z, s_a, comb, wa, wb, wo, g2, wr, br, ltri, counts)


def _dispatch_kernel(dest0_ref, dest1_ref, h_ref, xs_in_ref, xs_ref, sem):
    del xs_in_ref
    rows = h_ref.shape[0]
    base = pl.program_id(0) * rows

    def row_copy(t, dest_ref):
        return pltpu.make_async_copy(h_ref.at[pl.ds(t, 1)], xs_ref.at[pl.ds(dest_ref[base + t], 1)], sem)

    def start(t, c):
        row_copy(t, dest0_ref).start()
        row_copy(t, dest1_ref).start()
        return c

    lax.fori_loop(0, rows, start, 0, unroll=ROW_DMA_UNROLL)
    for _ in range(2):
        pltpu.make_async_copy(h_ref, xs_ref.at[pl.ds(0, rows)], sem).wait()


def _dispatch(dest, h2p, xs, rows):
    n = h2p.shape[0]
    return pl.pallas_call(
        _dispatch_kernel,
        grid_spec=pltpu.PrefetchScalarGridSpec(
            num_scalar_prefetch=2,
            grid=(n // rows,),
            in_specs=[pl.BlockSpec((rows, HALF_D), lambda i, *_: (i, 0)), pl.BlockSpec(memory_space=pl.ANY)],
            out_specs=pl.BlockSpec(memory_space=pl.ANY),
            scratch_shapes=[pltpu.SemaphoreType.DMA(())],
        ),
        out_shape=jax.ShapeDtypeStruct(xs.shape, xs.dtype),
        input_output_aliases={3: 0},
        compiler_params=_cparams("arbitrary"),
        name=f"moe_dispatch_{rows}",
    )(dest[0], dest[1], h2p, xs)


def _ffn_kernel(be_ref, nused_ref, nexte_ref, wslot_ref, xs_ref, wg_ref, wu_ref, wd_ref, ys_ref,
                wg_b, wu_b, wd_b, wg_f, wu_f, wd_f, wsem):
    i = pl.program_id(0)
    live = i < nused_ref[0]
    new_expert = jnp.logical_and(live, jnp.logical_or(i == 0, be_ref[i] != be_ref[jnp.maximum(i - 1, 0)]))

    def weight_copies(e, slot):
        return [pltpu.make_async_copy(src.at[e], dst.at[slot], wsem.at[slot])
                for src, dst in ((wg_ref, wg_f), (wu_ref, wu_f), (wd_ref, wd_f))]

    @pl.when(i == 0)
    def _():
        for cp in weight_copies(be_ref[0], 0):
            cp.start()

    @pl.when(new_expert)
    def _():
        slot = wslot_ref[i]
        for cp in weight_copies(be_ref[i], slot):
            cp.wait()
        wg_b[...] = wg_f[slot].astype(BF16)
        wu_b[...] = wu_f[slot].astype(BF16)
        wd_b[...] = wd_f[slot].astype(BF16)

        @pl.when(nexte_ref[i] >= 0)
        def _():
            for cp in weight_copies(nexte_ref[i], 1 - slot):
                cp.start()

    @pl.when(live)
    def _():
        lo, hi = _unpack_bf16_pair(xs_ref[...])
        gate = (jnp.dot(lo, wg_b[:HALF_D, :], preferred_element_type=F32)
                + jnp.dot(hi, wg_b[HALF_D:, :], preferred_element_type=F32))
        up = (jnp.dot(lo, wu_b[:HALF_D, :], preferred_element_type=F32)
              + jnp.dot(hi, wu_b[HALF_D:, :], preferred_element_type=F32))
        hid = (jax.nn.silu(gate) * up).astype(BF16)
        ys_ref[...] = jnp.dot(hid, wd_b[...], preferred_element_type=F32)

    @pl.when(i >= nused_ref[0])
    def _():
        ys_ref[...] = jnp.zeros_like(ys_ref)


def _expert_ffn(block_e, nused, xs, w_gate, w_up, w_down):
    idx = jnp.arange(MOE_NB, dtype=jnp.int32)
    change = ((block_e != jnp.roll(block_e, 1)) | (idx == 0)) & (idx < nused[0])
    wslot = ((jnp.cumsum(change.astype(jnp.int32)) - 1) & 1).astype(jnp.int32)
    change_at = jnp.where(change, idx, MOE_NB)
    next_change = jnp.flip(lax.cummin(jnp.flip(jnp.concatenate([change_at[1:], jnp.full((1,), MOE_NB, jnp.int32)]))))
    nexte = jnp.where(next_change < MOE_NB, block_e[jnp.minimum(next_change, MOE_NB - 1)], -1).astype(jnp.int32)
    hbm = pl.BlockSpec(memory_space=pl.ANY)
    return pl.pallas_call(
        _ffn_kernel,
        grid_spec=pltpu.PrefetchScalarGridSpec(
            num_scalar_prefetch=4,
            grid=(MOE_NB,),
            in_specs=[pl.BlockSpec((MOE_TB, HALF_D), lambda i, *_: (i, 0)), hbm, hbm, hbm],
            out_specs=pl.BlockSpec((MOE_TB, D_MODEL), lambda i, *_: (i, 0)),
            scratch_shapes=[pltpu.VMEM((D_MODEL, D_EXPERT), BF16), pltpu.VMEM((D_MODEL, D_EXPERT), BF16),
                            pltpu.VMEM((D_EXPERT, D_MODEL), BF16),
                            pltpu.VMEM((2, D_MODEL, D_EXPERT), F32), pltpu.VMEM((2, D_MODEL, D_EXPERT), F32),
                            pltpu.VMEM((2, D_EXPERT, D_MODEL), F32), pltpu.SemaphoreType.DMA((2,))],
        ),
        out_shape=jax.ShapeDtypeStruct((MOE_ROWS, D_MODEL), F32),
        compiler_params=_cparams("arbitrary"),
        name="moe_ffn",
    )(block_e, nused, nexte, wslot, xs, w_gate, w_up, w_down)


def _combine_kernel(dest0_ref, dest1_ref, x1_ref, rt_ref, gf_ref, ys_ref, y_ref, ya, yb, sem):
    rows = x1_ref.shape[0]
    base = pl.program_id(0) * rows

    def row_copy(t, dest_ref, buf):
        return pltpu.make_async_copy(ys_ref.at[pl.ds(dest_ref[base + t], 1)], buf.at[pl.ds(t, 1)], sem)

    def start(t, c):
        row_copy(t, dest0_ref, ya).start()
        row_copy(t, dest1_ref, yb).start()
        return c

    lax.fori_loop(0, rows, start, 0, unroll=ROW_DMA_UNROLL)
    pltpu.make_async_copy(ys_ref.at[pl.ds(0, rows)], ya, sem).wait()
    pltpu.make_async_copy(ys_ref.at[pl.ds(0, rows)], yb, sem).wait()
    x2 = x1_ref[...] + rt_ref[:, 4:5] * ya[...] + rt_ref[:, 5:6] * yb[...]
    y_ref[...] = _rms(x2, gf_ref[...])


def _combine(dest, x1, rt, gf, ys, rows):
    n = x1.shape[0]
    return pl.pallas_call(
        _combine_kernel,
        grid_spec=pltpu.PrefetchScalarGridSpec(
            num_scalar_prefetch=2,
            grid=(n // rows,),
            in_specs=[pl.BlockSpec((rows, D_MODEL), lambda i, *_: (i, 0)),
                      pl.BlockSpec((rows, LANES), lambda i, *_: (i, 0)),
                      pl.BlockSpec((1, D_MODEL), lambda i, *_: (0, 0)),
                      pl.BlockSpec(memory_space=pl.ANY)],
            out_specs=pl.BlockSpec((rows, D_MODEL), lambda i, *_: (i, 0)),
            scratch_shapes=[pltpu.VMEM((rows, D_MODEL), F32), pltpu.VMEM((rows, D_MODEL), F32),
                            pltpu.SemaphoreType.DMA(())],
        ),
        out_shape=jax.ShapeDtypeStruct((n, D_MODEL), F32),
        compiler_params=_cparams("arbitrary"),
        name=f"moe_combine_{rows}",
    )(dest[0], dest[1], x1, rt, gf, ys)


def _rope_tables(pos, xp):
    half = HEAD_DIM // 2
    inv = 1.0 / (xp.float32(ROPE_THETA) ** (xp.arange(half, dtype=xp.float32) * xp.float32(2.0 / HEAD_DIM)))
    ang = pos.astype(xp.float32)[:, None] * inv[None, :].astype(xp.float32)
    cos, sin = xp.cos(ang), xp.sin(ang)
    return xp.concatenate([cos, cos, cos, cos], axis=1), xp.concatenate([-sin, sin, -sin, sin], axis=1)


def _slot_dest(ids, pos, pstarts):
    ids = ids.astype(jnp.int32)
    experts = jnp.arange(N_EXPERTS, dtype=jnp.int32).reshape((N_EXPERTS,) + (1,) * ids.ndim)
    start = jnp.sum(jnp.where(ids[None] == experts, pstarts.reshape(experts.shape), 0), axis=0)
    return start + pos.astype(jnp.int32)


def kernel(x_prompt, x_sample, cache_kv_g0, cache_kv_g1, cache_kv_g2, norm1_g, w_in, b_in, a_ln_g, a_ln_b, w_spatial, b_spatial, w_a_proj, w_b_proj, w_o, norm2_g, w_group_router, b_group_router, w_expert_router, b_expert_router, w_gate, w_up, w_down, final_norm_g):
    x2 = x_prompt.reshape(T_PROMPT, D_MODEL)
    xs = x_sample.reshape(DEC_BATCH, D_MODEL)
    g1 = norm1_g[0][None, :]
    g2 = norm2_g[0][None, :]
    gf = final_norm_g[None, :]
    b_in2 = b_in[0][None, :]
    lng, lnb = a_ln_g[0][None, :], a_ln_b[0][None, :]

    causal = jnp.tril(jnp.ones((CHUNK, CHUNK), dtype=bool))
    ws_tril = jnp.where(causal[None], w_spatial[0], 0.0)
    wsp = jnp.concatenate([ws_tril[0::2], ws_tril[1::2]], axis=2).astype(BF16)
    bsp = jnp.repeat(b_spatial[0].T, A_GROUP_DIM, axis=1)
    ws0 = jnp.repeat(ws_tril[:, 0, 0].astype(BF16).astype(F32), A_GROUP_DIM)[None, :]
    bs0 = jnp.repeat(b_spatial[0][:, 0], A_GROUP_DIM)[None, :]
    w_router = jnp.zeros((D_MODEL, LANES), F32)
    w_router = w_router.at[:, :N_EXPERT_GROUPS].set(w_group_router[0])
    w_router = w_router.at[:, N_EXPERT_GROUPS:N_EXPERT_GROUPS + N_EXPERTS].set(w_expert_router[0])
    b_router = jnp.zeros((1, LANES), F32)
    b_router = b_router.at[0, :N_EXPERT_GROUPS].set(b_group_router[0])
    b_router = b_router.at[0, N_EXPERT_GROUPS:N_EXPERT_GROUPS + N_EXPERTS].set(b_expert_router[0])
    w_router = w_router.astype(BF16)
    ltri = jnp.tril(jnp.ones((TM, TM), BF16), -1)
    w_in_b, w_a_b, w_b_b, w_o_b = (w[0].astype(BF16) for w in (w_in, w_a_proj, w_b_proj, w_o))

    cos_p, sin_p = _rope_tables(np.arange(SEQ, dtype=np.int32), np)
    cos_s, sin_s = _rope_tables(jnp.full((1,), PAST_LEN, jnp.int32), jnp)

    uv, qkv0, qkv1, qkv2, gates, kvp0, kvp1, kvp2 = _inproj(
        x2, g1, w_in_b, b_in2, lng, lnb, cos_p, sin_p)
    ob = _attention((qkv0.reshape(BATCH, 1, SEQ, 3 * GROUP_W), qkv1, qkv2)).reshape(T_PROMPT, GROUP_W)
    x1_p, h2p_p, rt_p, rtt_p, counts_p = _merge(
        x2, uv, gates, ob, wsp, bsp, w_a_b, w_b_b, w_o_b, g2, w_router, b_router, ltri)

    z_s = _sample_inproj(xs, g1, w_in_b, b_in2)
    vrow, sa_s, qr_s, kr_s, kvs0, kvs1, kvs2 = _sample_pre(z_s, lng, lnb, ws0, bs0, cos_s, sin_s)
    per_head = lambda t: t.reshape(DEC_BATCH, 3, HEADS, HEAD_DIM)
    caches = [c.transpose(0, 1, 3, 4, 5, 2).reshape(DEC_BATCH, 2, HEADS, HEAD_DIM, c.shape[2])
              for c in (cache_kv_g0, cache_kv_g1, cache_kv_g2)]
    comb_s = _sample_attn(per_head(qr_s), per_head(kr_s), per_head(z_s[:, COL_VB:COL_VB + QK_W]), caches)
    comb_s = comb_s.reshape(DEC_BATCH, GROUP_W)
    x1_s, h2p_s, rt_s, counts = _sample_merge(
        xs, z_s, sa_s, comb_s, w_a_b, w_b_b, w_o_b, g2, w_router, b_router,
        ltri[:DEC_BATCH, :DEC_BATCH], counts_p)

    cnt = counts[0, :N_EXPERTS].astype(jnp.int32)
    padded = (cnt + MOE_TB - 1) // MOE_TB * MOE_TB
    pends = jnp.cumsum(padded)
    pstarts = pends - padded
    block_starts = jnp.arange(MOE_NB, dtype=jnp.int32) * MOE_TB
    block_e = jnp.minimum(jnp.sum((pends[None, :] <= block_starts[:, None]).astype(jnp.int32), axis=1),
                          N_EXPERTS - 1)
    nused = (pends[-1:] // MOE_TB).astype(jnp.int32)
    rtt_p = rtt_p.reshape(T_PROMPT // TM, ROUTE_ROWS, TM)
    dest_p = [_slot_dest(rtt_p[:, k], rtt_p[:, 2 + k], pstarts).reshape(T_PROMPT) for k in range(2)]
    dest_s = [_slot_dest(rt_s[:, k], rt_s[:, 2 + k], pstarts) for k in range(2)]

    rows = jnp.zeros((MOE_ROWS, HALF_D), U32)
    rows = _dispatch(dest_p, h2p_p, rows, TM)
    rows = _dispatch(dest_s, h2p_s, rows, DEC_BATCH)
    ys = _expert_ffn(block_e, nused, rows, w_gate[0], w_up[0], w_down[0])
    y_p = _combine(dest_p, x1_p, rt_p, gf, ys, TM)
    y_s = _combine(dest_s, x1_s, rt_s, gf, ys, DEC_BATCH)

    kv_shape = lambda n, w: (1, n, w, 2, HEADS, HEAD_DIM)
    window_rows = lambda t: t.reshape(1, BATCH, 2, HEADS, HEAD_DIM, t.shape[2]).transpose(0, 1, 5, 2, 3, 4)
    return (y_p.reshape(BATCH, SEQ, D_MODEL), y_s.reshape(DEC_BATCH, 1, D_MODEL),
            window_rows(kvp0), window_rows(kvp1), window_rows(kvp2),
            kvs0.reshape(kv_shape(DEC_BATCH, 1)), kvs1.reshape(kv_shape(DEC_BATCH, 1)),
            kvs2.reshape(kv_shape(DEC_BATCH, 1)), vrow.reshape(1, DEC_BATCH, 1, A_WIDTH))
```

```python
import functools

import jax
import jax.numpy as jnp
import numpy as np
from jax import lax
from jax.experimental import pallas as pl
from jax.experimental.pallas import tpu as pltpu

F32 = jnp.float32
BF16 = jnp.bfloat16
U32 = jnp.uint32

D_MODEL = 1024
BATCH = 2
SEQ = 8192
DEC_BATCH = 32
PAST_LEN = 8192
CHUNK = 128
A_GROUPS = 8
A_GROUP_DIM = 64
A_WIDTH = 512
HEAD_DIM = 64
HEADS = 4
GROUP_W = HEADS * HEAD_DIM
DILATIONS = (1, 4, 16)
SPAN = 128
QK_W = 768
IN_WIDTH = 5376
COL_U, COL_V, COL_Q, COL_K, COL_VB, COL_GA, COL_GB = 0, 512, 1024, 1792, 2560, 3328, 4352
N_EXPERT_GROUPS = 4
EXPERTS_PER_GROUP = 8
N_EXPERTS = 32
D_EXPERT = 512
ROPE_THETA = 10000.0
EPS = 1e-6

LANES = 128
T_PROMPT = BATCH * SEQ
TM = 1024
TM_IN = 1024
WINDOWS = tuple(min(SPAN * d, SEQ) for d in DILATIONS)
TA = SPAN * max(DILATIONS)
ATT_UNROLL = 4
ATT_GROUP = 3
MOE_TB = 512
ROW_DMA_UNROLL = 8
ROUTE_ROWS = 8
N_SLOTS = 2 * (T_PROMPT + DEC_BATCH)
MOE_NB = -(-N_SLOTS // MOE_TB) + N_EXPERTS
MOE_ROWS = MOE_NB * MOE_TB
HALF_D = D_MODEL // 2
NEG = -1e30
VMEM_LIMIT = 56 * 1024 * 1024


def _cparams(*sem):
    return pltpu.CompilerParams(dimension_semantics=sem, vmem_limit_bytes=VMEM_LIMIT)


def _rms(x, g):
    return x * lax.rsqrt(jnp.mean(x * x, axis=-1, keepdims=True) + EPS) * g


def _layer_norm(x, g, b):
    mu = jnp.mean(x, axis=-1, keepdims=True)
    xc = x - mu
    var = jnp.mean(xc * xc, axis=-1, keepdims=True)
    return xc * lax.rsqrt(var + EPS) * g + b


def _rope_chunk(x, cos, sin_signed):
    lane = lax.broadcasted_iota(jnp.int32, x.shape, 1)
    first_half = (lane % HEAD_DIM) < (HEAD_DIM // 2)
    swapped = jnp.where(first_half, pltpu.roll(x, LANES - HEAD_DIM // 2, 1), pltpu.roll(x, HEAD_DIM // 2, 1))
    return x * cos + swapped * sin_signed


def _pack_bf16_pair(h):
    lo = lax.bitcast_convert_type(h[:, :HALF_D].astype(BF16).astype(F32), U32)
    hi = lax.bitcast_convert_type(h[:, HALF_D:].astype(BF16).astype(F32), U32)
    return (hi & jnp.uint32(0xFFFF0000)) | (lo >> 16)


def _unpack_bf16_pair(p):
    lo = lax.bitcast_convert_type(p << 16, F32).astype(BF16)
    hi = lax.bitcast_convert_type(p & jnp.uint32(0xFFFF0000), F32).astype(BF16)
    return lo, hi


def _route(logits, prior_counts, ltri):
    rows = logits.shape[0]
    lane = lax.broadcasted_iota(jnp.int32, (rows, LANES), 1).astype(F32)
    is_g = lane < N_EXPERT_GROUPS
    gl = jnp.where(is_g, logits, NEG)
    gmax = jnp.max(gl, axis=1, keepdims=True)
    grp = jnp.min(jnp.where(gl == gmax, lane, float(LANES)), axis=1, keepdims=True)
    p_grp = 1.0 / jnp.sum(jnp.where(is_g, jnp.exp(gl - gmax), 0.0), axis=1, keepdims=True)
    lo = N_EXPERT_GROUPS + grp * EXPERTS_PER_GROUP
    el = jnp.where((lane >= lo) & (lane < lo + EXPERTS_PER_GROUP), logits, NEG)
    v1 = jnp.max(el, axis=1, keepdims=True)
    i1 = jnp.min(jnp.where(el == v1, lane, float(LANES)), axis=1, keepdims=True)
    el2 = jnp.where(lane == i1, NEG, el)
    v2 = jnp.max(el2, axis=1, keepdims=True)
    i2 = jnp.min(jnp.where(el2 == v2, lane, float(LANES)), axis=1, keepdims=True)
    t = jnp.exp(v2 - v1)
    gate1 = p_grp / (1.0 + t)
    gate2 = p_grp * t / (1.0 + t)
    e1 = i1 - N_EXPERT_GROUPS
    e2 = i2 - N_EXPERT_GROUPS
    hit1 = lane == e1
    hit2 = lane == e2
    onehot = jnp.where(hit1 | hit2, 1.0, 0.0)
    rank = jnp.dot(ltri, onehot.astype(BF16), preferred_element_type=F32) + prior_counts
    pos1 = jnp.sum(jnp.where(hit1, rank, 0.0), axis=1, keepdims=True)
    pos2 = jnp.sum(jnp.where(hit2, rank, 0.0), axis=1, keepdims=True)
    slab = jnp.where(lane == 0, e1, jnp.where(lane == 1, e2, jnp.where(lane == 2, pos1, jnp.where(
        lane == 3, pos2, jnp.where(lane == 4, gate1, jnp.where(lane == 5, gate2, 0.0))))))
    return slab, prior_counts + jnp.sum(onehot, axis=0, keepdims=True)


def _inproj_kernel(x_ref, g1_ref, w_ref, b_ref, lng_ref, lnb_ref, cos_ref, sin_ref,
                   uv_ref, qkv0_ref, qkv1_ref, qkv2_ref, gates_ref, kv0_ref, kv1_ref, kv2_ref, de_ref):
    tm = x_ref.shape[0]
    hb = _rms(x_ref[...], g1_ref[...]).astype(BF16)

    def seg(lo, width):
        return jnp.dot(hb, w_ref[:, lo:lo + width], preferred_element_type=F32) + b_ref[:, lo:lo + width]

    uv_ref[:, :A_WIDTH] = jax.nn.gelu(seg(COL_U, A_WIDTH)).astype(BF16)
    uv_ref[:, A_WIDTH:] = _layer_norm(jax.nn.gelu(seg(COL_V, A_WIDTH)), lng_ref[...], lnb_ref[...]).astype(BF16)
    gates_ref[:, :D_MODEL] = jax.nn.sigmoid(seg(COL_GA, D_MODEL)).astype(BF16)
    gates_ref[:, D_MODEL:] = jax.nn.sigmoid(seg(COL_GB, D_MODEL)).astype(BF16)

    cos = cos_ref[...]
    sin = sin_ref[...]
    qkv_refs = (qkv0_ref, qkv1_ref, qkv2_ref)
    kv_refs = (kv0_ref, kv1_ref, kv2_ref)
    for g, d in enumerate(DILATIONS):
        q = seg(COL_Q + g * GROUP_W, GROUP_W)
        k = seg(COL_K + g * GROUP_W, GROUP_W)
        v = seg(COL_VB + g * GROUP_W, GROUP_W)
        chunks = GROUP_W // LANES
        for c in range(chunks):
            sl = slice(c * LANES, (c + 1) * LANES)
            de_ref[c] = _rope_chunk(q[:, sl], cos, sin) * (HEAD_DIM ** -0.5)
            de_ref[chunks + c] = _rope_chunk(k[:, sl], cos, sin)
            de_ref[2 * chunks + c] = v[:, sl]
        for c in range(3 * chunks):
            sl = slice(c * LANES, (c + 1) * LANES)
            if d == 1:
                qkv_refs[g][:, sl] = de_ref[c].astype(BF16)
            else:
                for r in range(d):
                    qkv_refs[g][0, r, :, sl] = de_ref[c, pl.ds(r, tm // d, stride=d), :].astype(BF16)
        kv_rows = kv_refs[g].shape[2]
        for c in range(2 * chunks):
            kv_refs[g][0, c * LANES:(c + 1) * LANES, :] = de_ref[chunks + c, tm - kv_rows:, :].T


def _inproj(x2, g1, w_bf, b_in, lng, lnb, cos_t, sin_t):
    tm = TM_IN
    tiles_per_seq = SEQ // tm
    n_tiles = T_PROMPT // tm
    const = lambda i: (0, 0)

    def kv_spec(w):
        rows = min(w, tm)
        first = tiles_per_seq - w // rows
        return pl.BlockSpec((1, 2 * GROUP_W, rows),
                            lambda i: (i // tiles_per_seq, 0, jnp.maximum(i % tiles_per_seq - first, 0)))

    def regrouped_spec(d):
        return pl.BlockSpec((1, d, tm // d, 3 * GROUP_W), lambda i: (i // tiles_per_seq, 0, i % tiles_per_seq, 0))

    return pl.pallas_call(
        _inproj_kernel,
        grid=(n_tiles,),
        in_specs=[
            pl.BlockSpec((tm, D_MODEL), lambda i: (i, 0)),
            pl.BlockSpec((1, D_MODEL), const),
            pl.BlockSpec((D_MODEL, IN_WIDTH), const, pipeline_mode=pl.Buffered(1)),
            pl.BlockSpec((1, IN_WIDTH), const),
            pl.BlockSpec((1, A_WIDTH), const),
            pl.BlockSpec((1, A_WIDTH), const),
            pl.BlockSpec((tm, LANES), lambda i: (i % tiles_per_seq, 0)),
            pl.BlockSpec((tm, LANES), lambda i: (i % tiles_per_seq, 0)),
        ],
        out_specs=[
            pl.BlockSpec((tm, 2 * A_WIDTH), lambda i: (i, 0)),
            pl.BlockSpec((tm, 3 * GROUP_W), lambda i: (i, 0)),
            regrouped_spec(DILATIONS[1]),
            regrouped_spec(DILATIONS[2]),
            pl.BlockSpec((tm, 2 * D_MODEL), lambda i: (i, 0)),
            kv_spec(WINDOWS[0]), kv_spec(WINDOWS[1]), kv_spec(WINDOWS[2]),
        ],
        out_shape=[
            jax.ShapeDtypeStruct((T_PROMPT, 2 * A_WIDTH), BF16),
            jax.ShapeDtypeStruct((T_PROMPT, 3 * GROUP_W), BF16),
            jax.ShapeDtypeStruct((BATCH, DILATIONS[1], SEQ // DILATIONS[1], 3 * GROUP_W), BF16),
            jax.ShapeDtypeStruct((BATCH, DILATIONS[2], SEQ // DILATIONS[2], 3 * GROUP_W), BF16),
            jax.ShapeDtypeStruct((T_PROMPT, 2 * D_MODEL), BF16),
            jax.ShapeDtypeStruct((BATCH, 2 * GROUP_W, WINDOWS[0]), F32),
            jax.ShapeDtypeStruct((BATCH, 2 * GROUP_W, WINDOWS[1]), F32),
            jax.ShapeDtypeStruct((BATCH, 2 * GROUP_W, WINDOWS[2]), F32),
        ],
        scratch_shapes=[pltpu.VMEM((3 * GROUP_W // LANES, tm, LANES), F32)],
        compiler_params=_cparams("arbitrary"),
        name="inproj",
    )(x2, g1, w_bf, b_in, lng, lnb, cos_t, sin_t)


def _attn_kernel(c0_ref, p0_ref, c1_ref, p1_ref, c2_ref, p2_ref, o_ref, acc_ref, m_ref, l_ref):
    n = pl.program_id(1)
    qi = lax.broadcasted_iota(jnp.int32, (HEADS * SPAN, 2 * SPAN), 0) % SPAN
    ki = lax.broadcasted_iota(jnp.int32, (HEADS * SPAN, 2 * SPAN), 1)
    band = (ki >= qi) & (ki <= qi + SPAN)
    band_first = band & ((ki >= SPAN) | (n > 0))
    q_head = lax.broadcasted_iota(jnp.int32, (SPAN, GROUP_W), 1) // HEAD_DIM
    kv_head = lax.broadcasted_iota(jnp.int32, (2 * SPAN, GROUP_W), 1) // HEAD_DIM
    first_head_lanes = lax.broadcasted_iota(jnp.int32, (SPAN, LANES), 1) < HEAD_DIM
    contract_last = (((1,), (1,)), ((), ()))
    k_cols = slice(GROUP_W, 2 * GROUP_W)
    v_cols = slice(2 * GROUP_W, 3 * GROUP_W)

    def attend(blocks, fresh):
        scores = []
        for q, keys, _, mask, _ in blocks:
            q_heads = jnp.concatenate([jnp.where(q_head == h, q, jnp.zeros_like(q)) for h in range(HEADS)], axis=0)
            scores.append(jnp.where(mask, lax.dot_general(q_heads, keys, contract_last, preferred_element_type=F32), NEG))
        soft = []
        for s in scores:
            m = jnp.max(s, axis=1, keepdims=True)
            p = jnp.exp(s - m)
            soft.append((m, jnp.sum(p, axis=1, keepdims=True), p.astype(BF16)))
        accs = []
        for (_, _, pb), (_, _, vals, _, _) in zip(soft, blocks):
            p_heads = jnp.concatenate([pb[h * SPAN:(h + 1) * SPAN] for h in range(HEADS)], axis=1)
            v_heads = jnp.concatenate([jnp.where(kv_head == h, vals, jnp.zeros_like(vals)) for h in range(HEADS)],
                                      axis=0)
            accs.append(jnp.dot(p_heads, v_heads, preferred_element_type=F32))
        for (m, l, _), acc, (_, _, _, _, tok_rows) in zip(soft, accs, blocks):
            merge(m, l, acc, tok_rows, fresh)

    def merge(m, l, acc, tok_rows, fresh):
        heads_per_chunk = LANES // HEAD_DIM
        for c in range(GROUP_W // LANES):
            sl = slice(c * LANES, (c + 1) * LANES)
            lo = slice(c * heads_per_chunk * SPAN, (c * heads_per_chunk + 1) * SPAN)
            hi = slice((c * heads_per_chunk + 1) * SPAN, (c * heads_per_chunk + 2) * SPAN)
            m_b = jnp.where(first_head_lanes, m[lo], m[hi])
            l_b = jnp.where(first_head_lanes, l[lo], l[hi])
            if fresh:
                m_ref[c, tok_rows, :] = m_b
                l_ref[c, tok_rows, :] = l_b
                acc_ref[c, tok_rows, :] = acc[:, sl]
                continue
            m_old = m_ref[c, tok_rows, :]
            m_new = jnp.maximum(m_old, m_b)
            a_old = jnp.exp(m_old - m_new)
            a_blk = jnp.exp(m_b - m_new)
            l_ref[c, tok_rows, :] = a_old * l_ref[c, tok_rows, :] + a_blk * l_b
            acc_ref[c, tok_rows, :] = a_old * acc_ref[c, tok_rows, :] + a_blk * acc[:, sl]
            m_ref[c, tok_rows, :] = m_new

    def tok_rows(r, j, d):
        start = j * (SPAN * d) + r
        return pl.ds(start, SPAN) if d == 1 else pl.ds(start, SPAN, stride=d)

    def first_block(c_ref, p_ref, r, d):
        keys = jnp.concatenate([p_ref[0, r, :, k_cols], c_ref[0, r, :SPAN, k_cols]], axis=0)
        vals = jnp.concatenate([p_ref[0, r, :, v_cols], c_ref[0, r, :SPAN, v_cols]], axis=0)
        return c_ref[0, r, :SPAN, :GROUP_W], keys, vals, band_first, tok_rows(r, 0, d)

    def later_block(c_ref, r, j, d):
        q_rows = pl.ds(pl.multiple_of(j * SPAN, SPAN), SPAN)
        kv_rows = pl.ds(pl.multiple_of((j - 1) * SPAN, SPAN), 2 * SPAN)
        return (c_ref[0, r, q_rows, :GROUP_W], c_ref[0, r, kv_rows, k_cols], c_ref[0, r, kv_rows, v_cols],
                band, tok_rows(r, j, d))

    def stream(c_ref, p_ref, r, d, fresh):
        later = c_ref.shape[2] // SPAN - 1
        assert later % ATT_GROUP == 0
        attend([first_block(c_ref, p_ref, r, d)], fresh)

        def body(i, carry):
            j0 = 1 + i * ATT_GROUP
            attend([later_block(c_ref, r, j0 + k, d) for k in range(ATT_GROUP)], fresh)
            return carry

        lax.fori_loop(0, later // ATT_GROUP, body, 0)

    groups = sorted(zip(DILATIONS, ((c0_ref, p0_ref), (c1_ref, p1_ref), (c2_ref, p2_ref))), key=lambda g: -g[0])
    for idx, (d, (c_ref, p_ref)) in enumerate(groups):
        fresh = idx == 0
        if d == 1:
            stream(c_ref, p_ref, 0, d, fresh)
        elif c_ref.shape[2] == SPAN:
            def pair(i, carry, c_ref=c_ref, p_ref=p_ref, d=d, fresh=fresh):
                attend([first_block(c_ref, p_ref, i * ATT_UNROLL + k, d) for k in range(ATT_UNROLL)], fresh)
                return carry
            lax.fori_loop(0, d // ATT_UNROLL, pair, 0)
        else:
            lax.fori_loop(0, d, lambda r, carry, c_ref=c_ref, p_ref=p_ref, d=d, fresh=fresh:
                          (stream(c_ref, p_ref, r, d, fresh), carry)[1], 0)

    for c in range(GROUP_W // LANES):
        o_ref[0, :, c * LANES:(c + 1) * LANES] = (acc_ref[c] / l_ref[c]).astype(BF16)


def _attention(qkv_by_group):
    in_specs, args = [], []
    for qkv, d in zip(qkv_by_group, DILATIONS):
        rows = TA // d
        blocks_per_tile = rows // SPAN
        in_specs.append(pl.BlockSpec((1, d, rows, 3 * GROUP_W), lambda b, n: (b, 0, n, 0)))
        in_specs.append(pl.BlockSpec((1, d, SPAN, 3 * GROUP_W),
                                     lambda b, n, k=blocks_per_tile: (b, 0, jnp.maximum(n * k - 1, 0), 0)))
        args += [qkv, qkv]
    return pl.pallas_call(
        _attn_kernel,
        grid=(BATCH, SEQ // TA),
        in_specs=in_specs,
        out_specs=pl.BlockSpec((1, TA, GROUP_W), lambda b, n: (b, n, 0)),
        out_shape=jax.ShapeDtypeStruct((BATCH, SEQ, GROUP_W), BF16),
        scratch_shapes=[pltpu.VMEM((GROUP_W // LANES, TA, LANES), F32)] * 3,
        compiler_params=_cparams("arbitrary", "arbitrary"),
        name="attn",
    )(*args)


def _merge_kernel(x_ref, uv_ref, gates_ref, ob_ref,
                  wsp_ref, bsp_ref, wa_ref, wb_ref, wo_ref, g2_ref, wr_ref, br_ref, ltri_ref,
                  x1_ref, h2p_ref, rt_ref, rtt_ref, cnt_ref, run_ref):
    @pl.when(pl.program_id(0) == 0)
    def _():
        run_ref[...] = jnp.zeros_like(run_ref)

    lane = lax.broadcasted_iota(jnp.int32, (CHUNK, LANES), 1)
    left = lane < A_GROUP_DIM
    zero = jnp.zeros((CHUNK, LANES), BF16)
    n_chunks, n_pairs = TM // CHUNK, A_GROUPS // 2
    chunk_rows = lambda c: slice(c * CHUNK, (c + 1) * CHUNK)
    mixes = []
    for c in range(n_chunks):
        for p in range(n_pairs):
            vp = uv_ref[chunk_rows(c), A_WIDTH + p * LANES:A_WIDTH + (p + 1) * LANES]
            rhs = jnp.concatenate([jnp.where(left, vp, zero), jnp.where(left, zero, vp)], axis=0)
            mixes.append(jnp.dot(wsp_ref[p], rhs, preferred_element_type=F32))
    sa_chunks = []
    for c in range(n_chunks):
        mixed = jnp.concatenate(mixes[c * n_pairs:(c + 1) * n_pairs], axis=1) + bsp_ref[...]
        sa_chunks.append((uv_ref[chunk_rows(c), :A_WIDTH].astype(F32) * mixed).astype(BF16))
    s_a = jnp.concatenate(sa_chunks, axis=0)

    a = jnp.dot(s_a, wa_ref[...], preferred_element_type=F32)
    b = jnp.dot(ob_ref[...], wb_ref[...], preferred_element_type=F32)
    merged = gates_ref[:, :D_MODEL].astype(F32) * a + gates_ref[:, D_MODEL:].astype(F32) * b
    x1 = x_ref[...] + jnp.dot(merged.astype(BF16), wo_ref[...], preferred_element_type=F32)
    x1_ref[...] = x1

    h2 = _rms(x1, g2_ref[...])
    h2p_ref[...] = _pack_bf16_pair(h2)
    logits = jnp.dot(h2.astype(BF16), wr_ref[...], preferred_element_type=F32) + br_ref[...]
    slab, counts = _route(logits, run_ref[...], ltri_ref[...])
    rt_ref[...] = slab
    rtt_ref[...] = slab.T[:ROUTE_ROWS, :]
    run_ref[...] = counts
    cnt_ref[...] = counts


def _merge(x2, uv, gates, ob, wsp, bsp, wa, wb, wo, g2, wr, br, ltri):
    n_tiles = T_PROMPT // TM
    tile = lambda w: pl.BlockSpec((TM, w), lambda i: (i, 0))
    full = lambda a: pl.BlockSpec(a.shape, lambda i: (0,) * a.ndim)
    return pl.pallas_call(
        _merge_kernel,
        grid=(n_tiles,),
        in_specs=[tile(D_MODEL), tile(2 * A_WIDTH), tile(2 * D_MODEL), tile(GROUP_W),
                  full(wsp), full(bsp), full(wa), full(wb), full(wo), full(g2), full(wr), full(br),
                  full(ltri)],
        out_specs=[tile(D_MODEL), tile(HALF_D), tile(LANES), pl.BlockSpec((ROUTE_ROWS, TM), lambda i: (i, 0)),
                   pl.BlockSpec((1, LANES), lambda i: (0, 0))],
        out_shape=[
            jax.ShapeDtypeStruct((T_PROMPT, D_MODEL), F32),
            jax.ShapeDtypeStruct((T_PROMPT, HALF_D), U32),
            jax.ShapeDtypeStruct((T_PROMPT, LANES), F32),
            jax.ShapeDtypeStruct((n_tiles * ROUTE_ROWS, TM), F32),
            jax.ShapeDtypeStruct((1, LANES), F32),
        ],
        scratch_shapes=[pltpu.VMEM((1, LANES), F32)],
        compiler_params=_cparams("arbitrary"),
        name="merge_route",
    )(x2, uv, gates, ob, wsp, bsp, wa, wb, wo, g2, wr, br, ltri)


SAMPLE_COLS = 768


def _sample_inproj_kernel(x_ref, g1_ref, w_ref, b_ref, z_ref):
    hb = _rms(x_ref[...], g1_ref[...]).astype(BF16)
    z_ref[...] = jnp.dot(hb, w_ref[...], preferred_element_type=F32) + b_ref[...]


def _sample_inproj(xs, g1, w_in, b_in):
    const = lambda j: (0, 0)
    return pl.pallas_call(
        _sample_inproj_kernel,
        grid=(IN_WIDTH // SAMPLE_COLS,),
        in_specs=[pl.BlockSpec((DEC_BATCH, D_MODEL), const), pl.BlockSpec((1, D_MODEL), const),
                  pl.BlockSpec((D_MODEL, SAMPLE_COLS), lambda j: (0, j)),
                  pl.BlockSpec((1, SAMPLE_COLS), lambda j: (0, j))],
        out_specs=pl.BlockSpec((DEC_BATCH, SAMPLE_COLS), lambda j: (0, j)),
        out_shape=jax.ShapeDtypeStruct((DEC_BATCH, IN_WIDTH), F32),
        compiler_params=_cparams("arbitrary"),
        name="sample_inproj",
    )(xs, g1, w_in, b_in)


def _sample_pre_kernel(z_ref, lng_ref, lnb_ref, ws0_ref, bs0_ref, cos_ref, sin_ref,
                       vrow_ref, sa_ref, qr_ref, kr_ref, kv0_ref, kv1_ref, kv2_ref):
    u = jax.nn.gelu(z_ref[:, COL_U:COL_U + A_WIDTH])
    va = _layer_norm(jax.nn.gelu(z_ref[:, COL_V:COL_V + A_WIDTH]), lng_ref[...], lnb_ref[...])
    vrow_ref[...] = va
    sa_ref[...] = u * (ws0_ref[...] * va.astype(BF16).astype(F32) + bs0_ref[...])
    cos, sin = cos_ref[...], sin_ref[...]
    for g, kv_ref in enumerate((kv0_ref, kv1_ref, kv2_ref)):
        for c in range(GROUP_W // LANES):
            off = g * GROUP_W + c * LANES
            qr_ref[:, off:off + LANES] = (_rope_chunk(z_ref[:, COL_Q + off:COL_Q + off + LANES], cos, sin)
                                          * (HEAD_DIM ** -0.5))
            kr = _rope_chunk(z_ref[:, COL_K + off:COL_K + off + LANES], cos, sin)
            kr_ref[:, off:off + LANES] = kr
            kv_ref[:, c * LANES:(c + 1) * LANES] = kr
        kv_ref[:, GROUP_W:] = z_ref[:, COL_VB + g * GROUP_W:COL_VB + (g + 1) * GROUP_W]


def _sample_pre(z, lng, lnb, ws0, bs0, cos_s, sin_s):
    row = lambda w: jax.ShapeDtypeStruct((DEC_BATCH, w), F32)
    return pl.pallas_call(
        _sample_pre_kernel,
        out_shape=[row(A_WIDTH), row(A_WIDTH), row(QK_W), row(QK_W),
                   row(2 * GROUP_W), row(2 * GROUP_W), row(2 * GROUP_W)],
        compiler_params=pltpu.CompilerParams(vmem_limit_bytes=VMEM_LIMIT),
        name="sample_pre",
    )(z, lng, lnb, ws0, bs0, cos_s, sin_s)


def _sample_attn_kernel(q_ref, k_ref, v_ref, c0_ref, c1_ref, c2_ref, comb_ref):
    as_operand = lambda t: t.astype(BF16).astype(F32)
    contract_last = (((1,), (1,)), ((), ()))
    pad_rows = 8 // HEADS
    groups = tuple(zip((c0_ref, c1_ref, c2_ref), DILATIONS))
    pairs = [(h, g) for h in range(HEADS) for g in range(len(groups))]
    scores = {}
    for h, g in pairs:
        c_ref, d = groups[g]
        q_rows = jnp.concatenate([q_ref[0, g]] * pad_rows, axis=0).astype(BF16)
        s = jnp.dot(q_rows, c_ref[0, 0, h].astype(BF16), preferred_element_type=F32)[h:h + 1]
        if d > 1:
            pos = lax.broadcasted_iota(jnp.int32, s.shape, 1)
            s = jnp.where(pos % d == 0, s, NEG)
        scores[h, g] = s
    probs = {}
    for h, g in pairs:
        s = scores[h, g]
        qn, kn = (as_operand(r[0, g, h:h + 1, :]) for r in (q_ref, k_ref))
        s_n = jnp.sum(qn * kn, axis=1, keepdims=True)
        m = jnp.maximum(jnp.max(s, axis=1, keepdims=True), s_n)
        lse = m + jnp.log(jnp.sum(jnp.exp(s - m), axis=1, keepdims=True) + jnp.exp(s_n - m))
        probs[h, g] = (jnp.broadcast_to(jnp.exp(s - lse), (8, s.shape[1])).astype(BF16),
                       as_operand(jnp.exp(s_n - lse)), lse)
    outs = {}
    for h, g in pairs:
        p_rows, p_n, _ = probs[h, g]
        o = lax.dot_general(p_rows, groups[g][0][0, 1, h].astype(BF16), contract_last, preferred_element_type=F32)
        outs[h, g] = o[0:1] + p_n * as_operand(v_ref[0, g, h:h + 1, :])
    for h in range(HEADS):
        lses = [probs[h, g][2] for g in range(len(groups))]
        mx = jnp.maximum(jnp.maximum(lses[0], lses[1]), lses[2])
        ws = [jnp.exp(t - mx) for t in lses]
        tot = ws[0] + ws[1] + ws[2]
        comb_ref[0, h:h + 1, :] = (ws[0] * outs[h, 0] + ws[1] * outs[h, 1] + ws[2] * outs[h, 2]) / tot


def _sample_attn(qt, kt, vt, caches):
    new_spec = pl.BlockSpec((1, 3, HEADS, HEAD_DIM), lambda i: (i, 0, 0, 0))
    cache_spec = lambda c: pl.BlockSpec((1,) + c.shape[1:], lambda i: (i, 0, 0, 0, 0))
    return pl.pallas_call(
        _sample_attn_kernel,
        grid=(DEC_BATCH,),
        in_specs=[new_spec, new_spec, new_spec, cache_spec(caches[0]), cache_spec(caches[1]), cache_spec(caches[2])],
        out_specs=pl.BlockSpec((1, HEADS, HEAD_DIM), lambda i: (i, 0, 0)),
        out_shape=jax.ShapeDtypeStruct((DEC_BATCH, HEADS, HEAD_DIM), F32),
        compiler_params=_cparams("arbitrary"),
        name="sample_attn",
    )(qt, kt, vt, caches[0], caches[1], caches[2])


def _sample_merge_kernel(x_ref, z_ref, sa_ref, comb_ref, wa_ref, wb_ref, wo_ref, g2_ref, wr_ref, br_ref,
                         ltri_ref, cnt_in_ref, x1_ref, h2p_ref, rt_ref, cnt_ref):
    dot = lambda p, q: jnp.dot(p.astype(BF16), q, preferred_element_type=F32)
    a = dot(sa_ref[...], wa_ref[...])
    b = dot(comb_ref[...], wb_ref[...])
    merged = (jax.nn.sigmoid(z_ref[:, COL_GA:COL_GA + D_MODEL]) * a
              + jax.nn.sigmoid(z_ref[:, COL_GB:COL_GB + D_MODEL]) * b)
    x1 = x_ref[...] + dot(merged, wo_ref[...])
    x1_ref[...] = x1
    h2 = _rms(x1, g2_ref[...])
    h2p_ref[...] = _pack_bf16_pair(h2)
    logits = dot(h2, wr_ref[...]) + br_ref[...]
    slab, counts = _route(logits, cnt_in_ref[...], ltri_ref[...])
    rt_ref[...] = slab
    cnt_ref[...] = counts


def _sample_merge(xs, z, s_a, comb, wa, wb, wo, g2, wr, br, ltri, counts):
    return pl.pallas_call(
        _sample_merge_kernel,
        out_shape=[jax.ShapeDtypeStruct((DEC_BATCH, D_MODEL), F32),
                   jax.ShapeDtypeStruct((DEC_BATCH, HALF_D), U32),
                   jax.ShapeDtypeStruct((DEC_BATCH, LANES), F32),
                   jax.ShapeDtypeStruct((1, LANES), F32)],
        compiler_params=pltpu.CompilerParams(vmem_limit_bytes=VMEM_LIMIT),
        name="sample_merge_route",
    )(xs, z, s_a, comb, wa, wb, wo, g2, wr, br, ltri, counts)


def _dispatch_kernel(dest0_ref, dest1_ref, h_ref, xs_in_ref, xs_ref, sem):
    del xs_in_ref
    groups, sub = h_ref.shape[0], h_ref.shape[1]
    rows = groups * sub
    base = pl.program_id(0) * rows

    def start(j, c):
        for k in range(sub):
            for dest_ref in (dest0_ref, dest1_ref):
                d = dest_ref[base + j * sub + k]
                pltpu.make_async_copy(h_ref.at[j, pl.ds(k, 1)], xs_ref.at[pl.ds(d, 1)], sem).start()
        return c

    lax.fori_loop(0, groups, start, 0)
    for _ in range(2):
        pltpu.make_async_copy(xs_ref.at[pl.ds(0, rows)], xs_ref.at[pl.ds(0, rows)], sem).wait()


def _dispatch(dest, h2p, xs, rows):
    n = h2p.shape[0]
    sub = 8
    h2p = h2p.reshape(n // sub, sub, HALF_D)
    return pl.pallas_call(
        _dispatch_kernel,
        grid_spec=pltpu.PrefetchScalarGridSpec(
            num_scalar_prefetch=2,
            grid=(n // rows,),
            in_specs=[pl.BlockSpec((rows // sub, sub, HALF_D), lambda i, *_: (i, 0, 0)),
                      pl.BlockSpec(memory_space=pl.ANY)],
            out_specs=pl.BlockSpec(memory_space=pl.ANY),
            scratch_shapes=[pltpu.SemaphoreType.DMA(())],
        ),
        out_shape=jax.ShapeDtypeStruct(xs.shape, xs.dtype),
        input_output_aliases={3: 0},
        compiler_params=_cparams("arbitrary"),
        name=f"moe_dispatch_{rows}",
    )(dest[0], dest[1], h2p, xs)


def _ffn_kernel(be_ref, nused_ref, nexte_ref, wslot_ref, xs_ref, wg_ref, wu_ref, wd_ref, ys_ref,
                wg_b, wu_b, wd_b, wg_f, wu_f, wd_f, wsem):
    i = pl.program_id(0)
    live = i < nused_ref[0]
    new_expert = jnp.logical_and(live, jnp.logical_or(i == 0, be_ref[i] != be_ref[jnp.maximum(i - 1, 0)]))

    def weight_copies(e, slot):
        return [pltpu.make_async_copy(src.at[e], dst.at[slot], wsem.at[slot])
                for src, dst in ((wg_ref, wg_f), (wu_ref, wu_f), (wd_ref, wd_f))]

    @pl.when(i == 0)
    def _():
        for cp in weight_copies(be_ref[0], 0):
            cp.start()

    @pl.when(new_expert)
    def _():
        slot = wslot_ref[i]
        for cp in weight_copies(be_ref[i], slot):
            cp.wait()
        wg_b[...] = wg_f[slot].astype(BF16)
        wu_b[...] = wu_f[slot].astype(BF16)
        wd_b[...] = wd_f[slot].astype(BF16)

        @pl.when(nexte_ref[i] >= 0)
        def _():
            for cp in weight_copies(nexte_ref[i], 1 - slot):
                cp.start()

    @pl.when(live)
    def _():
        x = jnp.concatenate(_unpack_bf16_pair(xs_ref[...]), axis=1)
        gate = jnp.dot(x, wg_b[...], preferred_element_type=F32)
        up = jnp.dot(x, wu_b[...], preferred_element_type=F32)
        hid = (jax.nn.silu(gate) * up).astype(BF16)
        ys_ref[...] = jnp.dot(hid, wd_b[...], preferred_element_type=F32)

    @pl.when(i >= nused_ref[0])
    def _():
        ys_ref[...] = jnp.zeros_like(ys_ref)


def _expert_ffn(block_e, nused, xs, w_gate, w_up, w_down):
    idx = jnp.arange(MOE_NB, dtype=jnp.int32)
    change = ((block_e != jnp.roll(block_e, 1)) | (idx == 0)) & (idx < nused[0])
    wslot = ((jnp.cumsum(change.astype(jnp.int32)) - 1) & 1).astype(jnp.int32)
    change_at = jnp.where(change, idx, MOE_NB)
    next_change = jnp.flip(lax.cummin(jnp.flip(jnp.concatenate([change_at[1:], jnp.full((1,), MOE_NB, jnp.int32)]))))
    nexte = jnp.where(next_change < MOE_NB, block_e[jnp.minimum(next_change, MOE_NB - 1)], -1).astype(jnp.int32)
    hbm = pl.BlockSpec(memory_space=pl.ANY)
    return pl.pallas_call(
        _ffn_kernel,
        grid_spec=pltpu.PrefetchScalarGridSpec(
            num_scalar_prefetch=4,
            grid=(MOE_NB,),
            in_specs=[pl.BlockSpec((MOE_TB, HALF_D), lambda i, be, nu, *_: (jnp.minimum(i, nu[0] - 1), 0)),
                      hbm, hbm, hbm],
            out_specs=pl.BlockSpec((MOE_TB, D_MODEL), lambda i, *_: (i, 0)),
            scratch_shapes=[pltpu.VMEM((D_MODEL, D_EXPERT), BF16), pltpu.VMEM((D_MODEL, D_EXPERT), BF16),
                            pltpu.VMEM((D_EXPERT, D_MODEL), BF16),
                            pltpu.VMEM((2, D_MODEL, D_EXPERT), F32), pltpu.VMEM((2, D_MODEL, D_EXPERT), F32),
                            pltpu.VMEM((2, D_EXPERT, D_MODEL), F32), pltpu.SemaphoreType.DMA((2,))],
        ),
        out_shape=jax.ShapeDtypeStruct((MOE_ROWS, D_MODEL), F32),
        compiler_params=_cparams("arbitrary"),
        name="moe_ffn",
    )(block_e, nused, nexte, wslot, xs, w_gate, w_up, w_down)


def _combine_kernel(dest0_ref, dest1_ref, x1_ref, rt_ref, gf_ref, ys_ref, y_ref, ya, yb, sem):
    rows = x1_ref.shape[0]
    base = pl.program_id(0) * rows

    def row_copy(t, dest_ref, buf):
        return pltpu.make_async_copy(ys_ref.at[pl.ds(dest_ref[base + t], 1)], buf.at[pl.ds(t, 1)], sem)

    def start(t, c):
        row_copy(t, dest0_ref, ya).start()
        row_copy(t, dest1_ref, yb).start()
        return c

    lax.fori_loop(0, rows, start, 0, unroll=ROW_DMA_UNROLL)
    pltpu.make_async_copy(ys_ref.at[pl.ds(0, rows)], ya, sem).wait()
    pltpu.make_async_copy(ys_ref.at[pl.ds(0, rows)], yb, sem).wait()
    x2 = x1_ref[...] + rt_ref[:, 4:5] * ya[...] + rt_ref[:, 5:6] * yb[...]
    y_ref[...] = _rms(x2, gf_ref[...])


def _combine(dest, x1, rt, gf, ys, rows):
    n = x1.shape[0]
    return pl.pallas_call(
        _combine_kernel,
        grid_spec=pltpu.PrefetchScalarGridSpec(
            num_scalar_prefetch=2,
            grid=(n // rows,),
            in_specs=[pl.BlockSpec((rows, D_MODEL), lambda i, *_: (i, 0)),
                      pl.BlockSpec((rows, LANES), lambda i, *_: (i, 0)),
                      pl.BlockSpec((1, D_MODEL), lambda i, *_: (0, 0)),
                      pl.BlockSpec(memory_space=pl.ANY)],
            out_specs=pl.BlockSpec((rows, D_MODEL), lambda i, *_: (i, 0)),
            scratch_shapes=[pltpu.VMEM((rows, D_MODEL), F32), pltpu.VMEM((rows, D_MODEL), F32),
                            pltpu.SemaphoreType.DMA(())],
        ),
        out_shape=jax.ShapeDtypeStruct((n, D_MODEL), F32),
        compiler_params=_cparams("arbitrary"),
        name=f"moe_combine_{rows}",
    )(dest[0], dest[1], x1, rt, gf, ys)


def _rope_tables(pos, xp):
    half = HEAD_DIM // 2
    inv = 1.0 / (xp.float32(ROPE_THETA) ** (xp.arange(half, dtype=xp.float32) * xp.float32(2.0 / HEAD_DIM)))
    ang = pos.astype(xp.float32)[:, None] * inv[None, :].astype(xp.float32)
    cos, sin = xp.cos(ang), xp.sin(ang)
    return xp.concatenate([cos, cos, cos, cos], axis=1), xp.concatenate([-sin, sin, -sin, sin], axis=1)


def _slot_dest(ids, pos, pstarts):
    ids = ids.astype(jnp.int32)
    experts = jnp.arange(N_EXPERTS, dtype=jnp.int32).reshape((N_EXPERTS,) + (1,) * ids.ndim)
    start = jnp.sum(jnp.where(ids[None] == experts, pstarts.reshape(experts.shape), 0), axis=0)
    return start + pos.astype(jnp.int32)


def kernel(x_prompt, x_sample, cache_kv_g0, cache_kv_g1, cache_kv_g2, norm1_g, w_in, b_in, a_ln_g, a_ln_b, w_spatial, b_spatial, w_a_proj, w_b_proj, w_o, norm2_g, w_group_router, b_group_router, w_expert_router, b_expert_router, w_gate, w_up, w_down, final_norm_g):
    x2 = x_prompt.reshape(T_PROMPT, D_MODEL)
    xs = x_sample.reshape(DEC_BATCH, D_MODEL)
    g1 = norm1_g[0][None, :]
    g2 = norm2_g[0][None, :]
    gf = final_norm_g[None, :]
    b_in2 = b_in[0][None, :]
    lng, lnb = a_ln_g[0][None, :], a_ln_b[0][None, :]

    causal = jnp.tril(jnp.ones((CHUNK, CHUNK), dtype=bool))
    ws_tril = jnp.where(causal[None], w_spatial[0], 0.0)
    wsp = jnp.concatenate([ws_tril[0::2], ws_tril[1::2]], axis=2).astype(BF16)
    bsp = jnp.repeat(b_spatial[0].T, A_GROUP_DIM, axis=1)
    ws0 = jnp.repeat(ws_tril[:, 0, 0].astype(BF16).astype(F32), A_GROUP_DIM)[None, :]
    bs0 = jnp.repeat(b_spatial[0][:, 0], A_GROUP_DIM)[None, :]
    w_router = jnp.zeros((D_MODEL, LANES), F32)
    w_router = w_router.at[:, :N_EXPERT_GROUPS].set(w_group_router[0])
    w_router = w_router.at[:, N_EXPERT_GROUPS:N_EXPERT_GROUPS + N_EXPERTS].set(w_expert_router[0])
    b_router = jnp.zeros((1, LANES), F32)
    b_router = b_router.at[0, :N_EXPERT_GROUPS].set(b_group_router[0])
    b_router = b_router.at[0, N_EXPERT_GROUPS:N_EXPERT_GROUPS + N_EXPERTS].set(b_expert_router[0])
    w_router = w_router.astype(BF16)
    ltri = jnp.tril(jnp.ones((TM, TM), BF16), -1)
    w_in_b, w_a_b, w_b_b, w_o_b = (w[0].astype(BF16) for w in (w_in, w_a_proj, w_b_proj, w_o))

    cos_p, sin_p = _rope_tables(np.arange(SEQ, dtype=np.int32), np)
    cos_s, sin_s = _rope_tables(jnp.full((1,), PAST_LEN, jnp.int32), jnp)

    uv, qkv0, qkv1, qkv2, gates, kvp0, kvp1, kvp2 = _inproj(
        x2, g1, w_in_b, b_in2, lng, lnb, cos_p, sin_p)
    ob = _attention((qkv0.reshape(BATCH, 1, SEQ, 3 * GROUP_W), qkv1, qkv2)).reshape(T_PROMPT, GROUP_W)
    x1_p, h2p_p, rt_p, rtt_p, counts_p = _merge(
        x2, uv, gates, ob, wsp, bsp, w_a_b, w_b_b, w_o_b, g2, w_router, b_router, ltri)

    z_s = _sample_inproj(xs, g1, w_in_b, b_in2)
    vrow, sa_s, qr_s, kr_s, kvs0, kvs1, kvs2 = _sample_pre(z_s, lng, lnb, ws0, bs0, cos_s, sin_s)
    per_head = lambda t: t.reshape(DEC_BATCH, 3, HEADS, HEAD_DIM)
    caches = [c.transpose(0, 1, 3, 4, 5, 2).reshape(DEC_BATCH, 2, HEADS, HEAD_DIM, c.shape[2])
              for c in (cache_kv_g0, cache_kv_g1, cache_kv_g2)]
    comb_s = _sample_attn(per_head(qr_s), per_head(kr_s), per_head(z_s[:, COL_VB:COL_VB + QK_W]), caches)
    comb_s = comb_s.reshape(DEC_BATCH, GROUP_W)
    x1_s, h2p_s, rt_s, counts = _sample_merge(
        xs, z_s, sa_s, comb_s, w_a_b, w_b_b, w_o_b, g2, w_router, b_router,
        ltri[:DEC_BATCH, :DEC_BATCH], counts_p)

    cnt = counts[0, :N_EXPERTS].astype(jnp.int32)
    padded = (cnt + MOE_TB - 1) // MOE_TB * MOE_TB
    pends = jnp.cumsum(padded)
    pstarts = pends - padded
    block_starts = jnp.arange(MOE_NB, dtype=jnp.int32) * MOE_TB
    block_e = jnp.minimum(jnp.sum((pends[None, :] <= block_starts[:, None]).astype(jnp.int32), axis=1),
                          N_EXPERTS - 1)
    nused = (pends[-1:] // MOE_TB).astype(jnp.int32)
    rtt_p = rtt_p.reshape(T_PROMPT // TM, ROUTE_ROWS, TM)
    dest_p = [_slot_dest(rtt_p[:, k], rtt_p[:, 2 + k], pstarts).reshape(T_PROMPT) for k in range(2)]
    dest_s = [_slot_dest(rt_s[:, k], rt_s[:, 2 + k], pstarts) for k in range(2)]

    rows = jnp.zeros((MOE_ROWS, HALF_D), U32)
    rows = _dispatch(dest_p, h2p_p, rows, TM)
    rows = _dispatch(dest_s, h2p_s, rows, DEC_BATCH)
    ys = _expert_ffn(block_e, nused, rows, w_gate[0], w_up[0], w_down[0])
    y_p = _combine(dest_p, x1_p, rt_p, gf, ys, TM)
    y_s = _combine(dest_s, x1_s, rt_s, gf, ys, DEC_BATCH)

    kv_shape = lambda n, w: (1, n, w, 2, HEADS, HEAD_DIM)
    window_rows = lambda t: t.reshape(1, BATCH, 2, HEADS, HEAD_DIM, t.shape[2]).transpose(0, 1, 5, 2, 3, 4)
    return (y_p.reshape(BATCH, SEQ, D_MODEL), y_s.reshape(DEC_BATCH, 1, D_MODEL),
            window_rows(kvp0), window_rows(kvp1), window_rows(kvp2),
            kvs0.reshape(kv_shape(DEC_BATCH, 1)), kvs1.reshape(kv_shape(DEC_BATCH, 1)),
            kvs2.reshape(kv_shape(DEC_BATCH, 1)), vrow.reshape(1, DEC_BATCH, 1, A_WIDTH))
```

```python
import functools

import jax
import jax.numpy as jnp
import numpy as np
from jax import lax
from jax.experimental import pallas as pl
from jax.experimental.pallas import tpu as pltpu

F32 = jnp.float32
BF16 = jnp.bfloat16
U32 = jnp.uint32

D_MODEL = 1024
BATCH = 2
SEQ = 8192
DEC_BATCH = 32
PAST_LEN = 8192
CHUNK = 128
A_GROUPS = 8
A_GROUP_DIM = 64
A_WIDTH = 512
HEAD_DIM = 64
HEADS = 4
GROUP_W = HEADS * HEAD_DIM
DILATIONS = (1, 4, 16)
SPAN = 128
QK_W = 768
IN_WIDTH = 5376
COL_U, COL_V, COL_Q, COL_K, COL_VB, COL_GA, COL_GB = 0, 512, 1024, 1792, 2560, 3328, 4352
N_EXPERT_GROUPS = 4
EXPERTS_PER_GROUP = 8
N_EXPERTS = 32
D_EXPERT = 512
ROPE_THETA = 10000.0
EPS = 1e-6

LANES = 128
T_PROMPT = BATCH * SEQ
TM = 1024
TM_IN = 1024
WINDOWS = tuple(min(SPAN * d, SEQ) for d in DILATIONS)
TA = SPAN * max(DILATIONS)
ATT_UNROLL = 4
ATT_GROUP = 3
MOE_TB = 512
ROW_DMA_UNROLL = 8
ROUTE_ROWS = 8
N_SLOTS = 2 * (T_PROMPT + DEC_BATCH)
MOE_NB = -(-N_SLOTS // MOE_TB) + N_EXPERTS
MOE_ROWS = MOE_NB * MOE_TB
HALF_D = D_MODEL // 2
NEG = -1e30
VMEM_LIMIT = 56 * 1024 * 1024


def _cparams(*sem):
    return pltpu.CompilerParams(dimension_semantics=sem, vmem_limit_bytes=VMEM_LIMIT)


def _rms(x, g):
    return x * lax.rsqrt(jnp.mean(x * x, axis=-1, keepdims=True) + EPS) * g


def _layer_norm(x, g, b):
    mu = jnp.mean(x, axis=-1, keepdims=True)
    xc = x - mu
    var = jnp.mean(xc * xc, axis=-1, keepdims=True)
    return xc * lax.rsqrt(var + EPS) * g + b


def _rope_chunk(x, cos, sin_signed):
    lane = lax.broadcasted_iota(jnp.int32, x.shape, 1)
    first_half = (lane % HEAD_DIM) < (HEAD_DIM // 2)
    swapped = jnp.where(first_half, pltpu.roll(x, LANES - HEAD_DIM // 2, 1), pltpu.roll(x, HEAD_DIM // 2, 1))
    return x * cos + swapped * sin_signed


def _pack_bf16_pair(h):
    lo = lax.bitcast_convert_type(h[:, :HALF_D].astype(BF16).astype(F32), U32)
    hi = lax.bitcast_convert_type(h[:, HALF_D:].astype(BF16).astype(F32), U32)
    return (hi & jnp.uint32(0xFFFF0000)) | (lo >> 16)


def _unpack_bf16_pair(p):
    lo = lax.bitcast_convert_type(p << 16, F32).astype(BF16)
    hi = lax.bitcast_convert_type(p & jnp.uint32(0xFFFF0000), F32).astype(BF16)
    return lo, hi


def _route(logits, prior_counts, ltri):
    rows = logits.shape[0]
    lane = lax.broadcasted_iota(jnp.int32, (rows, LANES), 1).astype(F32)
    is_g = lane < N_EXPERT_GROUPS
    gl = jnp.where(is_g, logits, NEG)
    gmax = jnp.max(gl, axis=1, keepdims=True)
    grp = jnp.min(jnp.where(gl == gmax, lane, float(LANES)), axis=1, keepdims=True)
    p_grp = 1.0 / jnp.sum(jnp.where(is_g, jnp.exp(gl - gmax), 0.0), axis=1, keepdims=True)
    lo = N_EXPERT_GROUPS + grp * EXPERTS_PER_GROUP
    el = jnp.where((lane >= lo) & (lane < lo + EXPERTS_PER_GROUP), logits, NEG)
    v1 = jnp.max(el, axis=1, keepdims=True)
    i1 = jnp.min(jnp.where(el == v1, lane, float(LANES)), axis=1, keepdims=True)
    el2 = jnp.where(lane == i1, NEG, el)
    v2 = jnp.max(el2, axis=1, keepdims=True)
    i2 = jnp.min(jnp.where(el2 == v2, lane, float(LANES)), axis=1, keepdims=True)
    t = jnp.exp(v2 - v1)
    gate1 = p_grp / (1.0 + t)
    gate2 = p_grp * t / (1.0 + t)
    e1 = i1 - N_EXPERT_GROUPS
    e2 = i2 - N_EXPERT_GROUPS
    hit1 = lane == e1
    hit2 = lane == e2
    onehot = jnp.where(hit1 | hit2, 1.0, 0.0)
    rank = jnp.dot(ltri, onehot.astype(BF16), preferred_element_type=F32) + prior_counts
    pos1 = jnp.sum(jnp.where(hit1, rank, 0.0), axis=1, keepdims=True)
    pos2 = jnp.sum(jnp.where(hit2, rank, 0.0), axis=1, keepdims=True)
    slab = jnp.where(lane == 0, e1, jnp.where(lane == 1, e2, jnp.where(lane == 2, pos1, jnp.where(
        lane == 3, pos2, jnp.where(lane == 4, gate1, jnp.where(lane == 5, gate2, 0.0))))))
    return slab, prior_counts + jnp.sum(onehot, axis=0, keepdims=True)


def _inproj_kernel(x_ref, g1_ref, w_ref, b_ref, lng_ref, lnb_ref, cos_ref, sin_ref,
                   uv_ref, qkv0_ref, qkv1_ref, qkv2_ref, gates_ref, kv0_ref, kv1_ref, kv2_ref, de_ref):
    tm = x_ref.shape[0]
    hb = _rms(x_ref[...], g1_ref[...]).astype(BF16)

    def seg(lo, width):
        return jnp.dot(hb, w_ref[:, lo:lo + width], preferred_element_type=F32) + b_ref[:, lo:lo + width]

    uv_ref[:, :A_WIDTH] = jax.nn.gelu(seg(COL_U, A_WIDTH)).astype(BF16)
    uv_ref[:, A_WIDTH:] = _layer_norm(jax.nn.gelu(seg(COL_V, A_WIDTH)), lng_ref[...], lnb_ref[...]).astype(BF16)
    gates_ref[:, :D_MODEL] = jax.nn.sigmoid(seg(COL_GA, D_MODEL)).astype(BF16)
    gates_ref[:, D_MODEL:] = jax.nn.sigmoid(seg(COL_GB, D_MODEL)).astype(BF16)

    cos = cos_ref[...]
    sin = sin_ref[...]
    qkv_refs = (qkv0_ref, qkv1_ref, qkv2_ref)
    kv_refs = (kv0_ref, kv1_ref, kv2_ref)
    for g, d in enumerate(DILATIONS):
        q = seg(COL_Q + g * GROUP_W, GROUP_W)
        k = seg(COL_K + g * GROUP_W, GROUP_W)
        v = seg(COL_VB + g * GROUP_W, GROUP_W)
        chunks = GROUP_W // LANES
        for c in range(chunks):
            sl = slice(c * LANES, (c + 1) * LANES)
            de_ref[c] = _rope_chunk(q[:, sl], cos, sin) * (HEAD_DIM ** -0.5)
            de_ref[chunks + c] = _rope_chunk(k[:, sl], cos, sin)
            de_ref[2 * chunks + c] = v[:, sl]
        for c in range(3 * chunks):
            sl = slice(c * LANES, (c + 1) * LANES)
            if d == 1:
                qkv_refs[g][:, sl] = de_ref[c].astype(BF16)
            else:
                for r in range(d):
                    qkv_refs[g][0, r, :, sl] = de_ref[c, pl.ds(r, tm // d, stride=d), :].astype(BF16)
        kv_rows = kv_refs[g].shape[2]
        for c in range(2 * chunks):
            kv_refs[g][0, c * LANES:(c + 1) * LANES, :] = de_ref[chunks + c, tm - kv_rows:, :].T


def _inproj(x2, g1, w_bf, b_in, lng, lnb, cos_t, sin_t):
    tm = TM_IN
    tiles_per_seq = SEQ // tm
    n_tiles = T_PROMPT // tm
    const = lambda i: (0, 0)

    def kv_spec(w):
        rows = min(w, tm)
        first = tiles_per_seq - w // rows
        return pl.BlockSpec((1, 2 * GROUP_W, rows),
                            lambda i: (i // tiles_per_seq, 0, jnp.maximum(i % tiles_per_seq - first, 0)))

    def regrouped_spec(d):
        return pl.BlockSpec((1, d, tm // d, 3 * GROUP_W), lambda i: (i // tiles_per_seq, 0, i % tiles_per_seq, 0))

    return pl.pallas_call(
        _inproj_kernel,
        grid=(n_tiles,),
        in_specs=[
            pl.BlockSpec((tm, D_MODEL), lambda i: (i, 0)),
            pl.BlockSpec((1, D_MODEL), const),
            pl.BlockSpec((D_MODEL, IN_WIDTH), const, pipeline_mode=pl.Buffered(1)),
            pl.BlockSpec((1, IN_WIDTH), const),
            pl.BlockSpec((1, A_WIDTH), const),
            pl.BlockSpec((1, A_WIDTH), const),
            pl.BlockSpec((tm, LANES), lambda i: (i % tiles_per_seq, 0)),
            pl.BlockSpec((tm, LANES), lambda i: (i % tiles_per_seq, 0)),
        ],
        out_specs=[
            pl.BlockSpec((tm, 2 * A_WIDTH), lambda i: (i, 0)),
            pl.BlockSpec((tm, 3 * GROUP_W), lambda i: (i, 0)),
            regrouped_spec(DILATIONS[1]),
            regrouped_spec(DILATIONS[2]),
            pl.BlockSpec((tm, 2 * D_MODEL), lambda i: (i, 0)),
            kv_spec(WINDOWS[0]), kv_spec(WINDOWS[1]), kv_spec(WINDOWS[2]),
        ],
        out_shape=[
            jax.ShapeDtypeStruct((T_PROMPT, 2 * A_WIDTH), BF16),
            jax.ShapeDtypeStruct((T_PROMPT, 3 * GROUP_W), BF16),
            jax.ShapeDtypeStruct((BATCH, DILATIONS[1], SEQ // DILATIONS[1], 3 * GROUP_W), BF16),
            jax.ShapeDtypeStruct((BATCH, DILATIONS[2], SEQ // DILATIONS[2], 3 * GROUP_W), BF16),
            jax.ShapeDtypeStruct((T_PROMPT, 2 * D_MODEL), BF16),
            jax.ShapeDtypeStruct((BATCH, 2 * GROUP_W, WINDOWS[0]), F32),
            jax.ShapeDtypeStruct((BATCH, 2 * GROUP_W, WINDOWS[1]), F32),
            jax.ShapeDtypeStruct((BATCH, 2 * GROUP_W, WINDOWS[2]), F32),
        ],
        scratch_shapes=[pltpu.VMEM((3 * GROUP_W // LANES, tm, LANES), F32)],
        compiler_params=_cparams("arbitrary"),
        name="inproj",
    )(x2, g1, w_bf, b_in, lng, lnb, cos_t, sin_t)


def _attn_kernel(c0_ref, p0_ref, c1_ref, p1_ref, c2_ref, p2_ref, o_ref, acc_ref, m_ref, l_ref):
    n = pl.program_id(1)
    qi = lax.broadcasted_iota(jnp.int32, (HEADS * SPAN, 2 * SPAN), 0) % SPAN
    ki = lax.broadcasted_iota(jnp.int32, (HEADS * SPAN, 2 * SPAN), 1)
    band = (ki >= qi) & (ki <= qi + SPAN)
    band_first = band & ((ki >= SPAN) | (n > 0))
    q_head = lax.broadcasted_iota(jnp.int32, (SPAN, GROUP_W), 1) // HEAD_DIM
    kv_head = lax.broadcasted_iota(jnp.int32, (2 * SPAN, GROUP_W), 1) // HEAD_DIM
    first_head_lanes = lax.broadcasted_iota(jnp.int32, (SPAN, LANES), 1) < HEAD_DIM
    contract_last = (((1,), (1,)), ((), ()))
    k_cols = slice(GROUP_W, 2 * GROUP_W)
    v_cols = slice(2 * GROUP_W, 3 * GROUP_W)

    def attend(blocks, fresh):
        scores = []
        for q, keys, _, mask, _ in blocks:
            q_heads = jnp.concatenate([jnp.where(q_head == h, q, jnp.zeros_like(q)) for h in range(HEADS)], axis=0)
            scores.append(jnp.where(mask, lax.dot_general(q_heads, keys, contract_last, preferred_element_type=F32), NEG))
        soft = []
        for s in scores:
            m = jnp.max(s, axis=1, keepdims=True)
            p = jnp.exp(s - m)
            soft.append((m, jnp.sum(p, axis=1, keepdims=True), p.astype(BF16)))
        accs = []
        for (_, _, pb), (_, _, vals, _, _) in zip(soft, blocks):
            p_heads = jnp.concatenate([pb[h * SPAN:(h + 1) * SPAN] for h in range(HEADS)], axis=1)
            v_heads = jnp.concatenate([jnp.where(kv_head == h, vals, jnp.zeros_like(vals)) for h in range(HEADS)],
                                      axis=0)
            accs.append(jnp.dot(p_heads, v_heads, preferred_element_type=F32))
        for (m, l, _), acc, (_, _, _, _, tok_rows) in zip(soft, accs, blocks):
            merge(m, l, acc, tok_rows, fresh)

    def merge(m, l, acc, tok_rows, fresh):
        heads_per_chunk = LANES // HEAD_DIM
        for c in range(GROUP_W // LANES):
            sl = slice(c * LANES, (c + 1) * LANES)
            lo = slice(c * heads_per_chunk * SPAN, (c * heads_per_chunk + 1) * SPAN)
            hi = slice((c * heads_per_chunk + 1) * SPAN, (c * heads_per_chunk + 2) * SPAN)
            m_b = jnp.where(first_head_lanes, m[lo], m[hi])
            l_b = jnp.where(first_head_lanes, l[lo], l[hi])
            if fresh:
                m_ref[c, tok_rows, :] = m_b
                l_ref[c, tok_rows, :] = l_b
                acc_ref[c, tok_rows, :] = acc[:, sl]
                continue
            m_old = m_ref[c, tok_rows, :]
            m_new = jnp.maximum(m_old, m_b)
            a_old = jnp.exp(m_old - m_new)
            a_blk = jnp.exp(m_b - m_new)
            l_ref[c, tok_rows, :] = a_old * l_ref[c, tok_rows, :] + a_blk * l_b
            acc_ref[c, tok_rows, :] = a_old * acc_ref[c, tok_rows, :] + a_blk * acc[:, sl]
            m_ref[c, tok_rows, :] = m_new

    def tok_rows(r, j, d):
        start = j * (SPAN * d) + r
        return pl.ds(start, SPAN) if d == 1 else pl.ds(start, SPAN, stride=d)

    def first_block(c_ref, p_ref, r, d):
        keys = jnp.concatenate([p_ref[0, r, :, k_cols], c_ref[0, r, :SPAN, k_cols]], axis=0)
        vals = jnp.concatenate([p_ref[0, r, :, v_cols], c_ref[0, r, :SPAN, v_cols]], axis=0)
        return c_ref[0, r, :SPAN, :GROUP_W], keys, vals, band_first, tok_rows(r, 0, d)

    def later_block(c_ref, r, j, d):
        q_rows = pl.ds(pl.multiple_of(j * SPAN, SPAN), SPAN)
        kv_rows = pl.ds(pl.multiple_of((j - 1) * SPAN, SPAN), 2 * SPAN)
        return (c_ref[0, r, q_rows, :GROUP_W], c_ref[0, r, kv_rows, k_cols], c_ref[0, r, kv_rows, v_cols],
                band, tok_rows(r, j, d))

    def later_blocks(c_ref, r, d):
        later = c_ref.shape[2] // SPAN - 1
        assert later % ATT_GROUP == 0

        def body(i, carry):
            j0 = 1 + i * ATT_GROUP
            attend([later_block(c_ref, r, j0 + k, d) for k in range(ATT_GROUP)], False)
            return carry

        lax.fori_loop(0, later // ATT_GROUP, body, 0)

    groups = sorted(zip(DILATIONS, ((c0_ref, p0_ref), (c1_ref, p1_ref), (c2_ref, p2_ref))), key=lambda g: -g[0])
    d, (c_ref, p_ref) = groups[0]
    assert c_ref.shape[2] == SPAN and d % ATT_UNROLL == 0

    def widest(i, carry):
        attend([first_block(c_ref, p_ref, i * ATT_UNROLL + k, d) for k in range(ATT_UNROLL)], True)
        return carry

    lax.fori_loop(0, d // ATT_UNROLL, widest, 0)
    attend([first_block(cr, pr, r, dd) for dd, (cr, pr) in groups[1:] for r in range(dd)], False)
    for dd, (cr, pr) in groups[1:]:
        if dd == 1:
            later_blocks(cr, 0, dd)
        else:
            lax.fori_loop(0, dd, lambda r, carry, cr=cr, dd=dd: (later_blocks(cr, r, dd), carry)[1], 0)

    for c in range(GROUP_W // LANES):
        o_ref[0, :, c * LANES:(c + 1) * LANES] = (acc_ref[c] / l_ref[c]).astype(BF16)


def _attention(qkv_by_group):
    in_specs, args = [], []
    for qkv, d in zip(qkv_by_group, DILATIONS):
        rows = TA // d
        blocks_per_tile = rows // SPAN
        in_specs.append(pl.BlockSpec((1, d, rows, 3 * GROUP_W), lambda b, n: (b, 0, n, 0)))
        in_specs.append(pl.BlockSpec((1, d, SPAN, 3 * GROUP_W),
                                     lambda b, n, k=blocks_per_tile: (b, 0, jnp.maximum(n * k - 1, 0), 0)))
        args += [qkv, qkv]
    return pl.pallas_call(
        _attn_kernel,
        grid=(BATCH, SEQ // TA),
        in_specs=in_specs,
        out_specs=pl.BlockSpec((1, TA, GROUP_W), lambda b, n: (b, n, 0)),
        out_shape=jax.ShapeDtypeStruct((BATCH, SEQ, GROUP_W), BF16),
        scratch_shapes=[pltpu.VMEM((GROUP_W // LANES, TA, LANES), F32)] * 3,
        compiler_params=_cparams("arbitrary", "arbitrary"),
        name="attn",
    )(*args)


def _merge_kernel(x_ref, uv_ref, gates_ref, ob_ref,
                  wsp_ref, bsp_ref, wa_ref, wb_ref, wo_ref, g2_ref, wr_ref, br_ref, ltri_ref,
                  x1_ref, h2p_ref, rt_ref, rtt_ref, cnt_ref, run_ref):
    @pl.when(pl.program_id(0) == 0)
    def _():
        run_ref[...] = jnp.zeros_like(run_ref)

    lane = lax.broadcasted_iota(jnp.int32, (CHUNK, LANES), 1)
    left = lane < A_GROUP_DIM
    zero = jnp.zeros((CHUNK, LANES), BF16)
    n_chunks, n_pairs = TM // CHUNK, A_GROUPS // 2
    chunk_rows = lambda c: slice(c * CHUNK, (c + 1) * CHUNK)
    mixes = []
    for c in range(n_chunks):
        for p in range(n_pairs):
            vp = uv_ref[chunk_rows(c), A_WIDTH + p * LANES:A_WIDTH + (p + 1) * LANES]
            rhs = jnp.concatenate([jnp.where(left, vp, zero), jnp.where(left, zero, vp)], axis=0)
            mixes.append(jnp.dot(wsp_ref[p], rhs, preferred_element_type=F32))
    sa_chunks = []
    for c in range(n_chunks):
        mixed = jnp.concatenate(mixes[c * n_pairs:(c + 1) * n_pairs], axis=1) + bsp_ref[...]
        sa_chunks.append((uv_ref[chunk_rows(c), :A_WIDTH].astype(F32) * mixed).astype(BF16))
    s_a = jnp.concatenate(sa_chunks, axis=0)

    a = jnp.dot(s_a, wa_ref[...], preferred_element_type=F32)
    b = jnp.dot(ob_ref[...], wb_ref[...], preferred_element_type=F32)
    merged = gates_ref[:, :D_MODEL].astype(F32) * a + gates_ref[:, D_MODEL:].astype(F32) * b
    x1 = x_ref[...] + jnp.dot(merged.astype(BF16), wo_ref[...], preferred_element_type=F32)
    x1_ref[...] = x1

    h2 = _rms(x1, g2_ref[...])
    h2p_ref[...] = _pack_bf16_pair(h2)
    logits = jnp.dot(h2.astype(BF16), wr_ref[...], preferred_element_type=F32) + br_ref[...]
    slab, counts = _route(logits, run_ref[...], ltri_ref[...])
    rt_ref[...] = slab
    rtt_ref[...] = slab.T[:ROUTE_ROWS, :]
    run_ref[...] = counts
    cnt_ref[...] = counts


def _merge(x2, uv, gates, ob, wsp, bsp, wa, wb, wo, g2, wr, br, ltri):
    n_tiles = T_PROMPT // TM
    tile = lambda w: pl.BlockSpec((TM, w), lambda i: (i, 0))
    full = lambda a: pl.BlockSpec(a.shape, lambda i: (0,) * a.ndim)
    return pl.pallas_call(
        _merge_kernel,
        grid=(n_tiles,),
        in_specs=[tile(D_MODEL), tile(2 * A_WIDTH), tile(2 * D_MODEL), tile(GROUP_W),
                  full(wsp), full(bsp), full(wa), full(wb), full(wo), full(g2), full(wr), full(br),
                  full(ltri)],
        out_specs=[tile(D_MODEL), tile(HALF_D), tile(LANES), pl.BlockSpec((ROUTE_ROWS, TM), lambda i: (i, 0)),
                   pl.BlockSpec((1, LANES), lambda i: (0, 0))],
        out_shape=[
            jax.ShapeDtypeStruct((T_PROMPT, D_MODEL), F32),
            jax.ShapeDtypeStruct((T_PROMPT, HALF_D), U32),
            jax.ShapeDtypeStruct((T_PROMPT, LANES), F32),
            jax.ShapeDtypeStruct((n_tiles * ROUTE_ROWS, TM), F32),
            jax.ShapeDtypeStruct((1, LANES), F32),
        ],
        scratch_shapes=[pltpu.VMEM((1, LANES), F32)],
        compiler_params=_cparams("arbitrary"),
        name="merge_route",
    )(x2, uv, gates, ob, wsp, bsp, wa, wb, wo, g2, wr, br, ltri)


SAMPLE_COLS = 768


def _sample_inproj_kernel(x_ref, g1_ref, w_ref, b_ref, z_ref):
    hb = _rms(x_ref[...], g1_ref[...]).astype(BF16)
    z_ref[...] = jnp.dot(hb, w_ref[...], preferred_element_type=F32) + b_ref[...]


def _sample_inproj(xs, g1, w_in, b_in):
    const = lambda j: (0, 0)
    return pl.pallas_call(
        _sample_inproj_kernel,
        grid=(IN_WIDTH // SAMPLE_COLS,),
        in_specs=[pl.BlockSpec((DEC_BATCH, D_MODEL), const), pl.BlockSpec((1, D_MODEL), const),
                  pl.BlockSpec((D_MODEL, SAMPLE_COLS), lambda j: (0, j)),
                  pl.BlockSpec((1, SAMPLE_COLS), lambda j: (0, j))],
        out_specs=pl.BlockSpec((DEC_BATCH, SAMPLE_COLS), lambda j: (0, j)),
        out_shape=jax.ShapeDtypeStruct((DEC_BATCH, IN_WIDTH), F32),
        compiler_params=_cparams("arbitrary"),
        name="sample_inproj",
    )(xs, g1, w_in, b_in)


def _sample_pre_kernel(z_ref, lng_ref, lnb_ref, ws0_ref, bs0_ref, cos_ref, sin_ref,
                       vrow_ref, sa_ref, qr_ref, kr_ref, kv0_ref, kv1_ref, kv2_ref):
    u = jax.nn.gelu(z_ref[:, COL_U:COL_U + A_WIDTH])
    va = _layer_norm(jax.nn.gelu(z_ref[:, COL_V:COL_V + A_WIDTH]), lng_ref[...], lnb_ref[...])
    vrow_ref[...] = va
    sa_ref[...] = u * (ws0_ref[...] * va.astype(BF16).astype(F32) + bs0_ref[...])
    cos, sin = cos_ref[...], sin_ref[...]
    for g, kv_ref in enumerate((kv0_ref, kv1_ref, kv2_ref)):
        for c in range(GROUP_W // LANES):
            off = g * GROUP_W + c * LANES
            qr_ref[:, off:off + LANES] = (_rope_chunk(z_ref[:, COL_Q + off:COL_Q + off + LANES], cos, sin)
                                          * (HEAD_DIM ** -0.5))
            kr = _rope_chunk(z_ref[:, COL_K + off:COL_K + off + LANES], cos, sin)
            kr_ref[:, off:off + LANES] = kr
            kv_ref[:, c * LANES:(c + 1) * LANES] = kr
        kv_ref[:, GROUP_W:] = z_ref[:, COL_VB + g * GROUP_W:COL_VB + (g + 1) * GROUP_W]


def _sample_pre(z, lng, lnb, ws0, bs0, cos_s, sin_s):
    row = lambda w: jax.ShapeDtypeStruct((DEC_BATCH, w), F32)
    return pl.pallas_call(
        _sample_pre_kernel,
        out_shape=[row(A_WIDTH), row(A_WIDTH), row(QK_W), row(QK_W),
                   row(2 * GROUP_W), row(2 * GROUP_W), row(2 * GROUP_W)],
        compiler_params=pltpu.CompilerParams(vmem_limit_bytes=VMEM_LIMIT),
        name="sample_pre",
    )(z, lng, lnb, ws0, bs0, cos_s, sin_s)


def _sample_attn_kernel(q_ref, k_ref, v_ref, c0_ref, c1_ref, c2_ref, comb_ref):
    as_operand = lambda t: t.astype(BF16).astype(F32)
    contract_last = (((1,), (1,)), ((), ()))
    pad_rows = 8 // HEADS
    groups = tuple(zip((c0_ref, c1_ref, c2_ref), DILATIONS))
    pairs = [(h, g) for h in range(HEADS) for g in range(len(groups))]
    scores = {}
    for h, g in pairs:
        c_ref, d = groups[g]
        q_rows = jnp.concatenate([q_ref[0, g]] * pad_rows, axis=0).astype(BF16)
        s = jnp.dot(q_rows, c_ref[0, 0, h].astype(BF16), preferred_element_type=F32)[h:h + 1]
        if d > 1:
            pos = lax.broadcasted_iota(jnp.int32, s.shape, 1)
            s = jnp.where(pos % d == 0, s, NEG)
        scores[h, g] = s
    probs = {}
    for h, g in pairs:
        s = scores[h, g]
        qn, kn = (as_operand(r[0, g, h:h + 1, :]) for r in (q_ref, k_ref))
        s_n = jnp.sum(qn * kn, axis=1, keepdims=True)
        m = jnp.maximum(jnp.max(s, axis=1, keepdims=True), s_n)
        lse = m + jnp.log(jnp.sum(jnp.exp(s - m), axis=1, keepdims=True) + jnp.exp(s_n - m))
        probs[h, g] = (jnp.broadcast_to(jnp.exp(s - lse), (8, s.shape[1])).astype(BF16),
                       as_operand(jnp.exp(s_n - lse)), lse)
    outs = {}
    for h, g in pairs:
        p_rows, p_n, _ = probs[h, g]
        o = lax.dot_general(p_rows, groups[g][0][0, 1, h].astype(BF16), contract_last, preferred_element_type=F32)
        outs[h, g] = o[0:1] + p_n * as_operand(v_ref[0, g, h:h + 1, :])
    for h in range(HEADS):
        lses = [probs[h, g][2] for g in range(len(groups))]
        mx = jnp.maximum(jnp.maximum(lses[0], lses[1]), lses[2])
        ws = [jnp.exp(t - mx) for t in lses]
        tot = ws[0] + ws[1] + ws[2]
        comb_ref[0, h:h + 1, :] = (ws[0] * outs[h, 0] + ws[1] * outs[h, 1] + ws[2] * outs[h, 2]) / tot


def _sample_attn(qt, kt, vt, caches):
    new_spec = pl.BlockSpec((1, 3, HEADS, HEAD_DIM), lambda i: (i, 0, 0, 0))
    cache_spec = lambda c: pl.BlockSpec((1,) + c.shape[1:], lambda i: (i, 0, 0, 0, 0))
    return pl.pallas_call(
        _sample_attn_kernel,
        grid=(DEC_BATCH,),
        in_specs=[new_spec, new_spec, new_spec, cache_spec(caches[0]), cache_spec(caches[1]), cache_spec(caches[2])],
        out_specs=pl.BlockSpec((1, HEADS, HEAD_DIM), lambda i: (i, 0, 0)),
        out_shape=jax.ShapeDtypeStruct((DEC_BATCH, HEADS, HEAD_DIM), F32),
        compiler_params=_cparams("arbitrary"),
        name="sample_attn",
    )(qt, kt, vt, caches[0], caches[1], caches[2])


def _sample_merge_kernel(x_ref, z_ref, sa_ref, comb_ref, wa_ref, wb_ref, wo_ref, g2_ref, wr_ref, br_ref,
                         ltri_ref, cnt_in_ref, x1_ref, h2p_ref, rt_ref, cnt_ref):
    dot = lambda p, q: jnp.dot(p.astype(BF16), q, preferred_element_type=F32)
    a = dot(sa_ref[...], wa_ref[...])
    b = dot(comb_ref[...], wb_ref[...])
    merged = (jax.nn.sigmoid(z_ref[:, COL_GA:COL_GA + D_MODEL]) * a
              + jax.nn.sigmoid(z_ref[:, COL_GB:COL_GB + D_MODEL]) * b)
    x1 = x_ref[...] + dot(merged, wo_ref[...])
    x1_ref[...] = x1
    h2 = _rms(x1, g2_ref[...])
    h2p_ref[...] = _pack_bf16_pair(h2)
    logits = dot(h2, wr_ref[...]) + br_ref[...]
    slab, counts = _route(logits, cnt_in_ref[...], ltri_ref[...])
    rt_ref[...] = slab
    cnt_ref[...] = counts


def _sample_merge(xs, z, s_a, comb, wa, wb, wo, g2, wr, br, ltri, counts):
    return pl.pallas_call(
        _sample_merge_kernel,
        out_shape=[jax.ShapeDtypeStruct((DEC_BATCH, D_MODEL), F32),
                   jax.ShapeDtypeStruct((DEC_BATCH, HALF_D), U32),
                   jax.ShapeDtypeStruct((DEC_BATCH, LANES), F32),
                   jax.ShapeDtypeStruct((1, LANES), F32)],
        compiler_params=pltpu.CompilerParams(vmem_limit_bytes=VMEM_LIMIT),
        name="sample_merge_route",
    )(xs, z, s_a, comb, wa, wb, wo, g2, wr, br, ltri, counts)


def _dispatch_kernel(dest0_ref, dest1_ref, h_ref, xs_in_ref, xs_ref, sem):
    del xs_in_ref
    groups, sub = h_ref.shape[0], h_ref.shape[1]
    rows = groups * sub
    base = pl.program_id(0) * rows

    def start(j, c):
        for k in range(sub):
            for dest_ref in (dest0_ref, dest1_ref):
                d = dest_ref[base + j * sub + k]
                pltpu.make_async_copy(h_ref.at[j, pl.ds(k, 1)], xs_ref.at[pl.ds(d, 1)], sem).start()
        return c

    lax.fori_loop(0, groups, start, 0)
    for _ in range(2):
        pltpu.make_async_copy(xs_ref.at[pl.ds(0, rows)], xs_ref.at[pl.ds(0, rows)], sem).wait()


def _dispatch(dest, h2p, xs, rows):
    n = h2p.shape[0]
    sub = 8
    h2p = h2p.reshape(n // sub, sub, HALF_D)
    return pl.pallas_call(
        _dispatch_kernel,
        grid_spec=pltpu.PrefetchScalarGridSpec(
            num_scalar_prefetch=2,
            grid=(n // rows,),
            in_specs=[pl.BlockSpec((rows // sub, sub, HALF_D), lambda i, *_: (i, 0, 0)),
                      pl.BlockSpec(memory_space=pl.ANY)],
            out_specs=pl.BlockSpec(memory_space=pl.ANY),
            scratch_shapes=[pltpu.SemaphoreType.DMA(())],
        ),
        out_shape=jax.ShapeDtypeStruct(xs.shape, xs.dtype),
        input_output_aliases={3: 0},
        compiler_params=_cparams("arbitrary"),
        name=f"moe_dispatch_{rows}",
    )(dest[0], dest[1], h2p, xs)


def _ffn_kernel(be_ref, nused_ref, nexte_ref, wslot_ref, xs_ref, wg_ref, wu_ref, wd_ref, ys_ref,
                wg_b, wu_b, wd_b, wg_f, wu_f, wd_f, wsem):
    i = pl.program_id(0)
    live = i < nused_ref[0]
    new_expert = jnp.logical_and(live, jnp.logical_or(i == 0, be_ref[i] != be_ref[jnp.maximum(i - 1, 0)]))

    def weight_copies(e, slot):
        return [pltpu.make_async_copy(src.at[e], dst.at[slot], wsem.at[slot])
                for src, dst in ((wg_ref, wg_f), (wu_ref, wu_f), (wd_ref, wd_f))]

    @pl.when(i == 0)
    def _():
        for cp in weight_copies(be_ref[0], 0):
            cp.start()

    @pl.when(new_expert)
    def _():
        slot = wslot_ref[i]
        for cp in weight_copies(be_ref[i], slot):
            cp.wait()
        wg_b[...] = wg_f[slot].astype(BF16)
        wu_b[...] = wu_f[slot].astype(BF16)
        wd_b[...] = wd_f[slot].astype(BF16)

        @pl.when(nexte_ref[i] >= 0)
        def _():
            for cp in weight_copies(nexte_ref[i], 1 - slot):
                cp.start()

    @pl.when(live)
    def _():
        x = jnp.concatenate(_unpack_bf16_pair(xs_ref[...]), axis=1)
        gate = jnp.dot(x, wg_b[...], preferred_element_type=F32)
        up = jnp.dot(x, wu_b[...], preferred_element_type=F32)
        hid = (jax.nn.silu(gate) * up).astype(BF16)
        ys_ref[...] = jnp.dot(hid, wd_b[...], preferred_element_type=F32)

    @pl.when(i >= nused_ref[0])
    def _():
        ys_ref[...] = jnp.zeros_like(ys_ref)


def _expert_ffn(block_e, nused, xs, w_gate, w_up, w_down):
    idx = jnp.arange(MOE_NB, dtype=jnp.int32)
    change = ((block_e != jnp.roll(block_e, 1)) | (idx == 0)) & (idx < nused[0])
    wslot = ((jnp.cumsum(change.astype(jnp.int32)) - 1) & 1).astype(jnp.int32)
    change_at = jnp.where(change, idx, MOE_NB)
    next_change = jnp.flip(lax.cummin(jnp.flip(jnp.concatenate([change_at[1:], jnp.full((1,), MOE_NB, jnp.int32)]))))
    nexte = jnp.where(next_change < MOE_NB, block_e[jnp.minimum(next_change, MOE_NB - 1)], -1).astype(jnp.int32)
    hbm = pl.BlockSpec(memory_space=pl.ANY)
    return pl.pallas_call(
        _ffn_kernel,
        grid_spec=pltpu.PrefetchScalarGridSpec(
            num_scalar_prefetch=4,
            grid=(MOE_NB,),
            in_specs=[pl.BlockSpec((MOE_TB, HALF_D), lambda i, be, nu, *_: (jnp.minimum(i, nu[0] - 1), 0)),
                      hbm, hbm, hbm],
            out_specs=pl.BlockSpec((MOE_TB, D_MODEL), lambda i, *_: (i, 0)),
            scratch_shapes=[pltpu.VMEM((D_MODEL, D_EXPERT), BF16), pltpu.VMEM((D_MODEL, D_EXPERT), BF16),
                            pltpu.VMEM((D_EXPERT, D_MODEL), BF16),
                            pltpu.VMEM((2, D_MODEL, D_EXPERT), F32), pltpu.VMEM((2, D_MODEL, D_EXPERT), F32),
                            pltpu.VMEM((2, D_EXPERT, D_MODEL), F32), pltpu.SemaphoreType.DMA((2,))],
        ),
        out_shape=jax.ShapeDtypeStruct((MOE_ROWS, D_MODEL), F32),
        compiler_params=_cparams("arbitrary"),
        name="moe_ffn",
    )(block_e, nused, nexte, wslot, xs, w_gate, w_up, w_down)


def _combine_kernel(dest0_ref, dest1_ref, x1_ref, rt_ref, gf_ref, ys_ref, y_ref, ya, yb, sem):
    rows = x1_ref.shape[0]
    base = pl.program_id(0) * rows

    def row_copy(t, dest_ref, buf):
        return pltpu.make_async_copy(ys_ref.at[pl.ds(dest_ref[base + t], 1)], buf.at[pl.ds(t, 1)], sem)

    def start(t, c):
        row_copy(t, dest0_ref, ya).start()
        row_copy(t, dest1_ref, yb).start()
        return c

    lax.fori_loop(0, rows, start, 0, unroll=ROW_DMA_UNROLL)
    pltpu.make_async_copy(ys_ref.at[pl.ds(0, rows)], ya, sem).wait()
    pltpu.make_async_copy(ys_ref.at[pl.ds(0, rows)], yb, sem).wait()
    x2 = x1_ref[...] + rt_ref[:, 4:5] * ya[...] + rt_ref[:, 5:6] * yb[...]
    y_ref[...] = _rms(x2, gf_ref[...])


def _combine(dest, x1, rt, gf, ys, rows):
    n = x1.shape[0]
    return pl.pallas_call(
        _combine_kernel,
        grid_spec=pltpu.PrefetchScalarGridSpec(
            num_scalar_prefetch=2,
            grid=(n // rows,),
            in_specs=[pl.BlockSpec((rows, D_MODEL), lambda i, *_: (i, 0)),
                      pl.BlockSpec((rows, LANES), lambda i, *_: (i, 0)),
                      pl.BlockSpec((1, D_MODEL), lambda i, *_: (0, 0)),
                      pl.BlockSpec(memory_space=pl.ANY)],
            out_specs=pl.BlockSpec((rows, D_MODEL), lambda i, *_: (i, 0)),
            scratch_shapes=[pltpu.VMEM((rows, D_MODEL), F32), pltpu.VMEM((rows, D_MODEL), F32),
                            pltpu.SemaphoreType.DMA(())],
        ),
        out_shape=jax.ShapeDtypeStruct((n, D_MODEL), F32),
        compiler_params=_cparams("arbitrary"),
        name=f"moe_combine_{rows}",
    )(dest[0], dest[1], x1, rt, gf, ys)


def _rope_tables(pos, xp):
    half = HEAD_DIM // 2
    inv = 1.0 / (xp.float32(ROPE_THETA) ** (xp.arange(half, dtype=xp.float32) * xp.float32(2.0 / HEAD_DIM)))
    ang = pos.astype(xp.float32)[:, None] * inv[None, :].astype(xp.float32)
    cos, sin = xp.cos(ang), xp.sin(ang)
    return xp.concatenate([cos, cos, cos, cos], axis=1), xp.concatenate([-sin, sin, -sin, sin], axis=1)


def _slot_dest(ids, pos, pstarts):
    ids = ids.astype(jnp.int32)
    experts = jnp.arange(N_EXPERTS, dtype=jnp.int32).reshape((N_EXPERTS,) + (1,) * ids.ndim)
    start = jnp.sum(jnp.where(ids[None] == experts, pstarts.reshape(experts.shape), 0), axis=0)
    return start + pos.astype(jnp.int32)


def kernel(x_prompt, x_sample, cache_kv_g0, cache_kv_g1, cache_kv_g2, norm1_g, w_in, b_in, a_ln_g, a_ln_b, w_spatial, b_spatial, w_a_proj, w_b_proj, w_o, norm2_g, w_group_router, b_group_router, w_expert_router, b_expert_router, w_gate, w_up, w_down, final_norm_g):
    x2 = x_prompt.reshape(T_PROMPT, D_MODEL)
    xs = x_sample.reshape(DEC_BATCH, D_MODEL)
    g1 = norm1_g[0][None, :]
    g2 = norm2_g[0][None, :]
    gf = final_norm_g[None, :]
    b_in2 = b_in[0][None, :]
    lng, lnb = a_ln_g[0][None, :], a_ln_b[0][None, :]

    causal = jnp.tril(jnp.ones((CHUNK, CHUNK), dtype=bool))
    ws_tril = jnp.where(causal[None], w_spatial[0], 0.0)
    wsp = jnp.concatenate([ws_tril[0::2], ws_tril[1::2]], axis=2).astype(BF16)
    bsp = jnp.repeat(b_spatial[0].T, A_GROUP_DIM, axis=1)
    ws0 = jnp.repeat(ws_tril[:, 0, 0].astype(BF16).astype(F32), A_GROUP_DIM)[None, :]
    bs0 = jnp.repeat(b_spatial[0][:, 0], A_GROUP_DIM)[None, :]
    w_router = jnp.zeros((D_MODEL, LANES), F32)
    w_router = w_router.at[:, :N_EXPERT_GROUPS].set(w_group_router[0])
    w_router = w_router.at[:, N_EXPERT_GROUPS:N_EXPERT_GROUPS + N_EXPERTS].set(w_expert_router[0])
    b_router = jnp.zeros((1, LANES), F32)
    b_router = b_router.at[0, :N_EXPERT_GROUPS].set(b_group_router[0])
    b_router = b_router.at[0, N_EXPERT_GROUPS:N_EXPERT_GROUPS + N_EXPERTS].set(b_expert_router[0])
    w_router = w_router.astype(BF16)
    ltri = jnp.tril(jnp.ones((TM, TM), BF16), -1)
    w_in_b, w_a_b, w_b_b, w_o_b = (w[0].astype(BF16) for w in (w_in, w_a_proj, w_b_proj, w_o))

    cos_p, sin_p = _rope_tables(np.arange(SEQ, dtype=np.int32), np)
    cos_s, sin_s = _rope_tables(jnp.full((1,), PAST_LEN, jnp.int32), jnp)

    uv, qkv0, qkv1, qkv2, gates, kvp0, kvp1, kvp2 = _inproj(
        x2, g1, w_in_b, b_in2, lng, lnb, cos_p, sin_p)
    ob = _attention((qkv0.reshape(BATCH, 1, SEQ, 3 * GROUP_W), qkv1, qkv2)).reshape(T_PROMPT, GROUP_W)
    x1_p, h2p_p, rt_p, rtt_p, counts_p = _merge(
        x2, uv, gates, ob, wsp, bsp, w_a_b, w_b_b, w_o_b, g2, w_router, b_router, ltri)

    z_s = _sample_inproj(xs, g1, w_in_b, b_in2)
    vrow, sa_s, qr_s, kr_s, kvs0, kvs1, kvs2 = _sample_pre(z_s, lng, lnb, ws0, bs0, cos_s, sin_s)
    per_head = lambda t: t.reshape(DEC_BATCH, 3, HEADS, HEAD_DIM)
    caches = [c.transpose(0, 1, 3, 4, 5, 2).reshape(DEC_BATCH, 2, HEADS, HEAD_DIM, c.shape[2])
              for c in (cache_kv_g0, cache_kv_g1, cache_kv_g2)]
    comb_s = _sample_attn(per_head(qr_s), per_head(kr_s), per_head(z_s[:, COL_VB:COL_VB + QK_W]), caches)
    comb_s = comb_s.reshape(DEC_BATCH, GROUP_W)
    x1_s, h2p_s, rt_s, counts = _sample_merge(
        xs, z_s, sa_s, comb_s, w_a_b, w_b_b, w_o_b, g2, w_router, b_router,
        ltri[:DEC_BATCH, :DEC_BATCH], counts_p)

    cnt = counts[0, :N_EXPERTS].astype(jnp.int32)
    padded = (cnt + MOE_TB - 1) // MOE_TB * MOE_TB
    pends = jnp.cumsum(padded)
    pstarts = pends - padded
    block_starts = jnp.arange(MOE_NB, dtype=jnp.int32) * MOE_TB
    block_e = jnp.minimum(jnp.sum((pends[None, :] <= block_starts[:, None]).astype(jnp.int32), axis=1),
                          N_EXPERTS - 1)
    nused = (pends[-1:] // MOE_TB).astype(jnp.int32)
    rtt_p = rtt_p.reshape(T_PROMPT // TM, ROUTE_ROWS, TM)
    dest_p = [_slot_dest(rtt_p[:, k], rtt_p[:, 2 + k], pstarts).reshape(T_PROMPT) for k in range(2)]
    dest_s = [_slot_dest(rt_s[:, k], rt_s[:, 2 + k], pstarts) for k in range(2)]

    rows = jnp.zeros((MOE_ROWS, HALF_D), U32)
    rows = _dispatch(dest_p, h2p_p, rows, TM)
    rows = _dispatch(dest_s, h2p_s, rows, DEC_BATCH)
    ys = _expert_ffn(block_e, nused, rows, w_gate[0], w_up[0], w_down[0])
    y_p = _combine(dest_p, x1_p, rt_p, gf, ys, TM)
    y_s = _combine(dest_s, x1_s, rt_s, gf, ys, DEC_BATCH)

    kv_shape = lambda n, w: (1, n, w, 2, HEADS, HEAD_DIM)
    window_rows = lambda t: t.reshape(1, BATCH, 2, HEADS, HEAD_DIM, t.shape[2]).transpose(0, 1, 5, 2, 3, 4)
    return (y_p.reshape(BATCH, SEQ, D_MODEL), y_s.reshape(DEC_BATCH, 1, D_MODEL),
            window_rows(kvp0), window_rows(kvp1), window_rows(kvp2),
            kvs0.reshape(kv_shape(DEC_BATCH, 1)), kvs1.reshape(kv_shape(DEC_BATCH, 1)),
            kvs2.reshape(kv_shape(DEC_BATCH, 1)), vrow.reshape(1, DEC_BATCH, 1, A_WIDTH))
```

```python
import functools

import jax
import jax.numpy as jnp
import numpy as np
from jax import lax
from jax.experimental import pallas as pl
from jax.experimental.pallas import tpu as pltpu

F32 = jnp.float32
BF16 = jnp.bfloat16
U32 = jnp.uint32

D_MODEL = 1024
BATCH = 2
SEQ = 8192
DEC_BATCH = 32
PAST_LEN = 8192
CHUNK = 128
A_GROUPS = 8
A_GROUP_DIM = 64
A_WIDTH = 512
HEAD_DIM = 64
HEADS = 4
GROUP_W = HEADS * HEAD_DIM
DILATIONS = (1, 4, 16)
SPAN = 128
QK_W = 768
IN_WIDTH = 5376
COL_U, COL_V, COL_Q, COL_K, COL_VB, COL_GA, COL_GB = 0, 512, 1024, 1792, 2560, 3328, 4352
N_EXPERT_GROUPS = 4
EXPERTS_PER_GROUP = 8
N_EXPERTS = 32
D_EXPERT = 512
ROPE_THETA = 10000.0
EPS = 1e-6

LANES = 128
T_PROMPT = BATCH * SEQ
TM = 1024
TM_IN = 1024
WINDOWS = tuple(min(SPAN * d, SEQ) for d in DILATIONS)
TA = SPAN * max(DILATIONS)
ATT_UNROLL = 4
ATT_GROUP = 3
MOE_TB = 512
ROW_DMA_UNROLL = 8
ROUTE_ROWS = 8
N_SLOTS = 2 * (T_PROMPT + DEC_BATCH)
MOE_NB = -(-N_SLOTS // MOE_TB) + N_EXPERTS
MOE_ROWS = MOE_NB * MOE_TB
HALF_D = D_MODEL // 2
NEG = -1e30
VMEM_LIMIT = 56 * 1024 * 1024


def _cparams(*sem):
    return pltpu.CompilerParams(dimension_semantics=sem, vmem_limit_bytes=VMEM_LIMIT)


def _rms(x, g):
    return x * lax.rsqrt(jnp.mean(x * x, axis=-1, keepdims=True) + EPS) * g


def _layer_norm(x, g, b):
    mu = jnp.mean(x, axis=-1, keepdims=True)
    xc = x - mu
    var = jnp.mean(xc * xc, axis=-1, keepdims=True)
    return xc * lax.rsqrt(var + EPS) * g + b


def _rope_chunk(x, cos, sin_signed):
    lane = lax.broadcasted_iota(jnp.int32, x.shape, 1)
    first_half = (lane % HEAD_DIM) < (HEAD_DIM // 2)
    swapped = jnp.where(first_half, pltpu.roll(x, LANES - HEAD_DIM // 2, 1), pltpu.roll(x, HEAD_DIM // 2, 1))
    return x * cos + swapped * sin_signed


def _pack_bf16_pair(h):
    lo = lax.bitcast_convert_type(h[:, :HALF_D].astype(BF16).astype(F32), U32)
    hi = lax.bitcast_convert_type(h[:, HALF_D:].astype(BF16).astype(F32), U32)
    return (hi & jnp.uint32(0xFFFF0000)) | (lo >> 16)


def _unpack_bf16_pair(p):
    lo = lax.bitcast_convert_type(p << 16, F32).astype(BF16)
    hi = lax.bitcast_convert_type(p & jnp.uint32(0xFFFF0000), F32).astype(BF16)
    return lo, hi


def _route(logits, prior_counts, ltri):
    rows = logits.shape[0]
    lane = lax.broadcasted_iota(jnp.int32, (rows, LANES), 1).astype(F32)
    is_g = lane < N_EXPERT_GROUPS
    gl = jnp.where(is_g, logits, NEG)
    gmax = jnp.max(gl, axis=1, keepdims=True)
    grp = jnp.min(jnp.where(gl == gmax, lane, float(LANES)), axis=1, keepdims=True)
    p_grp = 1.0 / jnp.sum(jnp.where(is_g, jnp.exp(gl - gmax), 0.0), axis=1, keepdims=True)
    lo = N_EXPERT_GROUPS + grp * EXPERTS_PER_GROUP
    el = jnp.where((lane >= lo) & (lane < lo + EXPERTS_PER_GROUP), logits, NEG)
    v1 = jnp.max(el, axis=1, keepdims=True)
    i1 = jnp.min(jnp.where(el == v1, lane, float(LANES)), axis=1, keepdims=True)
    el2 = jnp.where(lane == i1, NEG, el)
    v2 = jnp.max(el2, axis=1, keepdims=True)
    i2 = jnp.min(jnp.where(el2 == v2, lane, float(LANES)), axis=1, keepdims=True)
    t = jnp.exp(v2 - v1)
    gate1 = p_grp / (1.0 + t)
    gate2 = p_grp * t / (1.0 + t)
    e1 = i1 - N_EXPERT_GROUPS
    e2 = i2 - N_EXPERT_GROUPS
    hit1 = lane == e1
    hit2 = lane == e2
    onehot = jnp.where(hit1 | hit2, 1.0, 0.0)
    rank = jnp.dot(ltri, onehot.astype(BF16), preferred_element_type=F32) + prior_counts
    pos1 = jnp.sum(jnp.where(hit1, rank, 0.0), axis=1, keepdims=True)
    pos2 = jnp.sum(jnp.where(hit2, rank, 0.0), axis=1, keepdims=True)
    slab = jnp.where(lane == 0, e1, jnp.where(lane == 1, e2, jnp.where(lane == 2, pos1, jnp.where(
        lane == 3, pos2, jnp.where(lane == 4, gate1, jnp.where(lane == 5, gate2, 0.0))))))
    return slab, prior_counts + jnp.sum(onehot, axis=0, keepdims=True)


def _inproj_kernel(x_ref, g1_ref, w_ref, b_ref, lng_ref, lnb_ref, cos_ref, sin_ref,
                   uv_ref, qkv0_ref, qkv1_ref, qkv2_ref, kv0_ref, kv1_ref, kv2_ref, de_ref):
    tm = x_ref.shape[0]
    hb = _rms(x_ref[...], g1_ref[...]).astype(BF16)

    def seg(lo, width):
        return jnp.dot(hb, w_ref[:, lo:lo + width], preferred_element_type=F32) + b_ref[:, lo:lo + width]

    uv_ref[:, :A_WIDTH] = jax.nn.gelu(seg(COL_U, A_WIDTH)).astype(BF16)
    uv_ref[:, A_WIDTH:] = _layer_norm(jax.nn.gelu(seg(COL_V, A_WIDTH)), lng_ref[...], lnb_ref[...]).astype(BF16)

    cos = cos_ref[...]
    sin = sin_ref[...]
    qkv_refs = (qkv0_ref, qkv1_ref, qkv2_ref)
    kv_refs = (kv0_ref, kv1_ref, kv2_ref)
    for g, d in enumerate(DILATIONS):
        q = seg(COL_Q + g * GROUP_W, GROUP_W)
        k = seg(COL_K + g * GROUP_W, GROUP_W)
        v = seg(COL_VB + g * GROUP_W, GROUP_W)
        chunks = GROUP_W // LANES
        for c in range(chunks):
            sl = slice(c * LANES, (c + 1) * LANES)
            de_ref[c] = _rope_chunk(q[:, sl], cos, sin) * (HEAD_DIM ** -0.5)
            de_ref[chunks + c] = _rope_chunk(k[:, sl], cos, sin)
            de_ref[2 * chunks + c] = v[:, sl]
        for c in range(3 * chunks):
            sl = slice(c * LANES, (c + 1) * LANES)
            if d == 1:
                qkv_refs[g][:, sl] = de_ref[c].astype(BF16)
            else:
                for r in range(d):
                    qkv_refs[g][0, r, :, sl] = de_ref[c, pl.ds(r, tm // d, stride=d), :].astype(BF16)
        kv_rows = kv_refs[g].shape[2]
        for c in range(2 * chunks):
            kv_refs[g][0, c * LANES:(c + 1) * LANES, :] = de_ref[chunks + c, tm - kv_rows:, :].T


def _inproj(x2, g1, w_bf, b_in, lng, lnb, cos_t, sin_t):
    tm = TM_IN
    tiles_per_seq = SEQ // tm
    n_tiles = T_PROMPT // tm
    const = lambda i: (0, 0)

    def kv_spec(w):
        rows = min(w, tm)
        first = tiles_per_seq - w // rows
        return pl.BlockSpec((1, 2 * GROUP_W, rows),
                            lambda i: (i // tiles_per_seq, 0, jnp.maximum(i % tiles_per_seq - first, 0)))

    def regrouped_spec(d):
        return pl.BlockSpec((1, d, tm // d, 3 * GROUP_W), lambda i: (i // tiles_per_seq, 0, i % tiles_per_seq, 0))

    return pl.pallas_call(
        _inproj_kernel,
        grid=(n_tiles,),
        in_specs=[
            pl.BlockSpec((tm, D_MODEL), lambda i: (i, 0)),
            pl.BlockSpec((1, D_MODEL), const),
            pl.BlockSpec((D_MODEL, IN_WIDTH), const, pipeline_mode=pl.Buffered(1)),
            pl.BlockSpec((1, IN_WIDTH), const),
            pl.BlockSpec((1, A_WIDTH), const),
            pl.BlockSpec((1, A_WIDTH), const),
            pl.BlockSpec((tm, LANES), lambda i: (i % tiles_per_seq, 0)),
            pl.BlockSpec((tm, LANES), lambda i: (i % tiles_per_seq, 0)),
        ],
        out_specs=[
            pl.BlockSpec((tm, 2 * A_WIDTH), lambda i: (i, 0)),
            pl.BlockSpec((tm, 3 * GROUP_W), lambda i: (i, 0)),
            regrouped_spec(DILATIONS[1]),
            regrouped_spec(DILATIONS[2]),
            kv_spec(WINDOWS[0]), kv_spec(WINDOWS[1]), kv_spec(WINDOWS[2]),
        ],
        out_shape=[
            jax.ShapeDtypeStruct((T_PROMPT, 2 * A_WIDTH), BF16),
            jax.ShapeDtypeStruct((T_PROMPT, 3 * GROUP_W), BF16),
            jax.ShapeDtypeStruct((BATCH, DILATIONS[1], SEQ // DILATIONS[1], 3 * GROUP_W), BF16),
            jax.ShapeDtypeStruct((BATCH, DILATIONS[2], SEQ // DILATIONS[2], 3 * GROUP_W), BF16),
            jax.ShapeDtypeStruct((BATCH, 2 * GROUP_W, WINDOWS[0]), F32),
            jax.ShapeDtypeStruct((BATCH, 2 * GROUP_W, WINDOWS[1]), F32),
            jax.ShapeDtypeStruct((BATCH, 2 * GROUP_W, WINDOWS[2]), F32),
        ],
        scratch_shapes=[pltpu.VMEM((3 * GROUP_W // LANES, tm, LANES), F32)],
        compiler_params=_cparams("arbitrary"),
        name="inproj",
    )(x2, g1, w_bf, b_in, lng, lnb, cos_t, sin_t)


def _attn_kernel(c0_ref, p0_ref, c1_ref, p1_ref, c2_ref, p2_ref, o_ref, acc_ref, m_ref, l_ref):
    n = pl.program_id(1)
    qi = lax.broadcasted_iota(jnp.int32, (HEADS * SPAN, 2 * SPAN), 0) % SPAN
    ki = lax.broadcasted_iota(jnp.int32, (HEADS * SPAN, 2 * SPAN), 1)
    band = (ki >= qi) & (ki <= qi + SPAN)
    band_first = band & ((ki >= SPAN) | (n > 0))
    q_head = lax.broadcasted_iota(jnp.int32, (SPAN, GROUP_W), 1) // HEAD_DIM
    kv_head = lax.broadcasted_iota(jnp.int32, (2 * SPAN, GROUP_W), 1) // HEAD_DIM
    first_head_lanes = lax.broadcasted_iota(jnp.int32, (SPAN, LANES), 1) < HEAD_DIM
    contract_last = (((1,), (1,)), ((), ()))
    k_cols = slice(GROUP_W, 2 * GROUP_W)
    v_cols = slice(2 * GROUP_W, 3 * GROUP_W)

    def attend(blocks, fresh):
        scores = []
        for q, keys, _, mask, _ in blocks:
            q_heads = jnp.concatenate([jnp.where(q_head == h, q, jnp.zeros_like(q)) for h in range(HEADS)], axis=0)
            scores.append(jnp.where(mask, lax.dot_general(q_heads, keys, contract_last, preferred_element_type=F32), NEG))
        soft = []
        for s in scores:
            m = jnp.max(s, axis=1, keepdims=True)
            p = jnp.exp(s - m)
            soft.append((m, jnp.sum(p, axis=1, keepdims=True), p.astype(BF16)))
        accs = []
        for (_, _, pb), (_, _, vals, _, _) in zip(soft, blocks):
            p_heads = jnp.concatenate([pb[h * SPAN:(h + 1) * SPAN] for h in range(HEADS)], axis=1)
            v_heads = jnp.concatenate([jnp.where(kv_head == h, vals, jnp.zeros_like(vals)) for h in range(HEADS)],
                                      axis=0)
            accs.append(jnp.dot(p_heads, v_heads, preferred_element_type=F32))
        for (m, l, _), acc, (_, _, _, _, tok_rows) in zip(soft, accs, blocks):
            merge(m, l, acc, tok_rows, fresh)

    def merge(m, l, acc, tok_rows, fresh):
        heads_per_chunk = LANES // HEAD_DIM
        for c in range(GROUP_W // LANES):
            sl = slice(c * LANES, (c + 1) * LANES)
            lo = slice(c * heads_per_chunk * SPAN, (c * heads_per_chunk + 1) * SPAN)
            hi = slice((c * heads_per_chunk + 1) * SPAN, (c * heads_per_chunk + 2) * SPAN)
            m_b = jnp.where(first_head_lanes, m[lo], m[hi])
            l_b = jnp.where(first_head_lanes, l[lo], l[hi])
            if fresh:
                m_ref[c, tok_rows, :] = m_b
                l_ref[c, tok_rows, :] = l_b
                acc_ref[c, tok_rows, :] = acc[:, sl]
                continue
            m_old = m_ref[c, tok_rows, :]
            m_new = jnp.maximum(m_old, m_b)
            a_old = jnp.exp(m_old - m_new)
            a_blk = jnp.exp(m_b - m_new)
            l_ref[c, tok_rows, :] = a_old * l_ref[c, tok_rows, :] + a_blk * l_b
            acc_ref[c, tok_rows, :] = a_old * acc_ref[c, tok_rows, :] + a_blk * acc[:, sl]
            m_ref[c, tok_rows, :] = m_new

    def tok_rows(r, j, d):
        start = j * (SPAN * d) + r
        return pl.ds(start, SPAN) if d == 1 else pl.ds(start, SPAN, stride=d)

    def first_block(c_ref, p_ref, r, d):
        keys = jnp.concatenate([p_ref[0, r, :, k_cols], c_ref[0, r, :SPAN, k_cols]], axis=0)
        vals = jnp.concatenate([p_ref[0, r, :, v_cols], c_ref[0, r, :SPAN, v_cols]], axis=0)
        return c_ref[0, r, :SPAN, :GROUP_W], keys, vals, band_first, tok_rows(r, 0, d)

    def later_block(c_ref, r, j, d):
        q_rows = pl.ds(pl.multiple_of(j * SPAN, SPAN), SPAN)
        kv_rows = pl.ds(pl.multiple_of((j - 1) * SPAN, SPAN), 2 * SPAN)
        return (c_ref[0, r, q_rows, :GROUP_W], c_ref[0, r, kv_rows, k_cols], c_ref[0, r, kv_rows, v_cols],
                band, tok_rows(r, j, d))

    def stream(c_ref, p_ref, r, d, fresh):
        later = c_ref.shape[2] // SPAN - 1
        assert later % ATT_GROUP == 0
        attend([first_block(c_ref, p_ref, r, d)], fresh)

        def body(i, carry):
            j0 = 1 + i * ATT_GROUP
            attend([later_block(c_ref, r, j0 + k, d) for k in range(ATT_GROUP)], fresh)
            return carry

        lax.fori_loop(0, later // ATT_GROUP, body, 0)

    groups = sorted(zip(DILATIONS, ((c0_ref, p0_ref), (c1_ref, p1_ref), (c2_ref, p2_ref))), key=lambda g: -g[0])
    for idx, (d, (c_ref, p_ref)) in enumerate(groups):
        fresh = idx == 0
        if d == 1:
            stream(c_ref, p_ref, 0, d, fresh)
        elif c_ref.shape[2] == SPAN:
            def pair(i, carry, c_ref=c_ref, p_ref=p_ref, d=d, fresh=fresh):
                attend([first_block(c_ref, p_ref, i * ATT_UNROLL + k, d) for k in range(ATT_UNROLL)], fresh)
                return carry
            lax.fori_loop(0, d // ATT_UNROLL, pair, 0)
        else:
            lax.fori_loop(0, d, lambda r, carry, c_ref=c_ref, p_ref=p_ref, d=d, fresh=fresh:
                          (stream(c_ref, p_ref, r, d, fresh), carry)[1], 0)

    for c in range(GROUP_W // LANES):
        o_ref[0, :, c * LANES:(c + 1) * LANES] = (acc_ref[c] / l_ref[c]).astype(BF16)


def _attention(qkv_by_group):
    in_specs, args = [], []
    for qkv, d in zip(qkv_by_group, DILATIONS):
        rows = TA // d
        blocks_per_tile = rows // SPAN
        in_specs.append(pl.BlockSpec((1, d, rows, 3 * GROUP_W), lambda b, n: (b, 0, n, 0)))
        in_specs.append(pl.BlockSpec((1, d, SPAN, 3 * GROUP_W),
                                     lambda b, n, k=blocks_per_tile: (b, 0, jnp.maximum(n * k - 1, 0), 0)))
        args += [qkv, qkv]
    return pl.pallas_call(
        _attn_kernel,
        grid=(BATCH, SEQ // TA),
        in_specs=in_specs,
        out_specs=pl.BlockSpec((1, TA, GROUP_W), lambda b, n: (b, n, 0)),
        out_shape=jax.ShapeDtypeStruct((BATCH, SEQ, GROUP_W), BF16),
        scratch_shapes=[pltpu.VMEM((GROUP_W // LANES, TA, LANES), F32)] * 3,
        compiler_params=_cparams("arbitrary", "arbitrary"),
        name="attn",
    )(*args)


def _merge_kernel(x_ref, uv_ref, g1_ref, wgt_ref, bgt_ref, ob_ref,
                  wsp_ref, bsp_ref, wa_ref, wb_ref, wo_ref, g2_ref, wr_ref, br_ref, ltri_ref,
                  x1_ref, h2p_ref, rt_ref, rtt_ref, cnt_ref, run_ref):
    @pl.when(pl.program_id(0) == 0)
    def _():
        run_ref[...] = jnp.zeros_like(run_ref)

    lane = lax.broadcasted_iota(jnp.int32, (CHUNK, LANES), 1)
    left = lane < A_GROUP_DIM
    zero = jnp.zeros((CHUNK, LANES), BF16)
    n_chunks, n_pairs = TM // CHUNK, A_GROUPS // 2
    chunk_rows = lambda c: slice(c * CHUNK, (c + 1) * CHUNK)
    mixes = []
    for c in range(n_chunks):
        for p in range(n_pairs):
            vp = uv_ref[chunk_rows(c), A_WIDTH + p * LANES:A_WIDTH + (p + 1) * LANES]
            rhs = jnp.concatenate([jnp.where(left, vp, zero), jnp.where(left, zero, vp)], axis=0)
            mixes.append(jnp.dot(wsp_ref[p], rhs, preferred_element_type=F32))
    sa_chunks = []
    for c in range(n_chunks):
        mixed = jnp.concatenate(mixes[c * n_pairs:(c + 1) * n_pairs], axis=1) + bsp_ref[...]
        sa_chunks.append((uv_ref[chunk_rows(c), :A_WIDTH].astype(F32) * mixed).astype(BF16))
    s_a = jnp.concatenate(sa_chunks, axis=0)

    a = jnp.dot(s_a, wa_ref[...], preferred_element_type=F32)
    b = jnp.dot(ob_ref[...], wb_ref[...], preferred_element_type=F32)
    hb = _rms(x_ref[...], g1_ref[...]).astype(BF16)
    gate_a = jax.nn.sigmoid(jnp.dot(hb, wgt_ref[:, :D_MODEL], preferred_element_type=F32) + bgt_ref[:, :D_MODEL])
    gate_b = jax.nn.sigmoid(jnp.dot(hb, wgt_ref[:, D_MODEL:], preferred_element_type=F32) + bgt_ref[:, D_MODEL:])
    merged = gate_a * a + gate_b * b
    x1 = x_ref[...] + jnp.dot(merged.astype(BF16), wo_ref[...], preferred_element_type=F32)
    x1_ref[...] = x1

    h2 = _rms(x1, g2_ref[...])
    h2p_ref[...] = _pack_bf16_pair(h2)
    logits = jnp.dot(h2.astype(BF16), wr_ref[...], preferred_element_type=F32) + br_ref[...]
    slab, counts = _route(logits, run_ref[...], ltri_ref[...])
    rt_ref[...] = slab
    rtt_ref[...] = slab.T[:ROUTE_ROWS, :]
    run_ref[...] = counts
    cnt_ref[...] = counts


def _merge(x2, uv, g1, wgt, bgt, ob, wsp, bsp, wa, wb, wo, g2, wr, br, ltri):
    n_tiles = T_PROMPT // TM
    tile = lambda w: pl.BlockSpec((TM, w), lambda i: (i, 0))
    full = lambda a: pl.BlockSpec(a.shape, lambda i: (0,) * a.ndim)
    return pl.pallas_call(
        _merge_kernel,
        grid=(n_tiles,),
        in_specs=[tile(D_MODEL), tile(2 * A_WIDTH), full(g1), full(wgt), full(bgt), tile(GROUP_W),
                  full(wsp), full(bsp), full(wa), full(wb), full(wo), full(g2), full(wr), full(br),
                  full(ltri)],
        out_specs=[tile(D_MODEL), tile(HALF_D), tile(LANES), pl.BlockSpec((ROUTE_ROWS, TM), lambda i: (i, 0)),
                   pl.BlockSpec((1, LANES), lambda i: (0, 0))],
        out_shape=[
            jax.ShapeDtypeStruct((T_PROMPT, D_MODEL), F32),
            jax.ShapeDtypeStruct((T_PROMPT, HALF_D), U32),
            jax.ShapeDtypeStruct((T_PROMPT, LANES), F32),
            jax.ShapeDtypeStruct((n_tiles * ROUTE_ROWS, TM), F32),
            jax.ShapeDtypeStruct((1, LANES), F32),
        ],
        scratch_shapes=[pltpu.VMEM((1, LANES), F32)],
        compiler_params=_cparams("arbitrary"),
        name="merge_route",
    )(x2, uv, g1, wgt, bgt, ob, wsp, bsp, wa, wb, wo, g2, wr, br, ltri)


SAMPLE_COLS = 768


def _sample_inproj_kernel(x_ref, g1_ref, w_ref, b_ref, z_ref):
    hb = _rms(x_ref[...], g1_ref[...]).astype(BF16)
    z_ref[...] = jnp.dot(hb, w_ref[...], preferred_element_type=F32) + b_ref[...]


def _sample_inproj(xs, g1, w_in, b_in):
    const = lambda j: (0, 0)
    return pl.pallas_call(
        _sample_inproj_kernel,
        grid=(IN_WIDTH // SAMPLE_COLS,),
        in_specs=[pl.BlockSpec((DEC_BATCH, D_MODEL), const), pl.BlockSpec((1, D_MODEL), const),
                  pl.BlockSpec((D_MODEL, SAMPLE_COLS), lambda j: (0, j)),
                  pl.BlockSpec((1, SAMPLE_COLS), lambda j: (0, j))],
        out_specs=pl.BlockSpec((DEC_BATCH, SAMPLE_COLS), lambda j: (0, j)),
        out_shape=jax.ShapeDtypeStruct((DEC_BATCH, IN_WIDTH), F32),
        compiler_params=_cparams("arbitrary"),
        name="sample_inproj",
    )(xs, g1, w_in, b_in)


def _sample_pre_kernel(z_ref, lng_ref, lnb_ref, ws0_ref, bs0_ref, cos_ref, sin_ref,
                       vrow_ref, sa_ref, qr_ref, kr_ref, kv0_ref, kv1_ref, kv2_ref):
    u = jax.nn.gelu(z_ref[:, COL_U:COL_U + A_WIDTH])
    va = _layer_norm(jax.nn.gelu(z_ref[:, COL_V:COL_V + A_WIDTH]), lng_ref[...], lnb_ref[...])
    vrow_ref[...] = va
    sa_ref[...] = u * (ws0_ref[...] * va.astype(BF16).astype(F32) + bs0_ref[...])
    cos, sin = cos_ref[...], sin_ref[...]
    for g, kv_ref in enumerate((kv0_ref, kv1_ref, kv2_ref)):
        for c in range(GROUP_W // LANES):
            off = g * GROUP_W + c * LANES
            qr_ref[:, off:off + LANES] = (_rope_chunk(z_ref[:, COL_Q + off:COL_Q + off + LANES], cos, sin)
                                          * (HEAD_DIM ** -0.5))
            kr = _rope_chunk(z_ref[:, COL_K + off:COL_K + off + LANES], cos, sin)
            kr_ref[:, off:off + LANES] = kr
            kv_ref[:, c * LANES:(c + 1) * LANES] = kr
        kv_ref[:, GROUP_W:] = z_ref[:, COL_VB + g * GROUP_W:COL_VB + (g + 1) * GROUP_W]


def _sample_pre(z, lng, lnb, ws0, bs0, cos_s, sin_s):
    row = lambda w: jax.ShapeDtypeStruct((DEC_BATCH, w), F32)
    return pl.pallas_call(
        _sample_pre_kernel,
        out_shape=[row(A_WIDTH), row(A_WIDTH), row(QK_W), row(QK_W),
                   row(2 * GROUP_W), row(2 * GROUP_W), row(2 * GROUP_W)],
        compiler_params=pltpu.CompilerParams(vmem_limit_bytes=VMEM_LIMIT),
        name="sample_pre",
    )(z, lng, lnb, ws0, bs0, cos_s, sin_s)


def _sample_attn_kernel(q_ref, k_ref, v_ref, c0_ref, c1_ref, c2_ref, comb_ref):
    as_operand = lambda t: t.astype(BF16).astype(F32)
    contract_last = (((1,), (1,)), ((), ()))
    pad_rows = 8 // HEADS
    groups = tuple(zip((c0_ref, c1_ref, c2_ref), DILATIONS))
    pairs = [(h, g) for h in range(HEADS) for g in range(len(groups))]
    scores = {}
    for h, g in pairs:
        c_ref, d = groups[g]
        q_rows = jnp.concatenate([q_ref[0, g]] * pad_rows, axis=0).astype(BF16)
        s = jnp.dot(q_rows, c_ref[0, 0, h].astype(BF16), preferred_element_type=F32)[h:h + 1]
        if d > 1:
            pos = lax.broadcasted_iota(jnp.int32, s.shape, 1)
            s = jnp.where(pos % d == 0, s, NEG)
        scores[h, g] = s
    probs = {}
    for h, g in pairs:
        s = scores[h, g]
        qn, kn = (as_operand(r[0, g, h:h + 1, :]) for r in (q_ref, k_ref))
        s_n = jnp.sum(qn * kn, axis=1, keepdims=True)
        m = jnp.maximum(jnp.max(s, axis=1, keepdims=True), s_n)
        lse = m + jnp.log(jnp.sum(jnp.exp(s - m), axis=1, keepdims=True) + jnp.exp(s_n - m))
        probs[h, g] = (jnp.broadcast_to(jnp.exp(s - lse), (8, s.shape[1])).astype(BF16),
                       as_operand(jnp.exp(s_n - lse)), lse)
    outs = {}
    for h, g in pairs:
        p_rows, p_n, _ = probs[h, g]
        o = lax.dot_general(p_rows, groups[g][0][0, 1, h].astype(BF16), contract_last, preferred_element_type=F32)
        outs[h, g] = o[0:1] + p_n * as_operand(v_ref[0, g, h:h + 1, :])
    for h in range(HEADS):
        lses = [probs[h, g][2] for g in range(len(groups))]
        mx = jnp.maximum(jnp.maximum(lses[0], lses[1]), lses[2])
        ws = [jnp.exp(t - mx) for t in lses]
        tot = ws[0] + ws[1] + ws[2]
        comb_ref[0, h:h + 1, :] = (ws[0] * outs[h, 0] + ws[1] * outs[h, 1] + ws[2] * outs[h, 2]) / tot


def _sample_attn(qt, kt, vt, caches):
    new_spec = pl.BlockSpec((1, 3, HEADS, HEAD_DIM), lambda i: (i, 0, 0, 0))
    cache_spec = lambda c: pl.BlockSpec((1,) + c.shape[1:], lambda i: (i, 0, 0, 0, 0))
    return pl.pallas_call(
        _sample_attn_kernel,
        grid=(DEC_BATCH,),
        in_specs=[new_spec, new_spec, new_spec, cache_spec(caches[0]), cache_spec(caches[1]), cache_spec(caches[2])],
        out_specs=pl.BlockSpec((1, HEADS, HEAD_DIM), lambda i: (i, 0, 0)),
        out_shape=jax.ShapeDtypeStruct((DEC_BATCH, HEADS, HEAD_DIM), F32),
        compiler_params=_cparams("arbitrary"),
        name="sample_attn",
    )(qt, kt, vt, caches[0], caches[1], caches[2])


def _sample_merge_kernel(x_ref, z_ref, sa_ref, comb_ref, wa_ref, wb_ref, wo_ref, g2_ref, wr_ref, br_ref,
                         ltri_ref, cnt_in_ref, x1_ref, h2p_ref, rt_ref, cnt_ref):
    dot = lambda p, q: jnp.dot(p.astype(BF16), q, preferred_element_type=F32)
    a = dot(sa_ref[...], wa_ref[...])
    b = dot(comb_ref[...], wb_ref[...])
    merged = (jax.nn.sigmoid(z_ref[:, COL_GA:COL_GA + D_MODEL]) * a
              + jax.nn.sigmoid(z_ref[:, COL_GB:COL_GB + D_MODEL]) * b)
    x1 = x_ref[...] + dot(merged, wo_ref[...])
    x1_ref[...] = x1
    h2 = _rms(x1, g2_ref[...])
    h2p_ref[...] = _pack_bf16_pair(h2)
    logits = dot(h2, wr_ref[...]) + br_ref[...]
    slab, counts = _route(logits, cnt_in_ref[...], ltri_ref[...])
    rt_ref[...] = slab
    cnt_ref[...] = counts


def _sample_merge(xs, z, s_a, comb, wa, wb, wo, g2, wr, br, ltri, counts):
    return pl.pallas_call(
        _sample_merge_kernel,
        out_shape=[jax.ShapeDtypeStruct((DEC_BATCH, D_MODEL), F32),
                   jax.ShapeDtypeStruct((DEC_BATCH, HALF_D), U32),
                   jax.ShapeDtypeStruct((DEC_BATCH, LANES), F32),
                   jax.ShapeDtypeStruct((1, LANES), F32)],
        compiler_params=pltpu.CompilerParams(vmem_limit_bytes=VMEM_LIMIT),
        name="sample_merge_route",
    )(xs, z, s_a, comb, wa, wb, wo, g2, wr, br, ltri, counts)


def _dispatch_kernel(dest0_ref, dest1_ref, h_ref, xs_in_ref, xs_ref, sem):
    del xs_in_ref
    groups, sub = h_ref.shape[0], h_ref.shape[1]
    rows = groups * sub
    base = pl.program_id(0) * rows

    def start(j, c):
        for k in range(sub):
            for dest_ref in (dest0_ref, dest1_ref):
                d = dest_ref[base + j * sub + k]
                pltpu.make_async_copy(h_ref.at[j, pl.ds(k, 1)], xs_ref.at[pl.ds(d, 1)], sem).start()
        return c

    lax.fori_loop(0, groups, start, 0)
    for _ in range(2):
        pltpu.make_async_copy(xs_ref.at[pl.ds(0, rows)], xs_ref.at[pl.ds(0, rows)], sem).wait()


def _dispatch(dest, h2p, xs, rows):
    n = h2p.shape[0]
    sub = 8
    h2p = h2p.reshape(n // sub, sub, HALF_D)
    return pl.pallas_call(
        _dispatch_kernel,
        grid_spec=pltpu.PrefetchScalarGridSpec(
            num_scalar_prefetch=2,
            grid=(n // rows,),
            in_specs=[pl.BlockSpec((rows // sub, sub, HALF_D), lambda i, *_: (i, 0, 0)),
                      pl.BlockSpec(memory_space=pl.ANY)],
            out_specs=pl.BlockSpec(memory_space=pl.ANY),
            scratch_shapes=[pltpu.SemaphoreType.DMA(())],
        ),
        out_shape=jax.ShapeDtypeStruct(xs.shape, xs.dtype),
        input_output_aliases={3: 0},
        compiler_params=_cparams("arbitrary"),
        name=f"moe_dispatch_{rows}",
    )(dest[0], dest[1], h2p, xs)


def _ffn_kernel(be_ref, nused_ref, nexte_ref, wslot_ref, xs_ref, wg_ref, wu_ref, wd_ref, ys_ref,
                wg_b, wu_b, wd_b, wg_f, wu_f, wd_f, wsem):
    i = pl.program_id(0)
    live = i < nused_ref[0]
    new_expert = jnp.logical_and(live, jnp.logical_or(i == 0, be_ref[i] != be_ref[jnp.maximum(i - 1, 0)]))

    def weight_copies(e, slot):
        return [pltpu.make_async_copy(src.at[e], dst.at[slot], wsem.at[slot])
                for src, dst in ((wg_ref, wg_f), (wu_ref, wu_f), (wd_ref, wd_f))]

    @pl.when(i == 0)
    def _():
        for cp in weight_copies(be_ref[0], 0):
            cp.start()

    @pl.when(new_expert)
    def _():
        slot = wslot_ref[i]
        for cp in weight_copies(be_ref[i], slot):
            cp.wait()
        wg_b[...] = wg_f[slot].astype(BF16)
        wu_b[...] = wu_f[slot].astype(BF16)
        wd_b[...] = wd_f[slot].astype(BF16)

        @pl.when(nexte_ref[i] >= 0)
        def _():
            for cp in weight_copies(nexte_ref[i], 1 - slot):
                cp.start()

    @pl.when(live)
    def _():
        x = jnp.concatenate(_unpack_bf16_pair(xs_ref[...]), axis=1)
        gate = jnp.dot(x, wg_b[...], preferred_element_type=F32)
        up = jnp.dot(x, wu_b[...], preferred_element_type=F32)
        hid = (jax.nn.silu(gate) * up).astype(BF16)
        ys_ref[...] = jnp.dot(hid, wd_b[...], preferred_element_type=F32)

    @pl.when(i >= nused_ref[0])
    def _():
        ys_ref[...] = jnp.zeros_like(ys_ref)


def _expert_ffn(block_e, nused, xs, w_gate, w_up, w_down):
    idx = jnp.arange(MOE_NB, dtype=jnp.int32)
    change = ((block_e != jnp.roll(block_e, 1)) | (idx == 0)) & (idx < nused[0])
    wslot = ((jnp.cumsum(change.astype(jnp.int32)) - 1) & 1).astype(jnp.int32)
    change_at = jnp.where(change, idx, MOE_NB)
    next_change = jnp.flip(lax.cummin(jnp.flip(jnp.concatenate([change_at[1:], jnp.full((1,), MOE_NB, jnp.int32)]))))
    nexte = jnp.where(next_change < MOE_NB, block_e[jnp.minimum(next_change, MOE_NB - 1)], -1).astype(jnp.int32)
    hbm = pl.BlockSpec(memory_space=pl.ANY)
    return pl.pallas_call(
        _ffn_kernel,
        grid_spec=pltpu.PrefetchScalarGridSpec(
            num_scalar_prefetch=4,
            grid=(MOE_NB,),
            in_specs=[pl.BlockSpec((MOE_TB, HALF_D), lambda i, be, nu, *_: (jnp.minimum(i, nu[0] - 1), 0)),
                      hbm, hbm, hbm],
            out_specs=pl.BlockSpec((MOE_TB, D_MODEL), lambda i, *_: (i, 0)),
            scratch_shapes=[pltpu.VMEM((D_MODEL, D_EXPERT), BF16), pltpu.VMEM((D_MODEL, D_EXPERT), BF16),
                            pltpu.VMEM((D_EXPERT, D_MODEL), BF16),
                            pltpu.VMEM((2, D_MODEL, D_EXPERT), F32), pltpu.VMEM((2, D_MODEL, D_EXPERT), F32),
                            pltpu.VMEM((2, D_EXPERT, D_MODEL), F32), pltpu.SemaphoreType.DMA((2,))],
        ),
        out_shape=jax.ShapeDtypeStruct((MOE_ROWS, D_MODEL), F32),
        compiler_params=_cparams("arbitrary"),
        name="moe_ffn",
    )(block_e, nused, nexte, wslot, xs, w_gate, w_up, w_down)


def _combine_kernel(dest0_ref, dest1_ref, x1_ref, rt_ref, gf_ref, ys_ref, y_ref, ya, yb, sem):
    rows = x1_ref.shape[0]
    base = pl.program_id(0) * rows

    def row_copy(t, dest_ref, buf):
        return pltpu.make_async_copy(ys_ref.at[pl.ds(dest_ref[base + t], 1)], buf.at[pl.ds(t, 1)], sem)

    def start(t, c):
        row_copy(t, dest0_ref, ya).start()
        row_copy(t, dest1_ref, yb).start()
        return c

    lax.fori_loop(0, rows, start, 0, unroll=ROW_DMA_UNROLL)
    pltpu.make_async_copy(ys_ref.at[pl.ds(0, rows)], ya, sem).wait()
    pltpu.make_async_copy(ys_ref.at[pl.ds(0, rows)], yb, sem).wait()
    x2 = x1_ref[...] + rt_ref[:, 4:5] * ya[...] + rt_ref[:, 5:6] * yb[...]
    y_ref[...] = _rms(x2, gf_ref[...])


def _combine(dest, x1, rt, gf, ys, rows):
    n = x1.shape[0]
    return pl.pallas_call(
        _combine_kernel,
        grid_spec=pltpu.PrefetchScalarGridSpec(
            num_scalar_prefetch=2,
            grid=(n // rows,),
            in_specs=[pl.BlockSpec((rows, D_MODEL), lambda i, *_: (i, 0)),
                      pl.BlockSpec((rows, LANES), lambda i, *_: (i, 0)),
                      pl.BlockSpec((1, D_MODEL), lambda i, *_: (0, 0)),
                      pl.BlockSpec(memory_space=pl.ANY)],
            out_specs=pl.BlockSpec((rows, D_MODEL), lambda i, *_: (i, 0)),
            scratch_shapes=[pltpu.VMEM((rows, D_MODEL), F32), pltpu.VMEM((rows, D_MODEL), F32),
                            pltpu.SemaphoreType.DMA(())],
        ),
        out_shape=jax.ShapeDtypeStruct((n, D_MODEL), F32),
        compiler_params=_cparams("arbitrary"),
        name=f"moe_combine_{rows}",
    )(dest[0], dest[1], x1, rt, gf, ys)


def _rope_tables(pos, xp):
    half = HEAD_DIM // 2
    inv = 1.0 / (xp.float32(ROPE_THETA) ** (xp.arange(half, dtype=xp.float32) * xp.float32(2.0 / HEAD_DIM)))
    ang = pos.astype(xp.float32)[:, None] * inv[None, :].astype(xp.float32)
    cos, sin = xp.cos(ang), xp.sin(ang)
    return xp.concatenate([cos, cos, cos, cos], axis=1), xp.concatenate([-sin, sin, -sin, sin], axis=1)


def _slot_dest(ids, pos, pstarts):
    ids = ids.astype(jnp.int32)
    experts = jnp.arange(N_EXPERTS, dtype=jnp.int32).reshape((N_EXPERTS,) + (1,) * ids.ndim)
    start = jnp.sum(jnp.where(ids[None] == experts, pstarts.reshape(experts.shape), 0), axis=0)
    return start + pos.astype(jnp.int32)


def kernel(x_prompt, x_sample, cache_kv_g0, cache_kv_g1, cache_kv_g2, norm1_g, w_in, b_in, a_ln_g, a_ln_b, w_spatial, b_spatial, w_a_proj, w_b_proj, w_o, norm2_g, w_group_router, b_group_router, w_expert_router, b_expert_router, w_gate, w_up, w_down, final_norm_g):
    x2 = x_prompt.reshape(T_PROMPT, D_MODEL)
    xs = x_sample.reshape(DEC_BATCH, D_MODEL)
    g1 = norm1_g[0][None, :]
    g2 = norm2_g[0][None, :]
    gf = final_norm_g[None, :]
    b_in2 = b_in[0][None, :]
    lng, lnb = a_ln_g[0][None, :], a_ln_b[0][None, :]

    causal = jnp.tril(jnp.ones((CHUNK, CHUNK), dtype=bool))
    ws_tril = jnp.where(causal[None], w_spatial[0], 0.0)
    wsp = jnp.concatenate([ws_tril[0::2], ws_tril[1::2]], axis=2).astype(BF16)
    bsp = jnp.repeat(b_spatial[0].T, A_GROUP_DIM, axis=1)
    ws0 = jnp.repeat(ws_tril[:, 0, 0].astype(BF16).astype(F32), A_GROUP_DIM)[None, :]
    bs0 = jnp.repeat(b_spatial[0][:, 0], A_GROUP_DIM)[None, :]
    w_router = jnp.zeros((D_MODEL, LANES), F32)
    w_router = w_router.at[:, :N_EXPERT_GROUPS].set(w_group_router[0])
    w_router = w_router.at[:, N_EXPERT_GROUPS:N_EXPERT_GROUPS + N_EXPERTS].set(w_expert_router[0])
    b_router = jnp.zeros((1, LANES), F32)
    b_router = b_router.at[0, :N_EXPERT_GROUPS].set(b_group_router[0])
    b_router = b_router.at[0, N_EXPERT_GROUPS:N_EXPERT_GROUPS + N_EXPERTS].set(b_expert_router[0])
    w_router = w_router.astype(BF16)
    ltri = jnp.tril(jnp.ones((TM, TM), BF16), -1)
    w_in_b, w_a_b, w_b_b, w_o_b = (w[0].astype(BF16) for w in (w_in, w_a_proj, w_b_proj, w_o))

    cos_p, sin_p = _rope_tables(np.arange(SEQ, dtype=np.int32), np)
    cos_s, sin_s = _rope_tables(jnp.full((1,), PAST_LEN, jnp.int32), jnp)

    uv, qkv0, qkv1, qkv2, kvp0, kvp1, kvp2 = _inproj(
        x2, g1, w_in_b, b_in2, lng, lnb, cos_p, sin_p)
    ob = _attention((qkv0.reshape(BATCH, 1, SEQ, 3 * GROUP_W), qkv1, qkv2)).reshape(T_PROMPT, GROUP_W)
    x1_p, h2p_p, rt_p, rtt_p, counts_p = _merge(
        x2, uv, g1, w_in_b[:, COL_GA:], b_in2[:, COL_GA:], ob, wsp, bsp, w_a_b, w_b_b, w_o_b, g2, w_router, b_router,
        ltri)

    z_s = _sample_inproj(xs, g1, w_in_b, b_in2)
    vrow, sa_s, qr_s, kr_s, kvs0, kvs1, kvs2 = _sample_pre(z_s, lng, lnb, ws0, bs0, cos_s, sin_s)
    per_head = lambda t: t.reshape(DEC_BATCH, 3, HEADS, HEAD_DIM)
    caches = [c.transpose(0, 1, 3, 4, 5, 2).reshape(DEC_BATCH, 2, HEADS, HEAD_DIM, c.shape[2])
              for c in (cache_kv_g0, cache_kv_g1, cache_kv_g2)]
    comb_s = _sample_attn(per_head(qr_s), per_head(kr_s), per_head(z_s[:, COL_VB:COL_VB + QK_W]), caches)
    comb_s = comb_s.reshape(DEC_BATCH, GROUP_W)
    x1_s, h2p_s, rt_s, counts = _sample_merge(
        xs, z_s, sa_s, comb_s, w_a_b, w_b_b, w_o_b, g2, w_router, b_router,
        ltri[:DEC_BATCH, :DEC_BATCH], counts_p)

    cnt = counts[0, :N_EXPERTS].astype(jnp.int32)
    padded = (cnt + MOE_TB - 1) // MOE_TB * MOE_TB
    pends = jnp.cumsum(padded)
    pstarts = pends - padded
    block_starts = jnp.arange(MOE_NB, dtype=jnp.int32) * MOE_TB
    block_e = jnp.minimum(jnp.sum((pends[None, :] <= block_starts[:, None]).astype(jnp.int32), axis=1),
                          N_EXPERTS - 1)
    nused = (pends[-1:] // MOE_TB).astype(jnp.int32)
    rtt_p = rtt_p.reshape(T_PROMPT // TM, ROUTE_ROWS, TM)
    dest_p = [_slot_dest(rtt_p[:, k], rtt_p[:, 2 + k], pstarts).reshape(T_PROMPT) for k in range(2)]
    dest_s = [_slot_dest(rt_s[:, k], rt_s[:, 2 + k], pstarts) for k in range(2)]

    rows = jnp.zeros((MOE_ROWS, HALF_D), U32)
    rows = _dispatch(dest_p, h2p_p, rows, TM)
    rows = _dispatch(dest_s, h2p_s, rows, DEC_BATCH)
    ys = _expert_ffn(block_e, nused, rows, w_gate[0], w_up[0], w_down[0])
    y_p = _combine(dest_p, x1_p, rt_p, gf, ys, TM)
    y_s = _combine(dest_s, x1_s, rt_s, gf, ys, DEC_BATCH)

    kv_shape = lambda n, w: (1, n, w, 2, HEADS, HEAD_DIM)
    window_rows = lambda t: t.reshape(1, BATCH, 2, HEADS, HEAD_DIM, t.shape[2]).transpose(0, 1, 5, 2, 3, 4)
    return (y_p.reshape(BATCH, SEQ, D_MODEL), y_s.reshape(DEC_BATCH, 1, D_MODEL),
            window_rows(kvp0), window_rows(kvp1), window_rows(kvp2),
            kvs0.reshape(kv_shape(DEC_BATCH, 1)), kvs1.reshape(kv_shape(DEC_BATCH, 1)),
            kvs2.reshape(kv_shape(DEC_BATCH, 1)), vrow.reshape(1, DEC_BATCH, 1, A_WIDTH))
```
